```python
import math
import jax, jax.numpy as jnp
from jax import lax
import numpy as np

D_MODEL = 1024
BATCH = 4
SEQ = 8192
DEPTH = 1

GRID_W = 64
CTX_LEN = 256

D_S5 = D_MODEL // 2
S5_GROUP_CH = 16
S5_GROUPS = D_S5 // S5_GROUP_CH
S5_STATE = 64

D_SGU = D_MODEL // 2
SGU_GROUPS = 8
SGU_GROUP_CH = D_SGU // SGU_GROUPS
CHUNK = 128
CHUNK_ROWS = CHUNK // GRID_W

D_IN_PROJ = D_S5 + 2 * D_SGU + 2 * D_MODEL
IN_SPLITS = [D_S5, D_S5 + D_SGU, D_S5 + 2 * D_SGU, D_S5 + 2 * D_SGU + D_MODEL]

N_EXPERTS = 256
TOP_K = 8
N_EXPERT_GROUPS = 8
TOPK_GROUPS = 4
D_EXPERT = D_MODEL // 4
D_SHARED = D_EXPERT
ROUTE_SCALE = 2.5
MOE_BLOCK = 128

EPS = 1e-6

kernel_name = "hybrid_s5_sgu_moe_dit_block"


def rmsnorm(x, g):
    xf = x.astype(jnp.float32)
    y = xf * lax.rsqrt(jnp.mean(xf * xf, axis=-1, keepdims=True) + EPS) * g.astype(jnp.float32)
    return y.astype(x.dtype)


def modulate(h, shift, scale):
    return h * (1.0 + scale) + shift


def _linear_recurrence(e1, e2):
    a1, b1 = e1
    a2, b2 = e2
    return a1 * a2, a2 * b1 + b2


def s5_discretise(a_re, a_im, log_dt, b_re, b_im, c_re, c_im):
    f32 = jnp.float32
    a = lax.complex(a_re.astype(f32), a_im.astype(f32))
    dt = jnp.exp(log_dt.astype(f32))[..., None]
    a_bar = jnp.exp(dt * a)
    b_bar = ((a_bar - 1.0) / a)[..., None] * lax.complex(b_re.astype(f32), b_im.astype(f32))
    c_mat = lax.complex(c_re.astype(f32), c_im.astype(f32))
    return a_bar, b_bar, c_mat


def s5_scan(u, a_bar, b_bar, c_mat, init, with_readout):
    bsz, t, _ = u.shape
    ug = u.astype(jnp.float32).reshape(bsz, t, S5_GROUPS, S5_GROUP_CH)
    uc = ug.astype(jnp.complex64)
    y = None
    finals = []
    for d in range(2):
        reverse = d == 1
        bu = jnp.einsum('btgj,gpj->btgp', uc, b_bar[d])
        a = jnp.broadcast_to(a_bar[d], (1, t) + a_bar.shape[1:])
        a_cum, s = lax.associative_scan(_linear_recurrence, (a, bu), reverse=reverse, axis=1)
        if init is not None:
            s = s + a_cum * init[d][:, None]
        finals.append(s[:, 0] if reverse else s[:, -1])
        if with_readout:
            y_d = jnp.real(jnp.einsum('btgp,gjp->btgj', s, c_mat[d]))
            y = y_d if y is None else y + y_d
    return ug, y, finals


def spatial_gating(u, v, n_chunks, norm_g, w_s, b_s):
    bsz, t, _ = u.shape
    u = jax.nn.gelu(u)
    v = rmsnorm(jax.nn.gelu(v), norm_g)
    vc = v.reshape(bsz, n_chunks, CHUNK, SGU_GROUPS, SGU_GROUP_CH)
    mixed = jnp.einsum('hts,bnshc->bnthc', w_s, vc) + b_s.T[:, :, None]
    return u * mixed.reshape(bsz, t, D_SGU)


def token_mixer(z, a_bar, b_bar, c_mat, init, n_chunks, s5_d, w_glu, b_glu,
                sgu_norm, sgu_w, sgu_b, w_branch_s5, w_branch_sgu, w_out):
    u_s5, u_sgu, v_sgu, g_s5, g_sgu = jnp.split(z, IN_SPLITS, axis=-1)
    ug, y, finals = s5_scan(u_s5, a_bar, b_bar, c_mat, init, True)
    y = y + s5_d.astype(jnp.float32).reshape(S5_GROUPS, S5_GROUP_CH) * ug
    y = jax.nn.gelu(y.reshape(z.shape[:2] + (D_S5,))).astype(z.dtype)
    y_s5 = y * jax.nn.sigmoid(y @ w_glu + b_glu)
    y_sgu = spatial_gating(u_sgu, v_sgu, n_chunks, sgu_norm, sgu_w, sgu_b)
    merged = (jax.nn.sigmoid(g_s5) * (y_s5 @ w_branch_s5)
              + jax.nn.sigmoid(g_sgu) * (y_sgu @ w_branch_sgu))
    return merged @ w_out, finals


def route(h, w_router, router_bias):
    n = h.shape[0]
    scores = jax.nn.sigmoid((h @ w_router).astype(jnp.float32))
    sel = scores + router_bias.astype(jnp.float32)
    per_group = N_EXPERTS // N_EXPERT_GROUPS
    grp_score = lax.top_k(sel.reshape(n, N_EXPERT_GROUPS, per_group), 2)[0].sum(-1)
    _, top_g = lax.top_k(grp_score, TOPK_GROUPS)
    gmask = jax.nn.one_hot(top_g, N_EXPERT_GROUPS, dtype=jnp.float32).sum(1) > 0
    sel = jnp.where(jnp.repeat(gmask, per_group, axis=1), sel, -jnp.inf)
    _, idx = lax.top_k(sel, TOP_K)
    w = jnp.take_along_axis(scores, idx, axis=1)
    w = w / jnp.sum(w, axis=-1, keepdims=True) * ROUTE_SCALE
    return idx, w


def routed_experts(h, idx, wts, w_gate, w_up, w_down):
    n, d = h.shape
    nk = n * TOP_K
    flat_e = idx.reshape(-1)
    flat_tok = (jnp.arange(nk, dtype=jnp.int32) // TOP_K)
    flat_w = wts.reshape(-1).astype(h.dtype)
    order = jnp.argsort(flat_e)
    e_sorted = flat_e[order]
    counts = jnp.bincount(flat_e, length=N_EXPERTS)
    starts = jnp.cumsum(counts) - counts
    padded = (counts + MOE_BLOCK - 1) // MOE_BLOCK * MOE_BLOCK
    pends = jnp.cumsum(padded)
    pstarts = pends - padded
    dest = pstarts[e_sorted] + (jnp.arange(nk) - starts[e_sorted])
    n_blocks = -(-nk // MOE_BLOCK) + N_EXPERTS
    cap = n_blocks * MOE_BLOCK
    tok_buf = jnp.full((cap,), n, jnp.int32).at[dest].set(flat_tok[order])
    w_buf = jnp.zeros((cap,), h.dtype).at[dest].set(flat_w[order])
    blk_expert = jnp.minimum(
        jnp.searchsorted(pends, jnp.arange(n_blocks) * MOE_BLOCK, side='right'), N_EXPERTS - 1)
    h_pad = jnp.concatenate([h, jnp.zeros((1, d), h.dtype)], axis=0)

    def expert_block(args):
        tok, w, e = args
        xb = h_pad[tok]
        hidden = jax.nn.silu(xb @ w_gate[e]) * (xb @ w_up[e])
        return (hidden @ w_down[e]) * w[:, None]

    out = lax.map(expert_block, (tok_buf.reshape(n_blocks, MOE_BLOCK),
                                 w_buf.reshape(n_blocks, MOE_BLOCK), blk_expert))
    return jax.ops.segment_sum(out.reshape(cap, d), tok_buf, num_segments=n + 1)[:n]


def moe_ffn(h, w_router, router_bias, w_exp_gate, w_exp_up, w_exp_down, w_sh_gate, w_sh_up, w_sh_down):
    idx, wts = route(h, w_router, router_bias)
    routed = routed_experts(h, idx, wts, w_exp_gate, w_exp_up, w_exp_down)
    shared = (jax.nn.silu(h @ w_sh_gate) * (h @ w_sh_up)) @ w_sh_down
    return routed + shared


def setup_inputs(seed: int = 0) -> dict:
    key = jax.random.key(seed)
    kit = iter(jax.random.split(key, 40))
    f32 = jnp.float32
    L = DEPTH

    def nrm(shape, scale):
        return jax.random.normal(next(kit), shape, f32) * scale

    n_idx = jnp.arange(S5_STATE, dtype=f32)
    return {
        "x": nrm((BATCH, SEQ, D_MODEL), 1.0),
        "c": nrm((BATCH, D_MODEL), 1.0),
        "ctx": nrm((BATCH, CTX_LEN, D_MODEL), 1.0),
        "c_ctx": nrm((D_MODEL,), 1.0),
        "w_mod": nrm((L, D_MODEL, 6 * D_MODEL), 0.5 * D_MODEL ** -0.5),
        "b_mod": nrm((L, 6 * D_MODEL), 0.02),
        "norm_pre_mix": 1.0 + nrm((L, D_MODEL), 0.05),
        "norm_post_mix": 1.0 + nrm((L, D_MODEL), 0.05),
        "norm_pre_ffn": 1.0 + nrm((L, D_MODEL), 0.05),
        "norm_post_ffn": 1.0 + nrm((L, D_MODEL), 0.05),
        "w_in": nrm((L, D_MODEL, D_IN_PROJ), D_MODEL ** -0.5),
        "s5_a_re": -0.5 + nrm((L, 2, S5_GROUPS, S5_STATE), 0.01),
        "s5_a_im": math.pi * n_idx + nrm((L, 2, S5_GROUPS, S5_STATE), 0.01),
        "s5_log_dt": jax.random.uniform(next(kit), (L, 2, S5_GROUPS), f32, math.log(1e-3), math.log(1e-1)),
        "s5_b_re": nrm((L, 2, S5_GROUPS, S5_STATE, S5_GROUP_CH), (2.0 * S5_GROUP_CH) ** -0.5),
        "s5_b_im": nrm((L, 2, S5_GROUPS, S5_STATE, S5_GROUP_CH), (2.0 * S5_GROUP_CH) ** -0.5),
        "s5_c_re": nrm((L, 2, S5_GROUPS, S5_GROUP_CH, S5_STATE), 0.5),
        "s5_c_im": nrm((L, 2, S5_GROUPS, S5_GROUP_CH, S5_STATE), 0.5),
        "s5_d": nrm((L, D_S5), 1.0),
        "w_glu": nrm((L, D_S5, D_S5), D_S5 ** -0.5),
        "b_glu": nrm((L, D_S5), 0.02),
        "sgu_norm": 1.0 + nrm((L, D_SGU), 0.05),
        "sgu_w": nrm((L, SGU_GROUPS, CHUNK, CHUNK), CHUNK ** -0.5),
        "sgu_b": 1.0 + nrm((L, SGU_GROUPS, CHUNK), 0.02),
        "w_branch_s5": nrm((L, D_S5, D_MODEL), D_S5 ** -0.5),
        "w_branch_sgu": nrm((L, D_SGU, D_MODEL), D_SGU ** -0.5),
        "w_out": nrm((L, D_MODEL, D_MODEL), D_MODEL ** -0.5),
        "w_router": nrm((L, D_MODEL, N_EXPERTS), D_MODEL ** -0.5),
        "router_bias": nrm((L, N_EXPERTS), 0.01),
        "w_exp_gate": nrm((L, N_EXPERTS, D_MODEL, D_EXPERT), D_MODEL ** -0.5),
        "w_exp_up": nrm((L, N_EXPERTS, D_MODEL, D_EXPERT), D_MODEL ** -0.5),
        "w_exp_down": nrm((L, N_EXPERTS, D_EXPERT, D_MODEL), D_EXPERT ** -0.5),
        "w_sh_gate": nrm((L, D_MODEL, D_SHARED), D_MODEL ** -0.5),
        "w_sh_up": nrm((L, D_MODEL, D_SHARED), D_MODEL ** -0.5),
        "w_sh_down": nrm((L, D_SHARED, D_MODEL), D_SHARED ** -0.5),
    }


def reference(x, c, ctx, c_ctx, w_mod, b_mod, norm_pre_mix, norm_post_mix, norm_pre_ffn, norm_post_ffn,
              w_in, s5_a_re, s5_a_im, s5_log_dt, s5_b_re, s5_b_im, s5_c_re, s5_c_im, s5_d, w_glu, b_glu,
              sgu_norm, sgu_w, sgu_b, w_branch_s5, w_branch_sgu, w_out,
              w_router, router_bias, w_exp_gate, w_exp_up, w_exp_down, w_sh_gate, w_sh_up, w_sh_down):
    rows = x.shape[1] // GRID_W
    lat_chunks = rows // CHUNK_ROWS
    ctx_chunks = ctx.shape[1] // CHUNK
    for l in range(DEPTH):
        last = l == DEPTH - 1
        mod_x = (jax.nn.silu(c) @ w_mod[l] + b_mod[l])[:, None, :]
        mod_c = jax.nn.silu(c_ctx) @ w_mod[l] + b_mod[l]
        sh_m, sc_m, gt_m, sh_f, sc_f, gt_f = jnp.split(mod_x, 6, axis=-1)
        csh_m, csc_m, cgt_m, csh_f, csc_f, cgt_f = jnp.split(mod_c, 6, axis=-1)
        a_bar, b_bar, c_mat = s5_discretise(s5_a_re[l], s5_a_im[l], s5_log_dt[l],
                                            s5_b_re[l], s5_b_im[l], s5_c_re[l], s5_c_im[l])
        mix_w = (s5_d[l], w_glu[l], b_glu[l], sgu_norm[l], sgu_w[l], sgu_b[l],
                 w_branch_s5[l], w_branch_sgu[l], w_out[l])
        moe_w = (w_router[l], router_bias[l], w_exp_gate[l], w_exp_up[l], w_exp_down[l],
                 w_sh_gate[l], w_sh_up[l], w_sh_down[l])

        hc = modulate(rmsnorm(ctx, norm_pre_mix[l]), csh_m, csc_m)
        if last:
            _, _, ctx_final = s5_scan(hc @ w_in[l][:, :D_S5], a_bar, b_bar, c_mat, None, False)
        else:
            mc, ctx_final = token_mixer(hc @ w_in[l], a_bar, b_bar, c_mat, None, ctx_chunks, *mix_w)
            ctx = ctx + cgt_m * rmsnorm(mc, norm_post_mix[l])

        hx = modulate(rmsnorm(x, norm_pre_mix[l]), sh_m, sc_m)
        mx, _ = token_mixer(hx @ w_in[l], a_bar, b_bar, c_mat, ctx_final, lat_chunks, *mix_w)
        x = x + gt_m * rmsnorm(mx, norm_post_mix[l])

        hx = modulate(rmsnorm(x, norm_pre_ffn[l]), sh_f, sc_f)
        if last:
            fx = moe_ffn(hx.reshape(-1, D_MODEL), *moe_w).reshape(x.shape)
        else:
            hc = modulate(rmsnorm(ctx, norm_pre_ffn[l]), csh_f, csc_f)
            n_lat = hx.shape[0] * hx.shape[1]
            f = moe_ffn(jnp.concatenate([hx.reshape(-1, D_MODEL), hc.reshape(-1, D_MODEL)], axis=0), *moe_w)
            fx = f[:n_lat].reshape(x.shape)
            ctx = ctx + cgt_f * rmsnorm(f[n_lat:].reshape(ctx.shape), norm_post_ffn[l])
        x = x + gt_f * rmsnorm(fx, norm_post_ffn[l])
    return x
```

```python
import functools
import math

import jax
import jax.numpy as jnp
from jax import lax
from jax.experimental import pallas as pl
from jax.experimental.pallas import tpu as pltpu

F32 = jnp.float32
BF16 = jnp.bfloat16
U32 = jnp.uint32
I32 = jnp.int32

EPS = 1e-6
S5_GROUP_CH = 16
S5_STATE = 64
S5_CHUNK = 64
SGU_GROUPS = 8
SGU_CHUNK = 128
N_EXPERT_GROUPS = 8
TOPK_GROUPS = 4
TOP_K = 8
ROUTE_SCALE = 2.5
LANES = 128
EXPERT_ROWS = 512
EXPERT_SUB = 128
VMEM_LIMIT = 56 * 1024 * 1024


def _params(sem):
    return pltpu.CompilerParams(dimension_semantics=sem, vmem_limit_bytes=VMEM_LIMIT)


def _rms(v, g):
    return v * lax.rsqrt(jnp.mean(v * v, axis=-1, keepdims=True) + EPS) * g


def _dot(a, b):
    return jnp.dot(a, b, preferred_element_type=F32)


def _pack_pair(a, b):
    lo = lax.bitcast_convert_type(a.astype(BF16).astype(F32), U32) >> 16
    hi = lax.bitcast_convert_type(b.astype(BF16).astype(F32), U32) & jnp.uint32(0xFFFF0000)
    return lo | hi


def _unpack_pair(p):
    lo = lax.bitcast_convert_type(p << 16, F32)
    hi = lax.bitcast_convert_type(p & jnp.uint32(0xFFFF0000), F32)
    return lo, hi


def _mod_kernel(c_ref, w_ref, b_ref, o_ref):
    cv = c_ref[...]
    s = cv * jax.nn.sigmoid(cv)
    o_ref[...] = _dot(s.astype(BF16), w_ref[...].astype(BF16)) + b_ref[...]


def _modulation(cpad, w_mod, b_mod):
    d, n6 = w_mod.shape
    tn = 1024
    return pl.pallas_call(
        _mod_kernel,
        out_shape=jax.ShapeDtypeStruct((cpad.shape[0], n6), F32),
        grid=(n6 // tn,),
        in_specs=[pl.BlockSpec(cpad.shape, lambda j: (0, 0)),
                  pl.BlockSpec((d, tn), lambda j: (0, j)),
                  pl.BlockSpec((1, tn), lambda j: (0, j))],
        out_specs=pl.BlockSpec((cpad.shape[0], tn), lambda j: (0, j)),
        compiler_params=_params(("parallel",)),
        name="mod",
    )(cpad, w_mod, b_mod.reshape(1, n6))


def _inproj_kernel(x_ref, g_ref, sh_ref, sc_ref, w_ref, sgun_ref,
                   u_ref, gu_ref, vn_ref, g1_ref, g2_ref, *, d_s5, d_sgu, d_model):
    h = _rms(x_ref[...], g_ref[...])
    hb = (h * (1.0 + sc_ref[0]) + sh_ref[0]).astype(BF16)
    o1, o2, o3, o4 = d_s5, d_s5 + d_sgu, d_s5 + 2 * d_sgu, d_s5 + 2 * d_sgu + d_model
    u_ref[...] = _dot(hb, w_ref[:, 0:o1]).astype(BF16)
    gu_ref[...] = jax.nn.gelu(_dot(hb, w_ref[:, o1:o2])).astype(BF16)
    v = jax.nn.gelu(_dot(hb, w_ref[:, o2:o3]))
    vn_ref[...] = _rms(v, sgun_ref[...]).astype(BF16)
    g1_ref[...] = jax.nn.sigmoid(_dot(hb, w_ref[:, o3:o4])).astype(BF16)
    g2_ref[...] = jax.nn.sigmoid(_dot(hb, w_ref[:, o4:o4 + d_model])).astype(BF16)


def _inproj(x2, norm_g, shift, scale, w_in_bf, sgu_norm, tokens_per_batch, tm, d_s5, d_sgu):
    n, d = x2.shape
    tiles_per_batch = tokens_per_batch // tm
    row = lambda i: (i, 0)
    fixed = lambda i: (0, 0)
    per_batch = lambda i: (i // tiles_per_batch, 0, 0)
    return pl.pallas_call(
        functools.partial(_inproj_kernel, d_s5=d_s5, d_sgu=d_sgu, d_model=d),
        out_shape=(jax.ShapeDtypeStruct((n, d_s5), BF16), jax.ShapeDtypeStruct((n, d_sgu), BF16),
                   jax.ShapeDtypeStruct((n, d_sgu), BF16), jax.ShapeDtypeStruct((n, d), BF16),
                   jax.ShapeDtypeStruct((n, d), BF16)),
        grid=(n // tm,),
        in_specs=[pl.BlockSpec((tm, d), row), pl.BlockSpec((1, d), fixed),
                  pl.BlockSpec((1, 1, d), per_batch), pl.BlockSpec((1, 1, d), per_batch),
                  pl.BlockSpec(w_in_bf.shape, fixed), pl.BlockSpec((1, d_sgu), fixed)],
        out_specs=(pl.BlockSpec((tm, d_s5), row), pl.BlockSpec((tm, d_sgu), row),
                   pl.BlockSpec((tm, d_sgu), row), pl.BlockSpec((tm, d), row), pl.BlockSpec((tm, d), row)),
        compiler_params=_params(("parallel",)),
        name="inproj",
    )(x2, norm_g.reshape(1, d), shift, scale, w_in_bf, sgu_norm.reshape(1, d_sgu))


def _inproj_u_kernel(x_ref, g_ref, sh_ref, sc_ref, w_ref, u_ref):
    h = _rms(x_ref[...], g_ref[...])
    hb = (h * (1.0 + sc_ref[...]) + sh_ref[...]).astype(BF16)
    u_ref[...] = _dot(hb, w_ref[...]).astype(BF16)


def _inproj_u(x2, norm_g, shift, scale, w_u_bf, tm):
    n, d = x2.shape
    d_s5 = w_u_bf.shape[1]
    row = lambda i: (i, 0)
    fixed = lambda i: (0, 0)
    return pl.pallas_call(
        _inproj_u_kernel,
        out_shape=jax.ShapeDtypeStruct((n, d_s5), BF16),
        grid=(n // tm,),
        in_specs=[pl.BlockSpec((tm, d), row), pl.BlockSpec((1, d), fixed), pl.BlockSpec((1, d), fixed),
                  pl.BlockSpec((1, d), fixed), pl.BlockSpec(w_u_bf.shape, fixed)],
        out_specs=pl.BlockSpec((tm, d_s5), row),
        compiler_params=_params(("parallel",)),
        name="inproj_ctx",
    )(x2, norm_g.reshape(1, d), shift.reshape(1, d), scale.reshape(1, d), w_u_bf)


def _s5_tables(a_re, a_im, log_dt, b_re, b_im, c_re, c_im, chunk, n_steps):
    hi = lax.Precision.HIGHEST
    g_n, p_n, j_n = a_re.shape[1], a_re.shape[2], b_re.shape[3]
    dt = jnp.exp(log_dt)[..., None]
    lam_re, lam_im = dt * a_re, dt * a_im
    ab_re, ab_im = jnp.exp(lam_re) * jnp.cos(lam_im), jnp.exp(lam_re) * jnp.sin(lam_im)
    den = a_re * a_re + a_im * a_im
    q_re = ((ab_re - 1.0) * a_re + ab_im * a_im) / den
    q_im = (ab_im * a_re - (ab_re - 1.0) * a_im) / den
    bb_re = q_re[..., None] * b_re - q_im[..., None] * b_im
    bb_im = q_re[..., None] * b_im + q_im[..., None] * b_re
    k = jnp.arange(chunk + 1, dtype=F32)[:, None, None, None]
    mag = jnp.exp(k * lam_re[None])
    pw_re, pw_im = mag * jnp.cos(k * lam_im[None]), mag * jnp.sin(k * lam_im[None])

    def lag_kernels(d):
        x_re = pw_re[:chunk, d, :, :, None] * bb_re[d][None] - pw_im[:chunk, d, :, :, None] * bb_im[d][None]
        x_im = pw_re[:chunk, d, :, :, None] * bb_im[d][None] + pw_im[:chunk, d, :, :, None] * bb_re[d][None]
        return (jnp.einsum('gjp,kgpi->gkij', c_re[d], x_re, precision=hi)
                - jnp.einsum('gjp,kgpi->gkij', c_im[d], x_im, precision=hi))

    kf, kb = lag_kernels(0), lag_kernels(1)
    t = jnp.arange(chunk)
    lag = t[None, :] - t[:, None]
    mf = jnp.where((lag >= 0)[None, :, :, None, None], kf[:, jnp.clip(lag, 0, chunk - 1)], 0.0)
    mb = jnp.where((lag <= 0)[None, :, :, None, None], kb[:, jnp.clip(-lag, 0, chunk - 1)], 0.0)
    m = (mf + mb).transpose(0, 1, 3, 2, 4).reshape(g_n, chunk * j_n, chunk * j_n)

    pf_re, pf_im = pw_re[:chunk, 0][::-1], pw_im[:chunk, 0][::-1]
    pb_re, pb_im = pw_re[:chunk, 1], pw_im[:chunk, 1]

    def state_in(p_re, p_im, d):
        e_re = p_re[..., None] * bb_re[d][None] - p_im[..., None] * bb_im[d][None]
        e_im = p_re[..., None] * bb_im[d][None] + p_im[..., None] * bb_re[d][None]
        return e_re.transpose(1, 0, 3, 2), e_im.transpose(1, 0, 3, 2)

    ef_re, ef_im = state_in(pf_re, pf_im, 0)
    eb_re, eb_im = state_in(pb_re, pb_im, 1)
    ke = jnp.concatenate([ef_re, eb_re, ef_im, eb_im], axis=-1).reshape(g_n, chunk * j_n, 4 * p_n)

    qf_re, qf_im = pw_re[1:chunk + 1, 0], pw_im[1:chunk + 1, 0]
    qb_re, qb_im = pw_re[1:chunk + 1, 1][::-1], pw_im[1:chunk + 1, 1][::-1]

    def state_out(p_re, p_im, d):
        cr, ci = c_re[d].transpose(0, 2, 1), c_im[d].transpose(0, 2, 1)
        pr, pi = p_re.transpose(1, 2, 0), p_im.transpose(1, 2, 0)
        o_re = pr[..., None] * cr[:, :, None, :] - pi[..., None] * ci[:, :, None, :]
        o_im = pr[..., None] * ci[:, :, None, :] + pi[..., None] * cr[:, :, None, :]
        return o_re, o_im

    of_re, of_im = state_out(qf_re, qf_im, 0)
    ob_re, ob_im = state_out(qb_re, qb_im, 1)
    kc = jnp.concatenate([of_re, ob_re, -of_im, -ob_im], axis=1).reshape(g_n, 4 * p_n, chunk * j_n)

    e = (chunk * (2.0 ** jnp.arange(n_steps, dtype=F32)))[:, None, None, None]
    mg = jnp.exp(e * lam_re[None])
    al_re, al_im = mg * jnp.cos(e * lam_im[None]), mg * jnp.sin(e * lam_im[None])
    alp = jnp.concatenate([al_re[:, 0], al_re[:, 1], al_im[:, 0], al_im[:, 1]], axis=-1)
    return m.astype(BF16), ke.astype(BF16), kc.astype(BF16), alp.transpose(1, 0, 2)


def _s5_kernel(*refs, nb, nc, readout):
    if readout:
        u_ref, ke_ref, alp_ref, init_ref, m_ref, kc_ref, fin_ref, y_ref = refs
    else:
        u_ref, ke_ref, alp_ref, init_ref, fin_ref = refs
    rows = nb * nc
    half = 2 * S5_STATE
    u = u_ref[0]
    e = _dot(u, ke_ref[0])
    er, ei = e[:, :half], e[:, half:]
    r_idx = lax.broadcasted_iota(I32, (rows, half), 0)
    if nc & (nc - 1) == 0:
        n_idx, b_idx = r_idx & (nc - 1), r_idx >> (nc.bit_length() - 1)
    else:
        n_idx, b_idx = lax.rem(r_idx, nc), lax.div(r_idx, nc)
    is_f = lax.broadcasted_iota(I32, (rows, half), 1) < S5_STATE
    seen = jnp.where(is_f, n_idx, nc - 1 - n_idx)
    init = init_ref[0]
    init_r = jnp.zeros((rows, half), F32)
    init_i = jnp.zeros((rows, half), F32)
    for b in range(nb):
        init_r = jnp.where(b_idx == b, init[b:b + 1, :half], init_r)
        init_i = jnp.where(b_idx == b, init[b:b + 1, half:], init_i)
    alp = alp_ref[0]
    ar, ai = alp[0:1, :half], alp[0:1, half:]
    er = er + jnp.where(seen == 0, ar * init_r - ai * init_i, 0.0)
    ei = ei + jnp.where(seen == 0, ar * init_i + ai * init_r, 0.0)

    def from_prev(v, dist):
        return jnp.where(is_f, pltpu.roll(v, dist, 0), pltpu.roll(v, rows - dist, 0))

    d, s = 1, 0
    while d < nc:
        ar, ai = alp[s:s + 1, :half], alp[s:s + 1, half:]
        sr = jnp.where(seen >= d, from_prev(er, d), 0.0)
        si = jnp.where(seen >= d, from_prev(ei, d), 0.0)
        er, ei = er + (ar * sr - ai * si), ei + (ar * si + ai * sr)
        d, s = d * 2, s + 1
    is_f_row = lax.broadcasted_iota(I32, (1, half), 1) < S5_STATE
    for b in range(nb):
        lo, hi = b * nc, b * nc + nc - 1
        fin_ref[0, b:b + 1, :half] = jnp.where(is_f_row, er[hi:hi + 1], er[lo:lo + 1])
        fin_ref[0, b:b + 1, half:] = jnp.where(is_f_row, ei[hi:hi + 1], ei[lo:lo + 1])
    if readout:
        if nc > 1:
            sin_r = jnp.where(seen >= 1, from_prev(er, 1), init_r)
            sin_i = jnp.where(seen >= 1, from_prev(ei, 1), init_i)
        else:
            sin_r, sin_i = init_r, init_i
        s_in = jnp.concatenate([sin_r, sin_i], axis=1).astype(BF16)
        y_ref[0] = (_dot(u, m_ref[0]) + _dot(s_in, kc_ref[0])).astype(BF16)


def _s5(ug, ke, alp, init, m=None, kc=None, *, nb, nc):
    g_n, rows, width = ug.shape
    readout = m is not None
    grp = lambda g: (g, 0, 0)
    in_specs = [pl.BlockSpec((1, rows, width), grp), pl.BlockSpec((1,) + ke.shape[1:], grp),
                pl.BlockSpec((1,) + alp.shape[1:], grp), pl.BlockSpec((1,) + init.shape[1:], grp)]
    out_shape = [jax.ShapeDtypeStruct(init.shape, F32)]
    out_specs = [pl.BlockSpec((1,) + init.shape[1:], grp)]
    args = [ug, ke, alp, init]
    if readout:
        in_specs += [pl.BlockSpec((1,) + m.shape[1:], grp), pl.BlockSpec((1,) + kc.shape[1:], grp)]
        out_shape.append(jax.ShapeDtypeStruct(ug.shape, BF16))
        out_specs.append(pl.BlockSpec((1, rows, width), grp))
        args += [m, kc]
    return pl.pallas_call(
        functools.partial(_s5_kernel, nb=nb, nc=nc, readout=readout),
        out_shape=tuple(out_shape), grid=(g_n,), in_specs=in_specs, out_specs=tuple(out_specs),
        compiler_params=_params(("parallel",)),
        name="s5_readout" if readout else "s5_state",
    )(*args)


def _to_groups(u, nb, nc, chunk, g_n):
    j_n = u.shape[1] // g_n
    return u.reshape(nb * nc, chunk, g_n, j_n).transpose(2, 0, 1, 3).reshape(g_n, nb * nc, chunk * j_n)


def _from_groups(y, nb, nc, chunk, g_n):
    j_n = y.shape[2] // chunk
    return y.reshape(g_n, nb * nc, chunk, j_n).transpose(1, 2, 0, 3).reshape(nb * nc * chunk, g_n * j_n)


def _mix_kernel(y_ref, u_ref, gu_ref, vn_ref, g1_ref, g2_ref, x_ref,
                d_ref, wglu_ref, bglu_ref, sguw_ref, sgub_ref, wb1_ref, wb2_ref, wout_ref,
                npost_ref, gt_ref, npre_ref, shf_ref, scf_ref, wr_ref,
                x1_ref, hp_ref, lg_ref, *, tm):
    y = jax.nn.gelu(y_ref[...].astype(F32) + d_ref[...] * u_ref[...].astype(F32))
    y_s5 = y * jax.nn.sigmoid(_dot(y.astype(BF16), wglu_ref[...]) + bglu_ref[...])

    lane = lax.broadcasted_iota(I32, (1, LANES), 1)
    m_lo = (lane < LANES // 2).astype(F32).astype(BF16)
    m_hi = (lane >= LANES // 2).astype(F32).astype(BF16)
    vn = vn_ref[...]
    chunks = []
    for c in range(tm // SGU_CHUNK):
        vc = vn[c * SGU_CHUNK:(c + 1) * SGU_CHUNK, :]
        tiles = []
        for q in range(vc.shape[1] // LANES):
            vt = vc[:, q * LANES:(q + 1) * LANES]
            rhs = jnp.concatenate([vt * m_lo, vt * m_hi], axis=0)
            tiles.append(_dot(sguw_ref[q], rhs))
        chunks.append(jnp.concatenate(tiles, axis=1) + sgub_ref[...])
    mixed = jnp.concatenate(chunks, axis=0)
    y_sgu = gu_ref[...].astype(F32) * mixed

    merged = (g1_ref[...].astype(F32) * _dot(y_s5.astype(BF16), wb1_ref[...])
              + g2_ref[...].astype(F32) * _dot(y_sgu.astype(BF16), wb2_ref[...]))
    mx = _dot(merged.astype(BF16), wout_ref[...])
    x1 = x_ref[...] + gt_ref[0] * _rms(mx, npost_ref[...])
    x1_ref[...] = x1
    hp = _rms(x1, npre_ref[...]) * (1.0 + scf_ref[0]) + shf_ref[0]
    half = hp.shape[1] // 2
    hp_ref[...] = _pack_pair(hp[:, :half], hp[:, half:])
    lg_ref[...] = _dot(hp.astype(BF16), wr_ref[...])


def _mix(y, u, gu, vn, g1, g2, x2, s5_d, wglu, bglu, sguw, sgub, wb1, wb2, wout,
         npost, gt, npre, shf, scf, wr, tokens_per_batch, tm):
    n, d = x2.shape
    d_s5, d_sgu, n_exp = y.shape[1], gu.shape[1], wr.shape[1]
    tiles_per_batch = tokens_per_batch // tm
    row = lambda i: (i, 0)
    fixed = lambda i: (0, 0)
    fixed3 = lambda i: (0, 0, 0)
    per_batch = lambda i: (i // tiles_per_batch, 0, 0)
    vec = lambda w: pl.BlockSpec((1, w), fixed)
    return pl.pallas_call(
        functools.partial(_mix_kernel, tm=tm),
        out_shape=(jax.ShapeDtypeStruct((n, d), F32), jax.ShapeDtypeStruct((n, d // 2), U32),
                   jax.ShapeDtypeStruct((n, n_exp), F32)),
        grid=(n // tm,),
        in_specs=[pl.BlockSpec((tm, d_s5), row), pl.BlockSpec((tm, d_s5), row), pl.BlockSpec((tm, d_sgu), row),
                  pl.BlockSpec((tm, d_sgu), row), pl.BlockSpec((tm, d), row), pl.BlockSpec((tm, d), row),
                  pl.BlockSpec((tm, d), row),
                  vec(d_s5), pl.BlockSpec(wglu.shape, fixed), vec(d_s5),
                  pl.BlockSpec(sguw.shape, fixed3), pl.BlockSpec(sgub.shape, fixed),
                  pl.BlockSpec(wb1.shape, fixed), pl.BlockSpec(wb2.shape, fixed), pl.BlockSpec(wout.shape, fixed),
                  vec(d), pl.BlockSpec((1, 1, d), per_batch), vec(d),
                  pl.BlockSpec((1, 1, d), per_batch), pl.BlockSpec((1, 1, d), per_batch),
                  pl.BlockSpec(wr.shape, fixed)],
        out_specs=(pl.BlockSpec((tm, d), row), pl.BlockSpec((tm, d // 2), row), pl.BlockSpec((tm, n_exp), row)),
        compiler_params=_params(("parallel",)),
        name="mix",
    )(y, u, gu, vn, g1, g2, x2, s5_d.reshape(1, d_s5), wglu, bglu.reshape(1, d_s5), sguw, sgub,
      wb1, wb2, wout, npost.reshape(1, d), gt, npre.reshape(1, d), shf, scf, wr)


def _route_kernel(lg_ref, bias_ref, idx_ref, w_ref, cnt_ref, *, n_exp, tn):
    scores = jax.nn.sigmoid(lg_ref[...])
    sel = scores + bias_ref[...]
    per_group = n_exp // N_EXPERT_GROUPS
    neg = jnp.float32(-jnp.inf)
    gs = []
    for g in range(N_EXPERT_GROUPS):
        sg = sel[g * per_group:(g + 1) * per_group, :]
        m1 = jnp.max(sg, axis=0, keepdims=True)
        is_m1 = sg == m1
        n_m1 = jnp.sum(is_m1.astype(F32), axis=0, keepdims=True)
        rest = jnp.max(jnp.where(is_m1, neg, sg), axis=0, keepdims=True)
        gs.append(m1 + jnp.where(n_m1 >= 2.0, m1, rest))
    gsm = jnp.concatenate(gs, axis=0)
    g_iota = lax.broadcasted_iota(I32, gsm.shape, 0)
    e_iota = lax.broadcasted_iota(I32, sel.shape, 0).astype(F32)
    masked = []
    for g in range(N_EXPERT_GROUPS):
        mine = gsm[g:g + 1, :]
        beats = jnp.where(gsm > mine, 1.0, jnp.where(gsm == mine, jnp.where(g_iota < g, 1.0, 0.0), 0.0))
        n_beats = jnp.sum(beats, axis=0, keepdims=True)
        masked.append(jnp.where(n_beats < float(TOPK_GROUPS), sel[g * per_group:(g + 1) * per_group, :], neg))
    selm = jnp.concatenate(masked, axis=0)
    picked, vals = [], []
    hot = jnp.zeros(sel.shape, F32)
    for _ in range(TOP_K):
        m = jnp.max(selm, axis=0, keepdims=True)
        first = jnp.min(jnp.where(selm == m, e_iota, float(n_exp)), axis=0, keepdims=True)
        one = e_iota == first
        picked.append(first)
        vals.append(jnp.sum(jnp.where(one, scores, 0.0), axis=0, keepdims=True))
        hot = hot + jnp.where(one, 1.0, 0.0)
        selm = jnp.where(one, neg, selm)
    idx_ref[...] = jnp.concatenate(picked, axis=0).astype(I32)
    wv = jnp.concatenate(vals, axis=0)
    w_ref[...] = wv / jnp.sum(wv, axis=0, keepdims=True) * ROUTE_SCALE

    @pl.when(pl.program_id(0) == 0)
    def _():
        cnt_ref[...] = jnp.zeros_like(cnt_ref)

    cnt_ref[...] += jnp.sum(hot, axis=1, keepdims=True)


def _route(logits_t, bias, tn):
    n_exp, n = logits_t.shape
    col = lambda i: (0, i)
    fixed = lambda i: (0, 0)
    return pl.pallas_call(
        functools.partial(_route_kernel, n_exp=n_exp, tn=tn),
        out_shape=(jax.ShapeDtypeStruct((TOP_K, n), I32), jax.ShapeDtypeStruct((TOP_K, n), F32),
                   jax.ShapeDtypeStruct((n_exp, 1), F32)),
        grid=(n // tn,),
        in_specs=[pl.BlockSpec((n_exp, tn), col), pl.BlockSpec((n_exp, 1), fixed)],
        out_specs=(pl.BlockSpec((TOP_K, tn), col), pl.BlockSpec((TOP_K, tn), col),
                   pl.BlockSpec((n_exp, 1), fixed)),
        compiler_params=_params(("arbitrary",)),
        name="route",
    )(logits_t, bias.reshape(n_exp, 1))


def _dest_kernel(idx_ref, start_ref, dest_ref, carry_ref, *, n_exp, tn):
    @pl.when(pl.program_id(0) == 0)
    def _():
        carry_ref[...] = start_ref[...]

    idx = idx_ref[...]
    e_iota = lax.broadcasted_iota(I32, (n_exp, tn), 0)
    hot = jnp.zeros((n_exp, tn), F32)
    for k in range(TOP_K):
        hot = hot + jnp.where(e_iota == idx[k:k + 1, :], 1.0, 0.0)
    before = jnp.where(lax.broadcasted_iota(I32, (tn, tn), 0) < lax.broadcasted_iota(I32, (tn, tn), 1), 1.0, 0.0)
    rank = _dot(hot.astype(BF16), before.astype(BF16)) + carry_ref[...]
    dest_ref[...] = jnp.concatenate(
        [jnp.sum(jnp.where(e_iota == idx[k:k + 1, :], rank, 0.0), axis=0, keepdims=True)
         for k in range(TOP_K)], axis=0).astype(I32)
    carry_ref[...] += jnp.sum(hot, axis=1, keepdims=True)


def _dest(idx, start, tn):
    n = idx.shape[1]
    n_exp = start.shape[0]
    col = lambda i: (0, i)
    return pl.pallas_call(
        functools.partial(_dest_kernel, n_exp=n_exp, tn=tn),
        out_shape=jax.ShapeDtypeStruct((TOP_K, n), I32),
        grid=(n // tn,),
        in_specs=[pl.BlockSpec((TOP_K, tn), col), pl.BlockSpec((n_exp, 1), lambda i: (0, 0))],
        out_specs=pl.BlockSpec((TOP_K, tn), col),
        scratch_shapes=[pltpu.VMEM((n_exp, 1), F32)],
        compiler_params=_params(("arbitrary",)),
        name="dest",
    )(idx, start)


def _dispatch_kernel(dest_ref, hp_ref, xs_ref, sem, *, tm):
    def row_copy(r, k):
        return pltpu.make_async_copy(hp_ref.at[pl.ds(r, 1), :], xs_ref.at[pl.ds(dest_ref[k, r], 1), :], sem)

    def start(r, c):
        for k in range(TOP_K):
            row_copy(r, k).start()
        return c

    def wait(r, c):
        for k in range(TOP_K):
            row_copy(r, k).wait()
        return c

    lax.fori_loop(0, tm, start, 0)
    lax.fori_loop(0, tm, wait, 0)


def _dispatch(dest, hp, cap, tm):
    n, w = hp.shape
    return pl.pallas_call(
        functools.partial(_dispatch_kernel, tm=tm),
        out_shape=jax.ShapeDtypeStruct((cap, w), U32),
        grid=(n // tm,),
        in_specs=[pl.BlockSpec((TOP_K, tm), lambda i: (0, i), memory_space=pltpu.SMEM),
                  pl.BlockSpec((tm, w), lambda i: (i, 0))],
        out_specs=pl.BlockSpec(memory_space=pl.ANY),
        scratch_shapes=[pltpu.SemaphoreType.DMA(())],
        compiler_params=_params(("arbitrary",)),
        name="dispatch",
    )(dest, hp)


def _expert_kernel(be_ref, bv_ref, xs_ref, wg_ref, wu_ref, wd_ref, ys_ref, wgu_s, wd_s, *, rows):
    b = pl.program_id(0)
    valid = bv_ref[b]
    changed = jnp.logical_or(b == 0, be_ref[b] != be_ref[jnp.maximum(b - 1, 0)])
    d_e = wg_ref.shape[2]

    @pl.when(jnp.logical_and(changed, valid > 0))
    def _():
        wgu_s[:, :d_e] = wg_ref[0].astype(BF16)
        wgu_s[:, d_e:] = wu_ref[0].astype(BF16)
        wd_s[...] = wd_ref[0].astype(BF16)

    for s in range(rows // EXPERT_SUB):
        @pl.when(valid > s * EXPERT_SUB)
        def _():
            lo, hi = _unpack_pair(xs_ref[s * EXPERT_SUB:(s + 1) * EXPERT_SUB, :])
            r_iota = lax.broadcasted_iota(I32, lo.shape, 0) + s * EXPERT_SUB
            live = r_iota < valid
            xb = jnp.concatenate([jnp.where(live, lo, 0.0), jnp.where(live, hi, 0.0)], axis=1).astype(BF16)
            gu = _dot(xb, wgu_s[...])
            gate, up = gu[:, :d_e], gu[:, d_e:]
            hid = (gate * jax.nn.sigmoid(gate) * up).astype(BF16)
            y = _dot(hid, wd_s[...])
            half = y.shape[1] // 2
            ys_ref[s * EXPERT_SUB:(s + 1) * EXPERT_SUB, :] = _pack_pair(y[:, :half], y[:, half:])


def _experts(blk_expert, blk_valid, xs, w_gate, w_up, w_down, rows):
    cap, w = xs.shape
    _, d, d_e = w_gate.shape
    grid_spec = pltpu.PrefetchScalarGridSpec(
        num_scalar_prefetch=2,
        grid=(cap // rows,),
        in_specs=[pl.BlockSpec((rows, w), lambda b, be, bv: (b, 0)),
                  pl.BlockSpec((1, d, d_e), lambda b, be, bv: (be[b], 0, 0)),
                  pl.BlockSpec((1, d, d_e), lambda b, be, bv: (be[b], 0, 0)),
                  pl.BlockSpec((1, d_e, d), lambda b, be, bv: (be[b], 0, 0))],
        out_specs=pl.BlockSpec((rows, w), lambda b, be, bv: (b, 0)),
        scratch_shapes=[pltpu.VMEM((d, 2 * d_e), BF16), pltpu.VMEM((d_e, d), BF16)],
    )
    return pl.pallas_call(
        functools.partial(_expert_kernel, rows=rows),
        out_shape=jax.ShapeDtypeStruct((cap, w), U32),
        grid_spec=grid_spec,
        compiler_params=_params(("arbitrary",)),
        name="experts",
    )(blk_expert, blk_valid, xs, w_gate, w_up, w_down)


def _combine_kernel(dest_ref, ys_ref, wt_ref, hp_ref, x1_ref, wsgu_ref, wsd_ref, npost_ref, gt_ref,
                    o_ref, buf, sem, *, tm):
    def row_copy(r, k):
        return pltpu.make_async_copy(ys_ref.at[pl.ds(dest_ref[k, r], 1), :], buf.at[k, pl.ds(r, 1), :], sem)

    def start(r, c):
        for k in range(TOP_K):
            row_copy(r, k).start()
        return c

    def wait(r, c):
        for k in range(TOP_K):
            row_copy(r, k).wait()
        return c

    lax.fori_loop(0, tm, start, 0)

    lo, hi = _unpack_pair(hp_ref[...])
    hb = jnp.concatenate([lo, hi], axis=1).astype(BF16)
    d_sh = wsd_ref.shape[0]
    gu = _dot(hb, wsgu_ref[...])
    gate, up = gu[:, :d_sh], gu[:, d_sh:]
    shared = _dot((gate * jax.nn.sigmoid(gate) * up).astype(BF16), wsd_ref[...])

    lax.fori_loop(0, tm, wait, 0)
    wt = wt_ref[...]
    acc_lo = jnp.zeros(lo.shape, F32)
    acc_hi = jnp.zeros(lo.shape, F32)
    for k in range(TOP_K):
        y_lo, y_hi = _unpack_pair(buf[k])
        acc_lo = acc_lo + wt[:, k:k + 1] * y_lo
        acc_hi = acc_hi + wt[:, k:k + 1] * y_hi
    fx = jnp.concatenate([acc_lo, acc_hi], axis=1) + shared
    o_ref[...] = x1_ref[...] + gt_ref[0] * _rms(fx, npost_ref[...])


def _combine(dest, ys, wt, hp, x1, wsgu, wsd, npost, gt, tokens_per_batch, tm):
    n, d = x1.shape
    w = hp.shape[1]
    tiles_per_batch = tokens_per_batch // tm
    row = lambda i: (i, 0)
    fixed = lambda i: (0, 0)
    return pl.pallas_call(
        functools.partial(_combine_kernel, tm=tm),
        out_shape=jax.ShapeDtypeStruct((n, d), F32),
        grid=(n // tm,),
        in_specs=[pl.BlockSpec((TOP_K, tm), lambda i: (0, i), memory_space=pltpu.SMEM),
                  pl.BlockSpec(memory_space=pl.ANY),
                  pl.BlockSpec((tm, TOP_K), row), pl.BlockSpec((tm, w), row), pl.BlockSpec((tm, d), row),
                  pl.BlockSpec(wsgu.shape, fixed), pl.BlockSpec(wsd.shape, fixed),
                  pl.BlockSpec((1, d), fixed), pl.BlockSpec((1, 1, d), lambda i: (i // tiles_per_batch, 0, 0))],
        out_specs=pl.BlockSpec((tm, d), row),
        scratch_shapes=[pltpu.VMEM((TOP_K, tm, w), U32), pltpu.SemaphoreType.DMA(())],
        compiler_params=_params(("arbitrary",)),
        name="combine",
    )(dest, ys, wt, hp, x1, wsgu, wsd, npost.reshape(1, d), gt)


def _layer(x, ctx, mod_x, mod_c, norm_pre_mix, norm_post_mix, norm_pre_ffn, norm_post_ffn,
           w_in, s5_a_re, s5_a_im, s5_log_dt, s5_b_re, s5_b_im, s5_c_re, s5_c_im, s5_d, w_glu, b_glu,
           sgu_norm, sgu_w, sgu_b, w_branch_s5, w_branch_sgu, w_out,
           w_router, router_bias, w_exp_gate, w_exp_up, w_exp_down, w_sh_gate, w_sh_up, w_sh_down):
    bsz, t_len, d = x.shape
    c_len = ctx.shape[1]
    n = bsz * t_len
    d_s5, d_sgu = w_glu.shape[0], sgu_norm.shape[0]
    g_n = d_s5 // S5_GROUP_CH
    n_exp = w_router.shape[1]
    chunk = S5_CHUNK
    tm = min(512, t_len)
    assert t_len % tm == 0 and tm % SGU_CHUNK == 0 and t_len % chunk == 0 and c_len % chunk == 0

    sh_m, sc_m, gt_m, sh_f, sc_f, gt_f = [v.reshape(bsz, 1, d) for v in jnp.split(mod_x, 6, axis=-1)]
    csh_m, csc_m = mod_c[:d], mod_c[d:2 * d]

    w_in_bf = w_in.astype(BF16)

    nc_lat, nc_ctx = t_len // chunk, c_len // chunk
    n_steps = max(1, (max(nc_lat, nc_ctx) - 1).bit_length())
    m_t, ke_t, kc_t, alp = _s5_tables(s5_a_re, s5_a_im, s5_log_dt, s5_b_re, s5_b_im, s5_c_re, s5_c_im,
                                      chunk, n_steps)

    ctx2 = ctx.reshape(bsz * c_len, d)
    u_ctx = _inproj_u(ctx2, norm_pre_mix, csh_m, csc_m, w_in_bf[:, :d_s5], min(512, bsz * c_len))
    zero_init = jnp.zeros((g_n, bsz, 4 * S5_STATE), F32)
    (ctx_final,) = _s5(_to_groups(u_ctx, bsz, nc_ctx, chunk, g_n), ke_t, alp, zero_init, nb=bsz, nc=nc_ctx)

    x2 = x.reshape(n, d)
    u, gu, vn, g1, g2 = _inproj(x2, norm_pre_mix, sh_m, sc_m, w_in_bf, sgu_norm, t_len, tm, d_s5, d_sgu)
    _, yg = _s5(_to_groups(u, bsz, nc_lat, chunk, g_n), ke_t, alp, ctx_final, m_t, kc_t, nb=bsz, nc=nc_lat)
    y = _from_groups(yg, bsz, nc_lat, chunk, g_n)

    ch = d_sgu // SGU_GROUPS
    per_tile = LANES // ch
    sguw = sgu_w.reshape(SGU_GROUPS // per_tile, per_tile, SGU_CHUNK, SGU_CHUNK)
    sguw = sguw.transpose(0, 2, 1, 3).reshape(SGU_GROUPS // per_tile, SGU_CHUNK, per_tile * SGU_CHUNK).astype(BF16)
    sgub = jnp.repeat(sgu_b.T, ch, axis=1)

    x1, hp, logits = _mix(y, u, gu, vn, g1, g2, x2, s5_d, w_glu.astype(BF16), b_glu, sguw, sgub,
                          w_branch_s5.astype(BF16), w_branch_sgu.astype(BF16), w_out.astype(BF16),
                          norm_post_mix, gt_m, norm_pre_ffn, sh_f, sc_f, w_router.astype(BF16), t_len, tm)

    tn = min(512, n)
    idx, wts, counts = _route(logits.T, router_bias, tn)
    rows = EXPERT_ROWS
    cnt = counts.reshape(n_exp).astype(I32)
    nblk = (cnt + rows - 1) // rows
    blk_end = jnp.cumsum(nblk)
    blk_start = blk_end - nblk
    n_blocks = (n * TOP_K) // rows + n_exp
    cap = n_blocks * rows
    b_ids = jnp.arange(n_blocks, dtype=I32)
    blk_expert = jnp.minimum(jnp.searchsorted(blk_end, b_ids, side='right'), n_exp - 1).astype(I32)
    blk_valid = jnp.clip(cnt[blk_expert] - (b_ids - blk_start[blk_expert]) * rows, 0, rows).astype(I32)
    dest = _dest(idx, (blk_start * rows).astype(F32).reshape(n_exp, 1), tn)

    td = min(256, n)
    xs = _dispatch(dest, hp, cap, td)
    ys = _experts(blk_expert, blk_valid, xs, w_exp_gate, w_exp_up, w_exp_down, rows)
    wsgu = jnp.concatenate([w_sh_gate, w_sh_up], axis=1).astype(BF16)
    out = _combine(dest, ys, wts.T, hp, x1, wsgu, w_sh_down.astype(BF16), norm_post_ffn, gt_f, t_len, td)
    return out.reshape(bsz, t_len, d)


def kernel(x, c, ctx, c_ctx, w_mod, b_mod, norm_pre_mix, norm_post_mix, norm_pre_ffn, norm_post_ffn, w_in, s5_a_re, s5_a_im, s5_log_dt, s5_b_re, s5_b_im, s5_c_re, s5_c_im, s5_d, w_glu, b_glu, sgu_norm, sgu_w, sgu_b, w_branch_s5, w_branch_sgu, w_out, w_router, router_bias, w_exp_gate, w_exp_up, w_exp_down, w_sh_gate, w_sh_up, w_sh_down):
    depth = w_mod.shape[0]
    assert depth == 1, "the context stream is only carried through the last layer's S5 states"
    bsz = x.shape[0]
    pad = (-(bsz + 1)) % 8
    cpad = jnp.concatenate([c, c_ctx[None, :], jnp.zeros((pad, c.shape[1]), c.dtype)], axis=0)
    mod = _modulation(cpad, w_mod[0], b_mod[0])
    return _layer(x, ctx, mod[:bsz], mod[bsz], norm_pre_mix[0], norm_post_mix[0], norm_pre_ffn[0],
                  norm_post_ffn[0], w_in[0], s5_a_re[0], s5_a_im[0], s5_log_dt[0], s5_b_re[0], s5_b_im[0],
                  s5_c_re[0], s5_c_im[0], s5_d[0], w_glu[0], b_glu[0], sgu_norm[0], sgu_w[0], sgu_b[0],
                  w_branch_s5[0], w_branch_sgu[0], w_out[0], w_router[0], router_bias[0],
                  w_exp_gate[0], w_exp_up[0], w_exp_down[0], w_sh_gate[0], w_sh_up[0], w_sh_down[0])
```

```python
import functools
import math

import jax
import jax.numpy as jnp
from jax import lax
from jax.experimental import pallas as pl
from jax.experimental.pallas import tpu as pltpu

F32 = jnp.float32
BF16 = jnp.bfloat16
U32 = jnp.uint32
I32 = jnp.int32

EPS = 1e-6
S5_GROUP_CH = 16
S5_STATE = 64
S5_CHUNK = 64
SGU_GROUPS = 8
SGU_CHUNK = 128
N_EXPERT_GROUPS = 8
TOPK_GROUPS = 4
TOP_K = 8
ROUTE_SCALE = 2.5
LANES = 128
EXPERT_ROWS = 512
EXPERT_SUB = 256
VMEM_LIMIT = 56 * 1024 * 1024


def _params(sem):
    return pltpu.CompilerParams(dimension_semantics=sem, vmem_limit_bytes=VMEM_LIMIT)


def _rms(v, g):
    return v * lax.rsqrt(jnp.mean(v * v, axis=-1, keepdims=True) + EPS) * g


def _dot(a, b):
    return jnp.dot(a, b, preferred_element_type=F32)


def _pack_pair(a, b):
    lo = lax.bitcast_convert_type(a.astype(BF16).astype(F32), U32) >> 16
    hi = lax.bitcast_convert_type(b.astype(BF16).astype(F32), U32) & jnp.uint32(0xFFFF0000)
    return lo | hi


def _unpack_pair(p):
    lo = lax.bitcast_convert_type(p << 16, F32)
    hi = lax.bitcast_convert_type(p & jnp.uint32(0xFFFF0000), F32)
    return lo, hi


def _mod_kernel(c_ref, w_ref, b_ref, o_ref):
    cv = c_ref[...]
    s = cv * jax.nn.sigmoid(cv)
    o_ref[...] = _dot(s.astype(BF16), w_ref[...].astype(BF16)) + b_ref[...]


def _modulation(cpad, w_mod, b_mod):
    d, n6 = w_mod.shape
    tn = 1024
    return pl.pallas_call(
        _mod_kernel,
        out_shape=jax.ShapeDtypeStruct((cpad.shape[0], n6), F32),
        grid=(n6 // tn,),
        in_specs=[pl.BlockSpec(cpad.shape, lambda j: (0, 0)),
                  pl.BlockSpec((d, tn), lambda j: (0, j)),
                  pl.BlockSpec((1, tn), lambda j: (0, j))],
        out_specs=pl.BlockSpec((cpad.shape[0], tn), lambda j: (0, j)),
        compiler_params=_params(("parallel",)),
        name="mod",
    )(cpad, w_mod, b_mod.reshape(1, n6))


def _inproj_kernel(x_ref, g_ref, sh_ref, sc_ref, w_ref, sgun_ref,
                   u_ref, gu_ref, vn_ref, g1_ref, g2_ref, *, d_s5, d_sgu, d_model):
    h = _rms(x_ref[...], g_ref[...])
    hb = (h * (1.0 + sc_ref[0]) + sh_ref[0]).astype(BF16)
    o1, o2, o3, o4 = d_s5, d_s5 + d_sgu, d_s5 + 2 * d_sgu, d_s5 + 2 * d_sgu + d_model
    u_ref[...] = _dot(hb, w_ref[:, 0:o1]).astype(BF16)
    gu_ref[...] = jax.nn.gelu(_dot(hb, w_ref[:, o1:o2])).astype(BF16)
    v = jax.nn.gelu(_dot(hb, w_ref[:, o2:o3]))
    vn_ref[...] = _rms(v, sgun_ref[...]).astype(BF16)
    g1_ref[...] = jax.nn.sigmoid(_dot(hb, w_ref[:, o3:o4])).astype(BF16)
    g2_ref[...] = jax.nn.sigmoid(_dot(hb, w_ref[:, o4:o4 + d_model])).astype(BF16)


def _inproj(x2, norm_g, shift, scale, w_in_bf, sgu_norm, tokens_per_batch, tm, d_s5, d_sgu):
    n, d = x2.shape
    tiles_per_batch = tokens_per_batch // tm
    row = lambda i: (i, 0)
    fixed = lambda i: (0, 0)
    per_batch = lambda i: (i // tiles_per_batch, 0, 0)
    return pl.pallas_call(
        functools.partial(_inproj_kernel, d_s5=d_s5, d_sgu=d_sgu, d_model=d),
        out_shape=(jax.ShapeDtypeStruct((n, d_s5), BF16), jax.ShapeDtypeStruct((n, d_sgu), BF16),
                   jax.ShapeDtypeStruct((n, d_sgu), BF16), jax.ShapeDtypeStruct((n, d), BF16),
                   jax.ShapeDtypeStruct((n, d), BF16)),
        grid=(n // tm,),
        in_specs=[pl.BlockSpec((tm, d), row), pl.BlockSpec((1, d), fixed),
                  pl.BlockSpec((1, 1, d), per_batch), pl.BlockSpec((1, 1, d), per_batch),
                  pl.BlockSpec(w_in_bf.shape, fixed), pl.BlockSpec((1, d_sgu), fixed)],
        out_specs=(pl.BlockSpec((tm, d_s5), row), pl.BlockSpec((tm, d_sgu), row),
                   pl.BlockSpec((tm, d_sgu), row), pl.BlockSpec((tm, d), row), pl.BlockSpec((tm, d), row)),
        compiler_params=_params(("parallel",)),
        name="inproj",
    )(x2, norm_g.reshape(1, d), shift, scale, w_in_bf, sgu_norm.reshape(1, d_sgu))


def _inproj_u_kernel(x_ref, g_ref, sh_ref, sc_ref, w_ref, u_ref):
    h = _rms(x_ref[...], g_ref[...])
    hb = (h * (1.0 + sc_ref[...]) + sh_ref[...]).astype(BF16)
    u_ref[...] = _dot(hb, w_ref[...]).astype(BF16)


def _inproj_u(x2, norm_g, shift, scale, w_u_bf, tm):
    n, d = x2.shape
    d_s5 = w_u_bf.shape[1]
    row = lambda i: (i, 0)
    fixed = lambda i: (0, 0)
    return pl.pallas_call(
        _inproj_u_kernel,
        out_shape=jax.ShapeDtypeStruct((n, d_s5), BF16),
        grid=(n // tm,),
        in_specs=[pl.BlockSpec((tm, d), row), pl.BlockSpec((1, d), fixed), pl.BlockSpec((1, d), fixed),
                  pl.BlockSpec((1, d), fixed), pl.BlockSpec(w_u_bf.shape, fixed)],
        out_specs=pl.BlockSpec((tm, d_s5), row),
        compiler_params=_params(("parallel",)),
        name="inproj_ctx",
    )(x2, norm_g.reshape(1, d), shift.reshape(1, d), scale.reshape(1, d), w_u_bf)


def _s5_tables(a_re, a_im, log_dt, b_re, b_im, c_re, c_im, chunk, n_steps):
    hi = lax.Precision.HIGHEST
    g_n, p_n, j_n = a_re.shape[1], a_re.shape[2], b_re.shape[3]
    dt = jnp.exp(log_dt)[..., None]
    lam_re, lam_im = dt * a_re, dt * a_im
    ab_re, ab_im = jnp.exp(lam_re) * jnp.cos(lam_im), jnp.exp(lam_re) * jnp.sin(lam_im)
    den = a_re * a_re + a_im * a_im
    q_re = ((ab_re - 1.0) * a_re + ab_im * a_im) / den
    q_im = (ab_im * a_re - (ab_re - 1.0) * a_im) / den
    bb_re = q_re[..., None] * b_re - q_im[..., None] * b_im
    bb_im = q_re[..., None] * b_im + q_im[..., None] * b_re
    k = jnp.arange(chunk + 1, dtype=F32)[:, None, None, None]
    mag = jnp.exp(k * lam_re[None])
    pw_re, pw_im = mag * jnp.cos(k * lam_im[None]), mag * jnp.sin(k * lam_im[None])

    def lag_kernels(d):
        x_re = pw_re[:chunk, d, :, :, None] * bb_re[d][None] - pw_im[:chunk, d, :, :, None] * bb_im[d][None]
        x_im = pw_re[:chunk, d, :, :, None] * bb_im[d][None] + pw_im[:chunk, d, :, :, None] * bb_re[d][None]
        return (jnp.einsum('gjp,kgpi->gkij', c_re[d], x_re, precision=hi)
                - jnp.einsum('gjp,kgpi->gkij', c_im[d], x_im, precision=hi))

    kf, kb = lag_kernels(0), lag_kernels(1)
    lags = jnp.concatenate([kb[:, 1:][:, ::-1], kf[:, 0:1] + kb[:, 0:1], kf[:, 1:],
                            jnp.zeros_like(kf[:, 0:1])], axis=1)
    lags = lags.transpose(0, 2, 3, 1)

    pf_re, pf_im = pw_re[:chunk, 0][::-1], pw_im[:chunk, 0][::-1]
    pb_re, pb_im = pw_re[:chunk, 1], pw_im[:chunk, 1]

    def state_in(p_re, p_im, d):
        e_re = p_re[..., None] * bb_re[d][None] - p_im[..., None] * bb_im[d][None]
        e_im = p_re[..., None] * bb_im[d][None] + p_im[..., None] * bb_re[d][None]
        return e_re.transpose(1, 3, 0, 2), e_im.transpose(1, 3, 0, 2)

    ef_re, ef_im = state_in(pf_re, pf_im, 0)
    eb_re, eb_im = state_in(pb_re, pb_im, 1)
    ke = jnp.concatenate([ef_re, eb_re, ef_im, eb_im], axis=-1).reshape(g_n, chunk * j_n, 4 * p_n)

    qf_re, qf_im = pw_re[1:chunk + 1, 0], pw_im[1:chunk + 1, 0]
    qb_re, qb_im = pw_re[1:chunk + 1, 1][::-1], pw_im[1:chunk + 1, 1][::-1]

    def state_out(p_re, p_im, d):
        cr, ci = c_re[d].transpose(0, 2, 1), c_im[d].transpose(0, 2, 1)
        pr, pi = p_re.transpose(1, 2, 0), p_im.transpose(1, 2, 0)
        o_re = pr[:, :, None, :] * cr[..., None] - pi[:, :, None, :] * ci[..., None]
        o_im = pr[:, :, None, :] * ci[..., None] + pi[:, :, None, :] * cr[..., None]
        return o_re, o_im

    of_re, of_im = state_out(qf_re, qf_im, 0)
    ob_re, ob_im = state_out(qb_re, qb_im, 1)
    kc = jnp.concatenate([of_re, ob_re, -of_im, -ob_im], axis=1).reshape(g_n, 4 * p_n, chunk * j_n)

    e = (chunk * (2.0 ** jnp.arange(n_steps, dtype=F32)))[:, None, None, None]
    mg = jnp.exp(e * lam_re[None])
    al_re, al_im = mg * jnp.cos(e * lam_im[None]), mg * jnp.sin(e * lam_im[None])
    alp = jnp.concatenate([al_re[:, 0], al_re[:, 1], al_im[:, 0], al_im[:, 1]], axis=-1)
    return lags, ke.astype(BF16), kc.astype(BF16), alp.transpose(1, 0, 2)


def _build_toeplitz(lag_ref, m_s, chunk):
    j_n = lag_ref.shape[1]
    per_tile = LANES // chunk
    low = lax.broadcasted_iota(I32, (chunk, LANES), 1) < chunk
    for i in range(j_n):
        for q in range(j_n // per_tile):
            tile = None
            for p in range(per_tile):
                j = q * per_tile + p
                row = jnp.broadcast_to(lag_ref[0, i, j:j + 1, :], (chunk, LANES))
                rot = pltpu.roll(row, (p * chunk + LANES - (chunk - 1)) % LANES, 1, stride=1, stride_axis=0)
                tile = rot if tile is None else jnp.where(low, tile, rot)
            m_s[i * chunk:(i + 1) * chunk, q * LANES:(q + 1) * LANES] = tile.astype(BF16)


def _s5_kernel(*refs, nb, nc, readout):
    if readout:
        u_ref, ke_ref, alp_ref, init_ref, lag_ref, kc_ref, fin_ref, y_ref, m_s = refs
    else:
        u_ref, ke_ref, alp_ref, init_ref, fin_ref = refs
    rows = nb * nc
    half = 2 * S5_STATE
    u = u_ref[0]
    e = _dot(u, ke_ref[0])
    er, ei = e[:, :half], e[:, half:]
    r_idx = lax.broadcasted_iota(I32, (rows, half), 0)
    if nc & (nc - 1) == 0:
        n_idx, b_idx = r_idx & (nc - 1), r_idx >> (nc.bit_length() - 1)
    else:
        n_idx, b_idx = lax.rem(r_idx, nc), lax.div(r_idx, nc)
    is_f = lax.broadcasted_iota(I32, (rows, half), 1) < S5_STATE
    seen = jnp.where(is_f, n_idx, nc - 1 - n_idx)
    init = init_ref[0]
    init_r = jnp.zeros((rows, half), F32)
    init_i = jnp.zeros((rows, half), F32)
    for b in range(nb):
        init_r = jnp.where(b_idx == b, init[b:b + 1, :half], init_r)
        init_i = jnp.where(b_idx == b, init[b:b + 1, half:], init_i)
    alp = alp_ref[0]
    ar, ai = alp[0:1, :half], alp[0:1, half:]
    er = er + jnp.where(seen == 0, ar * init_r - ai * init_i, 0.0)
    ei = ei + jnp.where(seen == 0, ar * init_i + ai * init_r, 0.0)

    def from_prev(v, dist):
        return jnp.where(is_f, pltpu.roll(v, dist, 0), pltpu.roll(v, rows - dist, 0))

    d, s = 1, 0
    while d < nc:
        ar, ai = alp[s:s + 1, :half], alp[s:s + 1, half:]
        sr = jnp.where(seen >= d, from_prev(er, d), 0.0)
        si = jnp.where(seen >= d, from_prev(ei, d), 0.0)
        er, ei = er + (ar * sr - ai * si), ei + (ar * si + ai * sr)
        d, s = d * 2, s + 1
    is_f_row = lax.broadcasted_iota(I32, (1, half), 1) < S5_STATE
    for b in range(nb):
        lo, hi = b * nc, b * nc + nc - 1
        fin_ref[0, b:b + 1, :half] = jnp.where(is_f_row, er[hi:hi + 1], er[lo:lo + 1])
        fin_ref[0, b:b + 1, half:] = jnp.where(is_f_row, ei[hi:hi + 1], ei[lo:lo + 1])
    if readout:
        if nc > 1:
            sin_r = jnp.where(seen >= 1, from_prev(er, 1), init_r)
            sin_i = jnp.where(seen >= 1, from_prev(ei, 1), init_i)
        else:
            sin_r, sin_i = init_r, init_i
        s_in = jnp.concatenate([sin_r, sin_i], axis=1).astype(BF16)
        _build_toeplitz(lag_ref, m_s, S5_CHUNK)
        y_ref[0] = (_dot(u, m_s[...]) + _dot(s_in, kc_ref[0])).astype(BF16)


def _s5(ug, ke, alp, init, m=None, kc=None, *, nb, nc):
    g_n, rows, width = ug.shape
    readout = m is not None
    grp = lambda g: (g, 0, 0)
    in_specs = [pl.BlockSpec((1, rows, width), grp), pl.BlockSpec((1,) + ke.shape[1:], grp),
                pl.BlockSpec((1,) + alp.shape[1:], grp), pl.BlockSpec((1,) + init.shape[1:], grp)]
    out_shape = [jax.ShapeDtypeStruct(init.shape, F32)]
    out_specs = [pl.BlockSpec((1,) + init.shape[1:], grp)]
    args = [ug, ke, alp, init]
    scratch = []
    if readout:
        assert m.shape[3] == LANES and 2 * S5_CHUNK == LANES
        in_specs += [pl.BlockSpec((1,) + m.shape[1:], lambda g: (g, 0, 0, 0)),
                     pl.BlockSpec((1,) + kc.shape[1:], grp)]
        out_shape.append(jax.ShapeDtypeStruct(ug.shape, BF16))
        out_specs.append(pl.BlockSpec((1, rows, width), grp))
        args += [m, kc]
        scratch = [pltpu.VMEM((width, width), BF16)]
    return pl.pallas_call(
        functools.partial(_s5_kernel, nb=nb, nc=nc, readout=readout),
        out_shape=tuple(out_shape), grid=(g_n,), in_specs=in_specs, out_specs=tuple(out_specs),
        scratch_shapes=scratch,
        compiler_params=_params(("parallel",)),
        name="s5_readout" if readout else "s5_state",
    )(*args)


def _to_groups(u, nb, nc, chunk, g_n):
    j_n = u.shape[1] // g_n
    return u.reshape(nb * nc, chunk, g_n, j_n).transpose(2, 0, 3, 1).reshape(g_n, nb * nc, j_n * chunk)


def _from_groups(y, nb, nc, chunk, g_n):
    j_n = y.shape[2] // chunk
    return y.reshape(g_n, nb * nc, j_n, chunk).transpose(1, 3, 0, 2).reshape(nb * nc * chunk, g_n * j_n)


def _mix_kernel(y_ref, u_ref, gu_ref, vn_ref, g1_ref, g2_ref, x_ref,
                d_ref, wglu_ref, bglu_ref, sguw_ref, sgub_ref, wb1_ref, wb2_ref, wout_ref,
                npost_ref, gt_ref, npre_ref, shf_ref, scf_ref, wr_ref,
                x1_ref, hp_ref, lg_ref, *, tm):
    y = jax.nn.gelu(y_ref[...].astype(F32) + d_ref[...] * u_ref[...].astype(F32))
    y_s5 = y * jax.nn.sigmoid(_dot(y.astype(BF16), wglu_ref[...]) + bglu_ref[...])

    lane = lax.broadcasted_iota(I32, (1, LANES), 1)
    m_lo = (lane < LANES // 2).astype(F32).astype(BF16)
    m_hi = (lane >= LANES // 2).astype(F32).astype(BF16)
    vn = vn_ref[...]
    chunks = []
    for c in range(tm // SGU_CHUNK):
        vc = vn[c * SGU_CHUNK:(c + 1) * SGU_CHUNK, :]
        tiles = []
        for q in range(vc.shape[1] // LANES):
            vt = vc[:, q * LANES:(q + 1) * LANES]
            rhs = jnp.concatenate([vt * m_lo, vt * m_hi], axis=0)
            tiles.append(_dot(sguw_ref[q], rhs))
        chunks.append(jnp.concatenate(tiles, axis=1) + sgub_ref[...])
    mixed = jnp.concatenate(chunks, axis=0)
    y_sgu = gu_ref[...].astype(F32) * mixed

    merged = (g1_ref[...].astype(F32) * _dot(y_s5.astype(BF16), wb1_ref[...])
              + g2_ref[...].astype(F32) * _dot(y_sgu.astype(BF16), wb2_ref[...]))
    mx = _dot(merged.astype(BF16), wout_ref[...])
    x1 = x_ref[...] + gt_ref[0] * _rms(mx, npost_ref[...])
    x1_ref[...] = x1
    hp = _rms(x1, npre_ref[...]) * (1.0 + scf_ref[0]) + shf_ref[0]
    half = hp.shape[1] // 2
    hp_ref[...] = _pack_pair(hp[:, :half], hp[:, half:])
    lg_ref[...] = _dot(hp.astype(BF16), wr_ref[...])


def _mix(y, u, gu, vn, g1, g2, x2, s5_d, wglu, bglu, sguw, sgub, wb1, wb2, wout,
         npost, gt, npre, shf, scf, wr, tokens_per_batch, tm):
    n, d = x2.shape
    d_s5, d_sgu, n_exp = y.shape[1], gu.shape[1], wr.shape[1]
    tiles_per_batch = tokens_per_batch // tm
    row = lambda i: (i, 0)
    fixed = lambda i: (0, 0)
    fixed3 = lambda i: (0, 0, 0)
    per_batch = lambda i: (i // tiles_per_batch, 0, 0)
    vec = lambda w: pl.BlockSpec((1, w), fixed)
    return pl.pallas_call(
        functools.partial(_mix_kernel, tm=tm),
        out_shape=(jax.ShapeDtypeStruct((n, d), F32), jax.ShapeDtypeStruct((n, d // 2), U32),
                   jax.ShapeDtypeStruct((n, n_exp), F32)),
        grid=(n // tm,),
        in_specs=[pl.BlockSpec((tm, d_s5), row), pl.BlockSpec((tm, d_s5), row), pl.BlockSpec((tm, d_sgu), row),
                  pl.BlockSpec((tm, d_sgu), row), pl.BlockSpec((tm, d), row), pl.BlockSpec((tm, d), row),
                  pl.BlockSpec((tm, d), row),
                  vec(d_s5), pl.BlockSpec(wglu.shape, fixed), vec(d_s5),
                  pl.BlockSpec(sguw.shape, fixed3), pl.BlockSpec(sgub.shape, fixed),
                  pl.BlockSpec(wb1.shape, fixed), pl.BlockSpec(wb2.shape, fixed), pl.BlockSpec(wout.shape, fixed),
                  vec(d), pl.BlockSpec((1, 1, d), per_batch), vec(d),
                  pl.BlockSpec((1, 1, d), per_batch), pl.BlockSpec((1, 1, d), per_batch),
                  pl.BlockSpec(wr.shape, fixed)],
        out_specs=(pl.BlockSpec((tm, d), row), pl.BlockSpec((tm, d // 2), row), pl.BlockSpec((tm, n_exp), row)),
        compiler_params=_params(("parallel",)),
        name="mix",
    )(y, u, gu, vn, g1, g2, x2, s5_d.reshape(1, d_s5), wglu, bglu.reshape(1, d_s5), sguw, sgub,
      wb1, wb2, wout, npost.reshape(1, d), gt, npre.reshape(1, d), shf, scf, wr)


def _route_kernel(lg_ref, bias_ref, idx_ref, w_ref, cnt_ref, *, n_exp, tn):
    scores = jax.nn.sigmoid(lg_ref[...])
    sel = scores + bias_ref[...]
    per_group = n_exp // N_EXPERT_GROUPS
    neg = jnp.float32(-jnp.inf)
    gs = []
    for g in range(N_EXPERT_GROUPS):
        sg = sel[g * per_group:(g + 1) * per_group, :]
        m1 = jnp.max(sg, axis=0, keepdims=True)
        is_m1 = sg == m1
        n_m1 = jnp.sum(is_m1.astype(F32), axis=0, keepdims=True)
        rest = jnp.max(jnp.where(is_m1, neg, sg), axis=0, keepdims=True)
        gs.append(m1 + jnp.where(n_m1 >= 2.0, m1, rest))
    gsm = jnp.concatenate(gs, axis=0)
    g_iota = lax.broadcasted_iota(I32, gsm.shape, 0)
    e_iota = lax.broadcasted_iota(I32, sel.shape, 0).astype(F32)
    masked = []
    for g in range(N_EXPERT_GROUPS):
        mine = gsm[g:g + 1, :]
        beats = jnp.where(gsm > mine, 1.0, jnp.where(gsm == mine, jnp.where(g_iota < g, 1.0, 0.0), 0.0))
        n_beats = jnp.sum(beats, axis=0, keepdims=True)
        masked.append(jnp.where(n_beats < float(TOPK_GROUPS), sel[g * per_group:(g + 1) * per_group, :], neg))
    selm = jnp.concatenate(masked, axis=0)
    picked, vals = [], []
    hot = jnp.zeros(sel.shape, F32)
    for _ in range(TOP_K):
        m = jnp.max(selm, axis=0, keepdims=True)
        first = jnp.min(jnp.where(selm == m, e_iota, float(n_exp)), axis=0, keepdims=True)
        one = e_iota == first
        picked.append(first)
        vals.append(jnp.sum(jnp.where(one, scores, 0.0), axis=0, keepdims=True))
        hot = hot + jnp.where(one, 1.0, 0.0)
        selm = jnp.where(one, neg, selm)
    idx_ref[...] = jnp.concatenate(picked, axis=0).astype(I32)
    wv = jnp.concatenate(vals, axis=0)
    w_ref[...] = wv / jnp.sum(wv, axis=0, keepdims=True) * ROUTE_SCALE

    @pl.when(pl.program_id(0) == 0)
    def _():
        cnt_ref[...] = jnp.zeros_like(cnt_ref)

    cnt_ref[...] += jnp.sum(hot, axis=1, keepdims=True)


def _route(logits_t, bias, tn):
    n_exp, n = logits_t.shape
    col = lambda i: (0, i)
    fixed = lambda i: (0, 0)
    return pl.pallas_call(
        functools.partial(_route_kernel, n_exp=n_exp, tn=tn),
        out_shape=(jax.ShapeDtypeStruct((TOP_K, n), I32), jax.ShapeDtypeStruct((TOP_K, n), F32),
                   jax.ShapeDtypeStruct((n_exp, 1), F32)),
        grid=(n // tn,),
        in_specs=[pl.BlockSpec((n_exp, tn), col), pl.BlockSpec((n_exp, 1), fixed)],
        out_specs=(pl.BlockSpec((TOP_K, tn), col), pl.BlockSpec((TOP_K, tn), col),
                   pl.BlockSpec((n_exp, 1), fixed)),
        compiler_params=_params(("arbitrary",)),
        name="route",
    )(logits_t, bias.reshape(n_exp, 1))


def _dest_kernel(idx_ref, start_ref, dest_ref, carry_ref, *, n_exp, tn):
    @pl.when(pl.program_id(0) == 0)
    def _():
        carry_ref[...] = start_ref[...]

    idx = idx_ref[...]
    e_iota = lax.broadcasted_iota(I32, (n_exp, tn), 0)
    hot = jnp.zeros((n_exp, tn), F32)
    for k in range(TOP_K):
        hot = hot + jnp.where(e_iota == idx[k:k + 1, :], 1.0, 0.0)
    before = jnp.where(lax.broadcasted_iota(I32, (tn, tn), 0) < lax.broadcasted_iota(I32, (tn, tn), 1), 1.0, 0.0)
    rank = _dot(hot.astype(BF16), before.astype(BF16)) + carry_ref[...]
    dest_ref[...] = jnp.concatenate(
        [jnp.sum(jnp.where(e_iota == idx[k:k + 1, :], rank, 0.0), axis=0, keepdims=True)
         for k in range(TOP_K)], axis=0).astype(I32)
    carry_ref[...] += jnp.sum(hot, axis=1, keepdims=True)


def _dest(idx, start, tn):
    n = idx.shape[1]
    n_exp = start.shape[0]
    col = lambda i: (0, i)
    return pl.pallas_call(
        functools.partial(_dest_kernel, n_exp=n_exp, tn=tn),
        out_shape=jax.ShapeDtypeStruct((TOP_K, n), I32),
        grid=(n // tn,),
        in_specs=[pl.BlockSpec((TOP_K, tn), col), pl.BlockSpec((n_exp, 1), lambda i: (0, 0))],
        out_specs=pl.BlockSpec((TOP_K, tn), col),
        scratch_shapes=[pltpu.VMEM((n_exp, 1), F32)],
        compiler_params=_params(("arbitrary",)),
        name="dest",
    )(idx, start)


def _dispatch_kernel(dest_ref, hp_ref, xs_ref, sem, *, tm):
    def row_copy(r, k):
        return pltpu.make_async_copy(hp_ref.at[pl.ds(r, 1), :], xs_ref.at[pl.ds(dest_ref[k, r], 1), :], sem)

    def start(r, c):
        for k in range(TOP_K):
            row_copy(r, k).start()
        return c

    def wait(r, c):
        for k in range(TOP_K):
            row_copy(r, k).wait()
        return c

    lax.fori_loop(0, tm, start, 0)
    lax.fori_loop(0, tm, wait, 0)


def _dispatch(dest, hp, cap, tm):
    n, w = hp.shape
    return pl.pallas_call(
        functools.partial(_dispatch_kernel, tm=tm),
        out_shape=jax.ShapeDtypeStruct((cap, w), U32),
        grid=(n // tm,),
        in_specs=[pl.BlockSpec((TOP_K, tm), lambda i: (0, i), memory_space=pltpu.SMEM),
                  pl.BlockSpec((tm, w), lambda i: (i, 0))],
        out_specs=pl.BlockSpec(memory_space=pl.ANY),
        scratch_shapes=[pltpu.SemaphoreType.DMA(())],
        compiler_params=_params(("arbitrary",)),
        name="dispatch",
    )(dest, hp)


def _expert_kernel(be_ref, bv_ref, xs_ref, wg_ref, wu_ref, wd_ref, ys_ref, wgu_s, wd_s, *, rows):
    b = pl.program_id(0)
    valid = bv_ref[b]
    changed = jnp.logical_or(b == 0, be_ref[b] != be_ref[jnp.maximum(b - 1, 0)])
    d_e = wg_ref.shape[2]

    @pl.when(jnp.logical_and(changed, valid > 0))
    def _():
        wgu_s[:, :d_e] = wg_ref[0].astype(BF16)
        wgu_s[:, d_e:] = wu_ref[0].astype(BF16)
        wd_s[...] = wd_ref[0].astype(BF16)

    for s in range(rows // EXPERT_SUB):
        @pl.when(valid > s * EXPERT_SUB)
        def _():
            lo, hi = _unpack_pair(xs_ref[s * EXPERT_SUB:(s + 1) * EXPERT_SUB, :])
            r_iota = lax.broadcasted_iota(I32, lo.shape, 0) + s * EXPERT_SUB
            live = r_iota < valid
            xb = jnp.concatenate([jnp.where(live, lo, 0.0), jnp.where(live, hi, 0.0)], axis=1).astype(BF16)
            gu = _dot(xb, wgu_s[...])
            gate, up = gu[:, :d_e], gu[:, d_e:]
            hid = (gate * jax.nn.sigmoid(gate) * up).astype(BF16)
            y = _dot(hid, wd_s[...])
            half = y.shape[1] // 2
            ys_ref[s * EXPERT_SUB:(s + 1) * EXPERT_SUB, :] = _pack_pair(y[:, :half], y[:, half:])


def _experts(blk_expert, blk_valid, xs, w_gate, w_up, w_down, rows):
    cap, w = xs.shape
    _, d, d_e = w_gate.shape
    grid_spec = pltpu.PrefetchScalarGridSpec(
        num_scalar_prefetch=2,
        grid=(cap // rows,),
        in_specs=[pl.BlockSpec((rows, w), lambda b, be, bv: (b, 0)),
                  pl.BlockSpec((1, d, d_e), lambda b, be, bv: (be[b], 0, 0)),
                  pl.BlockSpec((1, d, d_e), lambda b, be, bv: (be[b], 0, 0)),
                  pl.BlockSpec((1, d_e, d), lambda b, be, bv: (be[b], 0, 0))],
        out_specs=pl.BlockSpec((rows, w), lambda b, be, bv: (b, 0)),
        scratch_shapes=[pltpu.VMEM((d, 2 * d_e), BF16), pltpu.VMEM((d_e, d), BF16)],
    )
    return pl.pallas_call(
        functools.partial(_expert_kernel, rows=rows),
        out_shape=jax.ShapeDtypeStruct((cap, w), U32),
        grid_spec=grid_spec,
        compiler_params=_params(("arbitrary",)),
        name="experts",
    )(blk_expert, blk_valid, xs, w_gate, w_up, w_down)


def _combine_kernel(dest_ref, ys_ref, wt_ref, hp_ref, x1_ref, wsgu_ref, wsd_ref, npost_ref, gt_ref,
                    o_ref, buf, sem, *, tm):
    def row_copy(r, k):
        return pltpu.make_async_copy(ys_ref.at[pl.ds(dest_ref[k, r], 1), :], buf.at[k, pl.ds(r, 1), :], sem)

    def start(r, c):
        for k in range(TOP_K):
            row_copy(r, k).start()
        return c

    def wait(r, c):
        for k in range(TOP_K):
            row_copy(r, k).wait()
        return c

    lax.fori_loop(0, tm, start, 0)

    lo, hi = _unpack_pair(hp_ref[...])
    hb = jnp.concatenate([lo, hi], axis=1).astype(BF16)
    d_sh = wsd_ref.shape[0]
    gu = _dot(hb, wsgu_ref[...])
    gate, up = gu[:, :d_sh], gu[:, d_sh:]
    shared = _dot((gate * jax.nn.sigmoid(gate) * up).astype(BF16), wsd_ref[...])

    lax.fori_loop(0, tm, wait, 0)
    wt = wt_ref[...]
    acc_lo = jnp.zeros(lo.shape, F32)
    acc_hi = jnp.zeros(lo.shape, F32)
    for k in range(TOP_K):
        y_lo, y_hi = _unpack_pair(buf[k])
        acc_lo = acc_lo + wt[:, k:k + 1] * y_lo
        acc_hi = acc_hi + wt[:, k:k + 1] * y_hi
    fx = jnp.concatenate([acc_lo, acc_hi], axis=1) + shared
    o_ref[...] = x1_ref[...] + gt_ref[0] * _rms(fx, npost_ref[...])


def _combine(dest, ys, wt, hp, x1, wsgu, wsd, npost, gt, tokens_per_batch, tm):
    n, d = x1.shape
    w = hp.shape[1]
    tiles_per_batch = tokens_per_batch // tm
    row = lambda i: (i, 0)
    fixed = lambda i: (0, 0)
    return pl.pallas_call(
        functools.partial(_combine_kernel, tm=tm),
        out_shape=jax.ShapeDtypeStruct((n, d), F32),
        grid=(n // tm,),
        in_specs=[pl.BlockSpec((TOP_K, tm), lambda i: (0, i), memory_space=pltpu.SMEM),
                  pl.BlockSpec(memory_space=pl.ANY),
                  pl.BlockSpec((tm, TOP_K), row), pl.BlockSpec((tm, w), row), pl.BlockSpec((tm, d), row),
                  pl.BlockSpec(wsgu.shape, fixed), pl.BlockSpec(wsd.shape, fixed),
                  pl.BlockSpec((1, d), fixed), pl.BlockSpec((1, 1, d), lambda i: (i // tiles_per_batch, 0, 0))],
        out_specs=pl.BlockSpec((tm, d), row),
        scratch_shapes=[pltpu.VMEM((TOP_K, tm, w), U32), pltpu.SemaphoreType.DMA(())],
        compiler_params=_params(("arbitrary",)),
        name="combine",
    )(dest, ys, wt, hp, x1, wsgu, wsd, npost.reshape(1, d), gt)


def _layer(x, ctx, mod_x, mod_c, norm_pre_mix, norm_post_mix, norm_pre_ffn, norm_post_ffn,
           w_in, s5_a_re, s5_a_im, s5_log_dt, s5_b_re, s5_b_im, s5_c_re, s5_c_im, s5_d, w_glu, b_glu,
           sgu_norm, sgu_w, sgu_b, w_branch_s5, w_branch_sgu, w_out,
           w_router, router_bias, w_exp_gate, w_exp_up, w_exp_down, w_sh_gate, w_sh_up, w_sh_down):
    bsz, t_len, d = x.shape
    c_len = ctx.shape[1]
    n = bsz * t_len
    d_s5, d_sgu = w_glu.shape[0], sgu_norm.shape[0]
    g_n = d_s5 // S5_GROUP_CH
    n_exp = w_router.shape[1]
    chunk = S5_CHUNK
    tm = min(512, t_len)
    assert t_len % tm == 0 and tm % SGU_CHUNK == 0 and t_len % chunk == 0 and c_len % chunk == 0

    sh_m, sc_m, gt_m, sh_f, sc_f, gt_f = [v.reshape(bsz, 1, d) for v in jnp.split(mod_x, 6, axis=-1)]
    csh_m, csc_m = mod_c[:d], mod_c[d:2 * d]

    w_in_bf = w_in.astype(BF16)

    nc_lat, nc_ctx = t_len // chunk, c_len // chunk
    n_steps = max(1, (max(nc_lat, nc_ctx) - 1).bit_length())
    m_t, ke_t, kc_t, alp = _s5_tables(s5_a_re, s5_a_im, s5_log_dt, s5_b_re, s5_b_im, s5_c_re, s5_c_im,
                                      chunk, n_steps)

    ctx2 = ctx.reshape(bsz * c_len, d)
    u_ctx = _inproj_u(ctx2, norm_pre_mix, csh_m, csc_m, w_in_bf[:, :d_s5], min(512, bsz * c_len))
    zero_init = jnp.zeros((g_n, bsz, 4 * S5_STATE), F32)
    (ctx_final,) = _s5(_to_groups(u_ctx, bsz, nc_ctx, chunk, g_n), ke_t, alp, zero_init, nb=bsz, nc=nc_ctx)

    x2 = x.reshape(n, d)
    u, gu, vn, g1, g2 = _inproj(x2, norm_pre_mix, sh_m, sc_m, w_in_bf, sgu_norm, t_len, tm, d_s5, d_sgu)
    _, yg = _s5(_to_groups(u, bsz, nc_lat, chunk, g_n), ke_t, alp, ctx_final, m_t, kc_t, nb=bsz, nc=nc_lat)
    y = _from_groups(yg, bsz, nc_lat, chunk, g_n)

    ch = d_sgu // SGU_GROUPS
    per_tile = LANES // ch
    sguw = sgu_w.reshape(SGU_GROUPS // per_tile, per_tile, SGU_CHUNK, SGU_CHUNK)
    sguw = sguw.transpose(0, 2, 1, 3).reshape(SGU_GROUPS // per_tile, SGU_CHUNK, per_tile * SGU_CHUNK).astype(BF16)
    sgub = jnp.repeat(sgu_b.T, ch, axis=1)

    x1, hp, logits = _mix(y, u, gu, vn, g1, g2, x2, s5_d, w_glu.astype(BF16), b_glu, sguw, sgub,
                          w_branch_s5.astype(BF16), w_branch_sgu.astype(BF16), w_out.astype(BF16),
                          norm_post_mix, gt_m, norm_pre_ffn, sh_f, sc_f, w_router.astype(BF16), t_len, tm)

    tn = min(512, n)
    idx, wts, counts = _route(logits.T, router_bias, tn)
    rows = EXPERT_ROWS
    cnt = counts.reshape(n_exp).astype(I32)
    nblk = (cnt + rows - 1) // rows
    blk_end = jnp.cumsum(nblk)
    blk_start = blk_end - nblk
    n_blocks = (n * TOP_K) // rows + n_exp
    cap = n_blocks * rows
    b_ids = jnp.arange(n_blocks, dtype=I32)
    blk_expert = jnp.minimum(jnp.searchsorted(blk_end, b_ids, side='right'), n_exp - 1).astype(I32)
    blk_valid = jnp.clip(cnt[blk_expert] - (b_ids - blk_start[blk_expert]) * rows, 0, rows).astype(I32)
    dest = _dest(idx, (blk_start * rows).astype(F32).reshape(n_exp, 1), tn)

    td = min(256, n)
    xs = _dispatch(dest, hp, cap, td)
    ys = _experts(blk_expert, blk_valid, xs, w_exp_gate, w_exp_up, w_exp_down, rows)
    wsgu = jnp.concatenate([w_sh_gate, w_sh_up], axis=1).astype(BF16)
    out = _combine(dest, ys, wts.T, hp, x1, wsgu, w_sh_down.astype(BF16), norm_post_ffn, gt_f, t_len, td)
    return out.reshape(bsz, t_len, d)


def kernel(x, c, ctx, c_ctx, w_mod, b_mod, norm_pre_mix, norm_post_mix, norm_pre_ffn, norm_post_ffn, w_in, s5_a_re, s5_a_im, s5_log_dt, s5_b_re, s5_b_im, s5_c_re, s5_c_im, s5_d, w_glu, b_glu, sgu_norm, sgu_w, sgu_b, w_branch_s5, w_branch_sgu, w_out, w_router, router_bias, w_exp_gate, w_exp_up, w_exp_down, w_sh_gate, w_sh_up, w_sh_down):
    depth = w_mod.shape[0]
    assert depth == 1, "the context stream is only carried through the last layer's S5 states"
    bsz = x.shape[0]
    pad = (-(bsz + 1)) % 8
    cpad = jnp.concatenate([c, c_ctx[None, :], jnp.zeros((pad, c.shape[1]), c.dtype)], axis=0)
    mod = _modulation(cpad, w_mod[0], b_mod[0])
    return _layer(x, ctx, mod[:bsz], mod[bsz], norm_pre_mix[0], norm_post_mix[0], norm_pre_ffn[0],
                  norm_post_ffn[0], w_in[0], s5_a_re[0], s5_a_im[0], s5_log_dt[0], s5_b_re[0], s5_b_im[0],
                  s5_c_re[0], s5_c_im[0], s5_d[0], w_glu[0], b_glu[0], sgu_norm[0], sgu_w[0], sgu_b[0],
                  w_branch_s5[0], w_branch_sgu[0], w_out[0], w_router[0], router_bias[0],
                  w_exp_gate[0], w_exp_up[0], w_exp_down[0], w_sh_gate[0], w_sh_up[0], w_sh_down[0])
```

```python
import functools
import math

import jax
import jax.numpy as jnp
from jax import lax
from jax.experimental import pallas as pl
from jax.experimental.pallas import tpu as pltpu

F32 = jnp.float32
BF16 = jnp.bfloat16
I32 = jnp.int32

EPS = 1e-6
S5_GROUP_CH = 16
S5_STATE = 64
S5_CHUNK = 64
SGU_GROUPS = 8
SGU_CHUNK = 128
N_EXPERT_GROUPS = 8
TOPK_GROUPS = 4
TOP_K = 8
ROUTE_SCALE = 2.5
LANES = 128
EXPERT_ROWS = 512
EXPERT_SUB = 256
VMEM_LIMIT = 56 * 1024 * 1024


def _params(sem):
    return pltpu.CompilerParams(dimension_semantics=sem, vmem_limit_bytes=VMEM_LIMIT)


def _rms(v, g):
    return v * lax.rsqrt(jnp.mean(v * v, axis=-1, keepdims=True) + EPS) * g


def _dot(a, b):
    return jnp.dot(a, b, preferred_element_type=F32)


def _mod_kernel(c_ref, w_ref, b_ref, o_ref):
    cv = c_ref[...]
    s = cv * jax.nn.sigmoid(cv)
    o_ref[...] = _dot(s.astype(BF16), w_ref[...].astype(BF16)) + b_ref[...]


def _modulation(cpad, w_mod, b_mod):
    d, n6 = w_mod.shape
    tn = 1024
    return pl.pallas_call(
        _mod_kernel,
        out_shape=jax.ShapeDtypeStruct((cpad.shape[0], n6), F32),
        grid=(n6 // tn,),
        in_specs=[pl.BlockSpec(cpad.shape, lambda j: (0, 0)),
                  pl.BlockSpec((d, tn), lambda j: (0, j)),
                  pl.BlockSpec((1, tn), lambda j: (0, j))],
        out_specs=pl.BlockSpec((cpad.shape[0], tn), lambda j: (0, j)),
        compiler_params=_params(("parallel",)),
        name="mod",
    )(cpad, w_mod, b_mod.reshape(1, n6))


def _inproj_kernel(x_ref, g_ref, sh_ref, sc_ref, w_ref, sgun_ref,
                   u_ref, gu_ref, vn_ref, g1_ref, g2_ref, *, d_s5, d_sgu, d_model):
    h = _rms(x_ref[...], g_ref[...])
    hb = (h * (1.0 + sc_ref[0]) + sh_ref[0]).astype(BF16)
    o1, o2, o3, o4 = d_s5, d_s5 + d_sgu, d_s5 + 2 * d_sgu, d_s5 + 2 * d_sgu + d_model
    u_ref[...] = _dot(hb, w_ref[:, 0:o1]).astype(BF16)
    gu_ref[...] = jax.nn.gelu(_dot(hb, w_ref[:, o1:o2])).astype(BF16)
    v = jax.nn.gelu(_dot(hb, w_ref[:, o2:o3]))
    vn_ref[...] = _rms(v, sgun_ref[...]).astype(BF16)
    g1_ref[...] = jax.nn.sigmoid(_dot(hb, w_ref[:, o3:o4])).astype(BF16)
    g2_ref[...] = jax.nn.sigmoid(_dot(hb, w_ref[:, o4:o4 + d_model])).astype(BF16)


def _inproj(x2, norm_g, shift, scale, w_in_bf, sgu_norm, tokens_per_batch, tm, d_s5, d_sgu):
    n, d = x2.shape
    tiles_per_batch = tokens_per_batch // tm
    row = lambda i: (i, 0)
    fixed = lambda i: (0, 0)
    per_batch = lambda i: (i // tiles_per_batch, 0, 0)
    return pl.pallas_call(
        functools.partial(_inproj_kernel, d_s5=d_s5, d_sgu=d_sgu, d_model=d),
        out_shape=(jax.ShapeDtypeStruct((n, d_s5), BF16), jax.ShapeDtypeStruct((n, d_sgu), BF16),
                   jax.ShapeDtypeStruct((n, d_sgu), BF16), jax.ShapeDtypeStruct((n, d), BF16),
                   jax.ShapeDtypeStruct((n, d), BF16)),
        grid=(n // tm,),
        in_specs=[pl.BlockSpec((tm, d), row), pl.BlockSpec((1, d), fixed),
                  pl.BlockSpec((1, 1, d), per_batch), pl.BlockSpec((1, 1, d), per_batch),
                  pl.BlockSpec(w_in_bf.shape, fixed), pl.BlockSpec((1, d_sgu), fixed)],
        out_specs=(pl.BlockSpec((tm, d_s5), row), pl.BlockSpec((tm, d_sgu), row),
                   pl.BlockSpec((tm, d_sgu), row), pl.BlockSpec((tm, d), row), pl.BlockSpec((tm, d), row)),
        compiler_params=_params(("parallel",)),
        name="inproj",
    )(x2, norm_g.reshape(1, d), shift, scale, w_in_bf, sgu_norm.reshape(1, d_sgu))


def _inproj_u_kernel(x_ref, g_ref, sh_ref, sc_ref, w_ref, u_ref):
    h = _rms(x_ref[...], g_ref[...])
    hb = (h * (1.0 + sc_ref[...]) + sh_ref[...]).astype(BF16)
    u_ref[...] = _dot(hb, w_ref[...]).astype(BF16)


def _inproj_u(x2, norm_g, shift, scale, w_u_bf, tm):
    n, d = x2.shape
    d_s5 = w_u_bf.shape[1]
    row = lambda i: (i, 0)
    fixed = lambda i: (0, 0)
    return pl.pallas_call(
        _inproj_u_kernel,
        out_shape=jax.ShapeDtypeStruct((n, d_s5), BF16),
        grid=(n // tm,),
        in_specs=[pl.BlockSpec((tm, d), row), pl.BlockSpec((1, d), fixed), pl.BlockSpec((1, d), fixed),
                  pl.BlockSpec((1, d), fixed), pl.BlockSpec(w_u_bf.shape, fixed)],
        out_specs=pl.BlockSpec((tm, d_s5), row),
        compiler_params=_params(("parallel",)),
        name="inproj_ctx",
    )(x2, norm_g.reshape(1, d), shift.reshape(1, d), scale.reshape(1, d), w_u_bf)


def _s5_tables(a_re, a_im, log_dt, b_re, b_im, c_re, c_im, chunk, n_steps):
    hi = lax.Precision.HIGHEST
    g_n, p_n, j_n = a_re.shape[1], a_re.shape[2], b_re.shape[3]
    dt = jnp.exp(log_dt)[..., None]
    lam_re, lam_im = dt * a_re, dt * a_im
    ab_re, ab_im = jnp.exp(lam_re) * jnp.cos(lam_im), jnp.exp(lam_re) * jnp.sin(lam_im)
    den = a_re * a_re + a_im * a_im
    q_re = ((ab_re - 1.0) * a_re + ab_im * a_im) / den
    q_im = (ab_im * a_re - (ab_re - 1.0) * a_im) / den
    bb_re = q_re[..., None] * b_re - q_im[..., None] * b_im
    bb_im = q_re[..., None] * b_im + q_im[..., None] * b_re
    k = jnp.arange(chunk + 1, dtype=F32)[:, None, None, None]
    mag = jnp.exp(k * lam_re[None])
    pw_re, pw_im = mag * jnp.cos(k * lam_im[None]), mag * jnp.sin(k * lam_im[None])

    def lag_kernels(d):
        x_re = pw_re[:chunk, d, :, :, None] * bb_re[d][None] - pw_im[:chunk, d, :, :, None] * bb_im[d][None]
        x_im = pw_re[:chunk, d, :, :, None] * bb_im[d][None] + pw_im[:chunk, d, :, :, None] * bb_re[d][None]
        return (jnp.einsum('gjp,kgpi->gkij', c_re[d], x_re, precision=hi)
                - jnp.einsum('gjp,kgpi->gkij', c_im[d], x_im, precision=hi))

    kf, kb = lag_kernels(0), lag_kernels(1)
    lags = jnp.concatenate([kb[:, 1:][:, ::-1], kf[:, 0:1] + kb[:, 0:1], kf[:, 1:],
                            jnp.zeros_like(kf[:, 0:1])], axis=1)
    lags = lags.transpose(0, 2, 3, 1)

    pf_re, pf_im = pw_re[:chunk, 0][::-1], pw_im[:chunk, 0][::-1]
    pb_re, pb_im = pw_re[:chunk, 1], pw_im[:chunk, 1]

    def state_in(p_re, p_im, d):
        e_re = p_re[..., None] * bb_re[d][None] - p_im[..., None] * bb_im[d][None]
        e_im = p_re[..., None] * bb_im[d][None] + p_im[..., None] * bb_re[d][None]
        return e_re.transpose(1, 3, 0, 2), e_im.transpose(1, 3, 0, 2)

    ef_re, ef_im = state_in(pf_re, pf_im, 0)
    eb_re, eb_im = state_in(pb_re, pb_im, 1)
    ke = jnp.concatenate([ef_re, eb_re, ef_im, eb_im], axis=-1).reshape(g_n, chunk * j_n, 4 * p_n)

    qf_re, qf_im = pw_re[1:chunk + 1, 0], pw_im[1:chunk + 1, 0]
    qb_re, qb_im = pw_re[1:chunk + 1, 1][::-1], pw_im[1:chunk + 1, 1][::-1]

    def state_out(p_re, p_im, d):
        cr, ci = c_re[d].transpose(0, 2, 1), c_im[d].transpose(0, 2, 1)
        pr, pi = p_re.transpose(1, 2, 0), p_im.transpose(1, 2, 0)
        o_re = pr[:, :, None, :] * cr[..., None] - pi[:, :, None, :] * ci[..., None]
        o_im = pr[:, :, None, :] * ci[..., None] + pi[:, :, None, :] * cr[..., None]
        return o_re, o_im

    of_re, of_im = state_out(qf_re, qf_im, 0)
    ob_re, ob_im = state_out(qb_re, qb_im, 1)
    kc = jnp.concatenate([of_re, ob_re, -of_im, -ob_im], axis=1).reshape(g_n, 4 * p_n, chunk * j_n)

    e = (chunk * (2.0 ** jnp.arange(n_steps, dtype=F32)))[:, None, None, None]
    mg = jnp.exp(e * lam_re[None])
    al_re, al_im = mg * jnp.cos(e * lam_im[None]), mg * jnp.sin(e * lam_im[None])
    alp = jnp.concatenate([al_re[:, 0], al_re[:, 1], al_im[:, 0], al_im[:, 1]], axis=-1)
    return lags, ke.astype(BF16), kc.astype(BF16), alp.transpose(1, 0, 2)


def _build_toeplitz(lag_ref, m_s, chunk):
    j_n = lag_ref.shape[1]
    per_tile = LANES // chunk
    low = lax.broadcasted_iota(I32, (chunk, LANES), 1) < chunk
    for i in range(j_n):
        for q in range(j_n // per_tile):
            tile = None
            for p in range(per_tile):
                j = q * per_tile + p
                row = jnp.broadcast_to(lag_ref[0, i, j:j + 1, :], (chunk, LANES))
                rot = pltpu.roll(row, (p * chunk + LANES - (chunk - 1)) % LANES, 1, stride=1, stride_axis=0)
                tile = rot if tile is None else jnp.where(low, tile, rot)
            m_s[i * chunk:(i + 1) * chunk, q * LANES:(q + 1) * LANES] = tile.astype(BF16)


def _s5_kernel(*refs, nb, nc, readout):
    if readout:
        u_ref, ke_ref, alp_ref, init_ref, lag_ref, kc_ref, fin_ref, y_ref, m_s = refs
    else:
        u_ref, ke_ref, alp_ref, init_ref, fin_ref = refs
    rows = nb * nc
    half = 2 * S5_STATE
    u = u_ref[0]
    e = _dot(u, ke_ref[0])
    er, ei = e[:, :half], e[:, half:]
    r_idx = lax.broadcasted_iota(I32, (rows, half), 0)
    if nc & (nc - 1) == 0:
        n_idx, b_idx = r_idx & (nc - 1), r_idx >> (nc.bit_length() - 1)
    else:
        n_idx, b_idx = lax.rem(r_idx, nc), lax.div(r_idx, nc)
    is_f = lax.broadcasted_iota(I32, (rows, half), 1) < S5_STATE
    seen = jnp.where(is_f, n_idx, nc - 1 - n_idx)
    init = init_ref[0]
    init_r = jnp.zeros((rows, half), F32)
    init_i = jnp.zeros((rows, half), F32)
    for b in range(nb):
        init_r = jnp.where(b_idx == b, init[b:b + 1, :half], init_r)
        init_i = jnp.where(b_idx == b, init[b:b + 1, half:], init_i)
    alp = alp_ref[0]
    ar, ai = alp[0:1, :half], alp[0:1, half:]
    er = er + jnp.where(seen == 0, ar * init_r - ai * init_i, 0.0)
    ei = ei + jnp.where(seen == 0, ar * init_i + ai * init_r, 0.0)

    def from_prev(v, dist):
        return jnp.where(is_f, pltpu.roll(v, dist, 0), pltpu.roll(v, rows - dist, 0))

    d, s = 1, 0
    while d < nc:
        ar, ai = alp[s:s + 1, :half], alp[s:s + 1, half:]
        sr = jnp.where(seen >= d, from_prev(er, d), 0.0)
        si = jnp.where(seen >= d, from_prev(ei, d), 0.0)
        er, ei = er + (ar * sr - ai * si), ei + (ar * si + ai * sr)
        d, s = d * 2, s + 1
    is_f_row = lax.broadcasted_iota(I32, (1, half), 1) < S5_STATE
    for b in range(nb):
        lo, hi = b * nc, b * nc + nc - 1
        fin_ref[0, b:b + 1, :half] = jnp.where(is_f_row, er[hi:hi + 1], er[lo:lo + 1])
        fin_ref[0, b:b + 1, half:] = jnp.where(is_f_row, ei[hi:hi + 1], ei[lo:lo + 1])
    if readout:
        if nc > 1:
            sin_r = jnp.where(seen >= 1, from_prev(er, 1), init_r)
            sin_i = jnp.where(seen >= 1, from_prev(ei, 1), init_i)
        else:
            sin_r, sin_i = init_r, init_i
        s_in = jnp.concatenate([sin_r, sin_i], axis=1).astype(BF16)
        _build_toeplitz(lag_ref, m_s, S5_CHUNK)
        y_ref[0] = (_dot(u, m_s[...]) + _dot(s_in, kc_ref[0])).astype(BF16)


def _s5(ug, ke, alp, init, m=None, kc=None, *, nb, nc):
    g_n, rows, width = ug.shape
    readout = m is not None
    grp = lambda g: (g, 0, 0)
    in_specs = [pl.BlockSpec((1, rows, width), grp), pl.BlockSpec((1,) + ke.shape[1:], grp),
                pl.BlockSpec((1,) + alp.shape[1:], grp), pl.BlockSpec((1,) + init.shape[1:], grp)]
    out_shape = [jax.ShapeDtypeStruct(init.shape, F32)]
    out_specs = [pl.BlockSpec((1,) + init.shape[1:], grp)]
    args = [ug, ke, alp, init]
    scratch = []
    if readout:
        assert m.shape[3] == LANES and 2 * S5_CHUNK == LANES
        in_specs += [pl.BlockSpec((1,) + m.shape[1:], lambda g: (g, 0, 0, 0)),
                     pl.BlockSpec((1,) + kc.shape[1:], grp)]
        out_shape.append(jax.ShapeDtypeStruct(ug.shape, BF16))
        out_specs.append(pl.BlockSpec((1, rows, width), grp))
        args += [m, kc]
        scratch = [pltpu.VMEM((width, width), BF16)]
    return pl.pallas_call(
        functools.partial(_s5_kernel, nb=nb, nc=nc, readout=readout),
        out_shape=tuple(out_shape), grid=(g_n,), in_specs=in_specs, out_specs=tuple(out_specs),
        scratch_shapes=scratch,
        compiler_params=_params(("parallel",)),
        name="s5_readout" if readout else "s5_state",
    )(*args)


def _to_groups(u, nb, nc, chunk, g_n):
    j_n = u.shape[1] // g_n
    return u.reshape(nb * nc, chunk, g_n, j_n).transpose(2, 0, 3, 1).reshape(g_n, nb * nc, j_n * chunk)


def _from_groups(y, nb, nc, chunk, g_n):
    j_n = y.shape[2] // chunk
    return y.reshape(g_n, nb * nc, j_n, chunk).transpose(1, 3, 0, 2).reshape(nb * nc * chunk, g_n * j_n)


def _mix_kernel(y_ref, u_ref, gu_ref, vn_ref, g1_ref, g2_ref, x_ref,
                d_ref, wglu_ref, bglu_ref, sguw_ref, sgub_ref, wb1_ref, wb2_ref, wout_ref,
                npost_ref, gt_ref, npre_ref, shf_ref, scf_ref, wr_ref,
                x1_ref, hp_ref, lg_ref, *, tm):
    y = jax.nn.gelu(y_ref[...].astype(F32) + d_ref[...] * u_ref[...].astype(F32))
    y_s5 = y * jax.nn.sigmoid(_dot(y.astype(BF16), wglu_ref[...]) + bglu_ref[...])

    lane = lax.broadcasted_iota(I32, (1, LANES), 1)
    m_lo = (lane < LANES // 2).astype(F32).astype(BF16)
    m_hi = (lane >= LANES // 2).astype(F32).astype(BF16)
    vn = vn_ref[...]
    chunks = []
    for c in range(tm // SGU_CHUNK):
        vc = vn[c * SGU_CHUNK:(c + 1) * SGU_CHUNK, :]
        tiles = []
        for q in range(vc.shape[1] // LANES):
            vt = vc[:, q * LANES:(q + 1) * LANES]
            rhs = jnp.concatenate([vt * m_lo, vt * m_hi], axis=0)
            tiles.append(_dot(sguw_ref[q], rhs))
        chunks.append(jnp.concatenate(tiles, axis=1) + sgub_ref[...])
    mixed = jnp.concatenate(chunks, axis=0)
    y_sgu = gu_ref[...].astype(F32) * mixed

    merged = (g1_ref[...].astype(F32) * _dot(y_s5.astype(BF16), wb1_ref[...])
              + g2_ref[...].astype(F32) * _dot(y_sgu.astype(BF16), wb2_ref[...]))
    mx = _dot(merged.astype(BF16), wout_ref[...])
    x1 = x_ref[...] + gt_ref[0] * _rms(mx, npost_ref[...])
    x1_ref[...] = x1
    hp = _rms(x1, npre_ref[...]) * (1.0 + scf_ref[0]) + shf_ref[0]
    hp_ref[...] = hp
    lg_ref[...] = _dot(hp.astype(BF16), wr_ref[...])


def _mix(y, u, gu, vn, g1, g2, x2, s5_d, wglu, bglu, sguw, sgub, wb1, wb2, wout,
         npost, gt, npre, shf, scf, wr, tokens_per_batch, tm):
    n, d = x2.shape
    d_s5, d_sgu, n_exp = y.shape[1], gu.shape[1], wr.shape[1]
    tiles_per_batch = tokens_per_batch // tm
    row = lambda i: (i, 0)
    fixed = lambda i: (0, 0)
    fixed3 = lambda i: (0, 0, 0)
    per_batch = lambda i: (i // tiles_per_batch, 0, 0)
    vec = lambda w: pl.BlockSpec((1, w), fixed)
    return pl.pallas_call(
        functools.partial(_mix_kernel, tm=tm),
        out_shape=(jax.ShapeDtypeStruct((n, d), F32), jax.ShapeDtypeStruct((n, d), F32),
                   jax.ShapeDtypeStruct((n, n_exp), F32)),
        grid=(n // tm,),
        in_specs=[pl.BlockSpec((tm, d_s5), row), pl.BlockSpec((tm, d_s5), row), pl.BlockSpec((tm, d_sgu), row),
                  pl.BlockSpec((tm, d_sgu), row), pl.BlockSpec((tm, d), row), pl.BlockSpec((tm, d), row),
                  pl.BlockSpec((tm, d), row),
                  vec(d_s5), pl.BlockSpec(wglu.shape, fixed), vec(d_s5),
                  pl.BlockSpec(sguw.shape, fixed3), pl.BlockSpec(sgub.shape, fixed),
                  pl.BlockSpec(wb1.shape, fixed), pl.BlockSpec(wb2.shape, fixed), pl.BlockSpec(wout.shape, fixed),
                  vec(d), pl.BlockSpec((1, 1, d), per_batch), vec(d),
                  pl.BlockSpec((1, 1, d), per_batch), pl.BlockSpec((1, 1, d), per_batch),
                  pl.BlockSpec(wr.shape, fixed)],
        out_specs=(pl.BlockSpec((tm, d), row), pl.BlockSpec((tm, d), row), pl.BlockSpec((tm, n_exp), row)),
        compiler_params=_params(("parallel",)),
        name="mix",
    )(y, u, gu, vn, g1, g2, x2, s5_d.reshape(1, d_s5), wglu, bglu.reshape(1, d_s5), sguw, sgub,
      wb1, wb2, wout, npost.reshape(1, d), gt, npre.reshape(1, d), shf, scf, wr)


def _route_kernel(lg_ref, bias_ref, idx_ref, w_ref, cnt_ref, *, n_exp, tn):
    scores = jax.nn.sigmoid(lg_ref[...])
    sel = scores + bias_ref[...]
    per_group = n_exp // N_EXPERT_GROUPS
    neg = jnp.float32(-jnp.inf)
    gs = []
    for g in range(N_EXPERT_GROUPS):
        sg = sel[g * per_group:(g + 1) * per_group, :]
        m1 = jnp.max(sg, axis=0, keepdims=True)
        is_m1 = sg == m1
        n_m1 = jnp.sum(is_m1.astype(F32), axis=0, keepdims=True)
        rest = jnp.max(jnp.where(is_m1, neg, sg), axis=0, keepdims=True)
        gs.append(m1 + jnp.where(n_m1 >= 2.0, m1, rest))
    gsm = jnp.concatenate(gs, axis=0)
    g_iota = lax.broadcasted_iota(I32, gsm.shape, 0)
    e_iota = lax.broadcasted_iota(I32, sel.shape, 0).astype(F32)
    masked = []
    for g in range(N_EXPERT_GROUPS):
        mine = gsm[g:g + 1, :]
        beats = jnp.where(gsm > mine, 1.0, jnp.where(gsm == mine, jnp.where(g_iota < g, 1.0, 0.0), 0.0))
        n_beats = jnp.sum(beats, axis=0, keepdims=True)
        masked.append(jnp.where(n_beats < float(TOPK_GROUPS), sel[g * per_group:(g + 1) * per_group, :], neg))
    selm = jnp.concatenate(masked, axis=0)
    picked, vals = [], []
    hot = jnp.zeros(sel.shape, F32)
    for _ in range(TOP_K):
        m = jnp.max(selm, axis=0, keepdims=True)
        first = jnp.min(jnp.where(selm == m, e_iota, float(n_exp)), axis=0, keepdims=True)
        one = e_iota == first
        picked.append(first)
        vals.append(jnp.sum(jnp.where(one, scores, 0.0), axis=0, keepdims=True))
        hot = hot + jnp.where(one, 1.0, 0.0)
        selm = jnp.where(one, neg, selm)
    idx_ref[...] = jnp.concatenate(picked, axis=0).astype(I32)
    wv = jnp.concatenate(vals, axis=0)
    w_ref[...] = wv / jnp.sum(wv, axis=0, keepdims=True) * ROUTE_SCALE

    @pl.when(pl.program_id(0) == 0)
    def _():
        cnt_ref[...] = jnp.zeros_like(cnt_ref)

    cnt_ref[...] += jnp.sum(hot, axis=1, keepdims=True)


def _route(logits_t, bias, tn):
    n_exp, n = logits_t.shape
    col = lambda i: (0, i)
    fixed = lambda i: (0, 0)
    return pl.pallas_call(
        functools.partial(_route_kernel, n_exp=n_exp, tn=tn),
        out_shape=(jax.ShapeDtypeStruct((TOP_K, n), I32), jax.ShapeDtypeStruct((TOP_K, n), F32),
                   jax.ShapeDtypeStruct((n_exp, 1), F32)),
        grid=(n // tn,),
        in_specs=[pl.BlockSpec((n_exp, tn), col), pl.BlockSpec((n_exp, 1), fixed)],
        out_specs=(pl.BlockSpec((TOP_K, tn), col), pl.BlockSpec((TOP_K, tn), col),
                   pl.BlockSpec((n_exp, 1), fixed)),
        compiler_params=_params(("arbitrary",)),
        name="route",
    )(logits_t, bias.reshape(n_exp, 1))


def _dest_kernel(idx_ref, start_ref, dest_ref, carry_ref, *, n_exp, tn):
    @pl.when(pl.program_id(0) == 0)
    def _():
        carry_ref[...] = start_ref[...]

    idx = idx_ref[...]
    e_iota = lax.broadcasted_iota(I32, (n_exp, tn), 0)
    hot = jnp.zeros((n_exp, tn), F32)
    for k in range(TOP_K):
        hot = hot + jnp.where(e_iota == idx[k:k + 1, :], 1.0, 0.0)
    before = jnp.where(lax.broadcasted_iota(I32, (tn, tn), 0) < lax.broadcasted_iota(I32, (tn, tn), 1), 1.0, 0.0)
    rank = _dot(hot.astype(BF16), before.astype(BF16)) + carry_ref[...]
    dest_ref[...] = jnp.concatenate(
        [jnp.sum(jnp.where(e_iota == idx[k:k + 1, :], rank, 0.0), axis=0, keepdims=True)
         for k in range(TOP_K)], axis=0).astype(I32)
    carry_ref[...] += jnp.sum(hot, axis=1, keepdims=True)


def _dest(idx, start, tn):
    n = idx.shape[1]
    n_exp = start.shape[0]
    col = lambda i: (0, i)
    return pl.pallas_call(
        functools.partial(_dest_kernel, n_exp=n_exp, tn=tn),
        out_shape=jax.ShapeDtypeStruct((TOP_K, n), I32),
        grid=(n // tn,),
        in_specs=[pl.BlockSpec((TOP_K, tn), col), pl.BlockSpec((n_exp, 1), lambda i: (0, 0))],
        out_specs=pl.BlockSpec((TOP_K, tn), col),
        scratch_shapes=[pltpu.VMEM((n_exp, 1), F32)],
        compiler_params=_params(("arbitrary",)),
        name="dest",
    )(idx, start)


def _dispatch_kernel(dest_ref, hp_ref, xs_ref, sem, *, tm):
    def start(r, c):
        for k in range(TOP_K):
            row = dest_ref[r * TOP_K + k]
            pltpu.make_async_copy(hp_ref.at[pl.ds(r, 1), :], xs_ref.at[pl.ds(row, 1), :], sem).start(priority=k % 2)
        return c

    lax.fori_loop(0, tm, start, 0)
    for k in range(TOP_K):
        pltpu.make_async_copy(hp_ref, xs_ref.at[pl.ds(0, tm), :], sem).wait()


def _dispatch(dest, hp, cap, tm):
    n, w = hp.shape
    return pl.pallas_call(
        functools.partial(_dispatch_kernel, tm=tm),
        out_shape=jax.ShapeDtypeStruct((cap, w), hp.dtype),
        grid=(n // tm,),
        in_specs=[pl.BlockSpec((tm * TOP_K,), lambda i: (i,), memory_space=pltpu.SMEM),
                  pl.BlockSpec((tm, w), lambda i: (i, 0))],
        out_specs=pl.BlockSpec(memory_space=pl.ANY),
        scratch_shapes=[pltpu.SemaphoreType.DMA(())],
        compiler_params=_params(("arbitrary",)),
        name="dispatch",
    )(dest, hp)


def _expert_kernel(be_ref, bv_ref, nu_ref, nx_ref, xs_ref, wg_hbm, wu_hbm, wd_hbm, ys_ref,
                   wg_f, wu_f, wd_f, wgu_s, wd_s, sem, *, rows):
    b = pl.program_id(0)
    valid = bv_ref[b]
    changed = jnp.logical_or(b == 0, be_ref[b] != be_ref[jnp.maximum(b - 1, 0)])
    d_e = wg_hbm.shape[2]

    def weight_copies(e):
        return (pltpu.make_async_copy(wg_hbm.at[e], wg_f, sem.at[0]),
                pltpu.make_async_copy(wu_hbm.at[e], wu_f, sem.at[1]),
                pltpu.make_async_copy(wd_hbm.at[e], wd_f, sem.at[2]))

    @pl.when(jnp.logical_and(changed, valid > 0))
    def _():
        @pl.when(b == 0)
        def _():
            for cp in weight_copies(be_ref[b]):
                cp.start()

        for cp in weight_copies(be_ref[b]):
            cp.wait()
        wgu_s[:, :d_e] = wg_f[...].astype(BF16)
        wgu_s[:, d_e:] = wu_f[...].astype(BF16)
        wd_s[...] = wd_f[...].astype(BF16)

        @pl.when(nx_ref[b] >= 0)
        def _():
            for cp in weight_copies(nx_ref[b]):
                cp.start()

    for s in range(rows // EXPERT_SUB):
        @pl.when(valid > s * EXPERT_SUB)
        def _():
            x = xs_ref[s * EXPERT_SUB:(s + 1) * EXPERT_SUB, :]
            r_iota = lax.broadcasted_iota(I32, x.shape, 0) + s * EXPERT_SUB
            xb = jnp.where(r_iota < valid, x, 0.0).astype(BF16)
            gu = _dot(xb, wgu_s[...])
            gate, up = gu[:, :d_e], gu[:, d_e:]
            hid = (gate * jax.nn.sigmoid(gate) * up).astype(BF16)
            ys_ref[s * EXPERT_SUB:(s + 1) * EXPERT_SUB, :] = _dot(hid, wd_s[...])


def _experts(blk_expert, blk_valid, n_used, blk_next, xs, w_gate, w_up, w_down, rows):
    cap, w = xs.shape
    _, d, d_e = w_gate.shape
    rows_map = lambda b, be, bv, nu, nx: (jnp.minimum(b, nu[0] - 1), 0)
    hbm = pl.BlockSpec(memory_space=pl.ANY)
    grid_spec = pltpu.PrefetchScalarGridSpec(
        num_scalar_prefetch=4,
        grid=(cap // rows,),
        in_specs=[pl.BlockSpec((rows, w), rows_map), hbm, hbm, hbm],
        out_specs=pl.BlockSpec((rows, w), rows_map),
        scratch_shapes=[pltpu.VMEM((d, d_e), F32), pltpu.VMEM((d, d_e), F32), pltpu.VMEM((d_e, d), F32),
                        pltpu.VMEM((d, 2 * d_e), BF16), pltpu.VMEM((d_e, d), BF16),
                        pltpu.SemaphoreType.DMA((3,))],
    )
    return pl.pallas_call(
        functools.partial(_expert_kernel, rows=rows),
        out_shape=jax.ShapeDtypeStruct((cap, w), F32),
        grid_spec=grid_spec,
        compiler_params=_params(("arbitrary",)),
        name="experts",
    )(blk_expert, blk_valid, n_used, blk_next, xs, w_gate, w_up, w_down)


def _combine_kernel(dest_ref, ys_ref, wt_ref, hp_ref, x1_ref, wsgu_ref, wsd_ref, npost_ref, gt_ref,
                    o_ref, buf, sem, *, tm):
    def start(r, c):
        for k in range(TOP_K):
            row = dest_ref[r * TOP_K + k]
            pltpu.make_async_copy(ys_ref.at[pl.ds(row, 1), :], buf.at[k, pl.ds(r, 1), :], sem).start(priority=k % 2)
        return c

    lax.fori_loop(0, tm, start, 0)

    hb = hp_ref[...].astype(BF16)
    d_sh = wsd_ref.shape[0]
    gu = _dot(hb, wsgu_ref[...])
    gate, up = gu[:, :d_sh], gu[:, d_sh:]
    shared = _dot((gate * jax.nn.sigmoid(gate) * up).astype(BF16), wsd_ref[...])

    for k in range(TOP_K):
        pltpu.make_async_copy(ys_ref.at[pl.ds(0, tm), :], buf.at[k], sem).wait()
    wt = wt_ref[...]
    fx = shared
    for k in range(TOP_K):
        fx = fx + wt[:, k:k + 1] * buf[k]
    o_ref[...] = x1_ref[...] + gt_ref[0] * _rms(fx, npost_ref[...])


def _combine(dest, ys, wt, hp, x1, wsgu, wsd, npost, gt, tokens_per_batch, tm):
    n, d = x1.shape
    w = hp.shape[1]
    tiles_per_batch = tokens_per_batch // tm
    row = lambda i: (i, 0)
    fixed = lambda i: (0, 0)
    return pl.pallas_call(
        functools.partial(_combine_kernel, tm=tm),
        out_shape=jax.ShapeDtypeStruct((n, d), F32),
        grid=(n // tm,),
        in_specs=[pl.BlockSpec((tm * TOP_K,), lambda i: (i,), memory_space=pltpu.SMEM),
                  pl.BlockSpec(memory_space=pl.ANY),
                  pl.BlockSpec((tm, TOP_K), row), pl.BlockSpec((tm, w), row), pl.BlockSpec((tm, d), row),
                  pl.BlockSpec(wsgu.shape, fixed), pl.BlockSpec(wsd.shape, fixed),
                  pl.BlockSpec((1, d), fixed), pl.BlockSpec((1, 1, d), lambda i: (i // tiles_per_batch, 0, 0))],
        out_specs=pl.BlockSpec((tm, d), row),
        scratch_shapes=[pltpu.VMEM((TOP_K, tm, w), F32), pltpu.SemaphoreType.DMA(())],
        compiler_params=_params(("arbitrary",)),
        name="combine",
    )(dest, ys, wt, hp, x1, wsgu, wsd, npost.reshape(1, d), gt)


def _layer(x, ctx, mod_x, mod_c, norm_pre_mix, norm_post_mix, norm_pre_ffn, norm_post_ffn,
           w_in, s5_a_re, s5_a_im, s5_log_dt, s5_b_re, s5_b_im, s5_c_re, s5_c_im, s5_d, w_glu, b_glu,
           sgu_norm, sgu_w, sgu_b, w_branch_s5, w_branch_sgu, w_out,
           w_router, router_bias, w_exp_gate, w_exp_up, w_exp_down, w_sh_gate, w_sh_up, w_sh_down):
    bsz, t_len, d = x.shape
    c_len = ctx.shape[1]
    n = bsz * t_len
    d_s5, d_sgu = w_glu.shape[0], sgu_norm.shape[0]
    g_n = d_s5 // S5_GROUP_CH
    n_exp = w_router.shape[1]
    chunk = S5_CHUNK
    tm = min(512, t_len)
    assert t_len % tm == 0 and tm % SGU_CHUNK == 0 and t_len % chunk == 0 and c_len % chunk == 0

    sh_m, sc_m, gt_m, sh_f, sc_f, gt_f = [v.reshape(bsz, 1, d) for v in jnp.split(mod_x, 6, axis=-1)]
    csh_m, csc_m = mod_c[:d], mod_c[d:2 * d]

    w_in_bf = w_in.astype(BF16)

    nc_lat, nc_ctx = t_len // chunk, c_len // chunk
    n_steps = max(1, (max(nc_lat, nc_ctx) - 1).bit_length())
    m_t, ke_t, kc_t, alp = _s5_tables(s5_a_re, s5_a_im, s5_log_dt, s5_b_re, s5_b_im, s5_c_re, s5_c_im,
                                      chunk, n_steps)

    ctx2 = ctx.reshape(bsz * c_len, d)
    u_ctx = _inproj_u(ctx2, norm_pre_mix, csh_m, csc_m, w_in_bf[:, :d_s5], min(512, bsz * c_len))
    zero_init = jnp.zeros((g_n, bsz, 4 * S5_STATE), F32)
    (ctx_final,) = _s5(_to_groups(u_ctx, bsz, nc_ctx, chunk, g_n), ke_t, alp, zero_init, nb=bsz, nc=nc_ctx)

    x2 = x.reshape(n, d)
    u, gu, vn, g1, g2 = _inproj(x2, norm_pre_mix, sh_m, sc_m, w_in_bf, sgu_norm, t_len, tm, d_s5, d_sgu)
    _, yg = _s5(_to_groups(u, bsz, nc_lat, chunk, g_n), ke_t, alp, ctx_final, m_t, kc_t, nb=bsz, nc=nc_lat)
    y = _from_groups(yg, bsz, nc_lat, chunk, g_n)

    ch = d_sgu // SGU_GROUPS
    per_tile = LANES // ch
    sguw = sgu_w.reshape(SGU_GROUPS // per_tile, per_tile, SGU_CHUNK, SGU_CHUNK)
    sguw = sguw.transpose(0, 2, 1, 3).reshape(SGU_GROUPS // per_tile, SGU_CHUNK, per_tile * SGU_CHUNK).astype(BF16)
    sgub = jnp.repeat(sgu_b.T, ch, axis=1)

    x1, hp, logits = _mix(y, u, gu, vn, g1, g2, x2, s5_d, w_glu.astype(BF16), b_glu, sguw, sgub,
                          w_branch_s5.astype(BF16), w_branch_sgu.astype(BF16), w_out.astype(BF16),
                          norm_post_mix, gt_m, norm_pre_ffn, sh_f, sc_f, w_router.astype(BF16), t_len, tm)

    tn = min(512, n)
    idx, wts, counts = _route(logits.T, router_bias, tn)
    rows = EXPERT_ROWS
    cnt = counts.reshape(n_exp).astype(I32)
    nblk = (cnt + rows - 1) // rows
    blk_end = jnp.cumsum(nblk)
    blk_start = blk_end - nblk
    n_blocks = (n * TOP_K) // rows + n_exp
    cap = n_blocks * rows
    b_ids = jnp.arange(n_blocks, dtype=I32)
    blk_expert = jnp.minimum(jnp.sum((blk_end[None, :] <= b_ids[:, None]).astype(I32), axis=1), n_exp - 1)
    blk_valid = jnp.clip(cnt[blk_expert] - (b_ids - blk_start[blk_expert]) * rows, 0, rows).astype(I32)
    dest = _dest(idx, (blk_start * rows).astype(F32).reshape(n_exp, 1), tn)
    dest = dest.T.reshape(n * TOP_K)

    td = min(256, n)
    xs = _dispatch(dest, hp, cap, td)
    n_used = blk_end[-1:].astype(I32)
    after = blk_end[blk_expert]
    blk_next = jnp.where(after < n_used[0], blk_expert[jnp.minimum(after, n_blocks - 1)], -1).astype(I32)
    ys = _experts(blk_expert, blk_valid, n_used, blk_next, xs, w_exp_gate, w_exp_up, w_exp_down, rows)
    wsgu = jnp.concatenate([w_sh_gate, w_sh_up], axis=1).astype(BF16)
    out = _combine(dest, ys, wts.T, hp, x1, wsgu, w_sh_down.astype(BF16), norm_post_ffn, gt_f, t_len, td)
    return out.reshape(bsz, t_len, d)


def kernel(x, c, ctx, c_ctx, w_mod, b_mod, norm_pre_mix, norm_post_mix, norm_pre_ffn, norm_post_ffn, w_in, s5_a_re, s5_a_im, s5_log_dt, s5_b_re, s5_b_im, s5_c_re, s5_c_im, s5_d, w_glu, b_glu, sgu_norm, sgu_w, sgu_b, w_branch_s5, w_branch_sgu, w_out, w_router, router_bias, w_exp_gate, w_exp_up, w_exp_down, w_sh_gate, w_sh_up, w_sh_down):
    depth = w_mod.shape[0]
    assert depth == 1, "the context stream is only carried through the last layer's S5 states"
    bsz = x.shape[0]
    pad = (-(bsz + 1)) % 8
    cpad = jnp.concatenate([c, c_ctx[None, :], jnp.zeros((pad, c.shape[1]), c.dtype)], axis=0)
    mod = _modulation(cpad, w_mod[0], b_mod[0])
    return _layer(x, ctx, mod[:bsz], mod[bsz], norm_pre_mix[0], norm_post_mix[0], norm_pre_ffn[0],
                  norm_post_ffn[0], w_in[0], s5_a_re[0], s5_a_im[0], s5_log_dt[0], s5_b_re[0], s5_b_im[0],
                  s5_c_re[0], s5_c_im[0], s5_d[0], w_glu[0], b_glu[0], sgu_norm[0], sgu_w[0], sgu_b[0],
                  w_branch_s5[0], w_branch_sgu[0], w_out[0], w_router[0], router_bias[0],
                  w_exp_gate[0], w_exp_up[0], w_exp_down[0], w_sh_gate[0], w_sh_up[0], w_sh_down[0])
```

```python
import functools
import math

import jax
import jax.numpy as jnp
from jax import lax
from jax.experimental import pallas as pl
from jax.experimental.pallas import tpu as pltpu

F32 = jnp.float32
BF16 = jnp.bfloat16
I32 = jnp.int32

EPS = 1e-6
S5_GROUP_CH = 16
S5_STATE = 64
S5_CHUNK = 64
SGU_GROUPS = 8
SGU_CHUNK = 128
N_EXPERT_GROUPS = 8
TOPK_GROUPS = 4
TOP_K = 8
ROUTE_SCALE = 2.5
LANES = 128
EXPERT_ROWS = 512
EXPERT_SUB = 256
VMEM_LIMIT = 56 * 1024 * 1024


def _params(sem):
    return pltpu.CompilerParams(dimension_semantics=sem, vmem_limit_bytes=VMEM_LIMIT)


def _rms(v, g):
    return v * lax.rsqrt(jnp.mean(v * v, axis=-1, keepdims=True) + EPS) * g


def _dot(a, b):
    return jnp.dot(a, b, preferred_element_type=F32)


def _store_row_tiles(ref, first_row, val):
    pieces = val.shape[1] // LANES
    for s in range(pieces):
        ref[pl.ds(first_row * pieces + s, val.shape[0], stride=pieces), :] = val[:, s * LANES:(s + 1) * LANES]


def _load_row_tiles(ref, first_row, n_rows, pieces, lead=()):
    return jnp.concatenate([ref[lead + (pl.ds(first_row * pieces + s, n_rows, stride=pieces), slice(None))]
                            for s in range(pieces)], axis=1)


def _mod_kernel(c_ref, w_ref, b_ref, o_ref):
    cv = c_ref[...]
    s = cv * jax.nn.sigmoid(cv)
    o_ref[...] = _dot(s.astype(BF16), w_ref[...].astype(BF16)) + b_ref[...]


def _modulation(cpad, w_mod, b_mod):
    d, n6 = w_mod.shape
    tn = 1024
    return pl.pallas_call(
        _mod_kernel,
        out_shape=jax.ShapeDtypeStruct((cpad.shape[0], n6), F32),
        grid=(n6 // tn,),
        in_specs=[pl.BlockSpec(cpad.shape, lambda j: (0, 0)),
                  pl.BlockSpec((d, tn), lambda j: (0, j)),
                  pl.BlockSpec((1, tn), lambda j: (0, j))],
        out_specs=pl.BlockSpec((cpad.shape[0], tn), lambda j: (0, j)),
        compiler_params=_params(("parallel",)),
        name="mod",
    )(cpad, w_mod, b_mod.reshape(1, n6))


def _inproj_kernel(x_ref, g_ref, sh_ref, sc_ref, w_ref, sgun_ref,
                   u_ref, gu_ref, vn_ref, g1_ref, g2_ref, *, d_s5, d_sgu, d_model):
    h = _rms(x_ref[...], g_ref[...])
    hb = (h * (1.0 + sc_ref[0]) + sh_ref[0]).astype(BF16)
    o1, o2, o3, o4 = d_s5, d_s5 + d_sgu, d_s5 + 2 * d_sgu, d_s5 + 2 * d_sgu + d_model
    u_ref[...] = _dot(hb, w_ref[:, 0:o1]).astype(BF16)
    gu_ref[...] = jax.nn.gelu(_dot(hb, w_ref[:, o1:o2])).astype(BF16)
    v = jax.nn.gelu(_dot(hb, w_ref[:, o2:o3]))
    vn_ref[...] = _rms(v, sgun_ref[...]).astype(BF16)
    g1_ref[...] = jax.nn.sigmoid(_dot(hb, w_ref[:, o3:o4])).astype(BF16)
    g2_ref[...] = jax.nn.sigmoid(_dot(hb, w_ref[:, o4:o4 + d_model])).astype(BF16)


def _inproj(x2, norm_g, shift, scale, w_in_bf, sgu_norm, tokens_per_batch, tm, d_s5, d_sgu):
    n, d = x2.shape
    tiles_per_batch = tokens_per_batch // tm
    row = lambda i: (i, 0)
    fixed = lambda i: (0, 0)
    per_batch = lambda i: (i // tiles_per_batch, 0, 0)
    return pl.pallas_call(
        functools.partial(_inproj_kernel, d_s5=d_s5, d_sgu=d_sgu, d_model=d),
        out_shape=(jax.ShapeDtypeStruct((n, d_s5), BF16), jax.ShapeDtypeStruct((n, d_sgu), BF16),
                   jax.ShapeDtypeStruct((n, d_sgu), BF16), jax.ShapeDtypeStruct((n, d), BF16),
                   jax.ShapeDtypeStruct((n, d), BF16)),
        grid=(n // tm,),
        in_specs=[pl.BlockSpec((tm, d), row), pl.BlockSpec((1, d), fixed),
                  pl.BlockSpec((1, 1, d), per_batch), pl.BlockSpec((1, 1, d), per_batch),
                  pl.BlockSpec(w_in_bf.shape, fixed), pl.BlockSpec((1, d_sgu), fixed)],
        out_specs=(pl.BlockSpec((tm, d_s5), row), pl.BlockSpec((tm, d_sgu), row),
                   pl.BlockSpec((tm, d_sgu), row), pl.BlockSpec((tm, d), row), pl.BlockSpec((tm, d), row)),
        compiler_params=_params(("parallel",)),
        name="inproj",
    )(x2, norm_g.reshape(1, d), shift, scale, w_in_bf, sgu_norm.reshape(1, d_sgu))


def _inproj_u_kernel(x_ref, g_ref, sh_ref, sc_ref, w_ref, u_ref):
    h = _rms(x_ref[...], g_ref[...])
    hb = (h * (1.0 + sc_ref[...]) + sh_ref[...]).astype(BF16)
    u_ref[...] = _dot(hb, w_ref[...]).astype(BF16)


def _inproj_u(x2, norm_g, shift, scale, w_u_bf, tm):
    n, d = x2.shape
    d_s5 = w_u_bf.shape[1]
    row = lambda i: (i, 0)
    fixed = lambda i: (0, 0)
    return pl.pallas_call(
        _inproj_u_kernel,
        out_shape=jax.ShapeDtypeStruct((n, d_s5), BF16),
        grid=(n // tm,),
        in_specs=[pl.BlockSpec((tm, d), row), pl.BlockSpec((1, d), fixed), pl.BlockSpec((1, d), fixed),
                  pl.BlockSpec((1, d), fixed), pl.BlockSpec(w_u_bf.shape, fixed)],
        out_specs=pl.BlockSpec((tm, d_s5), row),
        compiler_params=_params(("parallel",)),
        name="inproj_ctx",
    )(x2, norm_g.reshape(1, d), shift.reshape(1, d), scale.reshape(1, d), w_u_bf)


def _s5_tables(a_re, a_im, log_dt, b_re, b_im, c_re, c_im, chunk, n_steps):
    hi = lax.Precision.HIGHEST
    g_n, p_n, j_n = a_re.shape[1], a_re.shape[2], b_re.shape[3]
    dt = jnp.exp(log_dt)[..., None]
    lam_re, lam_im = dt * a_re, dt * a_im
    ab_re, ab_im = jnp.exp(lam_re) * jnp.cos(lam_im), jnp.exp(lam_re) * jnp.sin(lam_im)
    den = a_re * a_re + a_im * a_im
    q_re = ((ab_re - 1.0) * a_re + ab_im * a_im) / den
    q_im = (ab_im * a_re - (ab_re - 1.0) * a_im) / den
    bb_re = q_re[..., None] * b_re - q_im[..., None] * b_im
    bb_im = q_re[..., None] * b_im + q_im[..., None] * b_re
    k = jnp.arange(chunk + 1, dtype=F32)[:, None, None, None]
    mag = jnp.exp(k * lam_re[None])
    pw_re, pw_im = mag * jnp.cos(k * lam_im[None]), mag * jnp.sin(k * lam_im[None])

    def lag_kernels(d):
        x_re = pw_re[:chunk, d, :, :, None] * bb_re[d][None] - pw_im[:chunk, d, :, :, None] * bb_im[d][None]
        x_im = pw_re[:chunk, d, :, :, None] * bb_im[d][None] + pw_im[:chunk, d, :, :, None] * bb_re[d][None]
        return (jnp.einsum('gjp,kgpi->gkij', c_re[d], x_re, precision=hi)
                - jnp.einsum('gjp,kgpi->gkij', c_im[d], x_im, precision=hi))

    kf, kb = lag_kernels(0), lag_kernels(1)
    lags = jnp.concatenate([kb[:, 1:][:, ::-1], kf[:, 0:1] + kb[:, 0:1], kf[:, 1:],
                            jnp.zeros_like(kf[:, 0:1])], axis=1)
    lags = lags.transpose(0, 2, 3, 1)

    pf_re, pf_im = pw_re[:chunk, 0][::-1], pw_im[:chunk, 0][::-1]
    pb_re, pb_im = pw_re[:chunk, 1], pw_im[:chunk, 1]

    def state_in(p_re, p_im, d):
        e_re = p_re[..., None] * bb_re[d][None] - p_im[..., None] * bb_im[d][None]
        e_im = p_re[..., None] * bb_im[d][None] + p_im[..., None] * bb_re[d][None]
        return e_re.transpose(1, 3, 0, 2), e_im.transpose(1, 3, 0, 2)

    ef_re, ef_im = state_in(pf_re, pf_im, 0)
    eb_re, eb_im = state_in(pb_re, pb_im, 1)
    ke = jnp.concatenate([ef_re, eb_re, ef_im, eb_im], axis=-1).reshape(g_n, chunk * j_n, 4 * p_n)

    qf_re, qf_im = pw_re[1:chunk + 1, 0], pw_im[1:chunk + 1, 0]
    qb_re, qb_im = pw_re[1:chunk + 1, 1][::-1], pw_im[1:chunk + 1, 1][::-1]

    def state_out(p_re, p_im, d):
        cr, ci = c_re[d].transpose(0, 2, 1), c_im[d].transpose(0, 2, 1)
        pr, pi = p_re.transpose(1, 2, 0), p_im.transpose(1, 2, 0)
        o_re = pr[:, :, None, :] * cr[..., None] - pi[:, :, None, :] * ci[..., None]
        o_im = pr[:, :, None, :] * ci[..., None] + pi[:, :, None, :] * cr[..., None]
        return o_re, o_im

    of_re, of_im = state_out(qf_re, qf_im, 0)
    ob_re, ob_im = state_out(qb_re, qb_im, 1)
    kc = jnp.concatenate([of_re, ob_re, -of_im, -ob_im], axis=1).reshape(g_n, 4 * p_n, chunk * j_n)

    e = (chunk * (2.0 ** jnp.arange(n_steps, dtype=F32)))[:, None, None, None]
    mg = jnp.exp(e * lam_re[None])
    al_re, al_im = mg * jnp.cos(e * lam_im[None]), mg * jnp.sin(e * lam_im[None])
    alp = jnp.concatenate([al_re[:, 0], al_re[:, 1], al_im[:, 0], al_im[:, 1]], axis=-1)
    return lags, ke.astype(BF16), kc.astype(BF16), alp.transpose(1, 0, 2)


def _build_toeplitz(lag_ref, m_s, chunk):
    j_n = lag_ref.shape[1]
    per_tile = LANES // chunk
    low = lax.broadcasted_iota(I32, (chunk, LANES), 1) < chunk
    for i in range(j_n):
        for q in range(j_n // per_tile):
            tile = None
            for p in range(per_tile):
                j = q * per_tile + p
                row = jnp.broadcast_to(lag_ref[0, i, j:j + 1, :], (chunk, LANES))
                rot = pltpu.roll(row, (p * chunk + LANES - (chunk - 1)) % LANES, 1, stride=1, stride_axis=0)
                tile = rot if tile is None else jnp.where(low, tile, rot)
            m_s[i * chunk:(i + 1) * chunk, q * LANES:(q + 1) * LANES] = tile.astype(BF16)


def _s5_kernel(*refs, nb, nc, readout):
    if readout:
        u_ref, ke_ref, alp_ref, init_ref, lag_ref, kc_ref, fin_ref, y_ref, m_s = refs
    else:
        u_ref, ke_ref, alp_ref, init_ref, fin_ref = refs
    rows = nb * nc
    half = 2 * S5_STATE
    u = u_ref[0]
    e = _dot(u, ke_ref[0])
    er, ei = e[:, :half], e[:, half:]
    r_idx = lax.broadcasted_iota(I32, (rows, half), 0)
    if nc & (nc - 1) == 0:
        n_idx, b_idx = r_idx & (nc - 1), r_idx >> (nc.bit_length() - 1)
    else:
        n_idx, b_idx = lax.rem(r_idx, nc), lax.div(r_idx, nc)
    is_f = lax.broadcasted_iota(I32, (rows, half), 1) < S5_STATE
    seen = jnp.where(is_f, n_idx, nc - 1 - n_idx)
    init = init_ref[0]
    init_r = jnp.zeros((rows, half), F32)
    init_i = jnp.zeros((rows, half), F32)
    for b in range(nb):
        init_r = jnp.where(b_idx == b, init[b:b + 1, :half], init_r)
        init_i = jnp.where(b_idx == b, init[b:b + 1, half:], init_i)
    alp = alp_ref[0]
    ar, ai = alp[0:1, :half], alp[0:1, half:]
    er = er + jnp.where(seen == 0, ar * init_r - ai * init_i, 0.0)
    ei = ei + jnp.where(seen == 0, ar * init_i + ai * init_r, 0.0)

    def from_prev(v, dist):
        return jnp.where(is_f, pltpu.roll(v, dist, 0), pltpu.roll(v, rows - dist, 0))

    d, s = 1, 0
    while d < nc:
        ar, ai = alp[s:s + 1, :half], alp[s:s + 1, half:]
        sr = jnp.where(seen >= d, from_prev(er, d), 0.0)
        si = jnp.where(seen >= d, from_prev(ei, d), 0.0)
        er, ei = er + (ar * sr - ai * si), ei + (ar * si + ai * sr)
        d, s = d * 2, s + 1
    is_f_row = lax.broadcasted_iota(I32, (1, half), 1) < S5_STATE
    for b in range(nb):
        lo, hi = b * nc, b * nc + nc - 1
        fin_ref[0, b:b + 1, :half] = jnp.where(is_f_row, er[hi:hi + 1], er[lo:lo + 1])
        fin_ref[0, b:b + 1, half:] = jnp.where(is_f_row, ei[hi:hi + 1], ei[lo:lo + 1])
    if readout:
        if nc > 1:
            sin_r = jnp.where(seen >= 1, from_prev(er, 1), init_r)
            sin_i = jnp.where(seen >= 1, from_prev(ei, 1), init_i)
        else:
            sin_r, sin_i = init_r, init_i
        s_in = jnp.concatenate([sin_r, sin_i], axis=1).astype(BF16)
        _build_toeplitz(lag_ref, m_s, S5_CHUNK)
        y_ref[0] = (_dot(u, m_s[...]) + _dot(s_in, kc_ref[0])).astype(BF16)


def _s5(ug, ke, alp, init, m=None, kc=None, *, nb, nc):
    g_n, rows, width = ug.shape
    readout = m is not None
    grp = lambda g: (g, 0, 0)
    in_specs = [pl.BlockSpec((1, rows, width), grp), pl.BlockSpec((1,) + ke.shape[1:], grp),
                pl.BlockSpec((1,) + alp.shape[1:], grp), pl.BlockSpec((1,) + init.shape[1:], grp)]
    out_shape = [jax.ShapeDtypeStruct(init.shape, F32)]
    out_specs = [pl.BlockSpec((1,) + init.shape[1:], grp)]
    args = [ug, ke, alp, init]
    scratch = []
    if readout:
        assert m.shape[3] == LANES and 2 * S5_CHUNK == LANES
        in_specs += [pl.BlockSpec((1,) + m.shape[1:], lambda g: (g, 0, 0, 0)),
                     pl.BlockSpec((1,) + kc.shape[1:], grp)]
        out_shape.append(jax.ShapeDtypeStruct(ug.shape, BF16))
        out_specs.append(pl.BlockSpec((1, rows, width), grp))
        args += [m, kc]
        scratch = [pltpu.VMEM((width, width), BF16)]
    return pl.pallas_call(
        functools.partial(_s5_kernel, nb=nb, nc=nc, readout=readout),
        out_shape=tuple(out_shape), grid=(g_n,), in_specs=in_specs, out_specs=tuple(out_specs),
        scratch_shapes=scratch,
        compiler_params=_params(("parallel",)),
        name="s5_readout" if readout else "s5_state",
    )(*args)


def _to_groups(u, nb, nc, chunk, g_n):
    j_n = u.shape[1] // g_n
    return u.reshape(nb * nc, chunk, g_n, j_n).transpose(2, 0, 3, 1).reshape(g_n, nb * nc, j_n * chunk)


def _from_groups(y, nb, nc, chunk, g_n):
    j_n = y.shape[2] // chunk
    return y.reshape(g_n, nb * nc, j_n, chunk).transpose(1, 3, 0, 2).reshape(nb * nc * chunk, g_n * j_n)


def _mix_kernel(y_ref, u_ref, gu_ref, vn_ref, g1_ref, g2_ref, x_ref,
                d_ref, wglu_ref, bglu_ref, sguw_ref, sgub_ref, wb1_ref, wb2_ref, wout_ref,
                npost_ref, gt_ref, npre_ref, shf_ref, scf_ref, wr_ref,
                x1_ref, hp_ref, lg_ref, *, tm):
    y = jax.nn.gelu(y_ref[...].astype(F32) + d_ref[...] * u_ref[...].astype(F32))
    y_s5 = y * jax.nn.sigmoid(_dot(y.astype(BF16), wglu_ref[...]) + bglu_ref[...])

    lane = lax.broadcasted_iota(I32, (1, LANES), 1)
    m_lo = (lane < LANES // 2).astype(F32).astype(BF16)
    m_hi = (lane >= LANES // 2).astype(F32).astype(BF16)
    vn = vn_ref[...]
    chunks = []
    for c in range(tm // SGU_CHUNK):
        vc = vn[c * SGU_CHUNK:(c + 1) * SGU_CHUNK, :]
        tiles = []
        for q in range(vc.shape[1] // LANES):
            vt = vc[:, q * LANES:(q + 1) * LANES]
            rhs = jnp.concatenate([vt * m_lo, vt * m_hi], axis=0)
            tiles.append(_dot(sguw_ref[q], rhs))
        chunks.append(jnp.concatenate(tiles, axis=1) + sgub_ref[...])
    mixed = jnp.concatenate(chunks, axis=0)
    y_sgu = gu_ref[...].astype(F32) * mixed

    merged = (g1_ref[...].astype(F32) * _dot(y_s5.astype(BF16), wb1_ref[...])
              + g2_ref[...].astype(F32) * _dot(y_sgu.astype(BF16), wb2_ref[...]))
    mx = _dot(merged.astype(BF16), wout_ref[...])
    x1 = x_ref[...] + gt_ref[0] * _rms(mx, npost_ref[...])
    x1_ref[...] = x1
    hp = _rms(x1, npre_ref[...]) * (1.0 + scf_ref[0]) + shf_ref[0]
    _store_row_tiles(hp_ref, 0, hp)
    lg_ref[...] = _dot(hp.astype(BF16), wr_ref[...])


def _mix(y, u, gu, vn, g1, g2, x2, s5_d, wglu, bglu, sguw, sgub, wb1, wb2, wout,
         npost, gt, npre, shf, scf, wr, tokens_per_batch, tm):
    n, d = x2.shape
    d_s5, d_sgu, n_exp = y.shape[1], gu.shape[1], wr.shape[1]
    tiles_per_batch = tokens_per_batch // tm
    row = lambda i: (i, 0)
    fixed = lambda i: (0, 0)
    fixed3 = lambda i: (0, 0, 0)
    per_batch = lambda i: (i // tiles_per_batch, 0, 0)
    vec = lambda w: pl.BlockSpec((1, w), fixed)
    return pl.pallas_call(
        functools.partial(_mix_kernel, tm=tm),
        out_shape=(jax.ShapeDtypeStruct((n, d), F32), jax.ShapeDtypeStruct((n * d // LANES, LANES), F32),
                   jax.ShapeDtypeStruct((n, n_exp), F32)),
        grid=(n // tm,),
        in_specs=[pl.BlockSpec((tm, d_s5), row), pl.BlockSpec((tm, d_s5), row), pl.BlockSpec((tm, d_sgu), row),
                  pl.BlockSpec((tm, d_sgu), row), pl.BlockSpec((tm, d), row), pl.BlockSpec((tm, d), row),
                  pl.BlockSpec((tm, d), row),
                  vec(d_s5), pl.BlockSpec(wglu.shape, fixed), vec(d_s5),
                  pl.BlockSpec(sguw.shape, fixed3), pl.BlockSpec(sgub.shape, fixed),
                  pl.BlockSpec(wb1.shape, fixed), pl.BlockSpec(wb2.shape, fixed), pl.BlockSpec(wout.shape, fixed),
                  vec(d), pl.BlockSpec((1, 1, d), per_batch), vec(d),
                  pl.BlockSpec((1, 1, d), per_batch), pl.BlockSpec((1, 1, d), per_batch),
                  pl.BlockSpec(wr.shape, fixed)],
        out_specs=(pl.BlockSpec((tm, d), row), pl.BlockSpec((tm * d // LANES, LANES), row),
                   pl.BlockSpec((tm, n_exp), row)),
        compiler_params=_params(("parallel",)),
        name="mix",
    )(y, u, gu, vn, g1, g2, x2, s5_d.reshape(1, d_s5), wglu, bglu.reshape(1, d_s5), sguw, sgub,
      wb1, wb2, wout, npost.reshape(1, d), gt, npre.reshape(1, d), shf, scf, wr)


def _route_kernel(lg_ref, bias_ref, idx_ref, w_ref, cnt_ref, *, n_exp, tn):
    scores = jax.nn.sigmoid(lg_ref[...])
    sel = scores + bias_ref[...]
    per_group = n_exp // N_EXPERT_GROUPS
    neg = jnp.float32(-jnp.inf)
    gs = []
    for g in range(N_EXPERT_GROUPS):
        sg = sel[g * per_group:(g + 1) * per_group, :]
        m1 = jnp.max(sg, axis=0, keepdims=True)
        is_m1 = sg == m1
        n_m1 = jnp.sum(is_m1.astype(F32), axis=0, keepdims=True)
        rest = jnp.max(jnp.where(is_m1, neg, sg), axis=0, keepdims=True)
        gs.append(m1 + jnp.where(n_m1 >= 2.0, m1, rest))
    gsm = jnp.concatenate(gs, axis=0)
    g_iota = lax.broadcasted_iota(I32, gsm.shape, 0)
    e_iota = lax.broadcasted_iota(I32, sel.shape, 0).astype(F32)
    masked = []
    for g in range(N_EXPERT_GROUPS):
        mine = gsm[g:g + 1, :]
        beats = jnp.where(gsm > mine, 1.0, jnp.where(gsm == mine, jnp.where(g_iota < g, 1.0, 0.0), 0.0))
        n_beats = jnp.sum(beats, axis=0, keepdims=True)
        masked.append(jnp.where(n_beats < float(TOPK_GROUPS), sel[g * per_group:(g + 1) * per_group, :], neg))
    selm = jnp.concatenate(masked, axis=0)
    picked, vals = [], []
    hot = jnp.zeros(sel.shape, F32)
    for _ in range(TOP_K):
        m = jnp.max(selm, axis=0, keepdims=True)
        first = jnp.min(jnp.where(selm == m, e_iota, float(n_exp)), axis=0, keepdims=True)
        one = e_iota == first
        picked.append(first)
        vals.append(jnp.sum(jnp.where(one, scores, 0.0), axis=0, keepdims=True))
        hot = hot + jnp.where(one, 1.0, 0.0)
        selm = jnp.where(one, neg, selm)
    idx_ref[...] = jnp.concatenate(picked, axis=0).astype(I32)
    wv = jnp.concatenate(vals, axis=0)
    w_ref[...] = wv / jnp.sum(wv, axis=0, keepdims=True) * ROUTE_SCALE

    @pl.when(pl.program_id(0) == 0)
    def _():
        cnt_ref[...] = jnp.zeros_like(cnt_ref)

    cnt_ref[...] += jnp.sum(hot, axis=1, keepdims=True)


def _route(logits_t, bias, tn):
    n_exp, n = logits_t.shape
    col = lambda i: (0, i)
    fixed = lambda i: (0, 0)
    return pl.pallas_call(
        functools.partial(_route_kernel, n_exp=n_exp, tn=tn),
        out_shape=(jax.ShapeDtypeStruct((TOP_K, n), I32), jax.ShapeDtypeStruct((TOP_K, n), F32),
                   jax.ShapeDtypeStruct((n_exp, 1), F32)),
        grid=(n // tn,),
        in_specs=[pl.BlockSpec((n_exp, tn), col), pl.BlockSpec((n_exp, 1), fixed)],
        out_specs=(pl.BlockSpec((TOP_K, tn), col), pl.BlockSpec((TOP_K, tn), col),
                   pl.BlockSpec((n_exp, 1), fixed)),
        compiler_params=_params(("arbitrary",)),
        name="route",
    )(logits_t, bias.reshape(n_exp, 1))


def _dest_kernel(idx_ref, start_ref, dest_ref, carry_ref, *, n_exp, tn):
    @pl.when(pl.program_id(0) == 0)
    def _():
        carry_ref[...] = start_ref[...]

    idx = idx_ref[...]
    e_iota = lax.broadcasted_iota(I32, (n_exp, tn), 0)
    hot = jnp.zeros((n_exp, tn), F32)
    for k in range(TOP_K):
        hot = hot + jnp.where(e_iota == idx[k:k + 1, :], 1.0, 0.0)
    before = jnp.where(lax.broadcasted_iota(I32, (tn, tn), 0) < lax.broadcasted_iota(I32, (tn, tn), 1), 1.0, 0.0)
    rank = _dot(hot.astype(BF16), before.astype(BF16)) + carry_ref[...]
    dest_ref[...] = jnp.concatenate(
        [jnp.sum(jnp.where(e_iota == idx[k:k + 1, :], rank, 0.0), axis=0, keepdims=True)
         for k in range(TOP_K)], axis=0).astype(I32)
    carry_ref[...] += jnp.sum(hot, axis=1, keepdims=True)


def _dest(idx, start, tn):
    n = idx.shape[1]
    n_exp = start.shape[0]
    col = lambda i: (0, i)
    return pl.pallas_call(
        functools.partial(_dest_kernel, n_exp=n_exp, tn=tn),
        out_shape=jax.ShapeDtypeStruct((TOP_K, n), I32),
        grid=(n // tn,),
        in_specs=[pl.BlockSpec((TOP_K, tn), col), pl.BlockSpec((n_exp, 1), lambda i: (0, 0))],
        out_specs=pl.BlockSpec((TOP_K, tn), col),
        scratch_shapes=[pltpu.VMEM((n_exp, 1), F32)],
        compiler_params=_params(("arbitrary",)),
        name="dest",
    )(idx, start)


def _dispatch_kernel(dest_ref, hp_ref, xs_ref, sem, *, tm, pieces):
    def start(r, c):
        src = hp_ref.at[pl.ds(pl.multiple_of(r * pieces, pieces), pieces), :]
        for k in range(TOP_K):
            row = pl.multiple_of(dest_ref[r * TOP_K + k], pieces)
            pltpu.make_async_copy(src, xs_ref.at[pl.ds(row, pieces), :], sem).start(priority=k % 2)
        return c

    lax.fori_loop(0, tm, start, 0)
    for k in range(TOP_K):
        pltpu.make_async_copy(hp_ref, xs_ref.at[pl.ds(0, tm * pieces), :], sem).wait()


def _dispatch(dest, hp, cap, tm, pieces):
    n = hp.shape[0] // pieces
    return pl.pallas_call(
        functools.partial(_dispatch_kernel, tm=tm, pieces=pieces),
        out_shape=jax.ShapeDtypeStruct((cap * pieces, LANES), hp.dtype),
        grid=(n // tm,),
        in_specs=[pl.BlockSpec((tm * TOP_K,), lambda i: (i,), memory_space=pltpu.SMEM),
                  pl.BlockSpec((tm * pieces, LANES), lambda i: (i, 0))],
        out_specs=pl.BlockSpec(memory_space=pl.ANY),
        scratch_shapes=[pltpu.SemaphoreType.DMA(())],
        compiler_params=_params(("arbitrary",)),
        name="dispatch",
    )(dest, hp)


def _expert_kernel(be_ref, bv_ref, nu_ref, nx_ref, xs_ref, wg_hbm, wu_hbm, wd_hbm, ys_ref,
                   wg_f, wu_f, wd_f, wgu_s, wd_s, sem, *, rows):
    b = pl.program_id(0)
    valid = bv_ref[b]
    changed = jnp.logical_or(b == 0, be_ref[b] != be_ref[jnp.maximum(b - 1, 0)])
    d_e = wg_hbm.shape[2]

    def weight_copies(e):
        return (pltpu.make_async_copy(wg_hbm.at[e], wg_f, sem.at[0]),
                pltpu.make_async_copy(wu_hbm.at[e], wu_f, sem.at[1]),
                pltpu.make_async_copy(wd_hbm.at[e], wd_f, sem.at[2]))

    @pl.when(jnp.logical_and(changed, valid > 0))
    def _():
        @pl.when(b == 0)
        def _():
            for cp in weight_copies(be_ref[b]):
                cp.start()

        for cp in weight_copies(be_ref[b]):
            cp.wait()
        wgu_s[:, :d_e] = wg_f[...].astype(BF16)
        wgu_s[:, d_e:] = wu_f[...].astype(BF16)
        wd_s[...] = wd_f[...].astype(BF16)

        @pl.when(nx_ref[b] >= 0)
        def _():
            for cp in weight_copies(nx_ref[b]):
                cp.start()

    for s in range(rows // EXPERT_SUB):
        @pl.when(valid > s * EXPERT_SUB)
        def _():
            x = _load_row_tiles(xs_ref, s * EXPERT_SUB, EXPERT_SUB, wg_hbm.shape[1] // LANES)
            r_iota = lax.broadcasted_iota(I32, x.shape, 0) + s * EXPERT_SUB
            xb = jnp.where(r_iota < valid, x, 0.0).astype(BF16)
            gu = _dot(xb, wgu_s[...])
            gate, up = gu[:, :d_e], gu[:, d_e:]
            hid = (gate * jax.nn.sigmoid(gate) * up).astype(BF16)
            _store_row_tiles(ys_ref, s * EXPERT_SUB, _dot(hid, wd_s[...]))


def _experts(blk_expert, blk_valid, n_used, blk_next, xs, w_gate, w_up, w_down, rows):
    _, d, d_e = w_gate.shape
    w = LANES
    cap = xs.shape[0] // (d // LANES)
    rows_phys = rows * (d // LANES)
    rows_map = lambda b, be, bv, nu, nx: (jnp.minimum(b, nu[0] - 1), 0)
    hbm = pl.BlockSpec(memory_space=pl.ANY)
    grid_spec = pltpu.PrefetchScalarGridSpec(
        num_scalar_prefetch=4,
        grid=(cap // rows,),
        in_specs=[pl.BlockSpec((rows_phys, w), rows_map), hbm, hbm, hbm],
        out_specs=pl.BlockSpec((rows_phys, w), rows_map),
        scratch_shapes=[pltpu.VMEM((d, d_e), F32), pltpu.VMEM((d, d_e), F32), pltpu.VMEM((d_e, d), F32),
                        pltpu.VMEM((d, 2 * d_e), BF16), pltpu.VMEM((d_e, d), BF16),
                        pltpu.SemaphoreType.DMA((3,))],
    )
    return pl.pallas_call(
        functools.partial(_expert_kernel, rows=rows),
        out_shape=jax.ShapeDtypeStruct(xs.shape, F32),
        grid_spec=grid_spec,
        compiler_params=_params(("arbitrary",)),
        name="experts",
    )(blk_expert, blk_valid, n_used, blk_next, xs, w_gate, w_up, w_down)


def _combine_kernel(dest_ref, ys_ref, wt_ref, hp_ref, x1_ref, wsgu_ref, wsd_ref, npost_ref, gt_ref,
                    o_ref, buf, sem, *, tm, pieces):
    def start(r, c):
        slot = pl.ds(pl.multiple_of(r * pieces, pieces), pieces)
        for k in range(TOP_K):
            row = pl.multiple_of(dest_ref[r * TOP_K + k], pieces)
            pltpu.make_async_copy(ys_ref.at[pl.ds(row, pieces), :], buf.at[k, slot, :], sem).start(priority=k % 2)
        return c

    lax.fori_loop(0, tm, start, 0)

    hb = _load_row_tiles(hp_ref, 0, tm, pieces).astype(BF16)
    d_sh = wsd_ref.shape[0]
    gu = _dot(hb, wsgu_ref[...])
    gate, up = gu[:, :d_sh], gu[:, d_sh:]
    shared = _dot((gate * jax.nn.sigmoid(gate) * up).astype(BF16), wsd_ref[...])

    for k in range(TOP_K):
        pltpu.make_async_copy(ys_ref.at[pl.ds(0, tm * pieces), :], buf.at[k], sem).wait()
    wt = wt_ref[...]
    fx = shared
    for k in range(TOP_K):
        fx = fx + wt[:, k:k + 1] * _load_row_tiles(buf, 0, tm, pieces, lead=(k,))
    o_ref[...] = x1_ref[...] + gt_ref[0] * _rms(fx, npost_ref[...])


def _combine(dest, ys, wt, hp, x1, wsgu, wsd, npost, gt, tokens_per_batch, tm):
    n, d = x1.shape
    pieces = d // LANES
    w = LANES
    tiles_per_batch = tokens_per_batch // tm
    row = lambda i: (i, 0)
    fixed = lambda i: (0, 0)
    return pl.pallas_call(
        functools.partial(_combine_kernel, tm=tm, pieces=pieces),
        out_shape=jax.ShapeDtypeStruct((n, d), F32),
        grid=(n // tm,),
        in_specs=[pl.BlockSpec((tm * TOP_K,), lambda i: (i,), memory_space=pltpu.SMEM),
                  pl.BlockSpec(memory_space=pl.ANY),
                  pl.BlockSpec((tm, TOP_K), row), pl.BlockSpec((tm * pieces, w), row), pl.BlockSpec((tm, d), row),
                  pl.BlockSpec(wsgu.shape, fixed), pl.BlockSpec(wsd.shape, fixed),
                  pl.BlockSpec((1, d), fixed), pl.BlockSpec((1, 1, d), lambda i: (i // tiles_per_batch, 0, 0))],
        out_specs=pl.BlockSpec((tm, d), row),
        scratch_shapes=[pltpu.VMEM((TOP_K, tm * pieces, w), F32), pltpu.SemaphoreType.DMA(())],
        compiler_params=_params(("arbitrary",)),
        name="combine",
    )(dest, ys, wt, hp, x1, wsgu, wsd, npost.reshape(1, d), gt)


def _layer(x, ctx, mod_x, mod_c, norm_pre_mix, norm_post_mix, norm_pre_ffn, norm_post_ffn,
           w_in, s5_a_re, s5_a_im, s5_log_dt, s5_b_re, s5_b_im, s5_c_re, s5_c_im, s5_d, w_glu, b_glu,
           sgu_norm, sgu_w, sgu_b, w_branch_s5, w_branch_sgu, w_out,
           w_router, router_bias, w_exp_gate, w_exp_up, w_exp_down, w_sh_gate, w_sh_up, w_sh_down):
    bsz, t_len, d = x.shape
    c_len = ctx.shape[1]
    n = bsz * t_len
    d_s5, d_sgu = w_glu.shape[0], sgu_norm.shape[0]
    g_n = d_s5 // S5_GROUP_CH
    n_exp = w_router.shape[1]
    chunk = S5_CHUNK
    tm = min(512, t_len)
    assert t_len % tm == 0 and tm % SGU_CHUNK == 0 and t_len % chunk == 0 and c_len % chunk == 0

    sh_m, sc_m, gt_m, sh_f, sc_f, gt_f = [v.reshape(bsz, 1, d) for v in jnp.split(mod_x, 6, axis=-1)]
    csh_m, csc_m = mod_c[:d], mod_c[d:2 * d]

    w_in_bf = w_in.astype(BF16)

    nc_lat, nc_ctx = t_len // chunk, c_len // chunk
    n_steps = max(1, (max(nc_lat, nc_ctx) - 1).bit_length())
    m_t, ke_t, kc_t, alp = _s5_tables(s5_a_re, s5_a_im, s5_log_dt, s5_b_re, s5_b_im, s5_c_re, s5_c_im,
                                      chunk, n_steps)

    ctx2 = ctx.reshape(bsz * c_len, d)
    u_ctx = _inproj_u(ctx2, norm_pre_mix, csh_m, csc_m, w_in_bf[:, :d_s5], min(512, bsz * c_len))
    zero_init = jnp.zeros((g_n, bsz, 4 * S5_STATE), F32)
    (ctx_final,) = _s5(_to_groups(u_ctx, bsz, nc_ctx, chunk, g_n), ke_t, alp, zero_init, nb=bsz, nc=nc_ctx)

    x2 = x.reshape(n, d)
    u, gu, vn, g1, g2 = _inproj(x2, norm_pre_mix, sh_m, sc_m, w_in_bf, sgu_norm, t_len, tm, d_s5, d_sgu)
    _, yg = _s5(_to_groups(u, bsz, nc_lat, chunk, g_n), ke_t, alp, ctx_final, m_t, kc_t, nb=bsz, nc=nc_lat)
    y = _from_groups(yg, bsz, nc_lat, chunk, g_n)

    ch = d_sgu // SGU_GROUPS
    per_tile = LANES // ch
    sguw = sgu_w.reshape(SGU_GROUPS // per_tile, per_tile, SGU_CHUNK, SGU_CHUNK)
    sguw = sguw.transpose(0, 2, 1, 3).reshape(SGU_GROUPS // per_tile, SGU_CHUNK, per_tile * SGU_CHUNK).astype(BF16)
    sgub = jnp.repeat(sgu_b.T, ch, axis=1)

    x1, hp, logits = _mix(y, u, gu, vn, g1, g2, x2, s5_d, w_glu.astype(BF16), b_glu, sguw, sgub,
                          w_branch_s5.astype(BF16), w_branch_sgu.astype(BF16), w_out.astype(BF16),
                          norm_post_mix, gt_m, norm_pre_ffn, sh_f, sc_f, w_router.astype(BF16), t_len, tm)

    tn = min(512, n)
    idx, wts, counts = _route(logits.T, router_bias, tn)
    rows = EXPERT_ROWS
    cnt = counts.reshape(n_exp).astype(I32)
    nblk = (cnt + rows - 1) // rows
    blk_end = jnp.cumsum(nblk)
    blk_start = blk_end - nblk
    n_blocks = (n * TOP_K) // rows + n_exp
    cap = n_blocks * rows
    b_ids = jnp.arange(n_blocks, dtype=I32)
    blk_expert = jnp.minimum(jnp.sum((blk_end[None, :] <= b_ids[:, None]).astype(I32), axis=1), n_exp - 1)
    blk_valid = jnp.clip(cnt[blk_expert] - (b_ids - blk_start[blk_expert]) * rows, 0, rows).astype(I32)
    dest = _dest(idx, (blk_start * rows).astype(F32).reshape(n_exp, 1), tn)
    pieces = d // LANES
    dest = dest.T.reshape(n * TOP_K) * pieces

    td = min(256, n)
    xs = _dispatch(dest, hp, cap, td, pieces)
    n_used = blk_end[-1:].astype(I32)
    after = blk_end[blk_expert]
    blk_next = jnp.where(after < n_used[0], blk_expert[jnp.minimum(after, n_blocks - 1)], -1).astype(I32)
    ys = _experts(blk_expert, blk_valid, n_used, blk_next, xs, w_exp_gate, w_exp_up, w_exp_down, rows)
    wsgu = jnp.concatenate([w_sh_gate, w_sh_up], axis=1).astype(BF16)
    out = _combine(dest, ys, wts.T, hp, x1, wsgu, w_sh_down.astype(BF16), norm_post_ffn, gt_f, t_len, td)
    return out.reshape(bsz, t_len, d)


def kernel(x, c, ctx, c_ctx, w_mod, b_mod, norm_pre_mix, norm_post_mix, norm_pre_ffn, norm_post_ffn, w_in, s5_a_re, s5_a_im, s5_log_dt, s5_b_re, s5_b_im, s5_c_re, s5_c_im, s5_d, w_glu, b_glu, sgu_norm, sgu_w, sgu_b, w_branch_s5, w_branch_sgu, w_out, w_router, router_bias, w_exp_gate, w_exp_up, w_exp_down, w_sh_gate, w_sh_up, w_sh_down):
    depth = w_mod.shape[0]
    assert depth == 1, "the context stream is only carried through the last layer's S5 states"
    bsz = x.shape[0]
    pad = (-(bsz + 1)) % 8
    cpad = jnp.concatenate([c, c_ctx[None, :], jnp.zeros((pad, c.shape[1]), c.dtype)], axis=0)
    mod = _modulation(cpad, w_mod[0], b_mod[0])
    return _layer(x, ctx, mod[:bsz], mod[bsz], norm_pre_mix[0], norm_post_mix[0], norm_pre_ffn[0],
                  norm_post_ffn[0], w_in[0], s5_a_re[0], s5_a_im[0], s5_log_dt[0], s5_b_re[0], s5_b_im[0],
                  s5_c_re[0], s5_c_im[0], s5_d[0], w_glu[0], b_glu[0], sgu_norm[0], sgu_w[0], sgu_b[0],
                  w_branch_s5[0], w_branch_sgu[0], w_out[0], w_router[0], router_bias[0],
                  w_exp_gate[0], w_exp_up[0], w_exp_down[0], w_sh_gate[0], w_sh_up[0], w_sh_down[0])
```

```python
import functools
import math

import jax
import jax.numpy as jnp
from jax import lax
from jax.experimental import pallas as pl
from jax.experimental.pallas import tpu as pltpu

F32 = jnp.float32
BF16 = jnp.bfloat16
I32 = jnp.int32

EPS = 1e-6
S5_GROUP_CH = 16
S5_STATE = 64
S5_CHUNK = 64
SGU_GROUPS = 8
SGU_CHUNK = 128
N_EXPERT_GROUPS = 8
TOPK_GROUPS = 4
TOP_K = 8
ROUTE_SCALE = 2.5
LANES = 128
EXPERT_ROWS = 512
EXPERT_SUB = 256
VMEM_LIMIT = 56 * 1024 * 1024


def _params(sem):
    return pltpu.CompilerParams(dimension_semantics=sem, vmem_limit_bytes=VMEM_LIMIT)


def _rms(v, g):
    return v * lax.rsqrt(jnp.mean(v * v, axis=-1, keepdims=True) + EPS) * g


def _dot(a, b):
    return jnp.dot(a, b, preferred_element_type=F32)


def _store_row_tiles(ref, first_row, val):
    pieces = val.shape[1] // LANES
    for s in range(pieces):
        ref[pl.ds(first_row * pieces + s, val.shape[0], stride=pieces), :] = val[:, s * LANES:(s + 1) * LANES]


def _load_row_tiles(ref, first_row, n_rows, pieces, lead=()):
    return jnp.concatenate([ref[lead + (pl.ds(first_row * pieces + s, n_rows, stride=pieces), slice(None))]
                            for s in range(pieces)], axis=1)


def _mod_kernel(c_ref, w_ref, b_ref, o_ref):
    cv = c_ref[...]
    s = cv * jax.nn.sigmoid(cv)
    o_ref[...] = _dot(s.astype(BF16), w_ref[...].astype(BF16)) + b_ref[...]


def _modulation(cpad, w_mod, b_mod):
    d, n6 = w_mod.shape
    tn = 1024
    return pl.pallas_call(
        _mod_kernel,
        out_shape=jax.ShapeDtypeStruct((cpad.shape[0], n6), F32),
        grid=(n6 // tn,),
        in_specs=[pl.BlockSpec(cpad.shape, lambda j: (0, 0)),
                  pl.BlockSpec((d, tn), lambda j: (0, j)),
                  pl.BlockSpec((1, tn), lambda j: (0, j))],
        out_specs=pl.BlockSpec((cpad.shape[0], tn), lambda j: (0, j)),
        compiler_params=_params(("parallel",)),
        name="mod",
    )(cpad, w_mod, b_mod.reshape(1, n6))


def _inproj_kernel(x_ref, g_ref, sh_ref, sc_ref, w_ref, sgun_ref,
                   u_ref, gu_ref, vn_ref, g1_ref, g2_ref, *, d_s5, d_sgu, d_model):
    h = _rms(x_ref[...], g_ref[...])
    hb = (h * (1.0 + sc_ref[0]) + sh_ref[0]).astype(BF16)
    o1, o2, o3, o4 = d_s5, d_s5 + d_sgu, d_s5 + 2 * d_sgu, d_s5 + 2 * d_sgu + d_model
    u_ref[...] = _dot(hb, w_ref[:, 0:o1]).astype(BF16)
    gu_ref[...] = jax.nn.gelu(_dot(hb, w_ref[:, o1:o2])).astype(BF16)
    v = jax.nn.gelu(_dot(hb, w_ref[:, o2:o3]))
    vn_ref[...] = _rms(v, sgun_ref[...]).astype(BF16)
    g1_ref[...] = jax.nn.sigmoid(_dot(hb, w_ref[:, o3:o4])).astype(BF16)
    g2_ref[...] = jax.nn.sigmoid(_dot(hb, w_ref[:, o4:o4 + d_model])).astype(BF16)


def _inproj(x2, norm_g, shift, scale, w_in_bf, sgu_norm, tokens_per_batch, tm, d_s5, d_sgu):
    n, d = x2.shape
    tiles_per_batch = tokens_per_batch // tm
    row = lambda i: (i, 0)
    fixed = lambda i: (0, 0)
    per_batch = lambda i: (i // tiles_per_batch, 0, 0)
    return pl.pallas_call(
        functools.partial(_inproj_kernel, d_s5=d_s5, d_sgu=d_sgu, d_model=d),
        out_shape=(jax.ShapeDtypeStruct((n, d_s5), BF16), jax.ShapeDtypeStruct((n, d_sgu), BF16),
                   jax.ShapeDtypeStruct((n, d_sgu), BF16), jax.ShapeDtypeStruct((n, d), BF16),
                   jax.ShapeDtypeStruct((n, d), BF16)),
        grid=(n // tm,),
        in_specs=[pl.BlockSpec((tm, d), row), pl.BlockSpec((1, d), fixed),
                  pl.BlockSpec((1, 1, d), per_batch), pl.BlockSpec((1, 1, d), per_batch),
                  pl.BlockSpec(w_in_bf.shape, fixed), pl.BlockSpec((1, d_sgu), fixed)],
        out_specs=(pl.BlockSpec((tm, d_s5), row), pl.BlockSpec((tm, d_sgu), row),
                   pl.BlockSpec((tm, d_sgu), row), pl.BlockSpec((tm, d), row), pl.BlockSpec((tm, d), row)),
        compiler_params=_params(("parallel",)),
        name="inproj",
    )(x2, norm_g.reshape(1, d), shift, scale, w_in_bf, sgu_norm.reshape(1, d_sgu))


def _inproj_u_kernel(x_ref, g_ref, sh_ref, sc_ref, w_ref, u_ref):
    h = _rms(x_ref[...], g_ref[...])
    hb = (h * (1.0 + sc_ref[...]) + sh_ref[...]).astype(BF16)
    u_ref[...] = _dot(hb, w_ref[...]).astype(BF16)


def _inproj_u(x2, norm_g, shift, scale, w_u_bf, tm):
    n, d = x2.shape
    d_s5 = w_u_bf.shape[1]
    row = lambda i: (i, 0)
    fixed = lambda i: (0, 0)
    return pl.pallas_call(
        _inproj_u_kernel,
        out_shape=jax.ShapeDtypeStruct((n, d_s5), BF16),
        grid=(n // tm,),
        in_specs=[pl.BlockSpec((tm, d), row), pl.BlockSpec((1, d), fixed), pl.BlockSpec((1, d), fixed),
                  pl.BlockSpec((1, d), fixed), pl.BlockSpec(w_u_bf.shape, fixed)],
        out_specs=pl.BlockSpec((tm, d_s5), row),
        compiler_params=_params(("parallel",)),
        name="inproj_ctx",
    )(x2, norm_g.reshape(1, d), shift.reshape(1, d), scale.reshape(1, d), w_u_bf)


def _s5_tables(a_re, a_im, log_dt, b_re, b_im, c_re, c_im, chunk, n_steps):
    hi = lax.Precision.HIGHEST
    g_n, p_n, j_n = a_re.shape[1], a_re.shape[2], b_re.shape[3]
    dt = jnp.exp(log_dt)[..., None]
    lam_re, lam_im = dt * a_re, dt * a_im
    ab_re, ab_im = jnp.exp(lam_re) * jnp.cos(lam_im), jnp.exp(lam_re) * jnp.sin(lam_im)
    den = a_re * a_re + a_im * a_im
    q_re = ((ab_re - 1.0) * a_re + ab_im * a_im) / den
    q_im = (ab_im * a_re - (ab_re - 1.0) * a_im) / den
    bb_re = q_re[..., None] * b_re - q_im[..., None] * b_im
    bb_im = q_re[..., None] * b_im + q_im[..., None] * b_re
    k = jnp.arange(chunk + 1, dtype=F32)[:, None, None, None]
    mag = jnp.exp(k * lam_re[None])
    pw_re, pw_im = mag * jnp.cos(k * lam_im[None]), mag * jnp.sin(k * lam_im[None])

    def lag_kernels(d):
        x_re = pw_re[:chunk, d, :, :, None] * bb_re[d][None] - pw_im[:chunk, d, :, :, None] * bb_im[d][None]
        x_im = pw_re[:chunk, d, :, :, None] * bb_im[d][None] + pw_im[:chunk, d, :, :, None] * bb_re[d][None]
        return (jnp.einsum('gjp,kgpi->gkij', c_re[d], x_re, precision=hi)
                - jnp.einsum('gjp,kgpi->gkij', c_im[d], x_im, precision=hi))

    kf, kb = lag_kernels(0), lag_kernels(1)
    lags = jnp.concatenate([kb[:, 1:][:, ::-1], kf[:, 0:1] + kb[:, 0:1], kf[:, 1:],
                            jnp.zeros_like(kf[:, 0:1])], axis=1)
    lags = lags.transpose(0, 2, 3, 1)

    pf_re, pf_im = pw_re[:chunk, 0][::-1], pw_im[:chunk, 0][::-1]
    pb_re, pb_im = pw_re[:chunk, 1], pw_im[:chunk, 1]

    def state_in(p_re, p_im, d):
        e_re = p_re[..., None] * bb_re[d][None] - p_im[..., None] * bb_im[d][None]
        e_im = p_re[..., None] * bb_im[d][None] + p_im[..., None] * bb_re[d][None]
        return e_re.transpose(1, 3, 0, 2), e_im.transpose(1, 3, 0, 2)

    ef_re, ef_im = state_in(pf_re, pf_im, 0)
    eb_re, eb_im = state_in(pb_re, pb_im, 1)
    ke = jnp.concatenate([ef_re, eb_re, ef_im, eb_im], axis=-1).reshape(g_n, chunk * j_n, 4 * p_n)

    qf_re, qf_im = pw_re[1:chunk + 1, 0], pw_im[1:chunk + 1, 0]
    qb_re, qb_im = pw_re[1:chunk + 1, 1][::-1], pw_im[1:chunk + 1, 1][::-1]

    def state_out(p_re, p_im, d):
        cr, ci = c_re[d].transpose(0, 2, 1), c_im[d].transpose(0, 2, 1)
        pr, pi = p_re.transpose(1, 2, 0), p_im.transpose(1, 2, 0)
        o_re = pr[:, :, None, :] * cr[..., None] - pi[:, :, None, :] * ci[..., None]
        o_im = pr[:, :, None, :] * ci[..., None] + pi[:, :, None, :] * cr[..., None]
        return o_re, o_im

    of_re, of_im = state_out(qf_re, qf_im, 0)
    ob_re, ob_im = state_out(qb_re, qb_im, 1)
    kc = jnp.concatenate([of_re, ob_re, -of_im, -ob_im], axis=1).reshape(g_n, 4 * p_n, chunk * j_n)

    e = (chunk * (2.0 ** jnp.arange(n_steps, dtype=F32)))[:, None, None, None]
    mg = jnp.exp(e * lam_re[None])
    al_re, al_im = mg * jnp.cos(e * lam_im[None]), mg * jnp.sin(e * lam_im[None])
    alp = jnp.concatenate([al_re[:, 0], al_re[:, 1], al_im[:, 0], al_im[:, 1]], axis=-1)
    return lags, ke.astype(BF16), kc.astype(BF16), alp.transpose(1, 0, 2)


def _build_toeplitz(lag_ref, m_s, chunk):
    j_n = lag_ref.shape[1]
    per_tile = LANES // chunk
    low = lax.broadcasted_iota(I32, (chunk, LANES), 1) < chunk
    for i in range(j_n):
        for q in range(j_n // per_tile):
            tile = None
            for p in range(per_tile):
                j = q * per_tile + p
                row = jnp.broadcast_to(lag_ref[0, i, j:j + 1, :], (chunk, LANES))
                rot = pltpu.roll(row, (p * chunk + LANES - (chunk - 1)) % LANES, 1, stride=1, stride_axis=0)
                tile = rot if tile is None else jnp.where(low, tile, rot)
            m_s[i * chunk:(i + 1) * chunk, q * LANES:(q + 1) * LANES] = tile.astype(BF16)


def _s5_kernel(*refs, nb, nc, readout):
    if readout:
        u_ref, ke_ref, alp_ref, init_ref, lag_ref, kc_ref, fin_ref, y_ref, m_s = refs
    else:
        u_ref, ke_ref, alp_ref, init_ref, fin_ref = refs
    rows = nb * nc
    half = 2 * S5_STATE
    u = u_ref[0]
    e = _dot(u, ke_ref[0])
    er, ei = e[:, :half], e[:, half:]
    r_idx = lax.broadcasted_iota(I32, (rows, half), 0)
    if nc & (nc - 1) == 0:
        n_idx, b_idx = r_idx & (nc - 1), r_idx >> (nc.bit_length() - 1)
    else:
        n_idx, b_idx = lax.rem(r_idx, nc), lax.div(r_idx, nc)
    is_f = lax.broadcasted_iota(I32, (rows, half), 1) < S5_STATE
    seen = jnp.where(is_f, n_idx, nc - 1 - n_idx)
    init = init_ref[0]
    init_r = jnp.zeros((rows, half), F32)
    init_i = jnp.zeros((rows, half), F32)
    for b in range(nb):
        init_r = jnp.where(b_idx == b, init[b:b + 1, :half], init_r)
        init_i = jnp.where(b_idx == b, init[b:b + 1, half:], init_i)
    alp = alp_ref[0]
    ar, ai = alp[0:1, :half], alp[0:1, half:]
    er = er + jnp.where(seen == 0, ar * init_r - ai * init_i, 0.0)
    ei = ei + jnp.where(seen == 0, ar * init_i + ai * init_r, 0.0)

    def from_prev(v, dist):
        return jnp.where(is_f, pltpu.roll(v, dist, 0), pltpu.roll(v, rows - dist, 0))

    d, s = 1, 0
    while d < nc:
        ar, ai = alp[s:s + 1, :half], alp[s:s + 1, half:]
        sr = jnp.where(seen >= d, from_prev(er, d), 0.0)
        si = jnp.where(seen >= d, from_prev(ei, d), 0.0)
        er, ei = er + (ar * sr - ai * si), ei + (ar * si + ai * sr)
        d, s = d * 2, s + 1
    is_f_row = lax.broadcasted_iota(I32, (1, half), 1) < S5_STATE
    for b in range(nb):
        lo, hi = b * nc, b * nc + nc - 1
        fin_ref[0, b:b + 1, :half] = jnp.where(is_f_row, er[hi:hi + 1], er[lo:lo + 1])
        fin_ref[0, b:b + 1, half:] = jnp.where(is_f_row, ei[hi:hi + 1], ei[lo:lo + 1])
    if readout:
        if nc > 1:
            sin_r = jnp.where(seen >= 1, from_prev(er, 1), init_r)
            sin_i = jnp.where(seen >= 1, from_prev(ei, 1), init_i)
        else:
            sin_r, sin_i = init_r, init_i
        s_in = jnp.concatenate([sin_r, sin_i], axis=1).astype(BF16)
        _build_toeplitz(lag_ref, m_s, S5_CHUNK)
        y_ref[0] = (_dot(u, m_s[...]) + _dot(s_in, kc_ref[0])).astype(BF16)


def _s5(ug, ke, alp, init, m=None, kc=None, *, nb, nc):
    g_n, rows, width = ug.shape
    readout = m is not None
    grp = lambda g: (g, 0, 0)
    in_specs = [pl.BlockSpec((1, rows, width), grp), pl.BlockSpec((1,) + ke.shape[1:], grp),
                pl.BlockSpec((1,) + alp.shape[1:], grp), pl.BlockSpec((1,) + init.shape[1:], grp)]
    out_shape = [jax.ShapeDtypeStruct(init.shape, F32)]
    out_specs = [pl.BlockSpec((1,) + init.shape[1:], grp)]
    args = [ug, ke, alp, init]
    scratch = []
    if readout:
        assert m.shape[3] == LANES and 2 * S5_CHUNK == LANES
        in_specs += [pl.BlockSpec((1,) + m.shape[1:], lambda g: (g, 0, 0, 0)),
                     pl.BlockSpec((1,) + kc.shape[1:], grp)]
        out_shape.append(jax.ShapeDtypeStruct(ug.shape, BF16))
        out_specs.append(pl.BlockSpec((1, rows, width), grp))
        args += [m, kc]
        scratch = [pltpu.VMEM((width, width), BF16)]
    return pl.pallas_call(
        functools.partial(_s5_kernel, nb=nb, nc=nc, readout=readout),
        out_shape=tuple(out_shape), grid=(g_n,), in_specs=in_specs, out_specs=tuple(out_specs),
        scratch_shapes=scratch,
        compiler_params=_params(("parallel",)),
        name="s5_readout" if readout else "s5_state",
    )(*args)


def _to_groups(u, nb, nc, chunk, g_n):
    j_n = u.shape[1] // g_n
    return u.reshape(nb * nc, chunk, g_n, j_n).transpose(2, 0, 3, 1).reshape(g_n, nb * nc, j_n * chunk)


def _from_groups(y, nb, nc, chunk, g_n):
    j_n = y.shape[2] // chunk
    return y.reshape(g_n, nb * nc, j_n, chunk).transpose(1, 3, 0, 2).reshape(nb * nc * chunk, g_n * j_n)


def _mix_kernel(y_ref, u_ref, gu_ref, vn_ref, g1_ref, g2_ref, x_ref,
                d_ref, wglu_ref, bglu_ref, sguw_ref, sgub_ref, wb1_ref, wb2_ref, wout_ref,
                npost_ref, gt_ref, npre_ref, shf_ref, scf_ref, wr_ref,
                x1_ref, hp_ref, lg_ref, *, tm):
    y = jax.nn.gelu(y_ref[...].astype(F32) + d_ref[...] * u_ref[...].astype(F32))
    y_s5 = y * jax.nn.sigmoid(_dot(y.astype(BF16), wglu_ref[...]) + bglu_ref[...])

    lane = lax.broadcasted_iota(I32, (1, LANES), 1)
    m_lo = (lane < LANES // 2).astype(F32).astype(BF16)
    m_hi = (lane >= LANES // 2).astype(F32).astype(BF16)
    vn = vn_ref[...]
    chunks = []
    for c in range(tm // SGU_CHUNK):
        vc = vn[c * SGU_CHUNK:(c + 1) * SGU_CHUNK, :]
        tiles = []
        for q in range(vc.shape[1] // LANES):
            vt = vc[:, q * LANES:(q + 1) * LANES]
            rhs = jnp.concatenate([vt * m_lo, vt * m_hi], axis=0)
            tiles.append(_dot(sguw_ref[q], rhs))
        chunks.append(jnp.concatenate(tiles, axis=1) + sgub_ref[...])
    mixed = jnp.concatenate(chunks, axis=0)
    y_sgu = gu_ref[...].astype(F32) * mixed

    merged = (g1_ref[...].astype(F32) * _dot(y_s5.astype(BF16), wb1_ref[...])
              + g2_ref[...].astype(F32) * _dot(y_sgu.astype(BF16), wb2_ref[...]))
    mx = _dot(merged.astype(BF16), wout_ref[...])
    x1 = x_ref[...] + gt_ref[0] * _rms(mx, npost_ref[...])
    x1_ref[...] = x1
    hp = _rms(x1, npre_ref[...]) * (1.0 + scf_ref[0]) + shf_ref[0]
    _store_row_tiles(hp_ref, 0, hp)
    lg_ref[...] = _dot(hp.astype(BF16), wr_ref[...])


def _mix(y, u, gu, vn, g1, g2, x2, s5_d, wglu, bglu, sguw, sgub, wb1, wb2, wout,
         npost, gt, npre, shf, scf, wr, tokens_per_batch, tm):
    n, d = x2.shape
    d_s5, d_sgu, n_exp = y.shape[1], gu.shape[1], wr.shape[1]
    tiles_per_batch = tokens_per_batch // tm
    row = lambda i: (i, 0)
    fixed = lambda i: (0, 0)
    fixed3 = lambda i: (0, 0, 0)
    per_batch = lambda i: (i // tiles_per_batch, 0, 0)
    vec = lambda w: pl.BlockSpec((1, w), fixed)
    return pl.pallas_call(
        functools.partial(_mix_kernel, tm=tm),
        out_shape=(jax.ShapeDtypeStruct((n, d), F32), jax.ShapeDtypeStruct((n * d // LANES, LANES), F32),
                   jax.ShapeDtypeStruct((n, n_exp), F32)),
        grid=(n // tm,),
        in_specs=[pl.BlockSpec((tm, d_s5), row), pl.BlockSpec((tm, d_s5), row), pl.BlockSpec((tm, d_sgu), row),
                  pl.BlockSpec((tm, d_sgu), row), pl.BlockSpec((tm, d), row), pl.BlockSpec((tm, d), row),
                  pl.BlockSpec((tm, d), row),
                  vec(d_s5), pl.BlockSpec(wglu.shape, fixed), vec(d_s5),
                  pl.BlockSpec(sguw.shape, fixed3), pl.BlockSpec(sgub.shape, fixed),
                  pl.BlockSpec(wb1.shape, fixed), pl.BlockSpec(wb2.shape, fixed), pl.BlockSpec(wout.shape, fixed),
                  vec(d), pl.BlockSpec((1, 1, d), per_batch), vec(d),
                  pl.BlockSpec((1, 1, d), per_batch), pl.BlockSpec((1, 1, d), per_batch),
                  pl.BlockSpec(wr.shape, fixed)],
        out_specs=(pl.BlockSpec((tm, d), row), pl.BlockSpec((tm * d // LANES, LANES), row),
                   pl.BlockSpec((tm, n_exp), row)),
        compiler_params=_params(("parallel",)),
        name="mix",
    )(y, u, gu, vn, g1, g2, x2, s5_d.reshape(1, d_s5), wglu, bglu.reshape(1, d_s5), sguw, sgub,
      wb1, wb2, wout, npost.reshape(1, d), gt, npre.reshape(1, d), shf, scf, wr)


def _route_kernel(lg_ref, bias_ref, idx_ref, w_ref, cnt_ref, *, n_exp, tn):
    scores = jax.nn.sigmoid(lg_ref[...])
    sel = scores + bias_ref[...]
    per_group = n_exp // N_EXPERT_GROUPS
    neg = jnp.float32(-jnp.inf)
    gs = []
    for g in range(N_EXPERT_GROUPS):
        sg = sel[g * per_group:(g + 1) * per_group, :]
        m1 = jnp.max(sg, axis=0, keepdims=True)
        is_m1 = sg == m1
        n_m1 = jnp.sum(is_m1.astype(F32), axis=0, keepdims=True)
        rest = jnp.max(jnp.where(is_m1, neg, sg), axis=0, keepdims=True)
        gs.append(m1 + jnp.where(n_m1 >= 2.0, m1, rest))
    gsm = jnp.concatenate(gs, axis=0)
    g_iota = lax.broadcasted_iota(I32, gsm.shape, 0)
    e_iota = lax.broadcasted_iota(I32, sel.shape, 0).astype(F32)
    masked = []
    for g in range(N_EXPERT_GROUPS):
        mine = gsm[g:g + 1, :]
        beats = jnp.where(gsm > mine, 1.0, jnp.where(gsm == mine, jnp.where(g_iota < g, 1.0, 0.0), 0.0))
        n_beats = jnp.sum(beats, axis=0, keepdims=True)
        masked.append(jnp.where(n_beats < float(TOPK_GROUPS), sel[g * per_group:(g + 1) * per_group, :], neg))
    selm = jnp.concatenate(masked, axis=0)
    picked, vals = [], []
    hot = jnp.zeros(sel.shape, F32)
    for _ in range(TOP_K):
        m = jnp.max(selm, axis=0, keepdims=True)
        first = jnp.min(jnp.where(selm == m, e_iota, float(n_exp)), axis=0, keepdims=True)
        one = e_iota == first
        picked.append(first)
        vals.append(jnp.sum(jnp.where(one, scores, 0.0), axis=0, keepdims=True))
        hot = hot + jnp.where(one, 1.0, 0.0)
        selm = jnp.where(one, neg, selm)
    idx_ref[...] = jnp.concatenate(picked, axis=0).astype(I32)
    wv = jnp.concatenate(vals, axis=0)
    w_ref[...] = wv / jnp.sum(wv, axis=0, keepdims=True) * ROUTE_SCALE

    @pl.when(pl.program_id(0) == 0)
    def _():
        cnt_ref[...] = jnp.zeros_like(cnt_ref)

    cnt_ref[...] += jnp.sum(hot, axis=1, keepdims=True)


def _route(logits_t, bias, tn):
    n_exp, n = logits_t.shape
    col = lambda i: (0, i)
    fixed = lambda i: (0, 0)
    return pl.pallas_call(
        functools.partial(_route_kernel, n_exp=n_exp, tn=tn),
        out_shape=(jax.ShapeDtypeStruct((TOP_K, n), I32), jax.ShapeDtypeStruct((TOP_K, n), F32),
                   jax.ShapeDtypeStruct((n_exp, 1), F32)),
        grid=(n // tn,),
        in_specs=[pl.BlockSpec((n_exp, tn), col), pl.BlockSpec((n_exp, 1), fixed)],
        out_specs=(pl.BlockSpec((TOP_K, tn), col), pl.BlockSpec((TOP_K, tn), col),
                   pl.BlockSpec((n_exp, 1), fixed)),
        compiler_params=_params(("arbitrary",)),
        name="route",
    )(logits_t, bias.reshape(n_exp, 1))


def _dest_kernel(idx_ref, start_ref, dest_ref, carry_ref, *, n_exp, tn):
    @pl.when(pl.program_id(0) == 0)
    def _():
        carry_ref[...] = start_ref[...]

    idx = idx_ref[...]
    e_iota = lax.broadcasted_iota(I32, (n_exp, tn), 0)
    hot = jnp.zeros((n_exp, tn), F32)
    for k in range(TOP_K):
        hot = hot + jnp.where(e_iota == idx[k:k + 1, :], 1.0, 0.0)
    before = jnp.where(lax.broadcasted_iota(I32, (tn, tn), 0) < lax.broadcasted_iota(I32, (tn, tn), 1), 1.0, 0.0)
    rank = _dot(hot.astype(BF16), before.astype(BF16)) + carry_ref[...]
    dest_ref[...] = jnp.concatenate(
        [jnp.sum(jnp.where(e_iota == idx[k:k + 1, :], rank, 0.0), axis=0, keepdims=True)
         for k in range(TOP_K)], axis=0).astype(I32)
    carry_ref[...] += jnp.sum(hot, axis=1, keepdims=True)


def _dest(idx, start, tn):
    n = idx.shape[1]
    n_exp = start.shape[0]
    col = lambda i: (0, i)
    return pl.pallas_call(
        functools.partial(_dest_kernel, n_exp=n_exp, tn=tn),
        out_shape=jax.ShapeDtypeStruct((TOP_K, n), I32),
        grid=(n // tn,),
        in_specs=[pl.BlockSpec((TOP_K, tn), col), pl.BlockSpec((n_exp, 1), lambda i: (0, 0))],
        out_specs=pl.BlockSpec((TOP_K, tn), col),
        scratch_shapes=[pltpu.VMEM((n_exp, 1), F32)],
        compiler_params=_params(("arbitrary",)),
        name="dest",
    )(idx, start)


def _dispatch_kernel(dest_ref, hp_ref, xs_ref, sem, *, tm, pieces):
    def start(r, c):
        src = hp_ref.at[pl.ds(pl.multiple_of(r * pieces, pieces), pieces), :]
        for k in range(TOP_K):
            row = pl.multiple_of(dest_ref[r * TOP_K + k], pieces)
            pltpu.make_async_copy(src, xs_ref.at[pl.ds(row, pieces), :], sem).start(priority=k % 2)
        return c

    lax.fori_loop(0, tm, start, 0)
    for k in range(TOP_K):
        pltpu.make_async_copy(hp_ref, xs_ref.at[pl.ds(0, tm * pieces), :], sem).wait()


def _dispatch(dest, hp, cap, tm, pieces):
    n = hp.shape[0] // pieces
    return pl.pallas_call(
        functools.partial(_dispatch_kernel, tm=tm, pieces=pieces),
        out_shape=jax.ShapeDtypeStruct((cap * pieces, LANES), hp.dtype),
        grid=(n // tm,),
        in_specs=[pl.BlockSpec((tm * TOP_K,), lambda i: (i,), memory_space=pltpu.SMEM),
                  pl.BlockSpec((tm * pieces, LANES), lambda i: (i, 0))],
        out_specs=pl.BlockSpec(memory_space=pl.ANY),
        scratch_shapes=[pltpu.SemaphoreType.DMA(())],
        compiler_params=_params(("arbitrary",)),
        name="dispatch",
    )(dest, hp)


def _expert_kernel(be_ref, bv_ref, nu_ref, nx_ref, xs_ref, wg_hbm, wu_hbm, wd_hbm, ys_ref,
                   wg_f, wu_f, wd_f, wgu_s, wd_s, sem, *, rows):
    b = pl.program_id(0)
    valid = bv_ref[b]
    changed = jnp.logical_or(b == 0, be_ref[b] != be_ref[jnp.maximum(b - 1, 0)])
    d_e = wg_hbm.shape[2]

    def weight_copies(e):
        return (pltpu.make_async_copy(wg_hbm.at[e], wg_f, sem.at[0]),
                pltpu.make_async_copy(wu_hbm.at[e], wu_f, sem.at[1]),
                pltpu.make_async_copy(wd_hbm.at[e], wd_f, sem.at[2]))

    @pl.when(jnp.logical_and(changed, valid > 0))
    def _():
        @pl.when(b == 0)
        def _():
            for cp in weight_copies(be_ref[b]):
                cp.start()

        for cp in weight_copies(be_ref[b]):
            cp.wait()
        wgu_s[:, :d_e] = wg_f[...].astype(BF16)
        wgu_s[:, d_e:] = wu_f[...].astype(BF16)
        wd_s[...] = wd_f[...].astype(BF16)

        @pl.when(nx_ref[b] >= 0)
        def _():
            for cp in weight_copies(nx_ref[b]):
                cp.start()

    for s in range(rows // EXPERT_SUB):
        @pl.when(valid > s * EXPERT_SUB)
        def _():
            x = _load_row_tiles(xs_ref, s * EXPERT_SUB, EXPERT_SUB, wg_hbm.shape[1] // LANES)
            r_iota = lax.broadcasted_iota(I32, x.shape, 0) + s * EXPERT_SUB
            xb = jnp.where(r_iota < valid, x, 0.0).astype(BF16)
            gu = _dot(xb, wgu_s[...])
            gate, up = gu[:, :d_e], gu[:, d_e:]
            hid = (gate * jax.nn.sigmoid(gate) * up).astype(BF16)
            _store_row_tiles(ys_ref, s * EXPERT_SUB, _dot(hid, wd_s[...]))


def _experts(blk_expert, blk_valid, n_used, blk_next, xs, w_gate, w_up, w_down, rows):
    _, d, d_e = w_gate.shape
    w = LANES
    cap = xs.shape[0] // (d // LANES)
    rows_phys = rows * (d // LANES)
    rows_map = lambda b, be, bv, nu, nx: (jnp.minimum(b, nu[0] - 1), 0)
    hbm = pl.BlockSpec(memory_space=pl.ANY)
    grid_spec = pltpu.PrefetchScalarGridSpec(
        num_scalar_prefetch=4,
        grid=(cap // rows,),
        in_specs=[pl.BlockSpec((rows_phys, w), rows_map), hbm, hbm, hbm],
        out_specs=pl.BlockSpec((rows_phys, w), rows_map),
        scratch_shapes=[pltpu.VMEM((d, d_e), F32), pltpu.VMEM((d, d_e), F32), pltpu.VMEM((d_e, d), F32),
                        pltpu.VMEM((d, 2 * d_e), BF16), pltpu.VMEM((d_e, d), BF16),
                        pltpu.SemaphoreType.DMA((3,))],
    )
    return pl.pallas_call(
        functools.partial(_expert_kernel, rows=rows),
        out_shape=jax.ShapeDtypeStruct(xs.shape, F32),
        grid_spec=grid_spec,
        compiler_params=_params(("arbitrary",)),
        name="experts",
    )(blk_expert, blk_valid, n_used, blk_next, xs, w_gate, w_up, w_down)


def _combine_kernel(dest_ref, next_ref, ys_ref, wt_ref, hp_ref, x1_ref, wsgu_ref, wsd_ref, npost_ref, gt_ref,
                    o_ref, buf, sem, *, tm, pieces):
    i = pl.program_id(0)
    cur = i % 2

    def gather(rows_ref, half):
        def start(r, c):
            slot = pl.ds(pl.multiple_of(r * pieces, pieces), pieces)
            for k in range(TOP_K):
                row = pl.multiple_of(rows_ref[r * TOP_K + k], pieces)
                pltpu.make_async_copy(ys_ref.at[pl.ds(row, pieces), :], buf.at[half, k, slot, :],
                                      sem.at[half]).start(priority=k % 2)
            return c

        lax.fori_loop(0, tm, start, 0)

    @pl.when(i == 0)
    def _():
        gather(dest_ref, 0)

    @pl.when(i + 1 < pl.num_programs(0))
    def _():
        gather(next_ref, 1 - cur)

    hb = _load_row_tiles(hp_ref, 0, tm, pieces).astype(BF16)
    d_sh = wsd_ref.shape[0]
    gu = _dot(hb, wsgu_ref[...])
    gate, up = gu[:, :d_sh], gu[:, d_sh:]
    shared = _dot((gate * jax.nn.sigmoid(gate) * up).astype(BF16), wsd_ref[...])

    for k in range(TOP_K):
        pltpu.make_async_copy(ys_ref.at[pl.ds(0, tm * pieces), :], buf.at[cur, k], sem.at[cur]).wait()
    wt = wt_ref[...]
    fx = shared
    for k in range(TOP_K):
        fx = fx + wt[:, k:k + 1] * _load_row_tiles(buf, 0, tm, pieces, lead=(cur, k))
    o_ref[...] = x1_ref[...] + gt_ref[0] * _rms(fx, npost_ref[...])


def _combine(dest, ys, wt, hp, x1, wsgu, wsd, npost, gt, tokens_per_batch, tm):
    n, d = x1.shape
    pieces = d // LANES
    w = LANES
    tiles_per_batch = tokens_per_batch // tm
    row = lambda i: (i, 0)
    fixed = lambda i: (0, 0)
    return pl.pallas_call(
        functools.partial(_combine_kernel, tm=tm, pieces=pieces),
        out_shape=jax.ShapeDtypeStruct((n, d), F32),
        grid=(n // tm,),
        in_specs=[pl.BlockSpec((tm * TOP_K,), lambda i: (i,), memory_space=pltpu.SMEM),
                  pl.BlockSpec((tm * TOP_K,), lambda i: (jnp.minimum(i + 1, n // tm - 1),), memory_space=pltpu.SMEM),
                  pl.BlockSpec(memory_space=pl.ANY),
                  pl.BlockSpec((tm, TOP_K), row), pl.BlockSpec((tm * pieces, w), row), pl.BlockSpec((tm, d), row),
                  pl.BlockSpec(wsgu.shape, fixed), pl.BlockSpec(wsd.shape, fixed),
                  pl.BlockSpec((1, d), fixed), pl.BlockSpec((1, 1, d), lambda i: (i // tiles_per_batch, 0, 0))],
        out_specs=pl.BlockSpec((tm, d), row),
        scratch_shapes=[pltpu.VMEM((2, TOP_K, tm * pieces, w), F32), pltpu.SemaphoreType.DMA((2,))],
        compiler_params=_params(("arbitrary",)),
        name="combine",
    )(dest, dest, ys, wt, hp, x1, wsgu, wsd, npost.reshape(1, d), gt)


def _layer(x, ctx, mod_x, mod_c, norm_pre_mix, norm_post_mix, norm_pre_ffn, norm_post_ffn,
           w_in, s5_a_re, s5_a_im, s5_log_dt, s5_b_re, s5_b_im, s5_c_re, s5_c_im, s5_d, w_glu, b_glu,
           sgu_norm, sgu_w, sgu_b, w_branch_s5, w_branch_sgu, w_out,
           w_router, router_bias, w_exp_gate, w_exp_up, w_exp_down, w_sh_gate, w_sh_up, w_sh_down):
    bsz, t_len, d = x.shape
    c_len = ctx.shape[1]
    n = bsz * t_len
    d_s5, d_sgu = w_glu.shape[0], sgu_norm.shape[0]
    g_n = d_s5 // S5_GROUP_CH
    n_exp = w_router.shape[1]
    chunk = S5_CHUNK
    tm = min(512, t_len)
    assert t_len % tm == 0 and tm % SGU_CHUNK == 0 and t_len % chunk == 0 and c_len % chunk == 0

    sh_m, sc_m, gt_m, sh_f, sc_f, gt_f = [v.reshape(bsz, 1, d) for v in jnp.split(mod_x, 6, axis=-1)]
    csh_m, csc_m = mod_c[:d], mod_c[d:2 * d]

    w_in_bf = w_in.astype(BF16)

    nc_lat, nc_ctx = t_len // chunk, c_len // chunk
    n_steps = max(1, (max(nc_lat, nc_ctx) - 1).bit_length())
    m_t, ke_t, kc_t, alp = _s5_tables(s5_a_re, s5_a_im, s5_log_dt, s5_b_re, s5_b_im, s5_c_re, s5_c_im,
                                      chunk, n_steps)

    ctx2 = ctx.reshape(bsz * c_len, d)
    u_ctx = _inproj_u(ctx2, norm_pre_mix, csh_m, csc_m, w_in_bf[:, :d_s5], min(512, bsz * c_len))
    zero_init = jnp.zeros((g_n, bsz, 4 * S5_STATE), F32)
    (ctx_final,) = _s5(_to_groups(u_ctx, bsz, nc_ctx, chunk, g_n), ke_t, alp, zero_init, nb=bsz, nc=nc_ctx)

    x2 = x.reshape(n, d)
    u, gu, vn, g1, g2 = _inproj(x2, norm_pre_mix, sh_m, sc_m, w_in_bf, sgu_norm, t_len, tm, d_s5, d_sgu)
    _, yg = _s5(_to_groups(u, bsz, nc_lat, chunk, g_n), ke_t, alp, ctx_final, m_t, kc_t, nb=bsz, nc=nc_lat)
    y = _from_groups(yg, bsz, nc_lat, chunk, g_n)

    ch = d_sgu // SGU_GROUPS
    per_tile = LANES // ch
    sguw = sgu_w.reshape(SGU_GROUPS // per_tile, per_tile, SGU_CHUNK, SGU_CHUNK)
    sguw = sguw.transpose(0, 2, 1, 3).reshape(SGU_GROUPS // per_tile, SGU_CHUNK, per_tile * SGU_CHUNK).astype(BF16)
    sgub = jnp.repeat(sgu_b.T, ch, axis=1)

    x1, hp, logits = _mix(y, u, gu, vn, g1, g2, x2, s5_d, w_glu.astype(BF16), b_glu, sguw, sgub,
                          w_branch_s5.astype(BF16), w_branch_sgu.astype(BF16), w_out.astype(BF16),
                          norm_post_mix, gt_m, norm_pre_ffn, sh_f, sc_f, w_router.astype(BF16), t_len, tm)

    tn = min(512, n)
    idx, wts, counts = _route(logits.T, router_bias, tn)
    rows = EXPERT_ROWS
    cnt = counts.reshape(n_exp).astype(I32)
    nblk = (cnt + rows - 1) // rows
    blk_end = jnp.cumsum(nblk)
    blk_start = blk_end - nblk
    n_blocks = (n * TOP_K) // rows + n_exp
    cap = n_blocks * rows
    b_ids = jnp.arange(n_blocks, dtype=I32)
    blk_expert = jnp.minimum(jnp.sum((blk_end[None, :] <= b_ids[:, None]).astype(I32), axis=1), n_exp - 1)
    blk_valid = jnp.clip(cnt[blk_expert] - (b_ids - blk_start[blk_expert]) * rows, 0, rows).astype(I32)
    dest = _dest(idx, (blk_start * rows).astype(F32).reshape(n_exp, 1), tn)
    pieces = d // LANES
    dest = dest.T.reshape(n * TOP_K) * pieces

    td = min(256, n)
    xs = _dispatch(dest, hp, cap, td, pieces)
    n_used = blk_end[-1:].astype(I32)
    after = blk_end[blk_expert]
    blk_next = jnp.where(after < n_used[0], blk_expert[jnp.minimum(after, n_blocks - 1)], -1).astype(I32)
    ys = _experts(blk_expert, blk_valid, n_used, blk_next, xs, w_exp_gate, w_exp_up, w_exp_down, rows)
    wsgu = jnp.concatenate([w_sh_gate, w_sh_up], axis=1).astype(BF16)
    out = _combine(dest, ys, wts.T, hp, x1, wsgu, w_sh_down.astype(BF16), norm_post_ffn, gt_f, t_len, td)
    return out.reshape(bsz, t_len, d)


def kernel(x, c, ctx, c_ctx, w_mod, b_mod, norm_pre_mix, norm_post_mix, norm_pre_ffn, norm_post_ffn, w_in, s5_a_re, s5_a_im, s5_log_dt, s5_b_re, s5_b_im, s5_c_re, s5_c_im, s5_d, w_glu, b_glu, sgu_norm, sgu_w, sgu_b, w_branch_s5, w_branch_sgu, w_out, w_router, router_bias, w_exp_gate, w_exp_up, w_exp_down, w_sh_gate, w_sh_up, w_sh_down):
    depth = w_mod.shape[0]
    assert depth == 1, "the context stream is only carried through the last layer's S5 states"
    bsz = x.shape[0]
    pad = (-(bsz + 1)) % 8
    cpad = jnp.concatenate([c, c_ctx[None, :], jnp.zeros((pad, c.shape[1]), c.dtype)], axis=0)
    mod = _modulation(cpad, w_mod[0], b_mod[0])
    return _layer(x, ctx, mod[:bsz], mod[bsz], norm_pre_mix[0], norm_post_mix[0], norm_pre_ffn[0],
                  norm_post_ffn[0], w_in[0], s5_a_re[0], s5_a_im[0], s5_log_dt[0], s5_b_re[0], s5_b_im[0],
                  s5_c_re[0], s5_c_im[0], s5_d[0], w_glu[0], b_glu[0], sgu_norm[0], sgu_w[0], sgu_b[0],
                  w_branch_s5[0], w_branch_sgu[0], w_out[0], w_router[0], router_bias[0],
                  w_exp_gate[0], w_exp_up[0], w_exp_down[0], w_sh_gate[0], w_sh_up[0], w_sh_down[0])
```

```python
import functools
import math

import jax
import jax.numpy as jnp
from jax import lax
from jax.experimental import pallas as pl
from jax.experimental.pallas import tpu as pltpu

F32 = jnp.float32
BF16 = jnp.bfloat16
I32 = jnp.int32

EPS = 1e-6
S5_GROUP_CH = 16
S5_STATE = 64
S5_CHUNK = 64
SGU_GROUPS = 8
SGU_CHUNK = 128
N_EXPERT_GROUPS = 8
TOPK_GROUPS = 4
TOP_K = 8
ROUTE_SCALE = 2.5
LANES = 128
EXPERT_ROWS = 256
EXPERT_SUB = 256
COMBINE_PARTS = 4
VMEM_LIMIT = 56 * 1024 * 1024


def _params(sem):
    return pltpu.CompilerParams(dimension_semantics=sem, vmem_limit_bytes=VMEM_LIMIT)


def _rms(v, g):
    return v * lax.rsqrt(jnp.mean(v * v, axis=-1, keepdims=True) + EPS) * g


def _dot(a, b):
    return jnp.dot(a, b, preferred_element_type=F32)


def _store_row_tiles(ref, first_row, val):
    pieces = val.shape[1] // LANES
    for s in range(pieces):
        ref[pl.ds(first_row * pieces + s, val.shape[0], stride=pieces), :] = val[:, s * LANES:(s + 1) * LANES]


def _load_row_tiles(ref, first_row, n_rows, pieces, lead=()):
    return jnp.concatenate([ref[lead + (pl.ds(first_row * pieces + s, n_rows, stride=pieces), slice(None))]
                            for s in range(pieces)], axis=1)


def _mod_kernel(c_ref, w_ref, b_ref, o_ref):
    cv = c_ref[...]
    s = cv * jax.nn.sigmoid(cv)
    o_ref[...] = _dot(s.astype(BF16), w_ref[...].astype(BF16)) + b_ref[...]


def _modulation(cpad, w_mod, b_mod):
    d, n6 = w_mod.shape
    tn = 1024
    return pl.pallas_call(
        _mod_kernel,
        out_shape=jax.ShapeDtypeStruct((cpad.shape[0], n6), F32),
        grid=(n6 // tn,),
        in_specs=[pl.BlockSpec(cpad.shape, lambda j: (0, 0)),
                  pl.BlockSpec((d, tn), lambda j: (0, j)),
                  pl.BlockSpec((1, tn), lambda j: (0, j))],
        out_specs=pl.BlockSpec((cpad.shape[0], tn), lambda j: (0, j)),
        compiler_params=_params(("parallel",)),
        name="mod",
    )(cpad, w_mod, b_mod.reshape(1, n6))


def _inproj_kernel(x_ref, g_ref, sh_ref, sc_ref, w_ref, sgun_ref,
                   u_ref, gu_ref, vn_ref, g1_ref, g2_ref, *, d_s5, d_sgu, d_model):
    h = _rms(x_ref[...], g_ref[...])
    hb = (h * (1.0 + sc_ref[0]) + sh_ref[0]).astype(BF16)
    o1, o2, o3, o4 = d_s5, d_s5 + d_sgu, d_s5 + 2 * d_sgu, d_s5 + 2 * d_sgu + d_model
    u_ref[...] = _dot(hb, w_ref[:, 0:o1]).astype(BF16)
    gu_ref[...] = jax.nn.gelu(_dot(hb, w_ref[:, o1:o2])).astype(BF16)
    v = jax.nn.gelu(_dot(hb, w_ref[:, o2:o3]))
    vn_ref[...] = _rms(v, sgun_ref[...]).astype(BF16)
    g1_ref[...] = jax.nn.sigmoid(_dot(hb, w_ref[:, o3:o4])).astype(BF16)
    g2_ref[...] = jax.nn.sigmoid(_dot(hb, w_ref[:, o4:o4 + d_model])).astype(BF16)


def _inproj(x2, norm_g, shift, scale, w_in_bf, sgu_norm, tokens_per_batch, tm, d_s5, d_sgu):
    n, d = x2.shape
    tiles_per_batch = tokens_per_batch // tm
    row = lambda i: (i, 0)
    fixed = lambda i: (0, 0)
    per_batch = lambda i: (i // tiles_per_batch, 0, 0)
    return pl.pallas_call(
        functools.partial(_inproj_kernel, d_s5=d_s5, d_sgu=d_sgu, d_model=d),
        out_shape=(jax.ShapeDtypeStruct((n, d_s5), BF16), jax.ShapeDtypeStruct((n, d_sgu), BF16),
                   jax.ShapeDtypeStruct((n, d_sgu), BF16), jax.ShapeDtypeStruct((n, d), BF16),
                   jax.ShapeDtypeStruct((n, d), BF16)),
        grid=(n // tm,),
        in_specs=[pl.BlockSpec((tm, d), row), pl.BlockSpec((1, d), fixed),
                  pl.BlockSpec((1, 1, d), per_batch), pl.BlockSpec((1, 1, d), per_batch),
                  pl.BlockSpec(w_in_bf.shape, fixed), pl.BlockSpec((1, d_sgu), fixed)],
        out_specs=(pl.BlockSpec((tm, d_s5), row), pl.BlockSpec((tm, d_sgu), row),
                   pl.BlockSpec((tm, d_sgu), row), pl.BlockSpec((tm, d), row), pl.BlockSpec((tm, d), row)),
        compiler_params=_params(("parallel",)),
        name="inproj",
    )(x2, norm_g.reshape(1, d), shift, scale, w_in_bf, sgu_norm.reshape(1, d_sgu))


def _inproj_u_kernel(x_ref, g_ref, sh_ref, sc_ref, w_ref, u_ref):
    h = _rms(x_ref[...], g_ref[...])
    hb = (h * (1.0 + sc_ref[...]) + sh_ref[...]).astype(BF16)
    u_ref[...] = _dot(hb, w_ref[...]).astype(BF16)


def _inproj_u(x2, norm_g, shift, scale, w_u_bf, tm):
    n, d = x2.shape
    d_s5 = w_u_bf.shape[1]
    row = lambda i: (i, 0)
    fixed = lambda i: (0, 0)
    return pl.pallas_call(
        _inproj_u_kernel,
        out_shape=jax.ShapeDtypeStruct((n, d_s5), BF16),
        grid=(n // tm,),
        in_specs=[pl.BlockSpec((tm, d), row), pl.BlockSpec((1, d), fixed), pl.BlockSpec((1, d), fixed),
                  pl.BlockSpec((1, d), fixed), pl.BlockSpec(w_u_bf.shape, fixed)],
        out_specs=pl.BlockSpec((tm, d_s5), row),
        compiler_params=_params(("parallel",)),
        name="inproj_ctx",
    )(x2, norm_g.reshape(1, d), shift.reshape(1, d), scale.reshape(1, d), w_u_bf)


def _s5_tables(a_re, a_im, log_dt, b_re, b_im, c_re, c_im, chunk, n_steps):
    hi = lax.Precision.HIGHEST
    g_n, p_n, j_n = a_re.shape[1], a_re.shape[2], b_re.shape[3]
    dt = jnp.exp(log_dt)[..., None]
    lam_re, lam_im = dt * a_re, dt * a_im
    ab_re, ab_im = jnp.exp(lam_re) * jnp.cos(lam_im), jnp.exp(lam_re) * jnp.sin(lam_im)
    den = a_re * a_re + a_im * a_im
    q_re = ((ab_re - 1.0) * a_re + ab_im * a_im) / den
    q_im = (ab_im * a_re - (ab_re - 1.0) * a_im) / den
    bb_re = q_re[..., None] * b_re - q_im[..., None] * b_im
    bb_im = q_re[..., None] * b_im + q_im[..., None] * b_re
    k = jnp.arange(chunk + 1, dtype=F32)[:, None, None, None]
    mag = jnp.exp(k * lam_re[None])
    pw_re, pw_im = mag * jnp.cos(k * lam_im[None]), mag * jnp.sin(k * lam_im[None])

    def lag_kernels(d):
        x_re = pw_re[:chunk, d, :, :, None] * bb_re[d][None] - pw_im[:chunk, d, :, :, None] * bb_im[d][None]
        x_im = pw_re[:chunk, d, :, :, None] * bb_im[d][None] + pw_im[:chunk, d, :, :, None] * bb_re[d][None]
        return (jnp.einsum('gjp,kgpi->gkij', c_re[d], x_re, precision=hi)
                - jnp.einsum('gjp,kgpi->gkij', c_im[d], x_im, precision=hi))

    kf, kb = lag_kernels(0), lag_kernels(1)
    lags = jnp.concatenate([kb[:, 1:][:, ::-1], kf[:, 0:1] + kb[:, 0:1], kf[:, 1:],
                            jnp.zeros_like(kf[:, 0:1])], axis=1)
    lags = lags.transpose(0, 2, 3, 1)

    pf_re, pf_im = pw_re[:chunk, 0][::-1], pw_im[:chunk, 0][::-1]
    pb_re, pb_im = pw_re[:chunk, 1], pw_im[:chunk, 1]

    def state_in(p_re, p_im, d):
        e_re = p_re[..., None] * bb_re[d][None] - p_im[..., None] * bb_im[d][None]
        e_im = p_re[..., None] * bb_im[d][None] + p_im[..., None] * bb_re[d][None]
        return e_re.transpose(1, 3, 0, 2), e_im.transpose(1, 3, 0, 2)

    ef_re, ef_im = state_in(pf_re, pf_im, 0)
    eb_re, eb_im = state_in(pb_re, pb_im, 1)
    ke = jnp.concatenate([ef_re, eb_re, ef_im, eb_im], axis=-1).reshape(g_n, chunk * j_n, 4 * p_n)

    qf_re, qf_im = pw_re[1:chunk + 1, 0], pw_im[1:chunk + 1, 0]
    qb_re, qb_im = pw_re[1:chunk + 1, 1][::-1], pw_im[1:chunk + 1, 1][::-1]

    def state_out(p_re, p_im, d):
        cr, ci = c_re[d].transpose(0, 2, 1), c_im[d].transpose(0, 2, 1)
        pr, pi = p_re.transpose(1, 2, 0), p_im.transpose(1, 2, 0)
        o_re = pr[:, :, None, :] * cr[..., None] - pi[:, :, None, :] * ci[..., None]
        o_im = pr[:, :, None, :] * ci[..., None] + pi[:, :, None, :] * cr[..., None]
        return o_re, o_im

    of_re, of_im = state_out(qf_re, qf_im, 0)
    ob_re, ob_im = state_out(qb_re, qb_im, 1)
    kc = jnp.concatenate([of_re, ob_re, -of_im, -ob_im], axis=1).reshape(g_n, 4 * p_n, chunk * j_n)

    e = (chunk * (2.0 ** jnp.arange(n_steps, dtype=F32)))[:, None, None, None]
    mg = jnp.exp(e * lam_re[None])
    al_re, al_im = mg * jnp.cos(e * lam_im[None]), mg * jnp.sin(e * lam_im[None])
    alp = jnp.concatenate([al_re[:, 0], al_re[:, 1], al_im[:, 0], al_im[:, 1]], axis=-1)
    return lags, ke.astype(BF16), kc.astype(BF16), alp.transpose(1, 0, 2)


def _build_toeplitz(lag_ref, m_s, chunk):
    j_n = lag_ref.shape[1]
    per_tile = LANES // chunk
    low = lax.broadcasted_iota(I32, (chunk, LANES), 1) < chunk
    for i in range(j_n):
        for q in range(j_n // per_tile):
            tile = None
            for p in range(per_tile):
                j = q * per_tile + p
                row = jnp.broadcast_to(lag_ref[0, i, j:j + 1, :], (chunk, LANES))
                rot = pltpu.roll(row, (p * chunk + LANES - (chunk - 1)) % LANES, 1, stride=1, stride_axis=0)
                tile = rot if tile is None else jnp.where(low, tile, rot)
            m_s[i * chunk:(i + 1) * chunk, q * LANES:(q + 1) * LANES] = tile.astype(BF16)


def _s5_kernel(*refs, nb, nc, readout):
    if readout:
        u_ref, ke_ref, alp_ref, init_ref, lag_ref, kc_ref, fin_ref, y_ref, m_s = refs
    else:
        u_ref, ke_ref, alp_ref, init_ref, fin_ref = refs
    rows = nb * nc
    half = 2 * S5_STATE
    u = u_ref[0]
    e = _dot(u, ke_ref[0])
    er, ei = e[:, :half], e[:, half:]
    r_idx = lax.broadcasted_iota(I32, (rows, half), 0)
    if nc & (nc - 1) == 0:
        n_idx, b_idx = r_idx & (nc - 1), r_idx >> (nc.bit_length() - 1)
    else:
        n_idx, b_idx = lax.rem(r_idx, nc), lax.div(r_idx, nc)
    is_f = lax.broadcasted_iota(I32, (rows, half), 1) < S5_STATE
    seen = jnp.where(is_f, n_idx, nc - 1 - n_idx)
    init = init_ref[0]
    init_r = jnp.zeros((rows, half), F32)
    init_i = jnp.zeros((rows, half), F32)
    for b in range(nb):
        init_r = jnp.where(b_idx == b, init[b:b + 1, :half], init_r)
        init_i = jnp.where(b_idx == b, init[b:b + 1, half:], init_i)
    alp = alp_ref[0]
    ar, ai = alp[0:1, :half], alp[0:1, half:]
    er = er + jnp.where(seen == 0, ar * init_r - ai * init_i, 0.0)
    ei = ei + jnp.where(seen == 0, ar * init_i + ai * init_r, 0.0)

    def from_prev(v, dist):
        return jnp.where(is_f, pltpu.roll(v, dist, 0), pltpu.roll(v, rows - dist, 0))

    d, s = 1, 0
    while d < nc:
        ar, ai = alp[s:s + 1, :half], alp[s:s + 1, half:]
        sr = jnp.where(seen >= d, from_prev(er, d), 0.0)
        si = jnp.where(seen >= d, from_prev(ei, d), 0.0)
        er, ei = er + (ar * sr - ai * si), ei + (ar * si + ai * sr)
        d, s = d * 2, s + 1
    is_f_row = lax.broadcasted_iota(I32, (1, half), 1) < S5_STATE
    for b in range(nb):
        lo, hi = b * nc, b * nc + nc - 1
        fin_ref[0, b:b + 1, :half] = jnp.where(is_f_row, er[hi:hi + 1], er[lo:lo + 1])
        fin_ref[0, b:b + 1, half:] = jnp.where(is_f_row, ei[hi:hi + 1], ei[lo:lo + 1])
    if readout:
        if nc > 1:
            sin_r = jnp.where(seen >= 1, from_prev(er, 1), init_r)
            sin_i = jnp.where(seen >= 1, from_prev(ei, 1), init_i)
        else:
            sin_r, sin_i = init_r, init_i
        s_in = jnp.concatenate([sin_r, sin_i], axis=1).astype(BF16)
        _build_toeplitz(lag_ref, m_s, S5_CHUNK)
        y_ref[0] = (_dot(u, m_s[...]) + _dot(s_in, kc_ref[0])).astype(BF16)


def _s5(ug, ke, alp, init, m=None, kc=None, *, nb, nc):
    g_n, rows, width = ug.shape
    readout = m is not None
    grp = lambda g: (g, 0, 0)
    in_specs = [pl.BlockSpec((1, rows, width), grp), pl.BlockSpec((1,) + ke.shape[1:], grp),
                pl.BlockSpec((1,) + alp.shape[1:], grp), pl.BlockSpec((1,) + init.shape[1:], grp)]
    out_shape = [jax.ShapeDtypeStruct(init.shape, F32)]
    out_specs = [pl.BlockSpec((1,) + init.shape[1:], grp)]
    args = [ug, ke, alp, init]
    scratch = []
    if readout:
        assert m.shape[3] == LANES and 2 * S5_CHUNK == LANES
        in_specs += [pl.BlockSpec((1,) + m.shape[1:], lambda g: (g, 0, 0, 0)),
                     pl.BlockSpec((1,) + kc.shape[1:], grp)]
        out_shape.append(jax.ShapeDtypeStruct(ug.shape, BF16))
        out_specs.append(pl.BlockSpec((1, rows, width), grp))
        args += [m, kc]
        scratch = [pltpu.VMEM((width, width), BF16)]
    return pl.pallas_call(
        functools.partial(_s5_kernel, nb=nb, nc=nc, readout=readout),
        out_shape=tuple(out_shape), grid=(g_n,), in_specs=in_specs, out_specs=tuple(out_specs),
        scratch_shapes=scratch,
        compiler_params=_params(("parallel",)),
        name="s5_readout" if readout else "s5_state",
    )(*args)


def _to_groups(u, nb, nc, chunk, g_n):
    j_n = u.shape[1] // g_n
    return u.reshape(nb * nc, chunk, g_n, j_n).transpose(2, 0, 3, 1).reshape(g_n, nb * nc, j_n * chunk)


def _from_groups(y, nb, nc, chunk, g_n):
    j_n = y.shape[2] // chunk
    return y.reshape(g_n, nb * nc, j_n, chunk).transpose(1, 3, 0, 2).reshape(nb * nc * chunk, g_n * j_n)


def _mix_kernel(y_ref, u_ref, gu_ref, vn_ref, g1_ref, g2_ref, x_ref,
                d_ref, wglu_ref, bglu_ref, sguw_ref, sgub_ref, wb1_ref, wb2_ref, wout_ref,
                npost_ref, gt_ref, npre_ref, shf_ref, scf_ref, wr_ref,
                x1_ref, hp_ref, lg_ref, *, tm):
    y = jax.nn.gelu(y_ref[...].astype(F32) + d_ref[...] * u_ref[...].astype(F32))
    y_s5 = y * jax.nn.sigmoid(_dot(y.astype(BF16), wglu_ref[...]) + bglu_ref[...])

    lane = lax.broadcasted_iota(I32, (1, LANES), 1)
    m_lo = (lane < LANES // 2).astype(F32).astype(BF16)
    m_hi = (lane >= LANES // 2).astype(F32).astype(BF16)
    vn = vn_ref[...]
    chunks = []
    for c in range(tm // SGU_CHUNK):
        vc = vn[c * SGU_CHUNK:(c + 1) * SGU_CHUNK, :]
        tiles = []
        for q in range(vc.shape[1] // LANES):
            vt = vc[:, q * LANES:(q + 1) * LANES]
            rhs = jnp.concatenate([vt * m_lo, vt * m_hi], axis=0)
            tiles.append(_dot(sguw_ref[q], rhs))
        chunks.append(jnp.concatenate(tiles, axis=1) + sgub_ref[...])
    mixed = jnp.concatenate(chunks, axis=0)
    y_sgu = gu_ref[...].astype(F32) * mixed

    merged = (g1_ref[...].astype(F32) * _dot(y_s5.astype(BF16), wb1_ref[...])
              + g2_ref[...].astype(F32) * _dot(y_sgu.astype(BF16), wb2_ref[...]))
    mx = _dot(merged.astype(BF16), wout_ref[...])
    x1 = x_ref[...] + gt_ref[0] * _rms(mx, npost_ref[...])
    x1_ref[...] = x1
    hp = _rms(x1, npre_ref[...]) * (1.0 + scf_ref[0]) + shf_ref[0]
    _store_row_tiles(hp_ref, 0, hp)
    lg_ref[...] = _dot(hp.astype(BF16), wr_ref[...])


def _mix(y, u, gu, vn, g1, g2, x2, s5_d, wglu, bglu, sguw, sgub, wb1, wb2, wout,
         npost, gt, npre, shf, scf, wr, tokens_per_batch, tm):
    n, d = x2.shape
    d_s5, d_sgu, n_exp = y.shape[1], gu.shape[1], wr.shape[1]
    tiles_per_batch = tokens_per_batch // tm
    row = lambda i: (i, 0)
    fixed = lambda i: (0, 0)
    fixed3 = lambda i: (0, 0, 0)
    per_batch = lambda i: (i // tiles_per_batch, 0, 0)
    vec = lambda w: pl.BlockSpec((1, w), fixed)
    return pl.pallas_call(
        functools.partial(_mix_kernel, tm=tm),
        out_shape=(jax.ShapeDtypeStruct((n, d), F32), jax.ShapeDtypeStruct((n * d // LANES, LANES), F32),
                   jax.ShapeDtypeStruct((n, n_exp), F32)),
        grid=(n // tm,),
        in_specs=[pl.BlockSpec((tm, d_s5), row), pl.BlockSpec((tm, d_s5), row), pl.BlockSpec((tm, d_sgu), row),
                  pl.BlockSpec((tm, d_sgu), row), pl.BlockSpec((tm, d), row), pl.BlockSpec((tm, d), row),
                  pl.BlockSpec((tm, d), row),
                  vec(d_s5), pl.BlockSpec(wglu.shape, fixed), vec(d_s5),
                  pl.BlockSpec(sguw.shape, fixed3), pl.BlockSpec(sgub.shape, fixed),
                  pl.BlockSpec(wb1.shape, fixed), pl.BlockSpec(wb2.shape, fixed), pl.BlockSpec(wout.shape, fixed),
                  vec(d), pl.BlockSpec((1, 1, d), per_batch), vec(d),
                  pl.BlockSpec((1, 1, d), per_batch), pl.BlockSpec((1, 1, d), per_batch),
                  pl.BlockSpec(wr.shape, fixed)],
        out_specs=(pl.BlockSpec((tm, d), row), pl.BlockSpec((tm * d // LANES, LANES), row),
                   pl.BlockSpec((tm, n_exp), row)),
        compiler_params=_params(("parallel",)),
        name="mix",
    )(y, u, gu, vn, g1, g2, x2, s5_d.reshape(1, d_s5), wglu, bglu.reshape(1, d_s5), sguw, sgub,
      wb1, wb2, wout, npost.reshape(1, d), gt, npre.reshape(1, d), shf, scf, wr)


def _route_kernel(lg_ref, bias_ref, idx_ref, w_ref, cnt_ref, *, n_exp, tn):
    per_group = n_exp // N_EXPERT_GROUPS
    neg = jnp.float32(-jnp.inf)

    scores = jax.nn.sigmoid(lg_ref[...])
    sel = scores + bias_ref[...]
    gs = []
    for g in range(N_EXPERT_GROUPS):
        sg = sel[g * per_group:(g + 1) * per_group, :]
        m1 = jnp.max(sg, axis=0, keepdims=True)
        is_m1 = sg == m1
        n_m1 = jnp.sum(jnp.where(is_m1, 1.0, 0.0), axis=0, keepdims=True)
        rest = jnp.max(jnp.where(is_m1, neg, sg), axis=0, keepdims=True)
        gs.append(m1 + jnp.where(n_m1 >= 2.0, m1, rest))
    gsm = jnp.concatenate(gs, axis=0)
    g_iota = lax.broadcasted_iota(I32, gsm.shape, 0)
    e_iota = lax.broadcasted_iota(I32, sel.shape, 0).astype(F32)
    masked = []
    for g in range(N_EXPERT_GROUPS):
        mine = gsm[g:g + 1, :]
        beats = jnp.where(gsm > mine, 1.0, jnp.where(gsm == mine, jnp.where(g_iota < g, 1.0, 0.0), 0.0))
        n_beats = jnp.sum(beats, axis=0, keepdims=True)
        masked.append(jnp.where(n_beats < float(TOPK_GROUPS), sel[g * per_group:(g + 1) * per_group, :], neg))
    start = jnp.concatenate(masked, axis=0)
    selm = start
    picked, vals = [], []
    for _ in range(TOP_K):
        m = jnp.max(selm, axis=0, keepdims=True)
        first = jnp.min(jnp.where(selm == m, e_iota, float(n_exp)), axis=0, keepdims=True)
        one = e_iota == first
        picked.append(first)
        vals.append(jnp.sum(jnp.where(one, scores, 0.0), axis=0, keepdims=True))
        selm = jnp.where(one, neg, selm)
    idx_ref[...] = jnp.concatenate(picked, axis=0).astype(I32)
    wv = jnp.concatenate(vals, axis=0)
    w_ref[...] = wv / jnp.sum(wv, axis=0, keepdims=True) * ROUTE_SCALE
    hot = jnp.where(selm == neg, jnp.where(start == neg, 0.0, 1.0), 0.0)

    @pl.when(pl.program_id(0) == 0)
    def _():
        cnt_ref[...] = jnp.zeros_like(cnt_ref)

    cnt_ref[...] += jnp.sum(hot, axis=1, keepdims=True)


def _route(logits_t, bias, tn):
    n_exp, n = logits_t.shape
    col = lambda i: (0, i)
    fixed = lambda i: (0, 0)
    return pl.pallas_call(
        functools.partial(_route_kernel, n_exp=n_exp, tn=tn),
        out_shape=(jax.ShapeDtypeStruct((TOP_K, n), I32), jax.ShapeDtypeStruct((TOP_K, n), F32),
                   jax.ShapeDtypeStruct((n_exp, 1), F32)),
        grid=(n // tn,),
        in_specs=[pl.BlockSpec((n_exp, tn), col), pl.BlockSpec((n_exp, 1), fixed)],
        out_specs=(pl.BlockSpec((TOP_K, tn), col), pl.BlockSpec((TOP_K, tn), col),
                   pl.BlockSpec((n_exp, 1), fixed)),
        compiler_params=_params(("arbitrary",)),
        name="route",
    )(logits_t, bias.reshape(n_exp, 1))


def _dest_kernel(idx_ref, start_ref, dest_ref, carry_ref, *, n_exp, tn):
    @pl.when(pl.program_id(0) == 0)
    def _():
        carry_ref[...] = start_ref[...]

    idx = idx_ref[...]
    e_iota = lax.broadcasted_iota(I32, (n_exp, tn), 0)
    hot = jnp.zeros((n_exp, tn), F32)
    for k in range(TOP_K):
        hot = hot + jnp.where(e_iota == idx[k:k + 1, :], 1.0, 0.0)
    before = jnp.where(lax.broadcasted_iota(I32, (tn, tn), 0) < lax.broadcasted_iota(I32, (tn, tn), 1), 1.0, 0.0)
    rank = _dot(hot.astype(BF16), before.astype(BF16)) + carry_ref[...]
    dest_ref[...] = jnp.concatenate(
        [jnp.sum(jnp.where(e_iota == idx[k:k + 1, :], rank, 0.0), axis=0, keepdims=True)
         for k in range(TOP_K)], axis=0).astype(I32)
    carry_ref[...] += jnp.sum(hot, axis=1, keepdims=True)


def _dest(idx, start, tn):
    n = idx.shape[1]
    n_exp = start.shape[0]
    col = lambda i: (0, i)
    return pl.pallas_call(
        functools.partial(_dest_kernel, n_exp=n_exp, tn=tn),
        out_shape=jax.ShapeDtypeStruct((TOP_K, n), I32),
        grid=(n // tn,),
        in_specs=[pl.BlockSpec((TOP_K, tn), col), pl.BlockSpec((n_exp, 1), lambda i: (0, 0))],
        out_specs=pl.BlockSpec((TOP_K, tn), col),
        scratch_shapes=[pltpu.VMEM((n_exp, 1), F32)],
        compiler_params=_params(("arbitrary",)),
        name="dest",
    )(idx, start)


def _dispatch_kernel(dest_ref, hp_ref, xs_ref, sem, *, tm, pieces):
    def start(r, c):
        src = hp_ref.at[pl.ds(pl.multiple_of(r * pieces, pieces), pieces), :]
        for k in range(TOP_K):
            row = pl.multiple_of(dest_ref[r * TOP_K + k], pieces)
            pltpu.make_async_copy(src, xs_ref.at[pl.ds(row, pieces), :], sem).start(priority=k % 2)
        return c

    lax.fori_loop(0, tm, start, 0)
    for k in range(TOP_K):
        pltpu.make_async_copy(hp_ref, xs_ref.at[pl.ds(0, tm * pieces), :], sem).wait()


def _dispatch(dest, hp, cap, tm, pieces):
    n = hp.shape[0] // pieces
    return pl.pallas_call(
        functools.partial(_dispatch_kernel, tm=tm, pieces=pieces),
        out_shape=jax.ShapeDtypeStruct((cap * pieces, LANES), hp.dtype),
        grid=(n // tm,),
        in_specs=[pl.BlockSpec((tm * TOP_K,), lambda i: (i,), memory_space=pltpu.SMEM),
                  pl.BlockSpec((tm * pieces, LANES), lambda i: (i, 0))],
        out_specs=pl.BlockSpec(memory_space=pl.ANY),
        scratch_shapes=[pltpu.SemaphoreType.DMA(())],
        compiler_params=_params(("arbitrary",)),
        name="dispatch",
    )(dest, hp)


def _expert_kernel(be_ref, bv_ref, nu_ref, nx_ref, xs_ref, wg_hbm, wu_hbm, wd_hbm, ys_ref,
                   wg_f, wu_f, wd_f, wgu_s, wd_s, sem, *, rows):
    b = pl.program_id(0)
    valid = bv_ref[b]
    changed = jnp.logical_or(b == 0, be_ref[b] != be_ref[jnp.maximum(b - 1, 0)])
    d_e = wg_hbm.shape[2]

    def weight_copies(e):
        return (pltpu.make_async_copy(wg_hbm.at[e], wg_f, sem.at[0]),
                pltpu.make_async_copy(wu_hbm.at[e], wu_f, sem.at[1]),
                pltpu.make_async_copy(wd_hbm.at[e], wd_f, sem.at[2]))

    @pl.when(jnp.logical_and(changed, valid > 0))
    def _():
        @pl.when(b == 0)
        def _():
            for cp in weight_copies(be_ref[b]):
                cp.start()

        for cp in weight_copies(be_ref[b]):
            cp.wait()
        wgu_s[:, :d_e] = wg_f[...].astype(BF16)
        wgu_s[:, d_e:] = wu_f[...].astype(BF16)
        wd_s[...] = wd_f[...].astype(BF16)

        @pl.when(nx_ref[b] >= 0)
        def _():
            for cp in weight_copies(nx_ref[b]):
                cp.start()

    for s in range(rows // EXPERT_SUB):
        @pl.when(valid > s * EXPERT_SUB)
        def _():
            x = _load_row_tiles(xs_ref, s * EXPERT_SUB, EXPERT_SUB, wg_hbm.shape[1] // LANES)
            r_iota = lax.broadcasted_iota(I32, x.shape, 0) + s * EXPERT_SUB
            xb = jnp.where(r_iota < valid, x, 0.0).astype(BF16)
            gu = _dot(xb, wgu_s[...])
            gate, up = gu[:, :d_e], gu[:, d_e:]
            hid = (gate * jax.nn.sigmoid(gate) * up).astype(BF16)
            _store_row_tiles(ys_ref, s * EXPERT_SUB, _dot(hid, wd_s[...]))


def _experts(blk_expert, blk_valid, n_used, blk_next, xs, w_gate, w_up, w_down, rows):
    _, d, d_e = w_gate.shape
    w = LANES
    cap = xs.shape[0] // (d // LANES)
    rows_phys = rows * (d // LANES)
    rows_map = lambda b, be, bv, nu, nx: (jnp.minimum(b, nu[0] - 1), 0)
    hbm = pl.BlockSpec(memory_space=pl.ANY)
    grid_spec = pltpu.PrefetchScalarGridSpec(
        num_scalar_prefetch=4,
        grid=(cap // rows,),
        in_specs=[pl.BlockSpec((rows_phys, w), rows_map), hbm, hbm, hbm],
        out_specs=pl.BlockSpec((rows_phys, w), rows_map),
        scratch_shapes=[pltpu.VMEM((d, d_e), F32), pltpu.VMEM((d, d_e), F32), pltpu.VMEM((d_e, d), F32),
                        pltpu.VMEM((d, 2 * d_e), BF16), pltpu.VMEM((d_e, d), BF16),
                        pltpu.SemaphoreType.DMA((3,))],
    )
    return pl.pallas_call(
        functools.partial(_expert_kernel, rows=rows),
        out_shape=jax.ShapeDtypeStruct(xs.shape, F32),
        grid_spec=grid_spec,
        compiler_params=_params(("arbitrary",)),
        name="experts",
    )(blk_expert, blk_valid, n_used, blk_next, xs, w_gate, w_up, w_down)


def _combine_kernel(dest_ref, next_ref, ys_ref, wt_ref, hp_ref, x1_ref, wsgu_ref, wsd_ref, npost_ref, gt_ref,
                    o_ref, buf, sem, *, tm, pieces):
    i = pl.program_id(0)
    cur = i % 2

    parts = COMBINE_PARTS
    tp = tm // parts

    def gather(rows_ref, half, lo, hi):
        def start(r, c):
            slot = pl.ds(pl.multiple_of(r * pieces, pieces), pieces)
            for k in range(TOP_K):
                row = pl.multiple_of(rows_ref[r * TOP_K + k], pieces)
                pltpu.make_async_copy(ys_ref.at[pl.ds(row, pieces), :], buf.at[half, k, slot, :],
                                      sem.at[half]).start(priority=k % 2)
            return c

        lax.fori_loop(lo, hi, start, 0)

    @pl.when(i == 0)
    def _():
        gather(dest_ref, 0, 0, tm)

    for k in range(TOP_K):
        pltpu.make_async_copy(ys_ref.at[pl.ds(0, tm * pieces), :], buf.at[cur, k], sem.at[cur]).wait()

    d_sh = wsd_ref.shape[0]
    for p in range(parts):
        @pl.when(i + 1 < pl.num_programs(0))
        def _():
            gather(next_ref, 1 - cur, p * tp, (p + 1) * tp)

        hb = _load_row_tiles(hp_ref, p * tp, tp, pieces).astype(BF16)
        gu = _dot(hb, wsgu_ref[...])
        gate, up = gu[:, :d_sh], gu[:, d_sh:]
        fx = _dot((gate * jax.nn.sigmoid(gate) * up).astype(BF16), wsd_ref[...])
        wt = wt_ref[p * tp:(p + 1) * tp, :]
        for k in range(TOP_K):
            fx = fx + wt[:, k:k + 1] * _load_row_tiles(buf, p * tp, tp, pieces, lead=(cur, k))
        rows = slice(p * tp, (p + 1) * tp)
        o_ref[rows, :] = x1_ref[rows, :] + gt_ref[0] * _rms(fx, npost_ref[...])


def _combine(dest, ys, wt, hp, x1, wsgu, wsd, npost, gt, tokens_per_batch, tm):
    n, d = x1.shape
    pieces = d // LANES
    w = LANES
    tiles_per_batch = tokens_per_batch // tm
    row = lambda i: (i, 0)
    fixed = lambda i: (0, 0)
    return pl.pallas_call(
        functools.partial(_combine_kernel, tm=tm, pieces=pieces),
        out_shape=jax.ShapeDtypeStruct((n, d), F32),
        grid=(n // tm,),
        in_specs=[pl.BlockSpec((tm * TOP_K,), lambda i: (i,), memory_space=pltpu.SMEM),
                  pl.BlockSpec((tm * TOP_K,), lambda i: (jnp.minimum(i + 1, n // tm - 1),), memory_space=pltpu.SMEM),
                  pl.BlockSpec(memory_space=pl.ANY),
                  pl.BlockSpec((tm, TOP_K), row), pl.BlockSpec((tm * pieces, w), row), pl.BlockSpec((tm, d), row),
                  pl.BlockSpec(wsgu.shape, fixed), pl.BlockSpec(wsd.shape, fixed),
                  pl.BlockSpec((1, d), fixed), pl.BlockSpec((1, 1, d), lambda i: (i // tiles_per_batch, 0, 0))],
        out_specs=pl.BlockSpec((tm, d), row),
        scratch_shapes=[pltpu.VMEM((2, TOP_K, tm * pieces, w), F32), pltpu.SemaphoreType.DMA((2,))],
        compiler_params=_params(("arbitrary",)),
        name="combine",
    )(dest, dest, ys, wt, hp, x1, wsgu, wsd, npost.reshape(1, d), gt)


def _layer(x, ctx, mod_x, mod_c, norm_pre_mix, norm_post_mix, norm_pre_ffn, norm_post_ffn,
           w_in, s5_a_re, s5_a_im, s5_log_dt, s5_b_re, s5_b_im, s5_c_re, s5_c_im, s5_d, w_glu, b_glu,
           sgu_norm, sgu_w, sgu_b, w_branch_s5, w_branch_sgu, w_out,
           w_router, router_bias, w_exp_gate, w_exp_up, w_exp_down, w_sh_gate, w_sh_up, w_sh_down):
    bsz, t_len, d = x.shape
    c_len = ctx.shape[1]
    n = bsz * t_len
    d_s5, d_sgu = w_glu.shape[0], sgu_norm.shape[0]
    g_n = d_s5 // S5_GROUP_CH
    n_exp = w_router.shape[1]
    chunk = S5_CHUNK
    tm = min(512, t_len)
    assert t_len % tm == 0 and tm % SGU_CHUNK == 0 and t_len % chunk == 0 and c_len % chunk == 0

    sh_m, sc_m, gt_m, sh_f, sc_f, gt_f = [v.reshape(bsz, 1, d) for v in jnp.split(mod_x, 6, axis=-1)]
    csh_m, csc_m = mod_c[:d], mod_c[d:2 * d]

    w_in_bf = w_in.astype(BF16)

    nc_lat, nc_ctx = t_len // chunk, c_len // chunk
    n_steps = max(1, (max(nc_lat, nc_ctx) - 1).bit_length())
    m_t, ke_t, kc_t, alp = _s5_tables(s5_a_re, s5_a_im, s5_log_dt, s5_b_re, s5_b_im, s5_c_re, s5_c_im,
                                      chunk, n_steps)

    ctx2 = ctx.reshape(bsz * c_len, d)
    u_ctx = _inproj_u(ctx2, norm_pre_mix, csh_m, csc_m, w_in_bf[:, :d_s5], min(512, bsz * c_len))
    zero_init = jnp.zeros((g_n, bsz, 4 * S5_STATE), F32)
    (ctx_final,) = _s5(_to_groups(u_ctx, bsz, nc_ctx, chunk, g_n), ke_t, alp, zero_init, nb=bsz, nc=nc_ctx)

    x2 = x.reshape(n, d)
    u, gu, vn, g1, g2 = _inproj(x2, norm_pre_mix, sh_m, sc_m, w_in_bf, sgu_norm, t_len, tm, d_s5, d_sgu)
    _, yg = _s5(_to_groups(u, bsz, nc_lat, chunk, g_n), ke_t, alp, ctx_final, m_t, kc_t, nb=bsz, nc=nc_lat)
    y = _from_groups(yg, bsz, nc_lat, chunk, g_n)

    ch = d_sgu // SGU_GROUPS
    per_tile = LANES // ch
    sguw = sgu_w.reshape(SGU_GROUPS // per_tile, per_tile, SGU_CHUNK, SGU_CHUNK)
    sguw = sguw.transpose(0, 2, 1, 3).reshape(SGU_GROUPS // per_tile, SGU_CHUNK, per_tile * SGU_CHUNK).astype(BF16)
    sgub = jnp.repeat(sgu_b.T, ch, axis=1)

    x1, hp, logits = _mix(y, u, gu, vn, g1, g2, x2, s5_d, w_glu.astype(BF16), b_glu, sguw, sgub,
                          w_branch_s5.astype(BF16), w_branch_sgu.astype(BF16), w_out.astype(BF16),
                          norm_post_mix, gt_m, norm_pre_ffn, sh_f, sc_f, w_router.astype(BF16), t_len, tm)

    tn = min(512, n)
    idx, wts, counts = _route(logits.T, router_bias, tn)
    rows = EXPERT_ROWS
    cnt = counts.reshape(n_exp).astype(I32)
    nblk = (cnt + rows - 1) // rows
    blk_end = jnp.cumsum(nblk)
    blk_start = blk_end - nblk
    n_blocks = (n * TOP_K) // rows + n_exp
    cap = n_blocks * rows
    b_ids = jnp.arange(n_blocks, dtype=I32)
    blk_expert = jnp.minimum(jnp.sum((blk_end[None, :] <= b_ids[:, None]).astype(I32), axis=1), n_exp - 1)
    blk_valid = jnp.clip(cnt[blk_expert] - (b_ids - blk_start[blk_expert]) * rows, 0, rows).astype(I32)
    dest = _dest(idx, (blk_start * rows).astype(F32).reshape(n_exp, 1), tn)
    pieces = d // LANES
    dest = dest.T.reshape(n * TOP_K) * pieces

    td = min(256, n)
    xs = _dispatch(dest, hp, cap, td, pieces)
    n_used = blk_end[-1:].astype(I32)
    after = blk_end[blk_expert]
    blk_next = jnp.where(after < n_used[0], blk_expert[jnp.minimum(after, n_blocks - 1)], -1).astype(I32)
    ys = _experts(blk_expert, blk_valid, n_used, blk_next, xs, w_exp_gate, w_exp_up, w_exp_down, rows)
    wsgu = jnp.concatenate([w_sh_gate, w_sh_up], axis=1).astype(BF16)
    out = _combine(dest, ys, wts.T, hp, x1, wsgu, w_sh_down.astype(BF16), norm_post_ffn, gt_f, t_len, td)
    return out.reshape(bsz, t_len, d)


def kernel(x, c, ctx, c_ctx, w_mod, b_mod, norm_pre_mix, norm_post_mix, norm_pre_ffn, norm_post_ffn, w_in, s5_a_re, s5_a_im, s5_log_dt, s5_b_re, s5_b_im, s5_c_re, s5_c_im, s5_d, w_glu, b_glu, sgu_norm, sgu_w, sgu_b, w_branch_s5, w_branch_sgu, w_out, w_router, router_bias, w_exp_gate, w_exp_up, w_exp_down, w_sh_gate, w_sh_up, w_sh_down):
    depth = w_mod.shape[0]
    assert depth == 1, "the context stream is only carried through the last layer's S5 states"
    bsz = x.shape[0]
    pad = (-(bsz + 1)) % 8
    cpad = jnp.concatenate([c, c_ctx[None, :], jnp.zeros((pad, c.shape[1]), c.dtype)], axis=0)
    mod = _modulation(cpad, w_mod[0], b_mod[0])
    return _layer(x, ctx, mod[:bsz], mod[bsz], norm_pre_mix[0], norm_post_mix[0], norm_pre_ffn[0],
                  norm_post_ffn[0], w_in[0], s5_a_re[0], s5_a_im[0], s5_log_dt[0], s5_b_re[0], s5_b_im[0],
                  s5_c_re[0], s5_c_im[0], s5_d[0], w_glu[0], b_glu[0], sgu_norm[0], sgu_w[0], sgu_b[0],
                  w_branch_s5[0], w_branch_sgu[0], w_out[0], w_router[0], router_bias[0],
                  w_exp_gate[0], w_exp_up[0], w_exp_down[0], w_sh_gate[0], w_sh_up[0], w_sh_down[0])
```

```python
import functools
import math

import jax
import jax.numpy as jnp
from jax import lax
from jax.experimental import pallas as pl
from jax.experimental.pallas import tpu as pltpu

F32 = jnp.float32
BF16 = jnp.bfloat16
I32 = jnp.int32

EPS = 1e-6
S5_GROUP_CH = 16
S5_STATE = 64
S5_CHUNK = 64
SGU_GROUPS = 8
SGU_CHUNK = 128
N_EXPERT_GROUPS = 8
TOPK_GROUPS = 4
TOP_K = 8
ROUTE_SCALE = 2.5
LANES = 128
EXPERT_ROWS = 256
VMEM_LIMIT = 56 * 1024 * 1024


def _params(sem):
    return pltpu.CompilerParams(dimension_semantics=sem, vmem_limit_bytes=VMEM_LIMIT)


def _rms(v, g):
    return v * lax.rsqrt(jnp.mean(v * v, axis=-1, keepdims=True) + EPS) * g


def _dot(a, b):
    return jnp.dot(a, b, preferred_element_type=F32)


def _store_row_tiles(ref, first_row, val):
    pieces = val.shape[1] // LANES
    for s in range(pieces):
        ref[pl.ds(first_row * pieces + s, val.shape[0], stride=pieces), :] = val[:, s * LANES:(s + 1) * LANES]


def _load_row_tiles(ref, first_row, n_rows, pieces, lead=()):
    return jnp.concatenate([ref[lead + (pl.ds(first_row * pieces + s, n_rows, stride=pieces), slice(None))]
                            for s in range(pieces)], axis=1)


def _mod_kernel(c_ref, w_ref, b_ref, o_ref):
    cv = c_ref[...]
    s = cv * jax.nn.sigmoid(cv)
    o_ref[...] = _dot(s.astype(BF16), w_ref[...].astype(BF16)) + b_ref[...]


def _modulation(cpad, w_mod, b_mod):
    d, n6 = w_mod.shape
    tn = 1024
    return pl.pallas_call(
        _mod_kernel,
        out_shape=jax.ShapeDtypeStruct((cpad.shape[0], n6), F32),
        grid=(n6 // tn,),
        in_specs=[pl.BlockSpec(cpad.shape, lambda j: (0, 0)),
                  pl.BlockSpec((d, tn), lambda j: (0, j)),
                  pl.BlockSpec((1, tn), lambda j: (0, j))],
        out_specs=pl.BlockSpec((cpad.shape[0], tn), lambda j: (0, j)),
        compiler_params=_params(("parallel",)),
        name="mod",
    )(cpad, w_mod, b_mod.reshape(1, n6))


def _inproj_kernel(x_ref, g_ref, sh_ref, sc_ref, w_ref, sgun_ref,
                   u_ref, gu_ref, vn_ref, g1_ref, g2_ref, *, d_s5, d_sgu, d_model):
    h = _rms(x_ref[...], g_ref[...])
    hb = (h * (1.0 + sc_ref[0]) + sh_ref[0]).astype(BF16)
    o1, o2, o3, o4 = d_s5, d_s5 + d_sgu, d_s5 + 2 * d_sgu, d_s5 + 2 * d_sgu + d_model
    u_ref[...] = _dot(hb, w_ref[:, 0:o1]).astype(BF16)
    gu_ref[...] = jax.nn.gelu(_dot(hb, w_ref[:, o1:o2])).astype(BF16)
    v = jax.nn.gelu(_dot(hb, w_ref[:, o2:o3]))
    vn_ref[...] = _rms(v, sgun_ref[...]).astype(BF16)
    g1_ref[...] = jax.nn.sigmoid(_dot(hb, w_ref[:, o3:o4])).astype(BF16)
    g2_ref[...] = jax.nn.sigmoid(_dot(hb, w_ref[:, o4:o4 + d_model])).astype(BF16)


def _inproj(x2, norm_g, shift, scale, w_in_bf, sgu_norm, tokens_per_batch, tm, d_s5, d_sgu):
    n, d = x2.shape
    tiles_per_batch = tokens_per_batch // tm
    row = lambda i: (i, 0)
    fixed = lambda i: (0, 0)
    per_batch = lambda i: (i // tiles_per_batch, 0, 0)
    return pl.pallas_call(
        functools.partial(_inproj_kernel, d_s5=d_s5, d_sgu=d_sgu, d_model=d),
        out_shape=(jax.ShapeDtypeStruct((n, d_s5), BF16), jax.ShapeDtypeStruct((n, d_sgu), BF16),
                   jax.ShapeDtypeStruct((n, d_sgu), BF16), jax.ShapeDtypeStruct((n, d), BF16),
                   jax.ShapeDtypeStruct((n, d), BF16)),
        grid=(n // tm,),
        in_specs=[pl.BlockSpec((tm, d), row), pl.BlockSpec((1, d), fixed),
                  pl.BlockSpec((1, 1, d), per_batch), pl.BlockSpec((1, 1, d), per_batch),
                  pl.BlockSpec(w_in_bf.shape, fixed), pl.BlockSpec((1, d_sgu), fixed)],
        out_specs=(pl.BlockSpec((tm, d_s5), row), pl.BlockSpec((tm, d_sgu), row),
                   pl.BlockSpec((tm, d_sgu), row), pl.BlockSpec((tm, d), row), pl.BlockSpec((tm, d), row)),
        compiler_params=_params(("parallel",)),
        name="inproj",
    )(x2, norm_g.reshape(1, d), shift, scale, w_in_bf, sgu_norm.reshape(1, d_sgu))


def _inproj_u_kernel(x_ref, g_ref, sh_ref, sc_ref, w_ref, u_ref):
    h = _rms(x_ref[...], g_ref[...])
    hb = (h * (1.0 + sc_ref[...]) + sh_ref[...]).astype(BF16)
    u_ref[...] = _dot(hb, w_ref[...]).astype(BF16)


def _inproj_u(x2, norm_g, shift, scale, w_u_bf, tm):
    n, d = x2.shape
    d_s5 = w_u_bf.shape[1]
    row = lambda i: (i, 0)
    fixed = lambda i: (0, 0)
    return pl.pallas_call(
        _inproj_u_kernel,
        out_shape=jax.ShapeDtypeStruct((n, d_s5), BF16),
        grid=(n // tm,),
        in_specs=[pl.BlockSpec((tm, d), row), pl.BlockSpec((1, d), fixed), pl.BlockSpec((1, d), fixed),
                  pl.BlockSpec((1, d), fixed), pl.BlockSpec(w_u_bf.shape, fixed)],
        out_specs=pl.BlockSpec((tm, d_s5), row),
        compiler_params=_params(("parallel",)),
        name="inproj_ctx",
    )(x2, norm_g.reshape(1, d), shift.reshape(1, d), scale.reshape(1, d), w_u_bf)


def _s5_tables(a_re, a_im, log_dt, b_re, b_im, c_re, c_im, chunk, n_steps):
    hi = lax.Precision.HIGHEST
    g_n, p_n, j_n = a_re.shape[1], a_re.shape[2], b_re.shape[3]
    dt = jnp.exp(log_dt)[..., None]
    lam_re, lam_im = dt * a_re, dt * a_im
    ab_re, ab_im = jnp.exp(lam_re) * jnp.cos(lam_im), jnp.exp(lam_re) * jnp.sin(lam_im)
    den = a_re * a_re + a_im * a_im
    q_re = ((ab_re - 1.0) * a_re + ab_im * a_im) / den
    q_im = (ab_im * a_re - (ab_re - 1.0) * a_im) / den
    bb_re = q_re[..., None] * b_re - q_im[..., None] * b_im
    bb_im = q_re[..., None] * b_im + q_im[..., None] * b_re
    k = jnp.arange(chunk + 1, dtype=F32)[:, None, None, None]
    mag = jnp.exp(k * lam_re[None])
    pw_re, pw_im = mag * jnp.cos(k * lam_im[None]), mag * jnp.sin(k * lam_im[None])

    def lag_kernels(d):
        x_re = pw_re[:chunk, d, :, :, None] * bb_re[d][None] - pw_im[:chunk, d, :, :, None] * bb_im[d][None]
        x_im = pw_re[:chunk, d, :, :, None] * bb_im[d][None] + pw_im[:chunk, d, :, :, None] * bb_re[d][None]
        return (jnp.einsum('gjp,kgpi->gkij', c_re[d], x_re, precision=hi)
                - jnp.einsum('gjp,kgpi->gkij', c_im[d], x_im, precision=hi))

    kf, kb = lag_kernels(0), lag_kernels(1)
    lags = jnp.concatenate([kb[:, 1:][:, ::-1], kf[:, 0:1] + kb[:, 0:1], kf[:, 1:],
                            jnp.zeros_like(kf[:, 0:1])], axis=1)
    lags = lags.transpose(0, 2, 3, 1)

    pf_re, pf_im = pw_re[:chunk, 0][::-1], pw_im[:chunk, 0][::-1]
    pb_re, pb_im = pw_re[:chunk, 1], pw_im[:chunk, 1]

    def state_in(p_re, p_im, d):
        e_re = p_re[..., None] * bb_re[d][None] - p_im[..., None] * bb_im[d][None]
        e_im = p_re[..., None] * bb_im[d][None] + p_im[..., None] * bb_re[d][None]
        return e_re.transpose(1, 3, 0, 2), e_im.transpose(1, 3, 0, 2)

    ef_re, ef_im = state_in(pf_re, pf_im, 0)
    eb_re, eb_im = state_in(pb_re, pb_im, 1)
    ke = jnp.concatenate([ef_re, eb_re, ef_im, eb_im], axis=-1).reshape(g_n, chunk * j_n, 4 * p_n)

    qf_re, qf_im = pw_re[1:chunk + 1, 0], pw_im[1:chunk + 1, 0]
    qb_re, qb_im = pw_re[1:chunk + 1, 1][::-1], pw_im[1:chunk + 1, 1][::-1]

    def state_out(p_re, p_im, d):
        cr, ci = c_re[d].transpose(0, 2, 1), c_im[d].transpose(0, 2, 1)
        pr, pi = p_re.transpose(1, 2, 0), p_im.transpose(1, 2, 0)
        o_re = pr[:, :, None, :] * cr[..., None] - pi[:, :, None, :] * ci[..., None]
        o_im = pr[:, :, None, :] * ci[..., None] + pi[:, :, None, :] * cr[..., None]
        return o_re, o_im

    of_re, of_im = state_out(qf_re, qf_im, 0)
    ob_re, ob_im = state_out(qb_re, qb_im, 1)
    kc = jnp.concatenate([of_re, ob_re, -of_im, -ob_im], axis=1).reshape(g_n, 4 * p_n, chunk * j_n)

    e = (chunk * (2.0 ** jnp.arange(n_steps, dtype=F32)))[:, None, None, None]
    mg = jnp.exp(e * lam_re[None])
    al_re, al_im = mg * jnp.cos(e * lam_im[None]), mg * jnp.sin(e * lam_im[None])
    alp = jnp.concatenate([al_re[:, 0], al_re[:, 1], al_im[:, 0], al_im[:, 1]], axis=-1)
    return lags, ke.astype(BF16), kc.astype(BF16), alp.transpose(1, 0, 2)


def _build_toeplitz(lag_ref, m_s, chunk):
    j_n = lag_ref.shape[1]
    per_tile = LANES // chunk
    low = lax.broadcasted_iota(I32, (chunk, LANES), 1) < chunk
    for i in range(j_n):
        for q in range(j_n // per_tile):
            tile = None
            for p in range(per_tile):
                j = q * per_tile + p
                row = jnp.broadcast_to(lag_ref[0, i, j:j + 1, :], (chunk, LANES))
                rot = pltpu.roll(row, (p * chunk + LANES - (chunk - 1)) % LANES, 1, stride=1, stride_axis=0)
                tile = rot if tile is None else jnp.where(low, tile, rot)
            m_s[i * chunk:(i + 1) * chunk, q * LANES:(q + 1) * LANES] = tile.astype(BF16)


def _s5_kernel(*refs, nb, nc, readout):
    if readout:
        u_ref, ke_ref, alp_ref, init_ref, lag_ref, kc_ref, fin_ref, y_ref, m_s = refs
    else:
        u_ref, ke_ref, alp_ref, init_ref, fin_ref = refs
    rows = nb * nc
    half = 2 * S5_STATE
    u = u_ref[0]
    e = _dot(u, ke_ref[0])
    er, ei = e[:, :half], e[:, half:]
    r_idx = lax.broadcasted_iota(I32, (rows, half), 0)
    if nc & (nc - 1) == 0:
        n_idx, b_idx = r_idx & (nc - 1), r_idx >> (nc.bit_length() - 1)
    else:
        n_idx, b_idx = lax.rem(r_idx, nc), lax.div(r_idx, nc)
    is_f = lax.broadcasted_iota(I32, (rows, half), 1) < S5_STATE
    seen = jnp.where(is_f, n_idx, nc - 1 - n_idx)
    init = init_ref[0]
    init_r = jnp.zeros((rows, half), F32)
    init_i = jnp.zeros((rows, half), F32)
    for b in range(nb):
        init_r = jnp.where(b_idx == b, init[b:b + 1, :half], init_r)
        init_i = jnp.where(b_idx == b, init[b:b + 1, half:], init_i)
    alp = alp_ref[0]
    ar, ai = alp[0:1, :half], alp[0:1, half:]
    er = er + jnp.where(seen == 0, ar * init_r - ai * init_i, 0.0)
    ei = ei + jnp.where(seen == 0, ar * init_i + ai * init_r, 0.0)

    def from_prev(v, dist):
        return jnp.where(is_f, pltpu.roll(v, dist, 0), pltpu.roll(v, rows - dist, 0))

    d, s = 1, 0
    while d < nc:
        ar, ai = alp[s:s + 1, :half], alp[s:s + 1, half:]
        sr = jnp.where(seen >= d, from_prev(er, d), 0.0)
        si = jnp.where(seen >= d, from_prev(ei, d), 0.0)
        er, ei = er + (ar * sr - ai * si), ei + (ar * si + ai * sr)
        d, s = d * 2, s + 1
    is_f_row = lax.broadcasted_iota(I32, (1, half), 1) < S5_STATE
    for b in range(nb):
        lo, hi = b * nc, b * nc + nc - 1
        fin_ref[0, b:b + 1, :half] = jnp.where(is_f_row, er[hi:hi + 1], er[lo:lo + 1])
        fin_ref[0, b:b + 1, half:] = jnp.where(is_f_row, ei[hi:hi + 1], ei[lo:lo + 1])
    if readout:
        if nc > 1:
            sin_r = jnp.where(seen >= 1, from_prev(er, 1), init_r)
            sin_i = jnp.where(seen >= 1, from_prev(ei, 1), init_i)
        else:
            sin_r, sin_i = init_r, init_i
        s_in = jnp.concatenate([sin_r, sin_i], axis=1).astype(BF16)
        _build_toeplitz(lag_ref, m_s, S5_CHUNK)
        y_ref[0] = (_dot(u, m_s[...]) + _dot(s_in, kc_ref[0])).astype(BF16)


def _s5(ug, ke, alp, init, m=None, kc=None, *, nb, nc):
    g_n, rows, width = ug.shape
    readout = m is not None
    grp = lambda g: (g, 0, 0)
    in_specs = [pl.BlockSpec((1, rows, width), grp), pl.BlockSpec((1,) + ke.shape[1:], grp),
                pl.BlockSpec((1,) + alp.shape[1:], grp), pl.BlockSpec((1,) + init.shape[1:], grp)]
    out_shape = [jax.ShapeDtypeStruct(init.shape, F32)]
    out_specs = [pl.BlockSpec((1,) + init.shape[1:], grp)]
    args = [ug, ke, alp, init]
    scratch = []
    if readout:
        assert m.shape[3] == LANES and 2 * S5_CHUNK == LANES
        in_specs += [pl.BlockSpec((1,) + m.shape[1:], lambda g: (g, 0, 0, 0)),
                     pl.BlockSpec((1,) + kc.shape[1:], grp)]
        out_shape.append(jax.ShapeDtypeStruct(ug.shape, BF16))
        out_specs.append(pl.BlockSpec((1, rows, width), grp))
        args += [m, kc]
        scratch = [pltpu.VMEM((width, width), BF16)]
    return pl.pallas_call(
        functools.partial(_s5_kernel, nb=nb, nc=nc, readout=readout),
        out_shape=tuple(out_shape), grid=(g_n,), in_specs=in_specs, out_specs=tuple(out_specs),
        scratch_shapes=scratch,
        compiler_params=_params(("parallel",)),
        name="s5_readout" if readout else "s5_state",
    )(*args)


def _to_groups(u, nb, nc, chunk, g_n):
    j_n = u.shape[1] // g_n
    return u.reshape(nb * nc, chunk, g_n, j_n).transpose(2, 0, 3, 1).reshape(g_n, nb * nc, j_n * chunk)


def _from_groups(y, nb, nc, chunk, g_n):
    j_n = y.shape[2] // chunk
    return y.reshape(g_n, nb * nc, j_n, chunk).transpose(1, 3, 0, 2).reshape(nb * nc * chunk, g_n * j_n)


def _mix_kernel(y_ref, u_ref, gu_ref, vn_ref, g1_ref, g2_ref, x_ref,
                d_ref, wglu_ref, bglu_ref, sguw_ref, sgub_ref, wb1_ref, wb2_ref, wout_ref,
                npost_ref, gt_ref, npre_ref, shf_ref, scf_ref, wr_ref,
                x1_ref, hp_ref, lg_ref, *, tm):
    y = jax.nn.gelu(y_ref[...].astype(F32) + d_ref[...] * u_ref[...].astype(F32))
    y_s5 = y * jax.nn.sigmoid(_dot(y.astype(BF16), wglu_ref[...]) + bglu_ref[...])

    lane = lax.broadcasted_iota(I32, (1, LANES), 1)
    m_lo = (lane < LANES // 2).astype(F32).astype(BF16)
    m_hi = (lane >= LANES // 2).astype(F32).astype(BF16)
    vn = vn_ref[...]
    chunks = []
    for c in range(tm // SGU_CHUNK):
        vc = vn[c * SGU_CHUNK:(c + 1) * SGU_CHUNK, :]
        tiles = []
        for q in range(vc.shape[1] // LANES):
            vt = vc[:, q * LANES:(q + 1) * LANES]
            rhs = jnp.concatenate([vt * m_lo, vt * m_hi], axis=0)
            tiles.append(_dot(sguw_ref[q], rhs))
        chunks.append(jnp.concatenate(tiles, axis=1) + sgub_ref[...])
    mixed = jnp.concatenate(chunks, axis=0)
    y_sgu = gu_ref[...].astype(F32) * mixed

    merged = (g1_ref[...].astype(F32) * _dot(y_s5.astype(BF16), wb1_ref[...])
              + g2_ref[...].astype(F32) * _dot(y_sgu.astype(BF16), wb2_ref[...]))
    mx = _dot(merged.astype(BF16), wout_ref[...])
    x1 = x_ref[...] + gt_ref[0] * _rms(mx, npost_ref[...])
    x1_ref[...] = x1
    hp = _rms(x1, npre_ref[...]) * (1.0 + scf_ref[0]) + shf_ref[0]
    _store_row_tiles(hp_ref, 0, hp)
    lg_ref[...] = _dot(hp.astype(BF16), wr_ref[...])


def _mix(y, u, gu, vn, g1, g2, x2, s5_d, wglu, bglu, sguw, sgub, wb1, wb2, wout,
         npost, gt, npre, shf, scf, wr, tokens_per_batch, tm):
    n, d = x2.shape
    d_s5, d_sgu, n_exp = y.shape[1], gu.shape[1], wr.shape[1]
    tiles_per_batch = tokens_per_batch // tm
    row = lambda i: (i, 0)
    fixed = lambda i: (0, 0)
    fixed3 = lambda i: (0, 0, 0)
    per_batch = lambda i: (i // tiles_per_batch, 0, 0)
    vec = lambda w: pl.BlockSpec((1, w), fixed)
    return pl.pallas_call(
        functools.partial(_mix_kernel, tm=tm),
        out_shape=(jax.ShapeDtypeStruct((n, d), F32), jax.ShapeDtypeStruct((n * d // LANES, LANES), F32),
                   jax.ShapeDtypeStruct((n, n_exp), F32)),
        grid=(n // tm,),
        in_specs=[pl.BlockSpec((tm, d_s5), row), pl.BlockSpec((tm, d_s5), row), pl.BlockSpec((tm, d_sgu), row),
                  pl.BlockSpec((tm, d_sgu), row), pl.BlockSpec((tm, d), row), pl.BlockSpec((tm, d), row),
                  pl.BlockSpec((tm, d), row),
                  vec(d_s5), pl.BlockSpec(wglu.shape, fixed), vec(d_s5),
                  pl.BlockSpec(sguw.shape, fixed3), pl.BlockSpec(sgub.shape, fixed),
                  pl.BlockSpec(wb1.shape, fixed), pl.BlockSpec(wb2.shape, fixed), pl.BlockSpec(wout.shape, fixed),
                  vec(d), pl.BlockSpec((1, 1, d), per_batch), vec(d),
                  pl.BlockSpec((1, 1, d), per_batch), pl.BlockSpec((1, 1, d), per_batch),
                  pl.BlockSpec(wr.shape, fixed)],
        out_specs=(pl.BlockSpec((tm, d), row), pl.BlockSpec((tm * d // LANES, LANES), row),
                   pl.BlockSpec((tm, n_exp), row)),
        compiler_params=_params(("parallel",)),
        name="mix",
    )(y, u, gu, vn, g1, g2, x2, s5_d.reshape(1, d_s5), wglu, bglu.reshape(1, d_s5), sguw, sgub,
      wb1, wb2, wout, npost.reshape(1, d), gt, npre.reshape(1, d), shf, scf, wr)


def _route_kernel(lg_ref, bias_ref, idx_ref, w_ref, cnt_ref, *, n_exp, tn):
    per_group = n_exp // N_EXPERT_GROUPS
    neg = jnp.float32(-jnp.inf)

    scores = jax.nn.sigmoid(lg_ref[...])
    sel = scores + bias_ref[...]
    gs = []
    for g in range(N_EXPERT_GROUPS):
        sg = sel[g * per_group:(g + 1) * per_group, :]
        m1 = jnp.max(sg, axis=0, keepdims=True)
        is_m1 = sg == m1
        n_m1 = jnp.sum(jnp.where(is_m1, 1.0, 0.0), axis=0, keepdims=True)
        rest = jnp.max(jnp.where(is_m1, neg, sg), axis=0, keepdims=True)
        gs.append(m1 + jnp.where(n_m1 >= 2.0, m1, rest))
    gsm = jnp.concatenate(gs, axis=0)
    g_iota = lax.broadcasted_iota(I32, gsm.shape, 0)
    e_iota = lax.broadcasted_iota(I32, sel.shape, 0).astype(F32)
    masked = []
    for g in range(N_EXPERT_GROUPS):
        mine = gsm[g:g + 1, :]
        beats = jnp.where(gsm > mine, 1.0, jnp.where(gsm == mine, jnp.where(g_iota < g, 1.0, 0.0), 0.0))
        n_beats = jnp.sum(beats, axis=0, keepdims=True)
        masked.append(jnp.where(n_beats < float(TOPK_GROUPS), sel[g * per_group:(g + 1) * per_group, :], neg))
    start = jnp.concatenate(masked, axis=0)
    selm = start
    picked, vals = [], []
    for _ in range(TOP_K):
        m = jnp.max(selm, axis=0, keepdims=True)
        first = jnp.min(jnp.where(selm == m, e_iota, float(n_exp)), axis=0, keepdims=True)
        one = e_iota == first
        picked.append(first)
        vals.append(jnp.sum(jnp.where(one, scores, 0.0), axis=0, keepdims=True))
        selm = jnp.where(one, neg, selm)
    idx_ref[...] = jnp.concatenate(picked, axis=0).astype(I32)
    wv = jnp.concatenate(vals, axis=0)
    w_ref[...] = wv / jnp.sum(wv, axis=0, keepdims=True) * ROUTE_SCALE
    hot = jnp.where(selm == neg, jnp.where(start == neg, 0.0, 1.0), 0.0)

    @pl.when(pl.program_id(0) == 0)
    def _():
        cnt_ref[...] = jnp.zeros_like(cnt_ref)

    cnt_ref[...] += jnp.sum(hot, axis=1, keepdims=True)


def _route(logits_t, bias, tn):
    n_exp, n = logits_t.shape
    col = lambda i: (0, i)
    fixed = lambda i: (0, 0)
    return pl.pallas_call(
        functools.partial(_route_kernel, n_exp=n_exp, tn=tn),
        out_shape=(jax.ShapeDtypeStruct((TOP_K, n), I32), jax.ShapeDtypeStruct((TOP_K, n), F32),
                   jax.ShapeDtypeStruct((n_exp, 1), F32)),
        grid=(n // tn,),
        in_specs=[pl.BlockSpec((n_exp, tn), col), pl.BlockSpec((n_exp, 1), fixed)],
        out_specs=(pl.BlockSpec((TOP_K, tn), col), pl.BlockSpec((TOP_K, tn), col),
                   pl.BlockSpec((n_exp, 1), fixed)),
        compiler_params=_params(("arbitrary",)),
        name="route",
    )(logits_t, bias.reshape(n_exp, 1))


def _dest_kernel(idx_ref, start_ref, dest_ref, carry_ref, *, n_exp, tn):
    @pl.when(pl.program_id(0) == 0)
    def _():
        carry_ref[...] = start_ref[...]

    idx = idx_ref[...]
    e_iota = lax.broadcasted_iota(I32, (n_exp, tn), 0)
    hot = jnp.zeros((n_exp, tn), F32)
    for k in range(TOP_K):
        hot = hot + jnp.where(e_iota == idx[k:k + 1, :], 1.0, 0.0)
    before = jnp.where(lax.broadcasted_iota(I32, (tn, tn), 0) < lax.broadcasted_iota(I32, (tn, tn), 1), 1.0, 0.0)
    rank = _dot(hot.astype(BF16), before.astype(BF16)) + carry_ref[...]
    dest_ref[...] = jnp.concatenate(
        [jnp.sum(jnp.where(e_iota == idx[k:k + 1, :], rank, 0.0), axis=0, keepdims=True)
         for k in range(TOP_K)], axis=0).astype(I32)
    carry_ref[...] += jnp.sum(hot, axis=1, keepdims=True)


def _dest(idx, start, tn):
    n = idx.shape[1]
    n_exp = start.shape[0]
    col = lambda i: (0, i)
    return pl.pallas_call(
        functools.partial(_dest_kernel, n_exp=n_exp, tn=tn),
        out_shape=jax.ShapeDtypeStruct((TOP_K, n), I32),
        grid=(n // tn,),
        in_specs=[pl.BlockSpec((TOP_K, tn), col), pl.BlockSpec((n_exp, 1), lambda i: (0, 0))],
        out_specs=pl.BlockSpec((TOP_K, tn), col),
        scratch_shapes=[pltpu.VMEM((n_exp, 1), F32)],
        compiler_params=_params(("arbitrary",)),
        name="dest",
    )(idx, start)


def _dispatch_kernel(dest_ref, hp_ref, xs_ref, sem, *, tm, pieces):
    def start(r, c):
        src = hp_ref.at[pl.ds(pl.multiple_of(r * pieces, pieces), pieces), :]
        for k in range(TOP_K):
            row = pl.multiple_of(dest_ref[r * TOP_K + k], pieces)
            pltpu.make_async_copy(src, xs_ref.at[pl.ds(row, pieces), :], sem).start(priority=k % 2)
        return c

    lax.fori_loop(0, tm, start, 0)
    for k in range(TOP_K):
        pltpu.make_async_copy(hp_ref, xs_ref.at[pl.ds(0, tm * pieces), :], sem).wait()


def _dispatch(dest, hp, cap, tm, pieces):
    n = hp.shape[0] // pieces
    return pl.pallas_call(
        functools.partial(_dispatch_kernel, tm=tm, pieces=pieces),
        out_shape=jax.ShapeDtypeStruct((cap * pieces, LANES), hp.dtype),
        grid=(n // tm,),
        in_specs=[pl.BlockSpec((tm * TOP_K,), lambda i: (i,), memory_space=pltpu.SMEM),
                  pl.BlockSpec((tm * pieces, LANES), lambda i: (i, 0))],
        out_specs=pl.BlockSpec(memory_space=pl.ANY),
        scratch_shapes=[pltpu.SemaphoreType.DMA(())],
        compiler_params=_params(("arbitrary",)),
        name="dispatch",
    )(dest, hp)


def _expert_kernel(cnt_ref, first_ref, tot_ref, xs_hbm, wg_ref, wu_ref, wd_ref, ys_hbm,
                   xbuf, ybuf, wgu_s, wd_s, xsem, ysem, *, rows, pieces):
    e = pl.program_id(0)
    cnt, first, total = cnt_ref[e], first_ref[e], tot_ref[0]
    n_blk = (cnt + rows - 1) // rows
    d_e = wg_ref.shape[2]
    phys = rows * pieces

    def block_rows(g):
        return pl.ds(pl.multiple_of(g * phys, phys), phys)

    def x_copy(g, slot):
        return pltpu.make_async_copy(xs_hbm.at[block_rows(g), :], xbuf.at[slot], xsem.at[slot])

    def y_copy(g, slot):
        return pltpu.make_async_copy(ybuf.at[slot], ys_hbm.at[block_rows(g), :], ysem.at[slot])

    @pl.when(n_blk > 0)
    def _():
        @pl.when(first == 0)
        def _():
            x_copy(0, 0).start()

        wgu_s[:, :d_e] = wg_ref[0].astype(BF16)
        wgu_s[:, d_e:] = wu_ref[0].astype(BF16)
        wd_s[...] = wd_ref[0].astype(BF16)

        def block(j, c):
            g = first + j
            slot = g % 2
            x_copy(g, slot).wait()

            @pl.when(g + 1 < total)
            def _():
                x_copy(g + 1, 1 - slot).start()

            @pl.when(g >= 2)
            def _():
                y_copy(g - 2, slot).wait()

            x = _load_row_tiles(xbuf, 0, rows, pieces, lead=(slot,))
            r_iota = lax.broadcasted_iota(I32, x.shape, 0) + j * rows
            xb = jnp.where(r_iota < cnt, x, 0.0).astype(BF16)
            gu = _dot(xb, wgu_s[...])
            gate, up = gu[:, :d_e], gu[:, d_e:]
            hid = (gate * jax.nn.sigmoid(gate) * up).astype(BF16)
            y = _dot(hid, wd_s[...])
            for s in range(pieces):
                ybuf[slot, pl.ds(s, rows, stride=pieces), :] = y[:, s * LANES:(s + 1) * LANES]
            y_copy(g, slot).start()
            return c

        lax.fori_loop(0, n_blk, block, 0)

        @pl.when(first + n_blk == total)
        def _():
            last = total - 1
            y_copy(last, last % 2).wait()

            @pl.when(total >= 2)
            def _():
                y_copy(last - 1, (last - 1) % 2).wait()


def _experts(cnt, blk_first, n_used, xs, w_gate, w_up, w_down, rows):
    n_exp, d, d_e = w_gate.shape
    pieces = d // LANES
    hbm = pl.BlockSpec(memory_space=pl.ANY)
    w_map = lambda e, cnt, first, tot: (e, 0, 0)
    grid_spec = pltpu.PrefetchScalarGridSpec(
        num_scalar_prefetch=3,
        grid=(n_exp,),
        in_specs=[hbm, pl.BlockSpec((1, d, d_e), w_map), pl.BlockSpec((1, d, d_e), w_map),
                  pl.BlockSpec((1, d_e, d), w_map)],
        out_specs=hbm,
        scratch_shapes=[pltpu.VMEM((2, rows * pieces, LANES), F32), pltpu.VMEM((2, rows * pieces, LANES), F32),
                        pltpu.VMEM((d, 2 * d_e), BF16), pltpu.VMEM((d_e, d), BF16),
                        pltpu.SemaphoreType.DMA((2,)), pltpu.SemaphoreType.DMA((2,))],
    )
    return pl.pallas_call(
        functools.partial(_expert_kernel, rows=rows, pieces=pieces),
        out_shape=jax.ShapeDtypeStruct(xs.shape, F32),
        grid_spec=grid_spec,
        compiler_params=_params(("arbitrary",)),
        name="experts",
    )(cnt, blk_first, n_used, xs, w_gate, w_up, w_down)


def _combine_kernel(dest_ref, next_ref, ys_ref, wt_ref, hp_ref, x1_ref, wsgu_ref, wsd_ref, npost_ref, gt_ref,
                    o_ref, buf, sem, *, tm, pieces):
    i = pl.program_id(0)
    cur = i % 2

    def gather(rows_ref, half):
        def start(r, c):
            slot = pl.ds(pl.multiple_of(r * pieces, pieces), pieces)
            for k in range(TOP_K):
                row = pl.multiple_of(rows_ref[r * TOP_K + k], pieces)
                pltpu.make_async_copy(ys_ref.at[pl.ds(row, pieces), :], buf.at[half, k, slot, :],
                                      sem.at[half]).start(priority=k % 2)
            return c

        lax.fori_loop(0, tm, start, 0)

    @pl.when(i == 0)
    def _():
        gather(dest_ref, 0)

    @pl.when(i + 1 < pl.num_programs(0))
    def _():
        gather(next_ref, 1 - cur)

    hb = _load_row_tiles(hp_ref, 0, tm, pieces).astype(BF16)
    d_sh = wsd_ref.shape[0]
    gu = _dot(hb, wsgu_ref[...])
    gate, up = gu[:, :d_sh], gu[:, d_sh:]
    shared = _dot((gate * jax.nn.sigmoid(gate) * up).astype(BF16), wsd_ref[...])

    for k in range(TOP_K):
        pltpu.make_async_copy(ys_ref.at[pl.ds(0, tm * pieces), :], buf.at[cur, k], sem.at[cur]).wait()
    wt = wt_ref[...]
    fx = shared
    for k in range(TOP_K):
        fx = fx + wt[:, k:k + 1] * _load_row_tiles(buf, 0, tm, pieces, lead=(cur, k))
    o_ref[...] = x1_ref[...] + gt_ref[0] * _rms(fx, npost_ref[...])


def _combine(dest, ys, wt, hp, x1, wsgu, wsd, npost, gt, tokens_per_batch, tm):
    n, d = x1.shape
    pieces = d // LANES
    w = LANES
    tiles_per_batch = tokens_per_batch // tm
    row = lambda i: (i, 0)
    fixed = lambda i: (0, 0)
    return pl.pallas_call(
        functools.partial(_combine_kernel, tm=tm, pieces=pieces),
        out_shape=jax.ShapeDtypeStruct((n, d), F32),
        grid=(n // tm,),
        in_specs=[pl.BlockSpec((tm * TOP_K,), lambda i: (i,), memory_space=pltpu.SMEM),
                  pl.BlockSpec((tm * TOP_K,), lambda i: (jnp.minimum(i + 1, n // tm - 1),), memory_space=pltpu.SMEM),
                  pl.BlockSpec(memory_space=pl.ANY),
                  pl.BlockSpec((tm, TOP_K), row), pl.BlockSpec((tm * pieces, w), row), pl.BlockSpec((tm, d), row),
                  pl.BlockSpec(wsgu.shape, fixed), pl.BlockSpec(wsd.shape, fixed),
                  pl.BlockSpec((1, d), fixed), pl.BlockSpec((1, 1, d), lambda i: (i // tiles_per_batch, 0, 0))],
        out_specs=pl.BlockSpec((tm, d), row),
        scratch_shapes=[pltpu.VMEM((2, TOP_K, tm * pieces, w), F32), pltpu.SemaphoreType.DMA((2,))],
        compiler_params=_params(("arbitrary",)),
        name="combine",
    )(dest, dest, ys, wt, hp, x1, wsgu, wsd, npost.reshape(1, d), gt)


def _layer(x, ctx, mod_x, mod_c, norm_pre_mix, norm_post_mix, norm_pre_ffn, norm_post_ffn,
           w_in, s5_a_re, s5_a_im, s5_log_dt, s5_b_re, s5_b_im, s5_c_re, s5_c_im, s5_d, w_glu, b_glu,
           sgu_norm, sgu_w, sgu_b, w_branch_s5, w_branch_sgu, w_out,
           w_router, router_bias, w_exp_gate, w_exp_up, w_exp_down, w_sh_gate, w_sh_up, w_sh_down):
    bsz, t_len, d = x.shape
    c_len = ctx.shape[1]
    n = bsz * t_len
    d_s5, d_sgu = w_glu.shape[0], sgu_norm.shape[0]
    g_n = d_s5 // S5_GROUP_CH
    n_exp = w_router.shape[1]
    chunk = S5_CHUNK
    tm = min(512, t_len)
    assert t_len % tm == 0 and tm % SGU_CHUNK == 0 and t_len % chunk == 0 and c_len % chunk == 0

    sh_m, sc_m, gt_m, sh_f, sc_f, gt_f = [v.reshape(bsz, 1, d) for v in jnp.split(mod_x, 6, axis=-1)]
    csh_m, csc_m = mod_c[:d], mod_c[d:2 * d]

    w_in_bf = w_in.astype(BF16)

    nc_lat, nc_ctx = t_len // chunk, c_len // chunk
    n_steps = max(1, (max(nc_lat, nc_ctx) - 1).bit_length())
    m_t, ke_t, kc_t, alp = _s5_tables(s5_a_re, s5_a_im, s5_log_dt, s5_b_re, s5_b_im, s5_c_re, s5_c_im,
                                      chunk, n_steps)

    ctx2 = ctx.reshape(bsz * c_len, d)
    u_ctx = _inproj_u(ctx2, norm_pre_mix, csh_m, csc_m, w_in_bf[:, :d_s5], min(512, bsz * c_len))
    zero_init = jnp.zeros((g_n, bsz, 4 * S5_STATE), F32)
    (ctx_final,) = _s5(_to_groups(u_ctx, bsz, nc_ctx, chunk, g_n), ke_t, alp, zero_init, nb=bsz, nc=nc_ctx)

    x2 = x.reshape(n, d)
    u, gu, vn, g1, g2 = _inproj(x2, norm_pre_mix, sh_m, sc_m, w_in_bf, sgu_norm, t_len, tm, d_s5, d_sgu)
    _, yg = _s5(_to_groups(u, bsz, nc_lat, chunk, g_n), ke_t, alp, ctx_final, m_t, kc_t, nb=bsz, nc=nc_lat)
    y = _from_groups(yg, bsz, nc_lat, chunk, g_n)

    ch = d_sgu // SGU_GROUPS
    per_tile = LANES // ch
    sguw = sgu_w.reshape(SGU_GROUPS // per_tile, per_tile, SGU_CHUNK, SGU_CHUNK)
    sguw = sguw.transpose(0, 2, 1, 3).reshape(SGU_GROUPS // per_tile, SGU_CHUNK, per_tile * SGU_CHUNK).astype(BF16)
    sgub = jnp.repeat(sgu_b.T, ch, axis=1)

    x1, hp, logits = _mix(y, u, gu, vn, g1, g2, x2, s5_d, w_glu.astype(BF16), b_glu, sguw, sgub,
                          w_branch_s5.astype(BF16), w_branch_sgu.astype(BF16), w_out.astype(BF16),
                          norm_post_mix, gt_m, norm_pre_ffn, sh_f, sc_f, w_router.astype(BF16), t_len, tm)

    tn = min(512, n)
    idx, wts, counts = _route(logits.T, router_bias, tn)
    rows = EXPERT_ROWS
    cnt = counts.reshape(n_exp).astype(I32)
    nblk = (cnt + rows - 1) // rows
    blk_end = jnp.cumsum(nblk)
    blk_start = blk_end - nblk
    cap = ((n * TOP_K) // rows + n_exp) * rows
    dest = _dest(idx, (blk_start * rows).astype(F32).reshape(n_exp, 1), tn)
    pieces = d // LANES
    dest = dest.T.reshape(n * TOP_K) * pieces

    td = min(256, n)
    xs = _dispatch(dest, hp, cap, td, pieces)
    ys = _experts(cnt, blk_start.astype(I32), blk_end[-1:].astype(I32), xs, w_exp_gate, w_exp_up, w_exp_down, rows)
    wsgu = jnp.concatenate([w_sh_gate, w_sh_up], axis=1).astype(BF16)
    out = _combine(dest, ys, wts.T, hp, x1, wsgu, w_sh_down.astype(BF16), norm_post_ffn, gt_f, t_len, td)
    return out.reshape(bsz, t_len, d)


def kernel(x, c, ctx, c_ctx, w_mod, b_mod, norm_pre_mix, norm_post_mix, norm_pre_ffn, norm_post_ffn, w_in, s5_a_re, s5_a_im, s5_log_dt, s5_b_re, s5_b_im, s5_c_re, s5_c_im, s5_d, w_glu, b_glu, sgu_norm, sgu_w, sgu_b, w_branch_s5, w_branch_sgu, w_out, w_router, router_bias, w_exp_gate, w_exp_up, w_exp_down, w_sh_gate, w_sh_up, w_sh_down):
    depth = w_mod.shape[0]
    assert depth == 1, "the context stream is only carried through the last layer's S5 states"
    bsz = x.shape[0]
    pad = (-(bsz + 1)) % 8
    cpad = jnp.concatenate([c, c_ctx[None, :], jnp.zeros((pad, c.shape[1]), c.dtype)], axis=0)
    mod = _modulation(cpad, w_mod[0], b_mod[0])
    return _layer(x, ctx, mod[:bsz], mod[bsz], norm_pre_mix[0], norm_post_mix[0], norm_pre_ffn[0],
                  norm_post_ffn[0], w_in[0], s5_a_re[0], s5_a_im[0], s5_log_dt[0], s5_b_re[0], s5_b_im[0],
                  s5_c_re[0], s5_c_im[0], s5_d[0], w_glu[0], b_glu[0], sgu_norm[0], sgu_w[0], sgu_b[0],
                  w_branch_s5[0], w_branch_sgu[0], w_out[0], w_router[0], router_bias[0],
                  w_exp_gate[0], w_exp_up[0], w_exp_down[0], w_sh_gate[0], w_sh_up[0], w_sh_down[0])
```

```python
import functools
import math

import jax
import jax.numpy as jnp
from jax import lax
from jax.experimental import pallas as pl
from jax.experimental.pallas import tpu as pltpu

F32 = jnp.float32
BF16 = jnp.bfloat16
I32 = jnp.int32

EPS = 1e-6
S5_GROUP_CH = 16
S5_STATE = 64
S5_CHUNK = 64
SGU_GROUPS = 8
SGU_CHUNK = 128
N_EXPERT_GROUPS = 8
TOPK_GROUPS = 4
TOP_K = 8
ROUTE_SCALE = 2.5
LANES = 128
EXPERT_ROWS = 256
EXPERT_X_SLOTS = 4
EXPERT_Y_SLOTS = 3
VMEM_LIMIT = 56 * 1024 * 1024


def _params(sem):
    return pltpu.CompilerParams(dimension_semantics=sem, vmem_limit_bytes=VMEM_LIMIT)


def _rms(v, g):
    return v * lax.rsqrt(jnp.mean(v * v, axis=-1, keepdims=True) + EPS) * g


def _dot(a, b):
    return jnp.dot(a, b, preferred_element_type=F32)


def _store_row_tiles(ref, first_row, val):
    pieces = val.shape[1] // LANES
    for s in range(pieces):
        ref[pl.ds(first_row * pieces + s, val.shape[0], stride=pieces), :] = val[:, s * LANES:(s + 1) * LANES]


def _load_row_tiles(ref, first_row, n_rows, pieces, lead=()):
    return jnp.concatenate([ref[lead + (pl.ds(first_row * pieces + s, n_rows, stride=pieces), slice(None))]
                            for s in range(pieces)], axis=1)


def _mod_kernel(c_ref, w_ref, b_ref, o_ref):
    cv = c_ref[...]
    s = cv * jax.nn.sigmoid(cv)
    o_ref[...] = _dot(s.astype(BF16), w_ref[...].astype(BF16)) + b_ref[...]


def _modulation(cpad, w_mod, b_mod):
    d, n6 = w_mod.shape
    tn = 1024
    return pl.pallas_call(
        _mod_kernel,
        out_shape=jax.ShapeDtypeStruct((cpad.shape[0], n6), F32),
        grid=(n6 // tn,),
        in_specs=[pl.BlockSpec(cpad.shape, lambda j: (0, 0)),
                  pl.BlockSpec((d, tn), lambda j: (0, j)),
                  pl.BlockSpec((1, tn), lambda j: (0, j))],
        out_specs=pl.BlockSpec((cpad.shape[0], tn), lambda j: (0, j)),
        compiler_params=_params(("parallel",)),
        name="mod",
    )(cpad, w_mod, b_mod.reshape(1, n6))


def _inproj_kernel(x_ref, g_ref, sh_ref, sc_ref, w_ref, sgun_ref,
                   u_ref, gu_ref, vn_ref, g1_ref, g2_ref, *, d_s5, d_sgu, d_model):
    h = _rms(x_ref[...], g_ref[...])
    hb = (h * (1.0 + sc_ref[0]) + sh_ref[0]).astype(BF16)
    o1, o2, o3, o4 = d_s5, d_s5 + d_sgu, d_s5 + 2 * d_sgu, d_s5 + 2 * d_sgu + d_model
    u_ref[...] = _dot(hb, w_ref[:, 0:o1]).astype(BF16)
    gu_ref[...] = jax.nn.gelu(_dot(hb, w_ref[:, o1:o2])).astype(BF16)
    v = jax.nn.gelu(_dot(hb, w_ref[:, o2:o3]))
    vn_ref[...] = _rms(v, sgun_ref[...]).astype(BF16)
    g1_ref[...] = jax.nn.sigmoid(_dot(hb, w_ref[:, o3:o4])).astype(BF16)
    g2_ref[...] = jax.nn.sigmoid(_dot(hb, w_ref[:, o4:o4 + d_model])).astype(BF16)


def _inproj(x2, norm_g, shift, scale, w_in_bf, sgu_norm, tokens_per_batch, tm, d_s5, d_sgu):
    n, d = x2.shape
    tiles_per_batch = tokens_per_batch // tm
    row = lambda i: (i, 0)
    fixed = lambda i: (0, 0)
    per_batch = lambda i: (i // tiles_per_batch, 0, 0)
    return pl.pallas_call(
        functools.partial(_inproj_kernel, d_s5=d_s5, d_sgu=d_sgu, d_model=d),
        out_shape=(jax.ShapeDtypeStruct((n, d_s5), BF16), jax.ShapeDtypeStruct((n, d_sgu), BF16),
                   jax.ShapeDtypeStruct((n, d_sgu), BF16), jax.ShapeDtypeStruct((n, d), BF16),
                   jax.ShapeDtypeStruct((n, d), BF16)),
        grid=(n // tm,),
        in_specs=[pl.BlockSpec((tm, d), row), pl.BlockSpec((1, d), fixed),
                  pl.BlockSpec((1, 1, d), per_batch), pl.BlockSpec((1, 1, d), per_batch),
                  pl.BlockSpec(w_in_bf.shape, fixed), pl.BlockSpec((1, d_sgu), fixed)],
        out_specs=(pl.BlockSpec((tm, d_s5), row), pl.BlockSpec((tm, d_sgu), row),
                   pl.BlockSpec((tm, d_sgu), row), pl.BlockSpec((tm, d), row), pl.BlockSpec((tm, d), row)),
        compiler_params=_params(("parallel",)),
        name="inproj",
    )(x2, norm_g.reshape(1, d), shift, scale, w_in_bf, sgu_norm.reshape(1, d_sgu))


def _inproj_u_kernel(x_ref, g_ref, sh_ref, sc_ref, w_ref, u_ref):
    h = _rms(x_ref[...], g_ref[...])
    hb = (h * (1.0 + sc_ref[...]) + sh_ref[...]).astype(BF16)
    u_ref[...] = _dot(hb, w_ref[...]).astype(BF16)


def _inproj_u(x2, norm_g, shift, scale, w_u_bf, tm):
    n, d = x2.shape
    d_s5 = w_u_bf.shape[1]
    row = lambda i: (i, 0)
    fixed = lambda i: (0, 0)
    return pl.pallas_call(
        _inproj_u_kernel,
        out_shape=jax.ShapeDtypeStruct((n, d_s5), BF16),
        grid=(n // tm,),
        in_specs=[pl.BlockSpec((tm, d), row), pl.BlockSpec((1, d), fixed), pl.BlockSpec((1, d), fixed),
                  pl.BlockSpec((1, d), fixed), pl.BlockSpec(w_u_bf.shape, fixed)],
        out_specs=pl.BlockSpec((tm, d_s5), row),
        compiler_params=_params(("parallel",)),
        name="inproj_ctx",
    )(x2, norm_g.reshape(1, d), shift.reshape(1, d), scale.reshape(1, d), w_u_bf)


def _s5_tables(a_re, a_im, log_dt, b_re, b_im, c_re, c_im, chunk, n_steps):
    hi = lax.Precision.HIGHEST
    g_n, p_n, j_n = a_re.shape[1], a_re.shape[2], b_re.shape[3]
    dt = jnp.exp(log_dt)[..., None]
    lam_re, lam_im = dt * a_re, dt * a_im
    ab_re, ab_im = jnp.exp(lam_re) * jnp.cos(lam_im), jnp.exp(lam_re) * jnp.sin(lam_im)
    den = a_re * a_re + a_im * a_im
    q_re = ((ab_re - 1.0) * a_re + ab_im * a_im) / den
    q_im = (ab_im * a_re - (ab_re - 1.0) * a_im) / den
    bb_re = q_re[..., None] * b_re - q_im[..., None] * b_im
    bb_im = q_re[..., None] * b_im + q_im[..., None] * b_re
    k = jnp.arange(chunk + 1, dtype=F32)[:, None, None, None]
    mag = jnp.exp(k * lam_re[None])
    pw_re, pw_im = mag * jnp.cos(k * lam_im[None]), mag * jnp.sin(k * lam_im[None])

    def lag_kernels(d):
        x_re = pw_re[:chunk, d, :, :, None] * bb_re[d][None] - pw_im[:chunk, d, :, :, None] * bb_im[d][None]
        x_im = pw_re[:chunk, d, :, :, None] * bb_im[d][None] + pw_im[:chunk, d, :, :, None] * bb_re[d][None]
        return (jnp.einsum('gjp,kgpi->gkij', c_re[d], x_re, precision=hi)
                - jnp.einsum('gjp,kgpi->gkij', c_im[d], x_im, precision=hi))

    kf, kb = lag_kernels(0), lag_kernels(1)
    lags = jnp.concatenate([kb[:, 1:][:, ::-1], kf[:, 0:1] + kb[:, 0:1], kf[:, 1:],
                            jnp.zeros_like(kf[:, 0:1])], axis=1)
    lags = lags.transpose(0, 2, 3, 1)

    pf_re, pf_im = pw_re[:chunk, 0][::-1], pw_im[:chunk, 0][::-1]
    pb_re, pb_im = pw_re[:chunk, 1], pw_im[:chunk, 1]

    def state_in(p_re, p_im, d):
        e_re = p_re[..., None] * bb_re[d][None] - p_im[..., None] * bb_im[d][None]
        e_im = p_re[..., None] * bb_im[d][None] + p_im[..., None] * bb_re[d][None]
        return e_re.transpose(1, 3, 0, 2), e_im.transpose(1, 3, 0, 2)

    ef_re, ef_im = state_in(pf_re, pf_im, 0)
    eb_re, eb_im = state_in(pb_re, pb_im, 1)
    ke = jnp.concatenate([ef_re, eb_re, ef_im, eb_im], axis=-1).reshape(g_n, chunk * j_n, 4 * p_n)

    qf_re, qf_im = pw_re[1:chunk + 1, 0], pw_im[1:chunk + 1, 0]
    qb_re, qb_im = pw_re[1:chunk + 1, 1][::-1], pw_im[1:chunk + 1, 1][::-1]

    def state_out(p_re, p_im, d):
        cr, ci = c_re[d].transpose(0, 2, 1), c_im[d].transpose(0, 2, 1)
        pr, pi = p_re.transpose(1, 2, 0), p_im.transpose(1, 2, 0)
        o_re = pr[:, :, None, :] * cr[..., None] - pi[:, :, None, :] * ci[..., None]
        o_im = pr[:, :, None, :] * ci[..., None] + pi[:, :, None, :] * cr[..., None]
        return o_re, o_im

    of_re, of_im = state_out(qf_re, qf_im, 0)
    ob_re, ob_im = state_out(qb_re, qb_im, 1)
    kc = jnp.concatenate([of_re, ob_re, -of_im, -ob_im], axis=1).reshape(g_n, 4 * p_n, chunk * j_n)

    e = (chunk * (2.0 ** jnp.arange(n_steps, dtype=F32)))[:, None, None, None]
    mg = jnp.exp(e * lam_re[None])
    al_re, al_im = mg * jnp.cos(e * lam_im[None]), mg * jnp.sin(e * lam_im[None])
    alp = jnp.concatenate([al_re[:, 0], al_re[:, 1], al_im[:, 0], al_im[:, 1]], axis=-1)
    return lags, ke.astype(BF16), kc.astype(BF16), alp.transpose(1, 0, 2)


def _build_toeplitz(lag_ref, m_s, chunk):
    j_n = lag_ref.shape[1]
    per_tile = LANES // chunk
    low = lax.broadcasted_iota(I32, (chunk, LANES), 1) < chunk
    for i in range(j_n):
        for q in range(j_n // per_tile):
            tile = None
            for p in range(per_tile):
                j = q * per_tile + p
                row = jnp.broadcast_to(lag_ref[0, i, j:j + 1, :], (chunk, LANES))
                rot = pltpu.roll(row, (p * chunk + LANES - (chunk - 1)) % LANES, 1, stride=1, stride_axis=0)
                tile = rot if tile is None else jnp.where(low, tile, rot)
            m_s[i * chunk:(i + 1) * chunk, q * LANES:(q + 1) * LANES] = tile.astype(BF16)


def _s5_kernel(*refs, nb, nc, readout):
    if readout:
        u_ref, ke_ref, alp_ref, init_ref, lag_ref, kc_ref, fin_ref, y_ref, m_s = refs
    else:
        u_ref, ke_ref, alp_ref, init_ref, fin_ref = refs
    rows = nb * nc
    half = 2 * S5_STATE
    u = u_ref[0]
    e = _dot(u, ke_ref[0])
    er, ei = e[:, :half], e[:, half:]
    r_idx = lax.broadcasted_iota(I32, (rows, half), 0)
    if nc & (nc - 1) == 0:
        n_idx, b_idx = r_idx & (nc - 1), r_idx >> (nc.bit_length() - 1)
    else:
        n_idx, b_idx = lax.rem(r_idx, nc), lax.div(r_idx, nc)
    is_f = lax.broadcasted_iota(I32, (rows, half), 1) < S5_STATE
    seen = jnp.where(is_f, n_idx, nc - 1 - n_idx)
    init = init_ref[0]
    init_r = jnp.zeros((rows, half), F32)
    init_i = jnp.zeros((rows, half), F32)
    for b in range(nb):
        init_r = jnp.where(b_idx == b, init[b:b + 1, :half], init_r)
        init_i = jnp.where(b_idx == b, init[b:b + 1, half:], init_i)
    alp = alp_ref[0]
    ar, ai = alp[0:1, :half], alp[0:1, half:]
    er = er + jnp.where(seen == 0, ar * init_r - ai * init_i, 0.0)
    ei = ei + jnp.where(seen == 0, ar * init_i + ai * init_r, 0.0)

    def from_prev(v, dist):
        return jnp.where(is_f, pltpu.roll(v, dist, 0), pltpu.roll(v, rows - dist, 0))

    d, s = 1, 0
    while d < nc:
        ar, ai = alp[s:s + 1, :half], alp[s:s + 1, half:]
        sr = jnp.where(seen >= d, from_prev(er, d), 0.0)
        si = jnp.where(seen >= d, from_prev(ei, d), 0.0)
        er, ei = er + (ar * sr - ai * si), ei + (ar * si + ai * sr)
        d, s = d * 2, s + 1
    is_f_row = lax.broadcasted_iota(I32, (1, half), 1) < S5_STATE
    for b in range(nb):
        lo, hi = b * nc, b * nc + nc - 1
        fin_ref[0, b:b + 1, :half] = jnp.where(is_f_row, er[hi:hi + 1], er[lo:lo + 1])
        fin_ref[0, b:b + 1, half:] = jnp.where(is_f_row, ei[hi:hi + 1], ei[lo:lo + 1])
    if readout:
        if nc > 1:
            sin_r = jnp.where(seen >= 1, from_prev(er, 1), init_r)
            sin_i = jnp.where(seen >= 1, from_prev(ei, 1), init_i)
        else:
            sin_r, sin_i = init_r, init_i
        s_in = jnp.concatenate([sin_r, sin_i], axis=1).astype(BF16)
        _build_toeplitz(lag_ref, m_s, S5_CHUNK)
        y_ref[0] = (_dot(u, m_s[...]) + _dot(s_in, kc_ref[0])).astype(BF16)


def _s5(ug, ke, alp, init, m=None, kc=None, *, nb, nc):
    g_n, rows, width = ug.shape
    readout = m is not None
    grp = lambda g: (g, 0, 0)
    in_specs = [pl.BlockSpec((1, rows, width), grp), pl.BlockSpec((1,) + ke.shape[1:], grp),
                pl.BlockSpec((1,) + alp.shape[1:], grp), pl.BlockSpec((1,) + init.shape[1:], grp)]
    out_shape = [jax.ShapeDtypeStruct(init.shape, F32)]
    out_specs = [pl.BlockSpec((1,) + init.shape[1:], grp)]
    args = [ug, ke, alp, init]
    scratch = []
    if readout:
        assert m.shape[3] == LANES and 2 * S5_CHUNK == LANES
        in_specs += [pl.BlockSpec((1,) + m.shape[1:], lambda g: (g, 0, 0, 0)),
                     pl.BlockSpec((1,) + kc.shape[1:], grp)]
        out_shape.append(jax.ShapeDtypeStruct(ug.shape, BF16))
        out_specs.append(pl.BlockSpec((1, rows, width), grp))
        args += [m, kc]
        scratch = [pltpu.VMEM((width, width), BF16)]
    return pl.pallas_call(
        functools.partial(_s5_kernel, nb=nb, nc=nc, readout=readout),
        out_shape=tuple(out_shape), grid=(g_n,), in_specs=in_specs, out_specs=tuple(out_specs),
        scratch_shapes=scratch,
        compiler_params=_params(("parallel",)),
        name="s5_readout" if readout else "s5_state",
    )(*args)


def _to_groups(u, nb, nc, chunk, g_n):
    j_n = u.shape[1] // g_n
    return u.reshape(nb * nc, chunk, g_n, j_n).transpose(2, 0, 3, 1).reshape(g_n, nb * nc, j_n * chunk)


def _from_groups(y, nb, nc, chunk, g_n):
    j_n = y.shape[2] // chunk
    return y.reshape(g_n, nb * nc, j_n, chunk).transpose(1, 3, 0, 2).reshape(nb * nc * chunk, g_n * j_n)


def _mix_kernel(y_ref, u_ref, gu_ref, vn_ref, g1_ref, g2_ref, x_ref,
                d_ref, wglu_ref, bglu_ref, sguw_ref, sgub_ref, wb1_ref, wb2_ref, wout_ref,
                npost_ref, gt_ref, npre_ref, shf_ref, scf_ref, wr_ref,
                x1_ref, hp_ref, lg_ref, *, tm):
    y = jax.nn.gelu(y_ref[...].astype(F32) + d_ref[...] * u_ref[...].astype(F32))
    y_s5 = y * jax.nn.sigmoid(_dot(y.astype(BF16), wglu_ref[...]) + bglu_ref[...])

    lane = lax.broadcasted_iota(I32, (1, LANES), 1)
    m_lo = (lane < LANES // 2).astype(F32).astype(BF16)
    m_hi = (lane >= LANES // 2).astype(F32).astype(BF16)
    vn = vn_ref[...]
    chunks = []
    for c in range(tm // SGU_CHUNK):
        vc = vn[c * SGU_CHUNK:(c + 1) * SGU_CHUNK, :]
        tiles = []
        for q in range(vc.shape[1] // LANES):
            vt = vc[:, q * LANES:(q + 1) * LANES]
            rhs = jnp.concatenate([vt * m_lo, vt * m_hi], axis=0)
            tiles.append(_dot(sguw_ref[q], rhs))
        chunks.append(jnp.concatenate(tiles, axis=1) + sgub_ref[...])
    mixed = jnp.concatenate(chunks, axis=0)
    y_sgu = gu_ref[...].astype(F32) * mixed

    merged = (g1_ref[...].astype(F32) * _dot(y_s5.astype(BF16), wb1_ref[...])
              + g2_ref[...].astype(F32) * _dot(y_sgu.astype(BF16), wb2_ref[...]))
    mx = _dot(merged.astype(BF16), wout_ref[...])
    x1 = x_ref[...] + gt_ref[0] * _rms(mx, npost_ref[...])
    x1_ref[...] = x1
    hp = _rms(x1, npre_ref[...]) * (1.0 + scf_ref[0]) + shf_ref[0]
    _store_row_tiles(hp_ref, 0, hp)
    lg_ref[...] = _dot(hp.astype(BF16), wr_ref[...])


def _mix(y, u, gu, vn, g1, g2, x2, s5_d, wglu, bglu, sguw, sgub, wb1, wb2, wout,
         npost, gt, npre, shf, scf, wr, tokens_per_batch, tm):
    n, d = x2.shape
    d_s5, d_sgu, n_exp = y.shape[1], gu.shape[1], wr.shape[1]
    tiles_per_batch = tokens_per_batch // tm
    row = lambda i: (i, 0)
    fixed = lambda i: (0, 0)
    fixed3 = lambda i: (0, 0, 0)
    per_batch = lambda i: (i // tiles_per_batch, 0, 0)
    vec = lambda w: pl.BlockSpec((1, w), fixed)
    return pl.pallas_call(
        functools.partial(_mix_kernel, tm=tm),
        out_shape=(jax.ShapeDtypeStruct((n, d), F32), jax.ShapeDtypeStruct((n * d // LANES, LANES), F32),
                   jax.ShapeDtypeStruct((n, n_exp), F32)),
        grid=(n // tm,),
        in_specs=[pl.BlockSpec((tm, d_s5), row), pl.BlockSpec((tm, d_s5), row), pl.BlockSpec((tm, d_sgu), row),
                  pl.BlockSpec((tm, d_sgu), row), pl.BlockSpec((tm, d), row), pl.BlockSpec((tm, d), row),
                  pl.BlockSpec((tm, d), row),
                  vec(d_s5), pl.BlockSpec(wglu.shape, fixed), vec(d_s5),
                  pl.BlockSpec(sguw.shape, fixed3), pl.BlockSpec(sgub.shape, fixed),
                  pl.BlockSpec(wb1.shape, fixed), pl.BlockSpec(wb2.shape, fixed), pl.BlockSpec(wout.shape, fixed),
                  vec(d), pl.BlockSpec((1, 1, d), per_batch), vec(d),
                  pl.BlockSpec((1, 1, d), per_batch), pl.BlockSpec((1, 1, d), per_batch),
                  pl.BlockSpec(wr.shape, fixed)],
        out_specs=(pl.BlockSpec((tm, d), row), pl.BlockSpec((tm * d // LANES, LANES), row),
                   pl.BlockSpec((tm, n_exp), row)),
        compiler_params=_params(("parallel",)),
        name="mix",
    )(y, u, gu, vn, g1, g2, x2, s5_d.reshape(1, d_s5), wglu, bglu.reshape(1, d_s5), sguw, sgub,
      wb1, wb2, wout, npost.reshape(1, d), gt, npre.reshape(1, d), shf, scf, wr)


def _route_kernel(lg_ref, bias_ref, idx_ref, w_ref, cnt_ref, *, n_exp, tn):
    per_group = n_exp // N_EXPERT_GROUPS
    neg = jnp.float32(-jnp.inf)

    scores = jax.nn.sigmoid(lg_ref[...])
    sel = scores + bias_ref[...]
    gs = []
    for g in range(N_EXPERT_GROUPS):
        sg = sel[g * per_group:(g + 1) * per_group, :]
        m1 = jnp.max(sg, axis=0, keepdims=True)
        is_m1 = sg == m1
        n_m1 = jnp.sum(jnp.where(is_m1, 1.0, 0.0), axis=0, keepdims=True)
        rest = jnp.max(jnp.where(is_m1, neg, sg), axis=0, keepdims=True)
        gs.append(m1 + jnp.where(n_m1 >= 2.0, m1, rest))
    gsm = jnp.concatenate(gs, axis=0)
    g_iota = lax.broadcasted_iota(I32, gsm.shape, 0)
    e_iota = lax.broadcasted_iota(I32, sel.shape, 0).astype(F32)
    masked = []
    for g in range(N_EXPERT_GROUPS):
        mine = gsm[g:g + 1, :]
        beats = jnp.where(gsm > mine, 1.0, jnp.where(gsm == mine, jnp.where(g_iota < g, 1.0, 0.0), 0.0))
        n_beats = jnp.sum(beats, axis=0, keepdims=True)
        masked.append(jnp.where(n_beats < float(TOPK_GROUPS), sel[g * per_group:(g + 1) * per_group, :], neg))
    start = jnp.concatenate(masked, axis=0)
    selm = start
    picked, vals = [], []
    for _ in range(TOP_K):
        m = jnp.max(selm, axis=0, keepdims=True)
        first = jnp.min(jnp.where(selm == m, e_iota, float(n_exp)), axis=0, keepdims=True)
        one = e_iota == first
        picked.append(first)
        vals.append(jnp.sum(jnp.where(one, scores, 0.0), axis=0, keepdims=True))
        selm = jnp.where(one, neg, selm)
    idx_ref[...] = jnp.concatenate(picked, axis=0).astype(I32)
    wv = jnp.concatenate(vals, axis=0)
    w_ref[...] = wv / jnp.sum(wv, axis=0, keepdims=True) * ROUTE_SCALE
    hot = jnp.where(selm == neg, jnp.where(start == neg, 0.0, 1.0), 0.0)

    @pl.when(pl.program_id(0) == 0)
    def _():
        cnt_ref[...] = jnp.zeros_like(cnt_ref)

    cnt_ref[...] += jnp.sum(hot, axis=1, keepdims=True)


def _route(logits_t, bias, tn):
    n_exp, n = logits_t.shape
    col = lambda i: (0, i)
    fixed = lambda i: (0, 0)
    return pl.pallas_call(
        functools.partial(_route_kernel, n_exp=n_exp, tn=tn),
        out_shape=(jax.ShapeDtypeStruct((TOP_K, n), I32), jax.ShapeDtypeStruct((TOP_K, n), F32),
                   jax.ShapeDtypeStruct((n_exp, 1), F32)),
        grid=(n // tn,),
        in_specs=[pl.BlockSpec((n_exp, tn), col), pl.BlockSpec((n_exp, 1), fixed)],
        out_specs=(pl.BlockSpec((TOP_K, tn), col), pl.BlockSpec((TOP_K, tn), col),
                   pl.BlockSpec((n_exp, 1), fixed)),
        compiler_params=_params(("arbitrary",)),
        name="route",
    )(logits_t, bias.reshape(n_exp, 1))


def _dest_kernel(idx_ref, start_ref, dest_ref, carry_ref, *, n_exp, tn):
    @pl.when(pl.program_id(0) == 0)
    def _():
        carry_ref[...] = start_ref[...]

    idx = idx_ref[...]
    e_iota = lax.broadcasted_iota(I32, (n_exp, tn), 0)
    hot = jnp.zeros((n_exp, tn), F32)
    for k in range(TOP_K):
        hot = hot + jnp.where(e_iota == idx[k:k + 1, :], 1.0, 0.0)
    before = jnp.where(lax.broadcasted_iota(I32, (tn, tn), 0) < lax.broadcasted_iota(I32, (tn, tn), 1), 1.0, 0.0)
    rank = _dot(hot.astype(BF16), before.astype(BF16)) + carry_ref[...]
    dest_ref[...] = jnp.concatenate(
        [jnp.sum(jnp.where(e_iota == idx[k:k + 1, :], rank, 0.0), axis=0, keepdims=True)
         for k in range(TOP_K)], axis=0).astype(I32)
    carry_ref[...] += jnp.sum(hot, axis=1, keepdims=True)


def _dest(idx, start, tn):
    n = idx.shape[1]
    n_exp = start.shape[0]
    col = lambda i: (0, i)
    return pl.pallas_call(
        functools.partial(_dest_kernel, n_exp=n_exp, tn=tn),
        out_shape=jax.ShapeDtypeStruct((TOP_K, n), I32),
        grid=(n // tn,),
        in_specs=[pl.BlockSpec((TOP_K, tn), col), pl.BlockSpec((n_exp, 1), lambda i: (0, 0))],
        out_specs=pl.BlockSpec((TOP_K, tn), col),
        scratch_shapes=[pltpu.VMEM((n_exp, 1), F32)],
        compiler_params=_params(("arbitrary",)),
        name="dest",
    )(idx, start)


def _dispatch_kernel(dest_ref, hp_ref, xs_ref, sem, *, tm, pieces):
    def start(r, c):
        src = hp_ref.at[pl.ds(pl.multiple_of(r * pieces, pieces), pieces), :]
        for k in range(TOP_K):
            row = pl.multiple_of(dest_ref[r * TOP_K + k], pieces)
            pltpu.make_async_copy(src, xs_ref.at[pl.ds(row, pieces), :], sem).start(priority=k % 2)
        return c

    lax.fori_loop(0, tm, start, 0)
    for k in range(TOP_K):
        pltpu.make_async_copy(hp_ref, xs_ref.at[pl.ds(0, tm * pieces), :], sem).wait()


def _dispatch(dest, hp, cap, tm, pieces):
    n = hp.shape[0] // pieces
    return pl.pallas_call(
        functools.partial(_dispatch_kernel, tm=tm, pieces=pieces),
        out_shape=jax.ShapeDtypeStruct((cap * pieces, LANES), hp.dtype),
        grid=(n // tm,),
        in_specs=[pl.BlockSpec((tm * TOP_K,), lambda i: (i,), memory_space=pltpu.SMEM),
                  pl.BlockSpec((tm * pieces, LANES), lambda i: (i, 0))],
        out_specs=pl.BlockSpec(memory_space=pl.ANY),
        scratch_shapes=[pltpu.SemaphoreType.DMA(())],
        compiler_params=_params(("arbitrary",)),
        name="dispatch",
    )(dest, hp)


def _expert_kernel(cnt_ref, first_ref, tot_ref, xs_hbm, wg_ref, wu_ref, wd_ref, ys_hbm,
                   xbuf, ybuf, wgu_s, wd_s, xsem, ysem, *, rows, pieces):
    e = pl.program_id(0)
    cnt, first, total = cnt_ref[e], first_ref[e], tot_ref[0]
    n_blk = (cnt + rows - 1) // rows
    d_e = wg_ref.shape[2]
    phys = rows * pieces

    def block_rows(g):
        return pl.ds(pl.multiple_of(g * phys, phys), phys)

    def x_copy(g, slot):
        return pltpu.make_async_copy(xs_hbm.at[block_rows(g), :], xbuf.at[slot], xsem.at[slot])

    def y_copy(g, slot):
        return pltpu.make_async_copy(ybuf.at[slot], ys_hbm.at[block_rows(g), :], ysem.at[slot])

    x_slots, y_slots = xbuf.shape[0], ybuf.shape[0]
    ahead = x_slots - 1

    @pl.when(n_blk > 0)
    def _():
        @pl.when(first == 0)
        def _():
            for a in range(ahead):
                @pl.when(a < total)
                def _():
                    x_copy(a, a).start()

        wgu_s[:, :d_e] = wg_ref[0].astype(BF16)
        wgu_s[:, d_e:] = wu_ref[0].astype(BF16)
        wd_s[...] = wd_ref[0].astype(BF16)

        def block(j, c):
            g = first + j
            slot = g % x_slots
            x_copy(g, slot).wait()

            @pl.when(g + ahead < total)
            def _():
                x_copy(g + ahead, (g + ahead) % x_slots).start()

            yslot = g % y_slots

            @pl.when(g >= y_slots)
            def _():
                y_copy(g - y_slots, yslot).wait()

            x = _load_row_tiles(xbuf, 0, rows, pieces, lead=(slot,))
            r_iota = lax.broadcasted_iota(I32, x.shape, 0) + j * rows
            xb = jnp.where(r_iota < cnt, x, 0.0).astype(BF16)
            gu = _dot(xb, wgu_s[...])
            gate, up = gu[:, :d_e], gu[:, d_e:]
            hid = (gate * jax.nn.sigmoid(gate) * up).astype(BF16)
            y = _dot(hid, wd_s[...])
            for s in range(pieces):
                ybuf[yslot, pl.ds(s, rows, stride=pieces), :] = y[:, s * LANES:(s + 1) * LANES]
            y_copy(g, yslot).start()
            return c

        lax.fori_loop(0, n_blk, block, 0)

        @pl.when(first + n_blk == total)
        def _():
            for back in range(1, y_slots + 1):
                @pl.when(total >= back)
                def _():
                    y_copy(total - back, (total - back) % y_slots).wait()


def _experts(cnt, blk_first, n_used, xs, w_gate, w_up, w_down, rows):
    n_exp, d, d_e = w_gate.shape
    pieces = d // LANES
    hbm = pl.BlockSpec(memory_space=pl.ANY)
    w_map = lambda e, cnt, first, tot: (e, 0, 0)
    grid_spec = pltpu.PrefetchScalarGridSpec(
        num_scalar_prefetch=3,
        grid=(n_exp,),
        in_specs=[hbm, pl.BlockSpec((1, d, d_e), w_map), pl.BlockSpec((1, d, d_e), w_map),
                  pl.BlockSpec((1, d_e, d), w_map)],
        out_specs=hbm,
        scratch_shapes=[pltpu.VMEM((EXPERT_X_SLOTS, rows * pieces, LANES), F32),
                        pltpu.VMEM((EXPERT_Y_SLOTS, rows * pieces, LANES), F32),
                        pltpu.VMEM((d, 2 * d_e), BF16), pltpu.VMEM((d_e, d), BF16),
                        pltpu.SemaphoreType.DMA((EXPERT_X_SLOTS,)), pltpu.SemaphoreType.DMA((EXPERT_Y_SLOTS,))],
    )
    return pl.pallas_call(
        functools.partial(_expert_kernel, rows=rows, pieces=pieces),
        out_shape=jax.ShapeDtypeStruct(xs.shape, F32),
        grid_spec=grid_spec,
        compiler_params=_params(("arbitrary",)),
        name="experts",
    )(cnt, blk_first, n_used, xs, w_gate, w_up, w_down)


def _combine_kernel(dest_ref, next_ref, ys_ref, wt_ref, hp_ref, x1_ref, wsgu_ref, wsd_ref, npost_ref, gt_ref,
                    o_ref, buf, sem, *, tm, pieces):
    i = pl.program_id(0)
    cur = i % 2

    def gather(rows_ref, half):
        def start(r, c):
            slot = pl.ds(pl.multiple_of(r * pieces, pieces), pieces)
            for k in range(TOP_K):
                row = pl.multiple_of(rows_ref[r * TOP_K + k], pieces)
                pltpu.make_async_copy(ys_ref.at[pl.ds(row, pieces), :], buf.at[half, k, slot, :],
                                      sem.at[half]).start(priority=k % 2)
            return c

        lax.fori_loop(0, tm, start, 0)

    @pl.when(i == 0)
    def _():
        gather(dest_ref, 0)

    @pl.when(i + 1 < pl.num_programs(0))
    def _():
        gather(next_ref, 1 - cur)

    hb = _load_row_tiles(hp_ref, 0, tm, pieces).astype(BF16)
    d_sh = wsd_ref.shape[0]
    gu = _dot(hb, wsgu_ref[...])
    gate, up = gu[:, :d_sh], gu[:, d_sh:]
    shared = _dot((gate * jax.nn.sigmoid(gate) * up).astype(BF16), wsd_ref[...])

    for k in range(TOP_K):
        pltpu.make_async_copy(ys_ref.at[pl.ds(0, tm * pieces), :], buf.at[cur, k], sem.at[cur]).wait()
    wt = wt_ref[...]
    fx = shared
    for k in range(TOP_K):
        fx = fx + wt[:, k:k + 1] * _load_row_tiles(buf, 0, tm, pieces, lead=(cur, k))
    o_ref[...] = x1_ref[...] + gt_ref[0] * _rms(fx, npost_ref[...])


def _combine(dest, ys, wt, hp, x1, wsgu, wsd, npost, gt, tokens_per_batch, tm):
    n, d = x1.shape
    pieces = d // LANES
    w = LANES
    tiles_per_batch = tokens_per_batch // tm
    row = lambda i: (i, 0)
    fixed = lambda i: (0, 0)
    return pl.pallas_call(
        functools.partial(_combine_kernel, tm=tm, pieces=pieces),
        out_shape=jax.ShapeDtypeStruct((n, d), F32),
        grid=(n // tm,),
        in_specs=[pl.BlockSpec((tm * TOP_K,), lambda i: (i,), memory_space=pltpu.SMEM),
                  pl.BlockSpec((tm * TOP_K,), lambda i: (jnp.minimum(i + 1, n // tm - 1),), memory_space=pltpu.SMEM),
                  pl.BlockSpec(memory_space=pl.ANY),
                  pl.BlockSpec((tm, TOP_K), row), pl.BlockSpec((tm * pieces, w), row), pl.BlockSpec((tm, d), row),
                  pl.BlockSpec(wsgu.shape, fixed), pl.BlockSpec(wsd.shape, fixed),
                  pl.BlockSpec((1, d), fixed), pl.BlockSpec((1, 1, d), lambda i: (i // tiles_per_batch, 0, 0))],
        out_specs=pl.BlockSpec((tm, d), row),
        scratch_shapes=[pltpu.VMEM((2, TOP_K, tm * pieces, w), F32), pltpu.SemaphoreType.DMA((2,))],
        compiler_params=_params(("arbitrary",)),
        name="combine",
    )(dest, dest, ys, wt, hp, x1, wsgu, wsd, npost.reshape(1, d), gt)


def _layer(x, ctx, mod_x, mod_c, norm_pre_mix, norm_post_mix, norm_pre_ffn, norm_post_ffn,
           w_in, s5_a_re, s5_a_im, s5_log_dt, s5_b_re, s5_b_im, s5_c_re, s5_c_im, s5_d, w_glu, b_glu,
           sgu_norm, sgu_w, sgu_b, w_branch_s5, w_branch_sgu, w_out,
           w_router, router_bias, w_exp_gate, w_exp_up, w_exp_down, w_sh_gate, w_sh_up, w_sh_down):
    bsz, t_len, d = x.shape
    c_len = ctx.shape[1]
    n = bsz * t_len
    d_s5, d_sgu = w_glu.shape[0], sgu_norm.shape[0]
    g_n = d_s5 // S5_GROUP_CH
    n_exp = w_router.shape[1]
    chunk = S5_CHUNK
    tm = min(512, t_len)
    assert t_len % tm == 0 and tm % SGU_CHUNK == 0 and t_len % chunk == 0 and c_len % chunk == 0

    sh_m, sc_m, gt_m, sh_f, sc_f, gt_f = [v.reshape(bsz, 1, d) for v in jnp.split(mod_x, 6, axis=-1)]
    csh_m, csc_m = mod_c[:d], mod_c[d:2 * d]

    w_in_bf = w_in.astype(BF16)

    nc_lat, nc_ctx = t_len // chunk, c_len // chunk
    n_steps = max(1, (max(nc_lat, nc_ctx) - 1).bit_length())
    m_t, ke_t, kc_t, alp = _s5_tables(s5_a_re, s5_a_im, s5_log_dt, s5_b_re, s5_b_im, s5_c_re, s5_c_im,
                                      chunk, n_steps)

    ctx2 = ctx.reshape(bsz * c_len, d)
    u_ctx = _inproj_u(ctx2, norm_pre_mix, csh_m, csc_m, w_in_bf[:, :d_s5], min(512, bsz * c_len))
    zero_init = jnp.zeros((g_n, bsz, 4 * S5_STATE), F32)
    (ctx_final,) = _s5(_to_groups(u_ctx, bsz, nc_ctx, chunk, g_n), ke_t, alp, zero_init, nb=bsz, nc=nc_ctx)

    x2 = x.reshape(n, d)
    u, gu, vn, g1, g2 = _inproj(x2, norm_pre_mix, sh_m, sc_m, w_in_bf, sgu_norm, t_len, tm, d_s5, d_sgu)
    _, yg = _s5(_to_groups(u, bsz, nc_lat, chunk, g_n), ke_t, alp, ctx_final, m_t, kc_t, nb=bsz, nc=nc_lat)
    y = _from_groups(yg, bsz, nc_lat, chunk, g_n)

    ch = d_sgu // SGU_GROUPS
    per_tile = LANES // ch
    sguw = sgu_w.reshape(SGU_GROUPS // per_tile, per_tile, SGU_CHUNK, SGU_CHUNK)
    sguw = sguw.transpose(0, 2, 1, 3).reshape(SGU_GROUPS // per_tile, SGU_CHUNK, per_tile * SGU_CHUNK).astype(BF16)
    sgub = jnp.repeat(sgu_b.T, ch, axis=1)

    x1, hp, logits = _mix(y, u, gu, vn, g1, g2, x2, s5_d, w_glu.astype(BF16), b_glu, sguw, sgub,
                          w_branch_s5.astype(BF16), w_branch_sgu.astype(BF16), w_out.astype(BF16),
                          norm_post_mix, gt_m, norm_pre_ffn, sh_f, sc_f, w_router.astype(BF16), t_len, tm)

    tn = min(512, n)
    idx, wts, counts = _route(logits.T, router_bias, tn)
    rows = EXPERT_ROWS
    cnt = counts.reshape(n_exp).astype(I32)
    nblk = (cnt + rows - 1) // rows
    blk_end = jnp.cumsum(nblk)
    blk_start = blk_end - nblk
    cap = ((n * TOP_K) // rows + n_exp) * rows
    dest = _dest(idx, (blk_start * rows).astype(F32).reshape(n_exp, 1), tn)
    pieces = d // LANES
    dest = dest.T.reshape(n * TOP_K) * pieces

    td = min(256, n)
    xs = _dispatch(dest, hp, cap, td, pieces)
    ys = _experts(cnt, blk_start.astype(I32), blk_end[-1:].astype(I32), xs, w_exp_gate, w_exp_up, w_exp_down, rows)
    wsgu = jnp.concatenate([w_sh_gate, w_sh_up], axis=1).astype(BF16)
    out = _combine(dest, ys, wts.T, hp, x1, wsgu, w_sh_down.astype(BF16), norm_post_ffn, gt_f, t_len, td)
    return out.reshape(bsz, t_len, d)


def kernel(x, c, ctx, c_ctx, w_mod, b_mod, norm_pre_mix, norm_post_mix, norm_pre_ffn, norm_post_ffn, w_in, s5_a_re, s5_a_im, s5_log_dt, s5_b_re, s5_b_im, s5_c_re, s5_c_im, s5_d, w_glu, b_glu, sgu_norm, sgu_w, sgu_b, w_branch_s5, w_branch_sgu, w_out, w_router, router_bias, w_exp_gate, w_exp_up, w_exp_down, w_sh_gate, w_sh_up, w_sh_down):
    depth = w_mod.shape[0]
    assert depth == 1, "the context stream is only carried through the last layer's S5 states"
    bsz = x.shape[0]
    pad = (-(bsz + 1)) % 8
    cpad = jnp.concatenate([c, c_ctx[None, :], jnp.zeros((pad, c.shape[1]), c.dtype)], axis=0)
    mod = _modulation(cpad, w_mod[0], b_mod[0])
    return _layer(x, ctx, mod[:bsz], mod[bsz], norm_pre_mix[0], norm_post_mix[0], norm_pre_ffn[0],
                  norm_post_ffn[0], w_in[0], s5_a_re[0], s5_a_im[0], s5_log_dt[0], s5_b_re[0], s5_b_im[0],
                  s5_c_re[0], s5_c_im[0], s5_d[0], w_glu[0], b_glu[0], sgu_norm[0], sgu_w[0], sgu_b[0],
                  w_branch_s5[0], w_branch_sgu[0], w_out[0], w_router[0], router_bias[0],
                  w_exp_gate[0], w_exp_up[0], w_exp_down[0], w_sh_gate[0], w_sh_up[0], w_sh_down[0])
```

```python
import functools
import math

import jax
import jax.numpy as jnp
from jax import lax
from jax.experimental import pallas as pl
from jax.experimental.pallas import tpu as pltpu

F32 = jnp.float32
BF16 = jnp.bfloat16
I32 = jnp.int32

EPS = 1e-6
S5_GROUP_CH = 16
S5_STATE = 64
S5_CHUNK = 64
SGU_GROUPS = 8
SGU_CHUNK = 128
N_EXPERT_GROUPS = 8
TOPK_GROUPS = 4
TOP_K = 8
ROUTE_SCALE = 2.5
LANES = 128
EXPERT_ROWS = 256
EXPERT_X_SLOTS = 4
EXPERT_Y_SLOTS = 3
VMEM_LIMIT = 56 * 1024 * 1024


def _params(sem):
    return pltpu.CompilerParams(dimension_semantics=sem, vmem_limit_bytes=VMEM_LIMIT)


def _rms(v, g):
    return v * lax.rsqrt(jnp.mean(v * v, axis=-1, keepdims=True) + EPS) * g


def _dot(a, b):
    return jnp.dot(a, b, preferred_element_type=F32)


def _store_row_tiles(ref, first_row, val):
    pieces = val.shape[1] // LANES
    for s in range(pieces):
        ref[pl.ds(first_row * pieces + s, val.shape[0], stride=pieces), :] = val[:, s * LANES:(s + 1) * LANES]


def _load_row_tiles(ref, first_row, n_rows, pieces, lead=()):
    return jnp.concatenate([ref[lead + (pl.ds(first_row * pieces + s, n_rows, stride=pieces), slice(None))]
                            for s in range(pieces)], axis=1)


def _mod_kernel(c_ref, w_ref, b_ref, o_ref):
    cv = c_ref[...]
    s = cv * jax.nn.sigmoid(cv)
    o_ref[...] = _dot(s.astype(BF16), w_ref[...].astype(BF16)) + b_ref[...]


def _modulation(cpad, w_mod, b_mod):
    d, n6 = w_mod.shape
    tn = 1024
    return pl.pallas_call(
        _mod_kernel,
        out_shape=jax.ShapeDtypeStruct((cpad.shape[0], n6), F32),
        grid=(n6 // tn,),
        in_specs=[pl.BlockSpec(cpad.shape, lambda j: (0, 0)),
                  pl.BlockSpec((d, tn), lambda j: (0, j)),
                  pl.BlockSpec((1, tn), lambda j: (0, j))],
        out_specs=pl.BlockSpec((cpad.shape[0], tn), lambda j: (0, j)),
        compiler_params=_params(("parallel",)),
        name="mod",
    )(cpad, w_mod, b_mod.reshape(1, n6))


def _inproj_kernel(x_ref, g_ref, sh_ref, sc_ref, w_ref, sgun_ref,
                   u_ref, gu_ref, vn_ref, g1_ref, g2_ref, *, d_s5, d_sgu, d_model):
    h = _rms(x_ref[...], g_ref[...])
    hb = (h * (1.0 + sc_ref[0]) + sh_ref[0]).astype(BF16)
    o1, o2, o3, o4 = d_s5, d_s5 + d_sgu, d_s5 + 2 * d_sgu, d_s5 + 2 * d_sgu + d_model
    u_ref[...] = _dot(hb, w_ref[:, 0:o1]).astype(BF16)
    gu_ref[...] = jax.nn.gelu(_dot(hb, w_ref[:, o1:o2])).astype(BF16)
    v = jax.nn.gelu(_dot(hb, w_ref[:, o2:o3]))
    vn_ref[...] = _rms(v, sgun_ref[...]).astype(BF16)
    g1_ref[...] = jax.nn.sigmoid(_dot(hb, w_ref[:, o3:o4])).astype(BF16)
    g2_ref[...] = jax.nn.sigmoid(_dot(hb, w_ref[:, o4:o4 + d_model])).astype(BF16)


def _inproj(x2, norm_g, shift, scale, w_in_bf, sgu_norm, tokens_per_batch, tm, d_s5, d_sgu):
    n, d = x2.shape
    tiles_per_batch = tokens_per_batch // tm
    row = lambda i: (i, 0)
    fixed = lambda i: (0, 0)
    per_batch = lambda i: (i // tiles_per_batch, 0, 0)
    return pl.pallas_call(
        functools.partial(_inproj_kernel, d_s5=d_s5, d_sgu=d_sgu, d_model=d),
        out_shape=(jax.ShapeDtypeStruct((n, d_s5), BF16), jax.ShapeDtypeStruct((n, d_sgu), BF16),
                   jax.ShapeDtypeStruct((n, d_sgu), BF16), jax.ShapeDtypeStruct((n, d), BF16),
                   jax.ShapeDtypeStruct((n, d), BF16)),
        grid=(n // tm,),
        in_specs=[pl.BlockSpec((tm, d), row), pl.BlockSpec((1, d), fixed),
                  pl.BlockSpec((1, 1, d), per_batch), pl.BlockSpec((1, 1, d), per_batch),
                  pl.BlockSpec(w_in_bf.shape, fixed), pl.BlockSpec((1, d_sgu), fixed)],
        out_specs=(pl.BlockSpec((tm, d_s5), row), pl.BlockSpec((tm, d_sgu), row),
                   pl.BlockSpec((tm, d_sgu), row), pl.BlockSpec((tm, d), row), pl.BlockSpec((tm, d), row)),
        compiler_params=_params(("parallel",)),
        name="inproj",
    )(x2, norm_g.reshape(1, d), shift, scale, w_in_bf, sgu_norm.reshape(1, d_sgu))


def _inproj_u_kernel(x_ref, g_ref, sh_ref, sc_ref, w_ref, u_ref):
    h = _rms(x_ref[...], g_ref[...])
    hb = (h * (1.0 + sc_ref[...]) + sh_ref[...]).astype(BF16)
    u_ref[...] = _dot(hb, w_ref[...]).astype(BF16)


def _inproj_u(x2, norm_g, shift, scale, w_u_bf, tm):
    n, d = x2.shape
    d_s5 = w_u_bf.shape[1]
    row = lambda i: (i, 0)
    fixed = lambda i: (0, 0)
    return pl.pallas_call(
        _inproj_u_kernel,
        out_shape=jax.ShapeDtypeStruct((n, d_s5), BF16),
        grid=(n // tm,),
        in_specs=[pl.BlockSpec((tm, d), row), pl.BlockSpec((1, d), fixed), pl.BlockSpec((1, d), fixed),
                  pl.BlockSpec((1, d), fixed), pl.BlockSpec(w_u_bf.shape, fixed)],
        out_specs=pl.BlockSpec((tm, d_s5), row),
        compiler_params=_params(("parallel",)),
        name="inproj_ctx",
    )(x2, norm_g.reshape(1, d), shift.reshape(1, d), scale.reshape(1, d), w_u_bf)


def _s5_tables(a_re, a_im, log_dt, b_re, b_im, c_re, c_im, chunk, n_steps):
    hi = lax.Precision.HIGHEST
    g_n, p_n, j_n = a_re.shape[1], a_re.shape[2], b_re.shape[3]
    dt = jnp.exp(log_dt)[..., None]
    lam_re, lam_im = dt * a_re, dt * a_im
    ab_re, ab_im = jnp.exp(lam_re) * jnp.cos(lam_im), jnp.exp(lam_re) * jnp.sin(lam_im)
    den = a_re * a_re + a_im * a_im
    q_re = ((ab_re - 1.0) * a_re + ab_im * a_im) / den
    q_im = (ab_im * a_re - (ab_re - 1.0) * a_im) / den
    bb_re = q_re[..., None] * b_re - q_im[..., None] * b_im
    bb_im = q_re[..., None] * b_im + q_im[..., None] * b_re
    k = jnp.arange(chunk + 1, dtype=F32)[:, None, None, None]
    mag = jnp.exp(k * lam_re[None])
    pw_re, pw_im = mag * jnp.cos(k * lam_im[None]), mag * jnp.sin(k * lam_im[None])

    def lag_kernels(d):
        x_re = pw_re[:chunk, d, :, :, None] * bb_re[d][None] - pw_im[:chunk, d, :, :, None] * bb_im[d][None]
        x_im = pw_re[:chunk, d, :, :, None] * bb_im[d][None] + pw_im[:chunk, d, :, :, None] * bb_re[d][None]
        return (jnp.einsum('gjp,kgpi->gkij', c_re[d], x_re, precision=hi)
                - jnp.einsum('gjp,kgpi->gkij', c_im[d], x_im, precision=hi))

    kf, kb = lag_kernels(0), lag_kernels(1)
    lags = jnp.concatenate([kb[:, 1:][:, ::-1], kf[:, 0:1] + kb[:, 0:1], kf[:, 1:],
                            jnp.zeros_like(kf[:, 0:1])], axis=1)
    lags = lags.transpose(0, 2, 3, 1)

    pf_re, pf_im = pw_re[:chunk, 0][::-1], pw_im[:chunk, 0][::-1]
    pb_re, pb_im = pw_re[:chunk, 1], pw_im[:chunk, 1]

    def state_in(p_re, p_im, d):
        e_re = p_re[..., None] * bb_re[d][None] - p_im[..., None] * bb_im[d][None]
        e_im = p_re[..., None] * bb_im[d][None] + p_im[..., None] * bb_re[d][None]
        return e_re.transpose(1, 3, 0, 2), e_im.transpose(1, 3, 0, 2)

    ef_re, ef_im = state_in(pf_re, pf_im, 0)
    eb_re, eb_im = state_in(pb_re, pb_im, 1)
    ke = jnp.concatenate([ef_re, eb_re, ef_im, eb_im], axis=-1).reshape(g_n, chunk * j_n, 4 * p_n)

    qf_re, qf_im = pw_re[1:chunk + 1, 0], pw_im[1:chunk + 1, 0]
    qb_re, qb_im = pw_re[1:chunk + 1, 1][::-1], pw_im[1:chunk + 1, 1][::-1]

    def state_out(p_re, p_im, d):
        cr, ci = c_re[d].transpose(0, 2, 1), c_im[d].transpose(0, 2, 1)
        pr, pi = p_re.transpose(1, 2, 0), p_im.transpose(1, 2, 0)
        o_re = pr[:, :, None, :] * cr[..., None] - pi[:, :, None, :] * ci[..., None]
        o_im = pr[:, :, None, :] * ci[..., None] + pi[:, :, None, :] * cr[..., None]
        return o_re, o_im

    of_re, of_im = state_out(qf_re, qf_im, 0)
    ob_re, ob_im = state_out(qb_re, qb_im, 1)
    kc = jnp.concatenate([of_re, ob_re, -of_im, -ob_im], axis=1).reshape(g_n, 4 * p_n, chunk * j_n)

    e = (chunk * (2.0 ** jnp.arange(n_steps, dtype=F32)))[:, None, None, None]
    mg = jnp.exp(e * lam_re[None])
    al_re, al_im = mg * jnp.cos(e * lam_im[None]), mg * jnp.sin(e * lam_im[None])
    alp = jnp.concatenate([al_re[:, 0], al_re[:, 1], al_im[:, 0], al_im[:, 1]], axis=-1)
    return lags, ke.astype(BF16), kc.astype(BF16), alp.transpose(1, 0, 2)


def _build_toeplitz(lag_ref, m_s, chunk):
    j_n = lag_ref.shape[1]
    per_tile = LANES // chunk
    low = lax.broadcasted_iota(I32, (chunk, LANES), 1) < chunk
    for i in range(j_n):
        for q in range(j_n // per_tile):
            tile = None
            for p in range(per_tile):
                j = q * per_tile + p
                row = jnp.broadcast_to(lag_ref[0, i, j:j + 1, :], (chunk, LANES))
                rot = pltpu.roll(row, (p * chunk + LANES - (chunk - 1)) % LANES, 1, stride=1, stride_axis=0)
                tile = rot if tile is None else jnp.where(low, tile, rot)
            m_s[i * chunk:(i + 1) * chunk, q * LANES:(q + 1) * LANES] = tile.astype(BF16)


def _s5_kernel(*refs, nb, nc, readout):
    if readout:
        u_ref, ke_ref, alp_ref, init_ref, lag_ref, kc_ref, fin_ref, y_ref, m_s = refs
    else:
        u_ref, ke_ref, alp_ref, init_ref, fin_ref = refs
    rows = nb * nc
    half = 2 * S5_STATE
    u = u_ref[0]
    e = _dot(u, ke_ref[0])
    er, ei = e[:, :half], e[:, half:]
    r_idx = lax.broadcasted_iota(I32, (rows, half), 0)
    if nc & (nc - 1) == 0:
        n_idx, b_idx = r_idx & (nc - 1), r_idx >> (nc.bit_length() - 1)
    else:
        n_idx, b_idx = lax.rem(r_idx, nc), lax.div(r_idx, nc)
    is_f = lax.broadcasted_iota(I32, (rows, half), 1) < S5_STATE
    seen = jnp.where(is_f, n_idx, nc - 1 - n_idx)
    init = init_ref[0]
    init_r = jnp.zeros((rows, half), F32)
    init_i = jnp.zeros((rows, half), F32)
    for b in range(nb):
        init_r = jnp.where(b_idx == b, init[b:b + 1, :half], init_r)
        init_i = jnp.where(b_idx == b, init[b:b + 1, half:], init_i)
    alp = alp_ref[0]
    ar, ai = alp[0:1, :half], alp[0:1, half:]
    er = er + jnp.where(seen == 0, ar * init_r - ai * init_i, 0.0)
    ei = ei + jnp.where(seen == 0, ar * init_i + ai * init_r, 0.0)

    def from_prev(v, dist):
        return jnp.where(is_f, pltpu.roll(v, dist, 0), pltpu.roll(v, rows - dist, 0))

    d, s = 1, 0
    while d < nc:
        ar, ai = alp[s:s + 1, :half], alp[s:s + 1, half:]
        sr = jnp.where(seen >= d, from_prev(er, d), 0.0)
        si = jnp.where(seen >= d, from_prev(ei, d), 0.0)
        er, ei = er + (ar * sr - ai * si), ei + (ar * si + ai * sr)
        d, s = d * 2, s + 1
    is_f_row = lax.broadcasted_iota(I32, (1, half), 1) < S5_STATE
    for b in range(nb):
        lo, hi = b * nc, b * nc + nc - 1
        fin_ref[0, b:b + 1, :half] = jnp.where(is_f_row, er[hi:hi + 1], er[lo:lo + 1])
        fin_ref[0, b:b + 1, half:] = jnp.where(is_f_row, ei[hi:hi + 1], ei[lo:lo + 1])
    if readout:
        if nc > 1:
            sin_r = jnp.where(seen >= 1, from_prev(er, 1), init_r)
            sin_i = jnp.where(seen >= 1, from_prev(ei, 1), init_i)
        else:
            sin_r, sin_i = init_r, init_i
        s_in = jnp.concatenate([sin_r, sin_i], axis=1).astype(BF16)
        _build_toeplitz(lag_ref, m_s, S5_CHUNK)
        y_ref[0] = (_dot(u, m_s[...]) + _dot(s_in, kc_ref[0])).astype(BF16)


def _s5(ug, ke, alp, init, m=None, kc=None, *, nb, nc):
    g_n, rows, width = ug.shape
    readout = m is not None
    grp = lambda g: (g, 0, 0)
    in_specs = [pl.BlockSpec((1, rows, width), grp), pl.BlockSpec((1,) + ke.shape[1:], grp),
                pl.BlockSpec((1,) + alp.shape[1:], grp), pl.BlockSpec((1,) + init.shape[1:], grp)]
    out_shape = [jax.ShapeDtypeStruct(init.shape, F32)]
    out_specs = [pl.BlockSpec((1,) + init.shape[1:], grp)]
    args = [ug, ke, alp, init]
    scratch = []
    if readout:
        assert m.shape[3] == LANES and 2 * S5_CHUNK == LANES
        in_specs += [pl.BlockSpec((1,) + m.shape[1:], lambda g: (g, 0, 0, 0)),
                     pl.BlockSpec((1,) + kc.shape[1:], grp)]
        out_shape.append(jax.ShapeDtypeStruct(ug.shape, BF16))
        out_specs.append(pl.BlockSpec((1, rows, width), grp))
        args += [m, kc]
        scratch = [pltpu.VMEM((width, width), BF16)]
    return pl.pallas_call(
        functools.partial(_s5_kernel, nb=nb, nc=nc, readout=readout),
        out_shape=tuple(out_shape), grid=(g_n,), in_specs=in_specs, out_specs=tuple(out_specs),
        scratch_shapes=scratch,
        compiler_params=_params(("parallel",)),
        name="s5_readout" if readout else "s5_state",
    )(*args)


def _to_groups(u, nb, nc, chunk, g_n):
    j_n = u.shape[1] // g_n
    return u.reshape(nb * nc, chunk, g_n, j_n).transpose(2, 0, 3, 1).reshape(g_n, nb * nc, j_n * chunk)


def _from_groups(y, nb, nc, chunk, g_n):
    j_n = y.shape[2] // chunk
    return y.reshape(g_n, nb * nc, j_n, chunk).transpose(1, 3, 0, 2).reshape(nb * nc * chunk, g_n * j_n)


def _mix_kernel(y_ref, u_ref, gu_ref, vn_ref, g1_ref, g2_ref, x_ref,
                d_ref, wglu_ref, bglu_ref, sguw_ref, sgub_ref, wb1_ref, wb2_ref, wout_ref,
                npost_ref, gt_ref, npre_ref, shf_ref, scf_ref, wr_ref,
                x1_ref, hp_ref, lg_ref, *, tm):
    y = jax.nn.gelu(y_ref[...].astype(F32) + d_ref[...] * u_ref[...].astype(F32))
    y_s5 = y * jax.nn.sigmoid(_dot(y.astype(BF16), wglu_ref[...]) + bglu_ref[...])

    lane = lax.broadcasted_iota(I32, (1, LANES), 1)
    m_lo = (lane < LANES // 2).astype(F32).astype(BF16)
    m_hi = (lane >= LANES // 2).astype(F32).astype(BF16)
    vn = vn_ref[...]
    chunks = []
    for c in range(tm // SGU_CHUNK):
        vc = vn[c * SGU_CHUNK:(c + 1) * SGU_CHUNK, :]
        tiles = []
        for q in range(vc.shape[1] // LANES):
            vt = vc[:, q * LANES:(q + 1) * LANES]
            rhs = jnp.concatenate([vt * m_lo, vt * m_hi], axis=0)
            tiles.append(_dot(sguw_ref[q], rhs))
        chunks.append(jnp.concatenate(tiles, axis=1) + sgub_ref[...])
    mixed = jnp.concatenate(chunks, axis=0)
    y_sgu = gu_ref[...].astype(F32) * mixed

    merged = (g1_ref[...].astype(F32) * _dot(y_s5.astype(BF16), wb1_ref[...])
              + g2_ref[...].astype(F32) * _dot(y_sgu.astype(BF16), wb2_ref[...]))
    mx = _dot(merged.astype(BF16), wout_ref[...])
    x1 = x_ref[...] + gt_ref[0] * _rms(mx, npost_ref[...])
    x1_ref[...] = x1
    hp = _rms(x1, npre_ref[...]) * (1.0 + scf_ref[0]) + shf_ref[0]
    _store_row_tiles(hp_ref, 0, hp)
    lg_ref[...] = _dot(hp.astype(BF16), wr_ref[...])


def _mix(y, u, gu, vn, g1, g2, x2, s5_d, wglu, bglu, sguw, sgub, wb1, wb2, wout,
         npost, gt, npre, shf, scf, wr, tokens_per_batch, tm):
    n, d = x2.shape
    d_s5, d_sgu, n_exp = y.shape[1], gu.shape[1], wr.shape[1]
    tiles_per_batch = tokens_per_batch // tm
    row = lambda i: (i, 0)
    fixed = lambda i: (0, 0)
    fixed3 = lambda i: (0, 0, 0)
    per_batch = lambda i: (i // tiles_per_batch, 0, 0)
    vec = lambda w: pl.BlockSpec((1, w), fixed)
    return pl.pallas_call(
        functools.partial(_mix_kernel, tm=tm),
        out_shape=(jax.ShapeDtypeStruct((n, d), F32), jax.ShapeDtypeStruct((n * d // LANES, LANES), F32),
                   jax.ShapeDtypeStruct((n, n_exp), F32)),
        grid=(n // tm,),
        in_specs=[pl.BlockSpec((tm, d_s5), row), pl.BlockSpec((tm, d_s5), row), pl.BlockSpec((tm, d_sgu), row),
                  pl.BlockSpec((tm, d_sgu), row), pl.BlockSpec((tm, d), row), pl.BlockSpec((tm, d), row),
                  pl.BlockSpec((tm, d), row),
                  vec(d_s5), pl.BlockSpec(wglu.shape, fixed), vec(d_s5),
                  pl.BlockSpec(sguw.shape, fixed3), pl.BlockSpec(sgub.shape, fixed),
                  pl.BlockSpec(wb1.shape, fixed), pl.BlockSpec(wb2.shape, fixed), pl.BlockSpec(wout.shape, fixed),
                  vec(d), pl.BlockSpec((1, 1, d), per_batch), vec(d),
                  pl.BlockSpec((1, 1, d), per_batch), pl.BlockSpec((1, 1, d), per_batch),
                  pl.BlockSpec(wr.shape, fixed)],
        out_specs=(pl.BlockSpec((tm, d), row), pl.BlockSpec((tm * d // LANES, LANES), row),
                   pl.BlockSpec((tm, n_exp), row)),
        compiler_params=_params(("parallel",)),
        name="mix",
    )(y, u, gu, vn, g1, g2, x2, s5_d.reshape(1, d_s5), wglu, bglu.reshape(1, d_s5), sguw, sgub,
      wb1, wb2, wout, npost.reshape(1, d), gt, npre.reshape(1, d), shf, scf, wr)


def _route_kernel(lg_ref, bias_ref, idx_ref, w_ref, cnt_ref, *, n_exp, tn):
    per_group = n_exp // N_EXPERT_GROUPS
    neg = jnp.float32(-jnp.inf)

    scores = jax.nn.sigmoid(lg_ref[...])
    sel = scores + bias_ref[...]
    gs = []
    for g in range(N_EXPERT_GROUPS):
        sg = sel[g * per_group:(g + 1) * per_group, :]
        m1 = jnp.max(sg, axis=0, keepdims=True)
        is_m1 = sg == m1
        n_m1 = jnp.sum(jnp.where(is_m1, 1.0, 0.0), axis=0, keepdims=True)
        rest = jnp.max(jnp.where(is_m1, neg, sg), axis=0, keepdims=True)
        gs.append(m1 + jnp.where(n_m1 >= 2.0, m1, rest))
    gsm = jnp.concatenate(gs, axis=0)
    g_iota = lax.broadcasted_iota(I32, gsm.shape, 0)
    e_iota = lax.broadcasted_iota(I32, sel.shape, 0).astype(F32)
    masked = []
    for g in range(N_EXPERT_GROUPS):
        mine = gsm[g:g + 1, :]
        beats = jnp.where(gsm > mine, 1.0, jnp.where(gsm == mine, jnp.where(g_iota < g, 1.0, 0.0), 0.0))
        n_beats = jnp.sum(beats, axis=0, keepdims=True)
        masked.append(jnp.where(n_beats < float(TOPK_GROUPS), sel[g * per_group:(g + 1) * per_group, :], neg))
    start = jnp.concatenate(masked, axis=0)
    selm = start
    picked, vals = [], []
    for _ in range(TOP_K):
        m = jnp.max(selm, axis=0, keepdims=True)
        first = jnp.min(jnp.where(selm == m, e_iota, float(n_exp)), axis=0, keepdims=True)
        one = e_iota == first
        picked.append(first)
        vals.append(jnp.sum(jnp.where(one, scores, 0.0), axis=0, keepdims=True))
        selm = jnp.where(one, neg, selm)
    idx_ref[...] = jnp.concatenate(picked, axis=0).astype(I32)
    wv = jnp.concatenate(vals, axis=0)
    w_ref[...] = wv / jnp.sum(wv, axis=0, keepdims=True) * ROUTE_SCALE
    hot = jnp.where(selm == neg, jnp.where(start == neg, 0.0, 1.0), 0.0)

    @pl.when(pl.program_id(0) == 0)
    def _():
        cnt_ref[...] = jnp.zeros_like(cnt_ref)

    cnt_ref[...] += jnp.sum(hot, axis=1, keepdims=True)


def _route(logits_t, bias, tn):
    n_exp, n = logits_t.shape
    col = lambda i: (0, i)
    fixed = lambda i: (0, 0)
    return pl.pallas_call(
        functools.partial(_route_kernel, n_exp=n_exp, tn=tn),
        out_shape=(jax.ShapeDtypeStruct((TOP_K, n), I32), jax.ShapeDtypeStruct((TOP_K, n), F32),
                   jax.ShapeDtypeStruct((n_exp, 1), F32)),
        grid=(n // tn,),
        in_specs=[pl.BlockSpec((n_exp, tn), col), pl.BlockSpec((n_exp, 1), fixed)],
        out_specs=(pl.BlockSpec((TOP_K, tn), col), pl.BlockSpec((TOP_K, tn), col),
                   pl.BlockSpec((n_exp, 1), fixed)),
        compiler_params=_params(("arbitrary",)),
        name="route",
    )(logits_t, bias.reshape(n_exp, 1))


def _dest_kernel(idx_ref, start_ref, dest_ref, carry_ref, *, n_exp, tn):
    @pl.when(pl.program_id(0) == 0)
    def _():
        carry_ref[...] = start_ref[...]

    idx = idx_ref[...]
    e_iota = lax.broadcasted_iota(I32, (n_exp, tn), 0)
    hot = jnp.zeros((n_exp, tn), F32)
    for k in range(TOP_K):
        hot = hot + jnp.where(e_iota == idx[k:k + 1, :], 1.0, 0.0)
    before = jnp.where(lax.broadcasted_iota(I32, (tn, tn), 0) < lax.broadcasted_iota(I32, (tn, tn), 1), 1.0, 0.0)
    rank = _dot(hot.astype(BF16), before.astype(BF16)) + carry_ref[...]
    dest_ref[...] = jnp.concatenate(
        [jnp.sum(jnp.where(e_iota == idx[k:k + 1, :], rank, 0.0), axis=0, keepdims=True)
         for k in range(TOP_K)], axis=0).astype(I32)
    carry_ref[...] += jnp.sum(hot, axis=1, keepdims=True)


def _dest(idx, start, tn):
    n = idx.shape[1]
    n_exp = start.shape[0]
    col = lambda i: (0, i)
    return pl.pallas_call(
        functools.partial(_dest_kernel, n_exp=n_exp, tn=tn),
        out_shape=jax.ShapeDtypeStruct((TOP_K, n), I32),
        grid=(n // tn,),
        in_specs=[pl.BlockSpec((TOP_K, tn), col), pl.BlockSpec((n_exp, 1), lambda i: (0, 0))],
        out_specs=pl.BlockSpec((TOP_K, tn), col),
        scratch_shapes=[pltpu.VMEM((n_exp, 1), F32)],
        compiler_params=_params(("arbitrary",)),
        name="dest",
    )(idx, start)


def _dispatch_kernel(dest_ref, hp_ref, xs_ref, sem, *, tm, pieces):
    def start(r, c):
        src = hp_ref.at[pl.ds(pl.multiple_of(r * pieces, pieces), pieces), :]
        for k in range(TOP_K):
            row = pl.multiple_of(dest_ref[r * TOP_K + k], pieces)
            pltpu.make_async_copy(src, xs_ref.at[pl.ds(row, pieces), :], sem).start(priority=k % 2)
        return c

    lax.fori_loop(0, tm, start, 0)
    for k in range(TOP_K):
        pltpu.make_async_copy(hp_ref, xs_ref.at[pl.ds(0, tm * pieces), :], sem).wait()


def _dispatch(dest, hp, cap, tm, pieces):
    n = hp.shape[0] // pieces
    return pl.pallas_call(
        functools.partial(_dispatch_kernel, tm=tm, pieces=pieces),
        out_shape=jax.ShapeDtypeStruct((cap * pieces, LANES), hp.dtype),
        grid=(n // tm,),
        in_specs=[pl.BlockSpec((tm * TOP_K,), lambda i: (i,), memory_space=pltpu.SMEM),
                  pl.BlockSpec((tm * pieces, LANES), lambda i: (i, 0))],
        out_specs=pl.BlockSpec(memory_space=pl.ANY),
        scratch_shapes=[pltpu.SemaphoreType.DMA(())],
        compiler_params=_params(("arbitrary",)),
        name="dispatch",
    )(dest, hp)


def _expert_kernel(cnt_ref, first_ref, tot_ref, xs_hbm, wg_ref, wu_ref, wd_ref, ys_hbm,
                   xbuf, ybuf, wgu_s, wd_s, xsem, ysem, *, rows, pieces):
    e = pl.program_id(0)
    cnt, first, total = cnt_ref[e], first_ref[e], tot_ref[0]
    n_blk = (cnt + rows - 1) // rows
    d_e = wg_ref.shape[2]
    phys = rows * pieces

    def block_rows(g):
        return pl.ds(pl.multiple_of(g * phys, phys), phys)

    def x_copy(g, slot):
        return pltpu.make_async_copy(xs_hbm.at[block_rows(g), :], xbuf.at[slot], xsem.at[slot])

    def y_copy(g, slot):
        return pltpu.make_async_copy(ybuf.at[slot], ys_hbm.at[block_rows(g), :], ysem.at[slot])

    x_slots, y_slots = xbuf.shape[0], ybuf.shape[0]
    ahead = x_slots - 1

    @pl.when(n_blk > 0)
    def _():
        @pl.when(first == 0)
        def _():
            for a in range(ahead):
                @pl.when(a < total)
                def _():
                    x_copy(a, a).start()

        wgu_s[:, :d_e] = wg_ref[0].astype(BF16)
        wgu_s[:, d_e:] = wu_ref[0].astype(BF16)
        wd_s[...] = wd_ref[0].astype(BF16)

        def block(j, c):
            g = first + j
            slot = g % x_slots
            x_copy(g, slot).wait()

            @pl.when(g + ahead < total)
            def _():
                x_copy(g + ahead, (g + ahead) % x_slots).start()

            yslot = g % y_slots

            @pl.when(g >= y_slots)
            def _():
                y_copy(g - y_slots, yslot).wait()

            x = _load_row_tiles(xbuf, 0, rows, pieces, lead=(slot,))
            r_iota = lax.broadcasted_iota(I32, x.shape, 0) + j * rows
            xb = jnp.where(r_iota < cnt, x, 0.0).astype(BF16)
            gu = _dot(xb, wgu_s[...])
            gate, up = gu[:, :d_e], gu[:, d_e:]
            hid = (gate * jax.nn.sigmoid(gate) * up).astype(BF16)
            y = _dot(hid, wd_s[...])
            for s in range(pieces):
                ybuf[yslot, pl.ds(s, rows, stride=pieces), :] = y[:, s * LANES:(s + 1) * LANES]
            y_copy(g, yslot).start()
            return c

        lax.fori_loop(0, n_blk, block, 0)

        @pl.when(first + n_blk == total)
        def _():
            for back in range(1, y_slots + 1):
                @pl.when(total >= back)
                def _():
                    y_copy(total - back, (total - back) % y_slots).wait()


def _experts(cnt, blk_first, n_used, xs, w_gate, w_up, w_down, rows):
    n_exp, d, d_e = w_gate.shape
    pieces = d // LANES
    hbm = pl.BlockSpec(memory_space=pl.ANY)
    w_map = lambda e, cnt, first, tot: (e, 0, 0)
    grid_spec = pltpu.PrefetchScalarGridSpec(
        num_scalar_prefetch=3,
        grid=(n_exp,),
        in_specs=[hbm, pl.BlockSpec((1, d, d_e), w_map), pl.BlockSpec((1, d, d_e), w_map),
                  pl.BlockSpec((1, d_e, d), w_map)],
        out_specs=hbm,
        scratch_shapes=[pltpu.VMEM((EXPERT_X_SLOTS, rows * pieces, LANES), F32),
                        pltpu.VMEM((EXPERT_Y_SLOTS, rows * pieces, LANES), F32),
                        pltpu.VMEM((d, 2 * d_e), BF16), pltpu.VMEM((d_e, d), BF16),
                        pltpu.SemaphoreType.DMA((EXPERT_X_SLOTS,)), pltpu.SemaphoreType.DMA((EXPERT_Y_SLOTS,))],
    )
    return pl.pallas_call(
        functools.partial(_expert_kernel, rows=rows, pieces=pieces),
        out_shape=jax.ShapeDtypeStruct(xs.shape, F32),
        grid_spec=grid_spec,
        compiler_params=_params(("arbitrary",)),
        name="experts",
    )(cnt, blk_first, n_used, xs, w_gate, w_up, w_down)


def _combine_kernel(dest_ref, next_ref, ys_ref, wt_ref, hp_ref, x1_ref, wsgu_ref, wsd_ref, npost_ref, gt_ref,
                    o_ref, buf, acc, sem, *, tm, pieces):
    i = pl.program_id(0)
    last = pl.num_programs(0) - 1
    cur = i % 2

    def start_token(rows_ref, half, r):
        slot = pl.ds(pl.multiple_of(r * pieces, pieces), pieces)
        for k in range(TOP_K):
            row = pl.multiple_of(rows_ref[r * TOP_K + k], pieces)
            pltpu.make_async_copy(ys_ref.at[pl.ds(row, pieces), :], buf.at[half, k, slot, :],
                                  sem.at[half]).start(priority=k % 2)

    def wait_half(half):
        for k in range(TOP_K):
            pltpu.make_async_copy(ys_ref.at[pl.ds(0, tm * pieces), :], buf.at[half, k], sem.at[half]).wait()

    @pl.when(i == 0)
    def _():
        def first(r, c):
            start_token(dest_ref, 0, r)
            return c

        lax.fori_loop(0, tm, first, 0)

    wait_half(cur)

    def token(r, c):
        start_token(next_ref, 1 - cur, r)
        slot = pl.ds(pl.multiple_of(r * pieces, pieces), pieces)
        terms = [wt_ref[r * TOP_K + k] * buf[cur, k, slot, :] for k in range(TOP_K)]
        while len(terms) > 1:
            terms = [terms[j] + terms[j + 1] for j in range(0, len(terms), 2)]
        acc[slot, :] = terms[0]
        return c

    lax.fori_loop(0, tm, token, 0, unroll=2)

    @pl.when(i == last)
    def _():
        wait_half(1 - cur)

    hb = _load_row_tiles(hp_ref, 0, tm, pieces).astype(BF16)
    d_sh = wsd_ref.shape[0]
    gu = _dot(hb, wsgu_ref[...])
    gate, up = gu[:, :d_sh], gu[:, d_sh:]
    shared = _dot((gate * jax.nn.sigmoid(gate) * up).astype(BF16), wsd_ref[...])
    fx = shared + _load_row_tiles(acc, 0, tm, pieces)
    o_ref[...] = x1_ref[...] + gt_ref[0] * _rms(fx, npost_ref[...])


def _combine(dest, ys, wt, hp, x1, wsgu, wsd, npost, gt, tokens_per_batch, tm):
    n, d = x1.shape
    pieces = d // LANES
    w = LANES
    tiles_per_batch = tokens_per_batch // tm
    row = lambda i: (i, 0)
    fixed = lambda i: (0, 0)
    return pl.pallas_call(
        functools.partial(_combine_kernel, tm=tm, pieces=pieces),
        out_shape=jax.ShapeDtypeStruct((n, d), F32),
        grid=(n // tm,),
        in_specs=[pl.BlockSpec((tm * TOP_K,), lambda i: (i,), memory_space=pltpu.SMEM),
                  pl.BlockSpec((tm * TOP_K,), lambda i: (jnp.minimum(i + 1, n // tm - 1),), memory_space=pltpu.SMEM),
                  pl.BlockSpec(memory_space=pl.ANY),
                  pl.BlockSpec((tm * TOP_K,), lambda i: (i,), memory_space=pltpu.SMEM),
                  pl.BlockSpec((tm * pieces, w), row), pl.BlockSpec((tm, d), row),
                  pl.BlockSpec(wsgu.shape, fixed), pl.BlockSpec(wsd.shape, fixed),
                  pl.BlockSpec((1, d), fixed), pl.BlockSpec((1, 1, d), lambda i: (i // tiles_per_batch, 0, 0))],
        out_specs=pl.BlockSpec((tm, d), row),
        scratch_shapes=[pltpu.VMEM((2, TOP_K, tm * pieces, w), F32), pltpu.VMEM((tm * pieces, w), F32),
                        pltpu.SemaphoreType.DMA((2,))],
        compiler_params=_params(("arbitrary",)),
        name="combine",
    )(dest, dest, ys, wt, hp, x1, wsgu, wsd, npost.reshape(1, d), gt)


def _layer(x, ctx, mod_x, mod_c, norm_pre_mix, norm_post_mix, norm_pre_ffn, norm_post_ffn,
           w_in, s5_a_re, s5_a_im, s5_log_dt, s5_b_re, s5_b_im, s5_c_re, s5_c_im, s5_d, w_glu, b_glu,
           sgu_norm, sgu_w, sgu_b, w_branch_s5, w_branch_sgu, w_out,
           w_router, router_bias, w_exp_gate, w_exp_up, w_exp_down, w_sh_gate, w_sh_up, w_sh_down):
    bsz, t_len, d = x.shape
    c_len = ctx.shape[1]
    n = bsz * t_len
    d_s5, d_sgu = w_glu.shape[0], sgu_norm.shape[0]
    g_n = d_s5 // S5_GROUP_CH
    n_exp = w_router.shape[1]
    chunk = S5_CHUNK
    tm = min(512, t_len)
    assert t_len % tm == 0 and tm % SGU_CHUNK == 0 and t_len % chunk == 0 and c_len % chunk == 0

    sh_m, sc_m, gt_m, sh_f, sc_f, gt_f = [v.reshape(bsz, 1, d) for v in jnp.split(mod_x, 6, axis=-1)]
    csh_m, csc_m = mod_c[:d], mod_c[d:2 * d]

    w_in_bf = w_in.astype(BF16)

    nc_lat, nc_ctx = t_len // chunk, c_len // chunk
    n_steps = max(1, (max(nc_lat, nc_ctx) - 1).bit_length())
    m_t, ke_t, kc_t, alp = _s5_tables(s5_a_re, s5_a_im, s5_log_dt, s5_b_re, s5_b_im, s5_c_re, s5_c_im,
                                      chunk, n_steps)

    ctx2 = ctx.reshape(bsz * c_len, d)
    u_ctx = _inproj_u(ctx2, norm_pre_mix, csh_m, csc_m, w_in_bf[:, :d_s5], min(512, bsz * c_len))
    zero_init = jnp.zeros((g_n, bsz, 4 * S5_STATE), F32)
    (ctx_final,) = _s5(_to_groups(u_ctx, bsz, nc_ctx, chunk, g_n), ke_t, alp, zero_init, nb=bsz, nc=nc_ctx)

    x2 = x.reshape(n, d)
    u, gu, vn, g1, g2 = _inproj(x2, norm_pre_mix, sh_m, sc_m, w_in_bf, sgu_norm, t_len, tm, d_s5, d_sgu)
    _, yg = _s5(_to_groups(u, bsz, nc_lat, chunk, g_n), ke_t, alp, ctx_final, m_t, kc_t, nb=bsz, nc=nc_lat)
    y = _from_groups(yg, bsz, nc_lat, chunk, g_n)

    ch = d_sgu // SGU_GROUPS
    per_tile = LANES // ch
    sguw = sgu_w.reshape(SGU_GROUPS // per_tile, per_tile, SGU_CHUNK, SGU_CHUNK)
    sguw = sguw.transpose(0, 2, 1, 3).reshape(SGU_GROUPS // per_tile, SGU_CHUNK, per_tile * SGU_CHUNK).astype(BF16)
    sgub = jnp.repeat(sgu_b.T, ch, axis=1)

    x1, hp, logits = _mix(y, u, gu, vn, g1, g2, x2, s5_d, w_glu.astype(BF16), b_glu, sguw, sgub,
                          w_branch_s5.astype(BF16), w_branch_sgu.astype(BF16), w_out.astype(BF16),
                          norm_post_mix, gt_m, norm_pre_ffn, sh_f, sc_f, w_router.astype(BF16), t_len, tm)

    tn = min(512, n)
    idx, wts, counts = _route(logits.T, router_bias, tn)
    rows = EXPERT_ROWS
    cnt = counts.reshape(n_exp).astype(I32)
    nblk = (cnt + rows - 1) // rows
    blk_end = jnp.cumsum(nblk)
    blk_start = blk_end - nblk
    cap = ((n * TOP_K) // rows + n_exp) * rows
    dest = _dest(idx, (blk_start * rows).astype(F32).reshape(n_exp, 1), tn)
    pieces = d // LANES
    dest = dest.T.reshape(n * TOP_K) * pieces

    xs = _dispatch(dest, hp, cap, min(512, t_len), pieces)
    ys = _experts(cnt, blk_start.astype(I32), blk_end[-1:].astype(I32), xs, w_exp_gate, w_exp_up, w_exp_down, rows)
    wsgu = jnp.concatenate([w_sh_gate, w_sh_up], axis=1).astype(BF16)
    out = _combine(dest, ys, wts.T.reshape(n * TOP_K), hp, x1, wsgu, w_sh_down.astype(BF16), norm_post_ffn,
                   gt_f, t_len, min(256, t_len))
    return out.reshape(bsz, t_len, d)


def kernel(x, c, ctx, c_ctx, w_mod, b_mod, norm_pre_mix, norm_post_mix, norm_pre_ffn, norm_post_ffn, w_in, s5_a_re, s5_a_im, s5_log_dt, s5_b_re, s5_b_im, s5_c_re, s5_c_im, s5_d, w_glu, b_glu, sgu_norm, sgu_w, sgu_b, w_branch_s5, w_branch_sgu, w_out, w_router, router_bias, w_exp_gate, w_exp_up, w_exp_down, w_sh_gate, w_sh_up, w_sh_down):
    depth = w_mod.shape[0]
    assert depth == 1, "the context stream is only carried through the last layer's S5 states"
    bsz = x.shape[0]
    pad = (-(bsz + 1)) % 8
    cpad = jnp.concatenate([c, c_ctx[None, :], jnp.zeros((pad, c.shape[1]), c.dtype)], axis=0)
    mod = _modulation(cpad, w_mod[0], b_mod[0])
    return _layer(x, ctx, mod[:bsz], mod[bsz], norm_pre_mix[0], norm_post_mix[0], norm_pre_ffn[0],
                  norm_post_ffn[0], w_in[0], s5_a_re[0], s5_a_im[0], s5_log_dt[0], s5_b_re[0], s5_b_im[0],
                  s5_c_re[0], s5_c_im[0], s5_d[0], w_glu[0], b_glu[0], sgu_norm[0], sgu_w[0], sgu_b[0],
                  w_branch_s5[0], w_branch_sgu[0], w_out[0], w_router[0], router_bias[0],
                  w_exp_gate[0], w_exp_up[0], w_exp_down[0], w_sh_gate[0], w_sh_up[0], w_sh_down[0])
```

```python
import functools
import math

import jax
import jax.numpy as jnp
from jax import lax
from jax.experimental import pallas as pl
from jax.experimental.pallas import tpu as pltpu

F32 = jnp.float32
BF16 = jnp.bfloat16
I32 = jnp.int32

EPS = 1e-6
S5_GROUP_CH = 16
S5_STATE = 64
S5_CHUNK = 64
SGU_GROUPS = 8
SGU_CHUNK = 128
N_EXPERT_GROUPS = 8
TOPK_GROUPS = 4
TOP_K = 8
ROUTE_SCALE = 2.5
LANES = 128
EXPERT_ROWS = 256
EXPERT_X_SLOTS = 4
EXPERT_Y_SLOTS = 3
VMEM_LIMIT = 56 * 1024 * 1024


def _params(sem):
    return pltpu.CompilerParams(dimension_semantics=sem, vmem_limit_bytes=VMEM_LIMIT)


def _rms(v, g):
    return v * lax.rsqrt(jnp.mean(v * v, axis=-1, keepdims=True) + EPS) * g


def _dot(a, b):
    return jnp.dot(a, b, preferred_element_type=F32)


def _store_row_tiles(ref, first_row, val):
    pieces = val.shape[1] // LANES
    for s in range(pieces):
        ref[pl.ds(first_row * pieces + s, val.shape[0], stride=pieces), :] = val[:, s * LANES:(s + 1) * LANES]


def _load_row_tiles(ref, first_row, n_rows, pieces, lead=()):
    return jnp.concatenate([ref[lead + (pl.ds(first_row * pieces + s, n_rows, stride=pieces), slice(None))]
                            for s in range(pieces)], axis=1)


def _mod_kernel(c_ref, w_ref, b_ref, o_ref):
    cv = c_ref[...]
    s = cv * jax.nn.sigmoid(cv)
    o_ref[...] = _dot(s.astype(BF16), w_ref[...].astype(BF16)) + b_ref[...]


def _modulation(cpad, w_mod, b_mod):
    d, n6 = w_mod.shape
    tn = 1024
    return pl.pallas_call(
        _mod_kernel,
        out_shape=jax.ShapeDtypeStruct((cpad.shape[0], n6), F32),
        grid=(n6 // tn,),
        in_specs=[pl.BlockSpec(cpad.shape, lambda j: (0, 0)),
                  pl.BlockSpec((d, tn), lambda j: (0, j)),
                  pl.BlockSpec((1, tn), lambda j: (0, j))],
        out_specs=pl.BlockSpec((cpad.shape[0], tn), lambda j: (0, j)),
        compiler_params=_params(("parallel",)),
        name="mod",
    )(cpad, w_mod, b_mod.reshape(1, n6))


def _inproj_kernel(x_ref, g_ref, sh_ref, sc_ref, w_ref, sgun_ref,
                   u_ref, gu_ref, vn_ref, g1_ref, g2_ref, *, d_s5, d_sgu, d_model):
    h = _rms(x_ref[...], g_ref[...])
    hb = (h * (1.0 + sc_ref[0]) + sh_ref[0]).astype(BF16)
    o1, o2, o3, o4 = d_s5, d_s5 + d_sgu, d_s5 + 2 * d_sgu, d_s5 + 2 * d_sgu + d_model
    u_ref[...] = _dot(hb, w_ref[:, 0:o1]).astype(BF16)
    gu_ref[...] = jax.nn.gelu(_dot(hb, w_ref[:, o1:o2])).astype(BF16)
    v = jax.nn.gelu(_dot(hb, w_ref[:, o2:o3]))
    vn_ref[...] = _rms(v, sgun_ref[...]).astype(BF16)
    g1_ref[...] = jax.nn.sigmoid(_dot(hb, w_ref[:, o3:o4])).astype(BF16)
    g2_ref[...] = jax.nn.sigmoid(_dot(hb, w_ref[:, o4:o4 + d_model])).astype(BF16)


def _inproj(x2, norm_g, shift, scale, w_in_bf, sgu_norm, tokens_per_batch, tm, d_s5, d_sgu):
    n, d = x2.shape
    tiles_per_batch = tokens_per_batch // tm
    row = lambda i: (i, 0)
    fixed = lambda i: (0, 0)
    per_batch = lambda i: (i // tiles_per_batch, 0, 0)
    return pl.pallas_call(
        functools.partial(_inproj_kernel, d_s5=d_s5, d_sgu=d_sgu, d_model=d),
        out_shape=(jax.ShapeDtypeStruct((n, d_s5), BF16), jax.ShapeDtypeStruct((n, d_sgu), BF16),
                   jax.ShapeDtypeStruct((n, d_sgu), BF16), jax.ShapeDtypeStruct((n, d), BF16),
                   jax.ShapeDtypeStruct((n, d), BF16)),
        grid=(n // tm,),
        in_specs=[pl.BlockSpec((tm, d), row), pl.BlockSpec((1, d), fixed),
                  pl.BlockSpec((1, 1, d), per_batch), pl.BlockSpec((1, 1, d), per_batch),
                  pl.BlockSpec(w_in_bf.shape, fixed), pl.BlockSpec((1, d_sgu), fixed)],
        out_specs=(pl.BlockSpec((tm, d_s5), row), pl.BlockSpec((tm, d_sgu), row),
                   pl.BlockSpec((tm, d_sgu), row), pl.BlockSpec((tm, d), row), pl.BlockSpec((tm, d), row)),
        compiler_params=_params(("parallel",)),
        name="inproj",
    )(x2, norm_g.reshape(1, d), shift, scale, w_in_bf, sgu_norm.reshape(1, d_sgu))


def _inproj_u_kernel(x_ref, g_ref, sh_ref, sc_ref, w_ref, u_ref):
    h = _rms(x_ref[...], g_ref[...])
    hb = (h * (1.0 + sc_ref[...]) + sh_ref[...]).astype(BF16)
    u_ref[...] = _dot(hb, w_ref[...]).astype(BF16)


def _inproj_u(x2, norm_g, shift, scale, w_u_bf, tm):
    n, d = x2.shape
    d_s5 = w_u_bf.shape[1]
    row = lambda i: (i, 0)
    fixed = lambda i: (0, 0)
    return pl.pallas_call(
        _inproj_u_kernel,
        out_shape=jax.ShapeDtypeStruct((n, d_s5), BF16),
        grid=(n // tm,),
        in_specs=[pl.BlockSpec((tm, d), row), pl.BlockSpec((1, d), fixed), pl.BlockSpec((1, d), fixed),
                  pl.BlockSpec((1, d), fixed), pl.BlockSpec(w_u_bf.shape, fixed)],
        out_specs=pl.BlockSpec((tm, d_s5), row),
        compiler_params=_params(("parallel",)),
        name="inproj_ctx",
    )(x2, norm_g.reshape(1, d), shift.reshape(1, d), scale.reshape(1, d), w_u_bf)


def _s5_tables(a_re, a_im, log_dt, b_re, b_im, c_re, c_im, chunk, n_steps):
    hi = lax.Precision.HIGHEST
    g_n, p_n, j_n = a_re.shape[1], a_re.shape[2], b_re.shape[3]
    dt = jnp.exp(log_dt)[..., None]
    lam_re, lam_im = dt * a_re, dt * a_im
    ab_re, ab_im = jnp.exp(lam_re) * jnp.cos(lam_im), jnp.exp(lam_re) * jnp.sin(lam_im)
    den = a_re * a_re + a_im * a_im
    q_re = ((ab_re - 1.0) * a_re + ab_im * a_im) / den
    q_im = (ab_im * a_re - (ab_re - 1.0) * a_im) / den
    bb_re = q_re[..., None] * b_re - q_im[..., None] * b_im
    bb_im = q_re[..., None] * b_im + q_im[..., None] * b_re
    k = jnp.arange(chunk + 1, dtype=F32)[:, None, None, None]
    mag = jnp.exp(k * lam_re[None])
    pw_re, pw_im = mag * jnp.cos(k * lam_im[None]), mag * jnp.sin(k * lam_im[None])

    def lag_kernels(d):
        x_re = pw_re[:chunk, d, :, :, None] * bb_re[d][None] - pw_im[:chunk, d, :, :, None] * bb_im[d][None]
        x_im = pw_re[:chunk, d, :, :, None] * bb_im[d][None] + pw_im[:chunk, d, :, :, None] * bb_re[d][None]
        return (jnp.einsum('gjp,kgpi->gkij', c_re[d], x_re, precision=hi)
                - jnp.einsum('gjp,kgpi->gkij', c_im[d], x_im, precision=hi))

    kf, kb = lag_kernels(0), lag_kernels(1)
    lags = jnp.concatenate([kb[:, 1:][:, ::-1], kf[:, 0:1] + kb[:, 0:1], kf[:, 1:],
                            jnp.zeros_like(kf[:, 0:1])], axis=1)
    lags = lags.transpose(0, 2, 3, 1)

    pf_re, pf_im = pw_re[:chunk, 0][::-1], pw_im[:chunk, 0][::-1]
    pb_re, pb_im = pw_re[:chunk, 1], pw_im[:chunk, 1]

    def state_in(p_re, p_im, d):
        e_re = p_re[..., None] * bb_re[d][None] - p_im[..., None] * bb_im[d][None]
        e_im = p_re[..., None] * bb_im[d][None] + p_im[..., None] * bb_re[d][None]
        return e_re.transpose(1, 3, 0, 2), e_im.transpose(1, 3, 0, 2)

    ef_re, ef_im = state_in(pf_re, pf_im, 0)
    eb_re, eb_im = state_in(pb_re, pb_im, 1)
    ke = jnp.concatenate([ef_re, eb_re, ef_im, eb_im], axis=-1).reshape(g_n, chunk * j_n, 4 * p_n)

    qf_re, qf_im = pw_re[1:chunk + 1, 0], pw_im[1:chunk + 1, 0]
    qb_re, qb_im = pw_re[1:chunk + 1, 1][::-1], pw_im[1:chunk + 1, 1][::-1]

    def state_out(p_re, p_im, d):
        cr, ci = c_re[d].transpose(0, 2, 1), c_im[d].transpose(0, 2, 1)
        pr, pi = p_re.transpose(1, 2, 0), p_im.transpose(1, 2, 0)
        o_re = pr[:, :, None, :] * cr[..., None] - pi[:, :, None, :] * ci[..., None]
        o_im = pr[:, :, None, :] * ci[..., None] + pi[:, :, None, :] * cr[..., None]
        return o_re, o_im

    of_re, of_im = state_out(qf_re, qf_im, 0)
    ob_re, ob_im = state_out(qb_re, qb_im, 1)
    kc = jnp.concatenate([of_re, ob_re, -of_im, -ob_im], axis=1).reshape(g_n, 4 * p_n, chunk * j_n)

    e = (chunk * (2.0 ** jnp.arange(n_steps, dtype=F32)))[:, None, None, None]
    mg = jnp.exp(e * lam_re[None])
    al_re, al_im = mg * jnp.cos(e * lam_im[None]), mg * jnp.sin(e * lam_im[None])
    alp = jnp.concatenate([al_re[:, 0], al_re[:, 1], al_im[:, 0], al_im[:, 1]], axis=-1)
    return lags, ke.astype(BF16), kc.astype(BF16), alp.transpose(1, 0, 2)


def _build_toeplitz(lag_ref, m_s, chunk):
    j_n = lag_ref.shape[1]
    per_tile = LANES // chunk
    low = lax.broadcasted_iota(I32, (chunk, LANES), 1) < chunk
    for i in range(j_n):
        for q in range(j_n // per_tile):
            tile = None
            for p in range(per_tile):
                j = q * per_tile + p
                row = jnp.broadcast_to(lag_ref[0, i, j:j + 1, :], (chunk, LANES))
                rot = pltpu.roll(row, (p * chunk + LANES - (chunk - 1)) % LANES, 1, stride=1, stride_axis=0)
                tile = rot if tile is None else jnp.where(low, tile, rot)
            m_s[i * chunk:(i + 1) * chunk, q * LANES:(q + 1) * LANES] = tile.astype(BF16)


def _s5_kernel(*refs, nb, nc, readout):
    if readout:
        u_ref, ke_ref, alp_ref, init_ref, lag_ref, kc_ref, fin_ref, y_ref, m_s = refs
    else:
        u_ref, ke_ref, alp_ref, init_ref, fin_ref = refs
    rows = nb * nc
    half = 2 * S5_STATE
    u = u_ref[0]
    e = _dot(u, ke_ref[0])
    er, ei = e[:, :half], e[:, half:]
    r_idx = lax.broadcasted_iota(I32, (rows, half), 0)
    if nc & (nc - 1) == 0:
        n_idx, b_idx = r_idx & (nc - 1), r_idx >> (nc.bit_length() - 1)
    else:
        n_idx, b_idx = lax.rem(r_idx, nc), lax.div(r_idx, nc)
    is_f = lax.broadcasted_iota(I32, (rows, half), 1) < S5_STATE
    seen = jnp.where(is_f, n_idx, nc - 1 - n_idx)
    init = init_ref[0]
    init_r = jnp.zeros((rows, half), F32)
    init_i = jnp.zeros((rows, half), F32)
    for b in range(nb):
        init_r = jnp.where(b_idx == b, init[b:b + 1, :half], init_r)
        init_i = jnp.where(b_idx == b, init[b:b + 1, half:], init_i)
    alp = alp_ref[0]
    ar, ai = alp[0:1, :half], alp[0:1, half:]
    er = er + jnp.where(seen == 0, ar * init_r - ai * init_i, 0.0)
    ei = ei + jnp.where(seen == 0, ar * init_i + ai * init_r, 0.0)

    def from_prev(v, dist):
        return jnp.where(is_f, pltpu.roll(v, dist, 0), pltpu.roll(v, rows - dist, 0))

    d, s = 1, 0
    while d < nc:
        ar, ai = alp[s:s + 1, :half], alp[s:s + 1, half:]
        sr = jnp.where(seen >= d, from_prev(er, d), 0.0)
        si = jnp.where(seen >= d, from_prev(ei, d), 0.0)
        er, ei = er + (ar * sr - ai * si), ei + (ar * si + ai * sr)
        d, s = d * 2, s + 1
    is_f_row = lax.broadcasted_iota(I32, (1, half), 1) < S5_STATE
    for b in range(nb):
        lo, hi = b * nc, b * nc + nc - 1
        fin_ref[0, b:b + 1, :half] = jnp.where(is_f_row, er[hi:hi + 1], er[lo:lo + 1])
        fin_ref[0, b:b + 1, half:] = jnp.where(is_f_row, ei[hi:hi + 1], ei[lo:lo + 1])
    if readout:
        if nc > 1:
            sin_r = jnp.where(seen >= 1, from_prev(er, 1), init_r)
            sin_i = jnp.where(seen >= 1, from_prev(ei, 1), init_i)
        else:
            sin_r, sin_i = init_r, init_i
        s_in = jnp.concatenate([sin_r, sin_i], axis=1).astype(BF16)
        _build_toeplitz(lag_ref, m_s, S5_CHUNK)
        y_ref[0] = (_dot(u, m_s[...]) + _dot(s_in, kc_ref[0])).astype(BF16)


def _s5(ug, ke, alp, init, m=None, kc=None, *, nb, nc):
    g_n, rows, width = ug.shape
    readout = m is not None
    grp = lambda g: (g, 0, 0)
    in_specs = [pl.BlockSpec((1, rows, width), grp), pl.BlockSpec((1,) + ke.shape[1:], grp),
                pl.BlockSpec((1,) + alp.shape[1:], grp), pl.BlockSpec((1,) + init.shape[1:], grp)]
    out_shape = [jax.ShapeDtypeStruct(init.shape, F32)]
    out_specs = [pl.BlockSpec((1,) + init.shape[1:], grp)]
    args = [ug, ke, alp, init]
    scratch = []
    if readout:
        assert m.shape[3] == LANES and 2 * S5_CHUNK == LANES
        in_specs += [pl.BlockSpec((1,) + m.shape[1:], lambda g: (g, 0, 0, 0)),
                     pl.BlockSpec((1,) + kc.shape[1:], grp)]
        out_shape.append(jax.ShapeDtypeStruct(ug.shape, BF16))
        out_specs.append(pl.BlockSpec((1, rows, width), grp))
        args += [m, kc]
        scratch = [pltpu.VMEM((width, width), BF16)]
    return pl.pallas_call(
        functools.partial(_s5_kernel, nb=nb, nc=nc, readout=readout),
        out_shape=tuple(out_shape), grid=(g_n,), in_specs=in_specs, out_specs=tuple(out_specs),
        scratch_shapes=scratch,
        compiler_params=_params(("parallel",)),
        name="s5_readout" if readout else "s5_state",
    )(*args)


def _to_groups(u, nb, nc, chunk, g_n):
    j_n = u.shape[1] // g_n
    return u.reshape(nb * nc, chunk, g_n, j_n).transpose(2, 0, 3, 1).reshape(g_n, nb * nc, j_n * chunk)


def _from_groups(y, nb, nc, chunk, g_n):
    j_n = y.shape[2] // chunk
    return y.reshape(g_n, nb * nc, j_n, chunk).transpose(1, 3, 0, 2).reshape(nb * nc * chunk, g_n * j_n)


def _mix_kernel(y_ref, u_ref, gu_ref, vn_ref, g1_ref, g2_ref, x_ref,
                d_ref, wglu_ref, bglu_ref, sguw_ref, sgub_ref, wb1_ref, wb2_ref, wout_ref,
                npost_ref, gt_ref, npre_ref, shf_ref, scf_ref, wr_ref,
                x1_ref, hp_ref, lg_ref, *, tm):
    y = jax.nn.gelu(y_ref[...].astype(F32) + d_ref[...] * u_ref[...].astype(F32))
    y_s5 = y * jax.nn.sigmoid(_dot(y.astype(BF16), wglu_ref[...]) + bglu_ref[...])

    lane = lax.broadcasted_iota(I32, (1, LANES), 1)
    m_lo = (lane < LANES // 2).astype(F32).astype(BF16)
    m_hi = (lane >= LANES // 2).astype(F32).astype(BF16)
    vn = vn_ref[...]
    chunks = []
    for c in range(tm // SGU_CHUNK):
        vc = vn[c * SGU_CHUNK:(c + 1) * SGU_CHUNK, :]
        tiles = []
        for q in range(vc.shape[1] // LANES):
            vt = vc[:, q * LANES:(q + 1) * LANES]
            rhs = jnp.concatenate([vt * m_lo, vt * m_hi], axis=0)
            tiles.append(_dot(sguw_ref[q], rhs))
        chunks.append(jnp.concatenate(tiles, axis=1) + sgub_ref[...])
    mixed = jnp.concatenate(chunks, axis=0)
    y_sgu = gu_ref[...].astype(F32) * mixed

    merged = (g1_ref[...].astype(F32) * _dot(y_s5.astype(BF16), wb1_ref[...])
              + g2_ref[...].astype(F32) * _dot(y_sgu.astype(BF16), wb2_ref[...]))
    mx = _dot(merged.astype(BF16), wout_ref[...])
    x1 = x_ref[...] + gt_ref[0] * _rms(mx, npost_ref[...])
    x1_ref[...] = x1
    hp = _rms(x1, npre_ref[...]) * (1.0 + scf_ref[0]) + shf_ref[0]
    _store_row_tiles(hp_ref, 0, hp)
    lg_ref[...] = _dot(hp.astype(BF16), wr_ref[...])


def _mix(y, u, gu, vn, g1, g2, x2, s5_d, wglu, bglu, sguw, sgub, wb1, wb2, wout,
         npost, gt, npre, shf, scf, wr, tokens_per_batch, tm):
    n, d = x2.shape
    d_s5, d_sgu, n_exp = y.shape[1], gu.shape[1], wr.shape[1]
    tiles_per_batch = tokens_per_batch // tm
    row = lambda i: (i, 0)
    fixed = lambda i: (0, 0)
    fixed3 = lambda i: (0, 0, 0)
    per_batch = lambda i: (i // tiles_per_batch, 0, 0)
    vec = lambda w: pl.BlockSpec((1, w), fixed)
    return pl.pallas_call(
        functools.partial(_mix_kernel, tm=tm),
        out_shape=(jax.ShapeDtypeStruct((n, d), F32), jax.ShapeDtypeStruct((n * d // LANES, LANES), F32),
                   jax.ShapeDtypeStruct((n, n_exp), F32)),
        grid=(n // tm,),
        in_specs=[pl.BlockSpec((tm, d_s5), row), pl.BlockSpec((tm, d_s5), row), pl.BlockSpec((tm, d_sgu), row),
                  pl.BlockSpec((tm, d_sgu), row), pl.BlockSpec((tm, d), row), pl.BlockSpec((tm, d), row),
                  pl.BlockSpec((tm, d), row),
                  vec(d_s5), pl.BlockSpec(wglu.shape, fixed), vec(d_s5),
                  pl.BlockSpec(sguw.shape, fixed3), pl.BlockSpec(sgub.shape, fixed),
                  pl.BlockSpec(wb1.shape, fixed), pl.BlockSpec(wb2.shape, fixed), pl.BlockSpec(wout.shape, fixed),
                  vec(d), pl.BlockSpec((1, 1, d), per_batch), vec(d),
                  pl.BlockSpec((1, 1, d), per_batch), pl.BlockSpec((1, 1, d), per_batch),
                  pl.BlockSpec(wr.shape, fixed)],
        out_specs=(pl.BlockSpec((tm, d), row), pl.BlockSpec((tm * d // LANES, LANES), row),
                   pl.BlockSpec((tm, n_exp), row)),
        compiler_params=_params(("parallel",)),
        name="mix",
    )(y, u, gu, vn, g1, g2, x2, s5_d.reshape(1, d_s5), wglu, bglu.reshape(1, d_s5), sguw, sgub,
      wb1, wb2, wout, npost.reshape(1, d), gt, npre.reshape(1, d), shf, scf, wr)


def _route_kernel(lg_ref, bias_ref, idx_ref, w_ref, cnt_ref, *, n_exp, tn):
    per_group = n_exp // N_EXPERT_GROUPS
    neg = jnp.float32(-jnp.inf)

    scores = jax.nn.sigmoid(lg_ref[...])
    sel = scores + bias_ref[...]
    gs = []
    for g in range(N_EXPERT_GROUPS):
        sg = sel[g * per_group:(g + 1) * per_group, :]
        m1 = jnp.max(sg, axis=0, keepdims=True)
        is_m1 = sg == m1
        n_m1 = jnp.sum(jnp.where(is_m1, 1.0, 0.0), axis=0, keepdims=True)
        rest = jnp.max(jnp.where(is_m1, neg, sg), axis=0, keepdims=True)
        gs.append(m1 + jnp.where(n_m1 >= 2.0, m1, rest))
    gsm = jnp.concatenate(gs, axis=0)
    g_iota = lax.broadcasted_iota(I32, gsm.shape, 0)
    e_iota = lax.broadcasted_iota(I32, sel.shape, 0).astype(F32)
    masked = []
    for g in range(N_EXPERT_GROUPS):
        mine = gsm[g:g + 1, :]
        beats = jnp.where(gsm > mine, 1.0, jnp.where(gsm == mine, jnp.where(g_iota < g, 1.0, 0.0), 0.0))
        n_beats = jnp.sum(beats, axis=0, keepdims=True)
        masked.append(jnp.where(n_beats < float(TOPK_GROUPS), sel[g * per_group:(g + 1) * per_group, :], neg))
    start = jnp.concatenate(masked, axis=0)
    selm = start
    picked, vals = [], []
    for _ in range(TOP_K):
        m = jnp.max(selm, axis=0, keepdims=True)
        first = jnp.min(jnp.where(selm == m, e_iota, float(n_exp)), axis=0, keepdims=True)
        one = e_iota == first
        picked.append(first)
        vals.append(jnp.sum(jnp.where(one, scores, 0.0), axis=0, keepdims=True))
        selm = jnp.where(one, neg, selm)
    idx_ref[...] = jnp.concatenate(picked, axis=0).astype(I32)
    wv = jnp.concatenate(vals, axis=0)
    w_ref[...] = wv / jnp.sum(wv, axis=0, keepdims=True) * ROUTE_SCALE
    hot = jnp.where(selm == neg, jnp.where(start == neg, 0.0, 1.0), 0.0)

    @pl.when(pl.program_id(0) == 0)
    def _():
        cnt_ref[...] = jnp.zeros_like(cnt_ref)

    cnt_ref[...] += jnp.sum(hot, axis=1, keepdims=True)


def _route(logits_t, bias, tn):
    n_exp, n = logits_t.shape
    col = lambda i: (0, i)
    fixed = lambda i: (0, 0)
    return pl.pallas_call(
        functools.partial(_route_kernel, n_exp=n_exp, tn=tn),
        out_shape=(jax.ShapeDtypeStruct((TOP_K, n), I32), jax.ShapeDtypeStruct((TOP_K, n), F32),
                   jax.ShapeDtypeStruct((n_exp, 1), F32)),
        grid=(n // tn,),
        in_specs=[pl.BlockSpec((n_exp, tn), col), pl.BlockSpec((n_exp, 1), fixed)],
        out_specs=(pl.BlockSpec((TOP_K, tn), col), pl.BlockSpec((TOP_K, tn), col),
                   pl.BlockSpec((n_exp, 1), fixed)),
        compiler_params=_params(("arbitrary",)),
        name="route",
    )(logits_t, bias.reshape(n_exp, 1))


def _dest_kernel(idx_ref, start_ref, dest_ref, carry_ref, *, n_exp, tn):
    @pl.when(pl.program_id(0) == 0)
    def _():
        carry_ref[...] = start_ref[...]

    idx = idx_ref[...]
    e_iota = lax.broadcasted_iota(I32, (n_exp, tn), 0)
    hot = jnp.zeros((n_exp, tn), F32)
    for k in range(TOP_K):
        hot = hot + jnp.where(e_iota == idx[k:k + 1, :], 1.0, 0.0)
    before = jnp.where(lax.broadcasted_iota(I32, (tn, tn), 0) < lax.broadcasted_iota(I32, (tn, tn), 1), 1.0, 0.0)
    rank = _dot(hot.astype(BF16), before.astype(BF16)) + carry_ref[...]
    dest_ref[...] = jnp.concatenate(
        [jnp.sum(jnp.where(e_iota == idx[k:k + 1, :], rank, 0.0), axis=0, keepdims=True)
         for k in range(TOP_K)], axis=0).astype(I32)
    carry_ref[...] += jnp.sum(hot, axis=1, keepdims=True)


def _dest(idx, start, tn):
    n = idx.shape[1]
    n_exp = start.shape[0]
    col = lambda i: (0, i)
    return pl.pallas_call(
        functools.partial(_dest_kernel, n_exp=n_exp, tn=tn),
        out_shape=jax.ShapeDtypeStruct((TOP_K, n), I32),
        grid=(n // tn,),
        in_specs=[pl.BlockSpec((TOP_K, tn), col), pl.BlockSpec((n_exp, 1), lambda i: (0, 0))],
        out_specs=pl.BlockSpec((TOP_K, tn), col),
        scratch_shapes=[pltpu.VMEM((n_exp, 1), F32)],
        compiler_params=_params(("arbitrary",)),
        name="dest",
    )(idx, start)


def _dispatch_kernel(dest_ref, hp_ref, xs_ref, sem, *, tm, pieces):
    def start(r, c):
        src = hp_ref.at[pl.ds(pl.multiple_of(r * pieces, pieces), pieces), :]
        for k in range(TOP_K):
            row = pl.multiple_of(dest_ref[r * TOP_K + k], pieces)
            pltpu.make_async_copy(src, xs_ref.at[pl.ds(row, pieces), :], sem).start(priority=k % 2)
        return c

    lax.fori_loop(0, tm, start, 0)
    for k in range(TOP_K):
        pltpu.make_async_copy(hp_ref, xs_ref.at[pl.ds(0, tm * pieces), :], sem).wait()


def _dispatch(dest, hp, cap, tm, pieces):
    n = hp.shape[0] // pieces
    return pl.pallas_call(
        functools.partial(_dispatch_kernel, tm=tm, pieces=pieces),
        out_shape=jax.ShapeDtypeStruct((cap * pieces, LANES), hp.dtype),
        grid=(n // tm,),
        in_specs=[pl.BlockSpec((tm * TOP_K,), lambda i: (i,), memory_space=pltpu.SMEM),
                  pl.BlockSpec((tm * pieces, LANES), lambda i: (i, 0))],
        out_specs=pl.BlockSpec(memory_space=pl.ANY),
        scratch_shapes=[pltpu.SemaphoreType.DMA(())],
        compiler_params=_params(("arbitrary",)),
        name="dispatch",
    )(dest, hp)


def _expert_kernel(cnt_ref, first_ref, tot_ref, xs_hbm, wg_ref, wu_ref, wd_ref, ys_hbm,
                   xbuf, ybuf, wgu_s, wd_s, xsem, ysem, *, rows, pieces):
    e = pl.program_id(0)
    cnt, first, total = cnt_ref[e], first_ref[e], tot_ref[0]
    n_blk = (cnt + rows - 1) // rows
    d_e = wg_ref.shape[2]
    phys = rows * pieces

    def block_rows(g):
        return pl.ds(pl.multiple_of(g * phys, phys), phys)

    def x_copy(g, slot):
        return pltpu.make_async_copy(xs_hbm.at[block_rows(g), :], xbuf.at[slot], xsem.at[slot])

    def y_copy(g, slot):
        return pltpu.make_async_copy(ybuf.at[slot], ys_hbm.at[block_rows(g), :], ysem.at[slot])

    x_slots, y_slots = xbuf.shape[0], ybuf.shape[0]
    ahead = x_slots - 1

    @pl.when(n_blk > 0)
    def _():
        @pl.when(first == 0)
        def _():
            for a in range(ahead):
                @pl.when(a < total)
                def _():
                    x_copy(a, a).start()

        wgu_s[:, :d_e] = wg_ref[0].astype(BF16)
        wgu_s[:, d_e:] = wu_ref[0].astype(BF16)
        wd_s[...] = wd_ref[0].astype(BF16)

        def block(j, c):
            g = first + j
            slot = g % x_slots
            x_copy(g, slot).wait()

            @pl.when(g + ahead < total)
            def _():
                x_copy(g + ahead, (g + ahead) % x_slots).start()

            yslot = g % y_slots

            @pl.when(g >= y_slots)
            def _():
                y_copy(g - y_slots, yslot).wait()

            x = _load_row_tiles(xbuf, 0, rows, pieces, lead=(slot,))
            r_iota = lax.broadcasted_iota(I32, x.shape, 0) + j * rows
            xb = jnp.where(r_iota < cnt, x, 0.0).astype(BF16)
            gu = _dot(xb, wgu_s[...])
            gate, up = gu[:, :d_e], gu[:, d_e:]
            hid = (gate * jax.nn.sigmoid(gate) * up).astype(BF16)
            y = _dot(hid, wd_s[...])
            for s in range(pieces):
                ybuf[yslot, pl.ds(s, rows, stride=pieces), :] = y[:, s * LANES:(s + 1) * LANES]
            y_copy(g, yslot).start()
            return c

        lax.fori_loop(0, n_blk, block, 0)

        @pl.when(first + n_blk == total)
        def _():
            for back in range(1, y_slots + 1):
                @pl.when(total >= back)
                def _():
                    y_copy(total - back, (total - back) % y_slots).wait()


def _experts(cnt, blk_first, n_used, xs, w_gate, w_up, w_down, rows):
    n_exp, d, d_e = w_gate.shape
    pieces = d // LANES
    hbm = pl.BlockSpec(memory_space=pl.ANY)
    w_map = lambda e, cnt, first, tot: (e, 0, 0)
    grid_spec = pltpu.PrefetchScalarGridSpec(
        num_scalar_prefetch=3,
        grid=(n_exp,),
        in_specs=[hbm, pl.BlockSpec((1, d, d_e), w_map), pl.BlockSpec((1, d, d_e), w_map),
                  pl.BlockSpec((1, d_e, d), w_map)],
        out_specs=hbm,
        scratch_shapes=[pltpu.VMEM((EXPERT_X_SLOTS, rows * pieces, LANES), F32),
                        pltpu.VMEM((EXPERT_Y_SLOTS, rows * pieces, LANES), F32),
                        pltpu.VMEM((d, 2 * d_e), BF16), pltpu.VMEM((d_e, d), BF16),
                        pltpu.SemaphoreType.DMA((EXPERT_X_SLOTS,)), pltpu.SemaphoreType.DMA((EXPERT_Y_SLOTS,))],
    )
    return pl.pallas_call(
        functools.partial(_expert_kernel, rows=rows, pieces=pieces),
        out_shape=jax.ShapeDtypeStruct(xs.shape, F32),
        grid_spec=grid_spec,
        compiler_params=_params(("arbitrary",)),
        name="experts",
    )(cnt, blk_first, n_used, xs, w_gate, w_up, w_down)


def _combine_kernel(dest_ref, next_ref, ys_ref, wt_ref, hp_ref, x1_ref, wsgu_ref, wsd_ref, npost_ref, gt_ref,
                    o_ref, buf, acc, sem, *, tm, pieces):
    i = pl.program_id(0)
    last = pl.num_programs(0) - 1
    cur = i % 2

    def start_token(rows_ref, half, r):
        slot = pl.ds(pl.multiple_of(r * pieces, pieces), pieces)
        for k in range(TOP_K):
            row = pl.multiple_of(rows_ref[r * TOP_K + k], pieces)
            pltpu.make_async_copy(ys_ref.at[pl.ds(row, pieces), :], buf.at[half, k, slot, :],
                                  sem.at[half]).start(priority=k % 2)

    def wait_half(half):
        for k in range(TOP_K):
            pltpu.make_async_copy(ys_ref.at[pl.ds(0, tm * pieces), :], buf.at[half, k], sem.at[half]).wait()

    @pl.when(i == 0)
    def _():
        def first(r, c):
            start_token(dest_ref, 0, r)
            return c

        lax.fori_loop(0, tm, first, 0)

    wait_half(cur)

    def token(r, c):
        start_token(next_ref, 1 - cur, r)
        slot = pl.ds(pl.multiple_of(r * pieces, pieces), pieces)
        terms = [wt_ref[r * TOP_K + k] * buf[cur, k, slot, :] for k in range(TOP_K)]
        while len(terms) > 1:
            terms = [terms[j] + terms[j + 1] for j in range(0, len(terms), 2)]
        acc[slot, :] = terms[0]
        return c

    lax.fori_loop(0, tm, token, 0, unroll=2)

    @pl.when(i == last)
    def _():
        wait_half(1 - cur)

    hb = _load_row_tiles(hp_ref, 0, tm, pieces).astype(BF16)
    d_sh = wsd_ref.shape[0]
    gu = _dot(hb, wsgu_ref[...])
    gate, up = gu[:, :d_sh], gu[:, d_sh:]
    shared = _dot((gate * jax.nn.sigmoid(gate) * up).astype(BF16), wsd_ref[...])
    fx = shared + _load_row_tiles(acc, 0, tm, pieces)
    o_ref[...] = x1_ref[...] + gt_ref[0] * _rms(fx, npost_ref[...])


def _combine(dest, ys, wt, hp, x1, wsgu, wsd, npost, gt, tokens_per_batch, tm):
    n, d = x1.shape
    pieces = d // LANES
    w = LANES
    tiles_per_batch = tokens_per_batch // tm
    row = lambda i: (i, 0)
    fixed = lambda i: (0, 0)
    return pl.pallas_call(
        functools.partial(_combine_kernel, tm=tm, pieces=pieces),
        out_shape=jax.ShapeDtypeStruct((n, d), F32),
        grid=(n // tm,),
        in_specs=[pl.BlockSpec((tm * TOP_K,), lambda i: (i,), memory_space=pltpu.SMEM),
                  pl.BlockSpec((tm * TOP_K,), lambda i: (jnp.minimum(i + 1, n // tm - 1),), memory_space=pltpu.SMEM),
                  pl.BlockSpec(memory_space=pl.ANY),
                  pl.BlockSpec((tm * TOP_K,), lambda i: (i,), memory_space=pltpu.SMEM),
                  pl.BlockSpec((tm * pieces, w), row), pl.BlockSpec((tm, d), row),
                  pl.BlockSpec(wsgu.shape, fixed), pl.BlockSpec(wsd.shape, fixed),
                  pl.BlockSpec((1, d), fixed), pl.BlockSpec((1, 1, d), lambda i: (i // tiles_per_batch, 0, 0))],
        out_specs=pl.BlockSpec((tm, d), row),
        scratch_shapes=[pltpu.VMEM((2, TOP_K, tm * pieces, w), F32), pltpu.VMEM((tm * pieces, w), F32),
                        pltpu.SemaphoreType.DMA((2,))],
        compiler_params=_params(("arbitrary",)),
        name="combine",
    )(dest, dest, ys, wt, hp, x1, wsgu, wsd, npost.reshape(1, d), gt)


def _layer(x, ctx, mod_x, mod_c, norm_pre_mix, norm_post_mix, norm_pre_ffn, norm_post_ffn,
           w_in, s5_a_re, s5_a_im, s5_log_dt, s5_b_re, s5_b_im, s5_c_re, s5_c_im, s5_d, w_glu, b_glu,
           sgu_norm, sgu_w, sgu_b, w_branch_s5, w_branch_sgu, w_out,
           w_router, router_bias, w_exp_gate, w_exp_up, w_exp_down, w_sh_gate, w_sh_up, w_sh_down):
    bsz, t_len, d = x.shape
    c_len = ctx.shape[1]
    n = bsz * t_len
    d_s5, d_sgu = w_glu.shape[0], sgu_norm.shape[0]
    g_n = d_s5 // S5_GROUP_CH
    n_exp = w_router.shape[1]
    chunk = S5_CHUNK
    tm = min(512, t_len)
    assert t_len % tm == 0 and tm % SGU_CHUNK == 0 and t_len % chunk == 0 and c_len % chunk == 0

    sh_m, sc_m, gt_m, sh_f, sc_f, gt_f = [v.reshape(bsz, 1, d) for v in jnp.split(mod_x, 6, axis=-1)]
    csh_m, csc_m = mod_c[:d], mod_c[d:2 * d]

    w_in_bf = w_in.astype(BF16)

    nc_lat, nc_ctx = t_len // chunk, c_len // chunk
    n_steps = max(1, (max(nc_lat, nc_ctx) - 1).bit_length())
    m_t, ke_t, kc_t, alp = _s5_tables(s5_a_re, s5_a_im, s5_log_dt, s5_b_re, s5_b_im, s5_c_re, s5_c_im,
                                      chunk, n_steps)

    ctx2 = ctx.reshape(bsz * c_len, d)
    u_ctx = _inproj_u(ctx2, norm_pre_mix, csh_m, csc_m, w_in_bf[:, :d_s5], min(512, bsz * c_len))
    zero_init = jnp.zeros((g_n, bsz, 4 * S5_STATE), F32)
    (ctx_final,) = _s5(_to_groups(u_ctx, bsz, nc_ctx, chunk, g_n), ke_t, alp, zero_init, nb=bsz, nc=nc_ctx)

    x2 = x.reshape(n, d)
    u, gu, vn, g1, g2 = _inproj(x2, norm_pre_mix, sh_m, sc_m, w_in_bf, sgu_norm, t_len, tm, d_s5, d_sgu)
    _, yg = _s5(_to_groups(u, bsz, nc_lat, chunk, g_n), ke_t, alp, ctx_final, m_t, kc_t, nb=bsz, nc=nc_lat)
    y = _from_groups(yg, bsz, nc_lat, chunk, g_n)

    ch = d_sgu // SGU_GROUPS
    per_tile = LANES // ch
    sguw = sgu_w.reshape(SGU_GROUPS // per_tile, per_tile, SGU_CHUNK, SGU_CHUNK)
    sguw = sguw.transpose(0, 2, 1, 3).reshape(SGU_GROUPS // per_tile, SGU_CHUNK, per_tile * SGU_CHUNK).astype(BF16)
    sgub = jnp.repeat(sgu_b.T, ch, axis=1)

    x1, hp, logits = _mix(y, u, gu, vn, g1, g2, x2, s5_d, w_glu.astype(BF16), b_glu, sguw, sgub,
                          w_branch_s5.astype(BF16), w_branch_sgu.astype(BF16), w_out.astype(BF16),
                          norm_post_mix, gt_m, norm_pre_ffn, sh_f, sc_f, w_router.astype(BF16), t_len, tm)

    tn = min(512, n)
    idx, wts, counts = _route(logits.T, router_bias, tn)
    rows = EXPERT_ROWS
    cnt = counts.reshape(n_exp).astype(I32)
    nblk = (cnt + rows - 1) // rows
    blk_end = jnp.cumsum(nblk)
    blk_start = blk_end - nblk
    cap = ((n * TOP_K) // rows + n_exp) * rows
    dest = _dest(idx, (blk_start * rows).astype(F32).reshape(n_exp, 1), tn)
    pieces = d // LANES
    dest = dest.T.reshape(n * TOP_K) * pieces

    xs = _dispatch(dest, hp, cap, min(512, t_len), pieces)
    ys = _experts(cnt, blk_start.astype(I32), blk_end[-1:].astype(I32), xs, w_exp_gate, w_exp_up, w_exp_down, rows)
    wsgu = jnp.concatenate([w_sh_gate, w_sh_up], axis=1).astype(BF16)
    out = _combine(dest, ys, wts.T.reshape(n * TOP_K), hp, x1, wsgu, w_sh_down.astype(BF16), norm_post_ffn,
                   gt_f, t_len, min(512, t_len))
    return out.reshape(bsz, t_len, d)


def kernel(x, c, ctx, c_ctx, w_mod, b_mod, norm_pre_mix, norm_post_mix, norm_pre_ffn, norm_post_ffn, w_in, s5_a_re, s5_a_im, s5_log_dt, s5_b_re, s5_b_im, s5_c_re, s5_c_im, s5_d, w_glu, b_glu, sgu_norm, sgu_w, sgu_b, w_branch_s5, w_branch_sgu, w_out, w_router, router_bias, w_exp_gate, w_exp_up, w_exp_down, w_sh_gate, w_sh_up, w_sh_down):
    depth = w_mod.shape[0]
    assert depth == 1, "the context stream is only carried through the last layer's S5 states"
    bsz = x.shape[0]
    pad = (-(bsz + 1)) % 8
    cpad = jnp.concatenate([c, c_ctx[None, :], jnp.zeros((pad, c.shape[1]), c.dtype)], axis=0)
    mod = _modulation(cpad, w_mod[0], b_mod[0])
    return _layer(x, ctx, mod[:bsz], mod[bsz], norm_pre_mix[0], norm_post_mix[0], norm_pre_ffn[0],
                  norm_post_ffn[0], w_in[0], s5_a_re[0], s5_a_im[0], s5_log_dt[0], s5_b_re[0], s5_b_im[0],
                  s5_c_re[0], s5_c_im[0], s5_d[0], w_glu[0], b_glu[0], sgu_norm[0], sgu_w[0], sgu_b[0],
                  w_branch_s5[0], w_branch_sgu[0], w_out[0], w_router[0], router_bias[0],
                  w_exp_gate[0], w_exp_up[0], w_exp_down[0], w_sh_gate[0], w_sh_up[0], w_sh_down[0])
```

```python
import functools
import math

import jax
import jax.numpy as jnp
from jax import lax
from jax.experimental import pallas as pl
from jax.experimental.pallas import tpu as pltpu

F32 = jnp.float32
BF16 = jnp.bfloat16
I32 = jnp.int32

EPS = 1e-6
S5_GROUP_CH = 16
S5_STATE = 64
S5_CHUNK = 64
SGU_GROUPS = 8
SGU_CHUNK = 128
N_EXPERT_GROUPS = 8
TOPK_GROUPS = 4
TOP_K = 8
ROUTE_SCALE = 2.5
LANES = 128
EXPERT_ROWS = 256
EXPERT_X_SLOTS = 4
EXPERT_Y_SLOTS = 3
VMEM_LIMIT = 56 * 1024 * 1024


def _params(sem):
    return pltpu.CompilerParams(dimension_semantics=sem, vmem_limit_bytes=VMEM_LIMIT)


def _rms(v, g):
    return v * lax.rsqrt(jnp.mean(v * v, axis=-1, keepdims=True) + EPS) * g


def _dot(a, b):
    return jnp.dot(a, b, preferred_element_type=F32)


def _store_row_tiles(ref, first_row, val):
    pieces = val.shape[1] // LANES
    for s in range(pieces):
        ref[pl.ds(first_row * pieces + s, val.shape[0], stride=pieces), :] = val[:, s * LANES:(s + 1) * LANES]


def _load_row_tiles(ref, first_row, n_rows, pieces, lead=()):
    return jnp.concatenate([ref[lead + (pl.ds(first_row * pieces + s, n_rows, stride=pieces), slice(None))]
                            for s in range(pieces)], axis=1)


def _mod_kernel(c_ref, w_ref, b_ref, o_ref):
    cv = c_ref[...]
    s = cv * jax.nn.sigmoid(cv)
    o_ref[...] = _dot(s.astype(BF16), w_ref[...].astype(BF16)) + b_ref[...]


def _modulation(cpad, w_mod, b_mod):
    d, n6 = w_mod.shape
    tn = 1024
    return pl.pallas_call(
        _mod_kernel,
        out_shape=jax.ShapeDtypeStruct((cpad.shape[0], n6), F32),
        grid=(n6 // tn,),
        in_specs=[pl.BlockSpec(cpad.shape, lambda j: (0, 0)),
                  pl.BlockSpec((d, tn), lambda j: (0, j)),
                  pl.BlockSpec((1, tn), lambda j: (0, j))],
        out_specs=pl.BlockSpec((cpad.shape[0], tn), lambda j: (0, j)),
        compiler_params=_params(("parallel",)),
        name="mod",
    )(cpad, w_mod, b_mod.reshape(1, n6))


def _inproj_kernel(x_ref, g_ref, sh_ref, sc_ref, w_ref, sgun_ref,
                   u_ref, gu_ref, vn_ref, g1_ref, g2_ref, *, d_s5, d_sgu, d_model):
    h = _rms(x_ref[...], g_ref[...])
    hb = (h * (1.0 + sc_ref[0]) + sh_ref[0]).astype(BF16)
    o1, o2, o3, o4 = d_s5, d_s5 + d_sgu, d_s5 + 2 * d_sgu, d_s5 + 2 * d_sgu + d_model
    u_ref[...] = _dot(hb, w_ref[:, 0:o1]).astype(BF16)
    gu_ref[...] = jax.nn.gelu(_dot(hb, w_ref[:, o1:o2])).astype(BF16)
    v = jax.nn.gelu(_dot(hb, w_ref[:, o2:o3]))
    vn_ref[...] = _rms(v, sgun_ref[...]).astype(BF16)
    g1_ref[...] = jax.nn.sigmoid(_dot(hb, w_ref[:, o3:o4])).astype(BF16)
    g2_ref[...] = jax.nn.sigmoid(_dot(hb, w_ref[:, o4:o4 + d_model])).astype(BF16)


def _inproj(x2, norm_g, shift, scale, w_in_bf, sgu_norm, tokens_per_batch, tm, d_s5, d_sgu):
    n, d = x2.shape
    tiles_per_batch = tokens_per_batch // tm
    row = lambda i: (i, 0)
    fixed = lambda i: (0, 0)
    per_batch = lambda i: (i // tiles_per_batch, 0, 0)
    return pl.pallas_call(
        functools.partial(_inproj_kernel, d_s5=d_s5, d_sgu=d_sgu, d_model=d),
        out_shape=(jax.ShapeDtypeStruct((n, d_s5), BF16), jax.ShapeDtypeStruct((n, d_sgu), BF16),
                   jax.ShapeDtypeStruct((n, d_sgu), BF16), jax.ShapeDtypeStruct((n, d), BF16),
                   jax.ShapeDtypeStruct((n, d), BF16)),
        grid=(n // tm,),
        in_specs=[pl.BlockSpec((tm, d), row), pl.BlockSpec((1, d), fixed),
                  pl.BlockSpec((1, 1, d), per_batch), pl.BlockSpec((1, 1, d), per_batch),
                  pl.BlockSpec(w_in_bf.shape, fixed), pl.BlockSpec((1, d_sgu), fixed)],
        out_specs=(pl.BlockSpec((tm, d_s5), row), pl.BlockSpec((tm, d_sgu), row),
                   pl.BlockSpec((tm, d_sgu), row), pl.BlockSpec((tm, d), row), pl.BlockSpec((tm, d), row)),
        compiler_params=_params(("parallel",)),
        name="inproj",
    )(x2, norm_g.reshape(1, d), shift, scale, w_in_bf, sgu_norm.reshape(1, d_sgu))


def _inproj_u_kernel(x_ref, g_ref, sh_ref, sc_ref, w_ref, u_ref):
    h = _rms(x_ref[...], g_ref[...])
    hb = (h * (1.0 + sc_ref[...]) + sh_ref[...]).astype(BF16)
    u_ref[...] = _dot(hb, w_ref[...]).astype(BF16)


def _inproj_u(x2, norm_g, shift, scale, w_u_bf, tm):
    n, d = x2.shape
    d_s5 = w_u_bf.shape[1]
    row = lambda i: (i, 0)
    fixed = lambda i: (0, 0)
    return pl.pallas_call(
        _inproj_u_kernel,
        out_shape=jax.ShapeDtypeStruct((n, d_s5), BF16),
        grid=(n // tm,),
        in_specs=[pl.BlockSpec((tm, d), row), pl.BlockSpec((1, d), fixed), pl.BlockSpec((1, d), fixed),
                  pl.BlockSpec((1, d), fixed), pl.BlockSpec(w_u_bf.shape, fixed)],
        out_specs=pl.BlockSpec((tm, d_s5), row),
        compiler_params=_params(("parallel",)),
        name="inproj_ctx",
    )(x2, norm_g.reshape(1, d), shift.reshape(1, d), scale.reshape(1, d), w_u_bf)


def _s5_tables(a_re, a_im, log_dt, b_re, b_im, c_re, c_im, chunk, n_steps):
    hi = lax.Precision.HIGHEST
    g_n, p_n, j_n = a_re.shape[1], a_re.shape[2], b_re.shape[3]
    dt = jnp.exp(log_dt)[..., None]
    lam_re, lam_im = dt * a_re, dt * a_im
    ab_re, ab_im = jnp.exp(lam_re) * jnp.cos(lam_im), jnp.exp(lam_re) * jnp.sin(lam_im)
    den = a_re * a_re + a_im * a_im
    q_re = ((ab_re - 1.0) * a_re + ab_im * a_im) / den
    q_im = (ab_im * a_re - (ab_re - 1.0) * a_im) / den
    bb_re = q_re[..., None] * b_re - q_im[..., None] * b_im
    bb_im = q_re[..., None] * b_im + q_im[..., None] * b_re
    k = jnp.arange(chunk + 1, dtype=F32)[:, None, None, None]
    mag = jnp.exp(k * lam_re[None])
    pw_re, pw_im = mag * jnp.cos(k * lam_im[None]), mag * jnp.sin(k * lam_im[None])

    ct_re, ct_im = c_re.transpose(0, 1, 3, 2), c_im.transpose(0, 1, 3, 2)

    def lag_kernels(d):
        w_re = bb_re[d][:, :, :, None] * ct_re[d][:, :, None, :] - bb_im[d][:, :, :, None] * ct_im[d][:, :, None, :]
        w_im = bb_re[d][:, :, :, None] * ct_im[d][:, :, None, :] + bb_im[d][:, :, :, None] * ct_re[d][:, :, None, :]
        w = jnp.concatenate([w_re, -w_im], axis=1).reshape(g_n, 2 * p_n, j_n * j_n)
        a = jnp.concatenate([pw_re[:chunk, d], pw_im[:chunk, d]], axis=-1).transpose(1, 0, 2)
        return jnp.einsum('gkp,gpn->gkn', a, w, precision=hi)

    kf, kb = lag_kernels(0), lag_kernels(1)
    lags = jnp.concatenate([kb[:, 1:][:, ::-1], kf[:, 0:1] + kb[:, 0:1], kf[:, 1:],
                            jnp.zeros_like(kf[:, 0:1])], axis=1)
    lags = lags.transpose(0, 2, 1).reshape(g_n, j_n, j_n, 2 * chunk)

    def both(f, b, axis):
        return jnp.concatenate([f, b], axis=axis)

    pt_re = both(pw_re[:chunk, 0][::-1], pw_re[:chunk, 1], -1).transpose(1, 0, 2)
    pt_im = both(pw_im[:chunk, 0][::-1], pw_im[:chunk, 1], -1).transpose(1, 0, 2)
    bt_re = both(bb_re[0].transpose(0, 2, 1), bb_re[1].transpose(0, 2, 1), -1)
    bt_im = both(bb_im[0].transpose(0, 2, 1), bb_im[1].transpose(0, 2, 1), -1)
    e_re = pt_re[:, None] * bt_re[:, :, None] - pt_im[:, None] * bt_im[:, :, None]
    e_im = pt_re[:, None] * bt_im[:, :, None] + pt_im[:, None] * bt_re[:, :, None]
    ke = jnp.concatenate([e_re, e_im], axis=-1).reshape(g_n, j_n * chunk, 4 * p_n)

    q_re = both(pw_re[1:chunk + 1, 0], pw_re[1:chunk + 1, 1][::-1], -1).transpose(1, 2, 0)
    q_im = both(pw_im[1:chunk + 1, 0], pw_im[1:chunk + 1, 1][::-1], -1).transpose(1, 2, 0)
    q_re, q_im = jnp.tile(q_re, (1, 1, j_n)), jnp.tile(q_im, (1, 1, j_n))
    cc_re = jnp.repeat(both(ct_re[0], ct_re[1], 1), chunk, axis=2)
    cc_im = jnp.repeat(both(ct_im[0], ct_im[1], 1), chunk, axis=2)
    kc = jnp.concatenate([q_re * cc_re - q_im * cc_im, -(q_re * cc_im + q_im * cc_re)], axis=1)

    e = (chunk * (2.0 ** jnp.arange(n_steps, dtype=F32)))[:, None, None, None]
    mg = jnp.exp(e * lam_re[None])
    al_re, al_im = mg * jnp.cos(e * lam_im[None]), mg * jnp.sin(e * lam_im[None])
    alp = jnp.concatenate([al_re[:, 0], al_re[:, 1], al_im[:, 0], al_im[:, 1]], axis=-1)
    return lags, ke.astype(BF16), kc.astype(BF16), alp.transpose(1, 0, 2)


def _build_toeplitz(lag_ref, m_s, chunk):
    j_n = lag_ref.shape[1]
    per_tile = LANES // chunk
    low = lax.broadcasted_iota(I32, (chunk, LANES), 1) < chunk
    for i in range(j_n):
        for q in range(j_n // per_tile):
            tile = None
            for p in range(per_tile):
                j = q * per_tile + p
                row = jnp.broadcast_to(lag_ref[0, i, j:j + 1, :], (chunk, LANES))
                rot = pltpu.roll(row, (p * chunk + LANES - (chunk - 1)) % LANES, 1, stride=1, stride_axis=0)
                tile = rot if tile is None else jnp.where(low, tile, rot)
            m_s[i * chunk:(i + 1) * chunk, q * LANES:(q + 1) * LANES] = tile.astype(BF16)


def _s5_kernel(*refs, nb, nc, readout):
    if readout:
        u_ref, ke_ref, alp_ref, init_ref, lag_ref, kc_ref, fin_ref, y_ref, m_s = refs
    else:
        u_ref, ke_ref, alp_ref, init_ref, fin_ref = refs
    rows = nb * nc
    half = 2 * S5_STATE
    u = u_ref[0]
    e = _dot(u, ke_ref[0])
    er, ei = e[:, :half], e[:, half:]
    r_idx = lax.broadcasted_iota(I32, (rows, half), 0)
    if nc & (nc - 1) == 0:
        n_idx, b_idx = r_idx & (nc - 1), r_idx >> (nc.bit_length() - 1)
    else:
        n_idx, b_idx = lax.rem(r_idx, nc), lax.div(r_idx, nc)
    is_f = lax.broadcasted_iota(I32, (rows, half), 1) < S5_STATE
    seen = jnp.where(is_f, n_idx, nc - 1 - n_idx)
    init = init_ref[0]
    init_r = jnp.zeros((rows, half), F32)
    init_i = jnp.zeros((rows, half), F32)
    for b in range(nb):
        init_r = jnp.where(b_idx == b, init[b:b + 1, :half], init_r)
        init_i = jnp.where(b_idx == b, init[b:b + 1, half:], init_i)
    alp = alp_ref[0]
    ar, ai = alp[0:1, :half], alp[0:1, half:]
    er = er + jnp.where(seen == 0, ar * init_r - ai * init_i, 0.0)
    ei = ei + jnp.where(seen == 0, ar * init_i + ai * init_r, 0.0)

    def from_prev(v, dist):
        return jnp.where(is_f, pltpu.roll(v, dist, 0), pltpu.roll(v, rows - dist, 0))

    d, s = 1, 0
    while d < nc:
        ar, ai = alp[s:s + 1, :half], alp[s:s + 1, half:]
        sr = jnp.where(seen >= d, from_prev(er, d), 0.0)
        si = jnp.where(seen >= d, from_prev(ei, d), 0.0)
        er, ei = er + (ar * sr - ai * si), ei + (ar * si + ai * sr)
        d, s = d * 2, s + 1
    is_f_row = lax.broadcasted_iota(I32, (1, half), 1) < S5_STATE
    for b in range(nb):
        lo, hi = b * nc, b * nc + nc - 1
        fin_ref[0, b:b + 1, :half] = jnp.where(is_f_row, er[hi:hi + 1], er[lo:lo + 1])
        fin_ref[0, b:b + 1, half:] = jnp.where(is_f_row, ei[hi:hi + 1], ei[lo:lo + 1])
    if readout:
        if nc > 1:
            sin_r = jnp.where(seen >= 1, from_prev(er, 1), init_r)
            sin_i = jnp.where(seen >= 1, from_prev(ei, 1), init_i)
        else:
            sin_r, sin_i = init_r, init_i
        s_in = jnp.concatenate([sin_r, sin_i], axis=1).astype(BF16)
        _build_toeplitz(lag_ref, m_s, S5_CHUNK)
        y_ref[0] = (_dot(u, m_s[...]) + _dot(s_in, kc_ref[0])).astype(BF16)


def _s5(ug, ke, alp, init, m=None, kc=None, *, nb, nc):
    g_n, rows, width = ug.shape
    readout = m is not None
    grp = lambda g: (g, 0, 0)
    in_specs = [pl.BlockSpec((1, rows, width), grp), pl.BlockSpec((1,) + ke.shape[1:], grp),
                pl.BlockSpec((1,) + alp.shape[1:], grp), pl.BlockSpec((1,) + init.shape[1:], grp)]
    out_shape = [jax.ShapeDtypeStruct(init.shape, F32)]
    out_specs = [pl.BlockSpec((1,) + init.shape[1:], grp)]
    args = [ug, ke, alp, init]
    scratch = []
    if readout:
        assert m.shape[3] == LANES and 2 * S5_CHUNK == LANES
        in_specs += [pl.BlockSpec((1,) + m.shape[1:], lambda g: (g, 0, 0, 0)),
                     pl.BlockSpec((1,) + kc.shape[1:], grp)]
        out_shape.append(jax.ShapeDtypeStruct(ug.shape, BF16))
        out_specs.append(pl.BlockSpec((1, rows, width), grp))
        args += [m, kc]
        scratch = [pltpu.VMEM((width, width), BF16)]
    return pl.pallas_call(
        functools.partial(_s5_kernel, nb=nb, nc=nc, readout=readout),
        out_shape=tuple(out_shape), grid=(g_n,), in_specs=in_specs, out_specs=tuple(out_specs),
        scratch_shapes=scratch,
        compiler_params=_params(("parallel",)),
        name="s5_readout" if readout else "s5_state",
    )(*args)


def _to_groups(u, nb, nc, chunk, g_n):
    j_n = u.shape[1] // g_n
    return u.reshape(nb * nc, chunk, g_n, j_n).transpose(2, 0, 3, 1).reshape(g_n, nb * nc, j_n * chunk)


def _from_groups(y, nb, nc, chunk, g_n):
    j_n = y.shape[2] // chunk
    return y.reshape(g_n, nb * nc, j_n, chunk).transpose(1, 3, 0, 2).reshape(nb * nc * chunk, g_n * j_n)


def _mix_kernel(y_ref, u_ref, gu_ref, vn_ref, g1_ref, g2_ref, x_ref,
                d_ref, wglu_ref, bglu_ref, sguw_ref, sgub_ref, wb1_ref, wb2_ref, wout_ref,
                npost_ref, gt_ref, npre_ref, shf_ref, scf_ref, wr_ref,
                x1_ref, hp_ref, lg_ref, *, tm):
    y = jax.nn.gelu(y_ref[...].astype(F32) + d_ref[...] * u_ref[...].astype(F32))
    y_s5 = y * jax.nn.sigmoid(_dot(y.astype(BF16), wglu_ref[...]) + bglu_ref[...])

    lane = lax.broadcasted_iota(I32, (1, LANES), 1)
    m_lo = (lane < LANES // 2).astype(F32).astype(BF16)
    m_hi = (lane >= LANES // 2).astype(F32).astype(BF16)
    vn = vn_ref[...]
    chunks = []
    for c in range(tm // SGU_CHUNK):
        vc = vn[c * SGU_CHUNK:(c + 1) * SGU_CHUNK, :]
        tiles = []
        for q in range(vc.shape[1] // LANES):
            vt = vc[:, q * LANES:(q + 1) * LANES]
            rhs = jnp.concatenate([vt * m_lo, vt * m_hi], axis=0)
            tiles.append(_dot(sguw_ref[q], rhs))
        chunks.append(jnp.concatenate(tiles, axis=1) + sgub_ref[...])
    mixed = jnp.concatenate(chunks, axis=0)
    y_sgu = gu_ref[...].astype(F32) * mixed

    merged = (g1_ref[...].astype(F32) * _dot(y_s5.astype(BF16), wb1_ref[...])
              + g2_ref[...].astype(F32) * _dot(y_sgu.astype(BF16), wb2_ref[...]))
    mx = _dot(merged.astype(BF16), wout_ref[...])
    x1 = x_ref[...] + gt_ref[0] * _rms(mx, npost_ref[...])
    x1_ref[...] = x1
    hp = _rms(x1, npre_ref[...]) * (1.0 + scf_ref[0]) + shf_ref[0]
    _store_row_tiles(hp_ref, 0, hp)
    lg_ref[...] = _dot(hp.astype(BF16), wr_ref[...])


def _mix(y, u, gu, vn, g1, g2, x2, s5_d, wglu, bglu, sguw, sgub, wb1, wb2, wout,
         npost, gt, npre, shf, scf, wr, tokens_per_batch, tm):
    n, d = x2.shape
    d_s5, d_sgu, n_exp = y.shape[1], gu.shape[1], wr.shape[1]
    tiles_per_batch = tokens_per_batch // tm
    row = lambda i: (i, 0)
    fixed = lambda i: (0, 0)
    fixed3 = lambda i: (0, 0, 0)
    per_batch = lambda i: (i // tiles_per_batch, 0, 0)
    vec = lambda w: pl.BlockSpec((1, w), fixed)
    return pl.pallas_call(
        functools.partial(_mix_kernel, tm=tm),
        out_shape=(jax.ShapeDtypeStruct((n, d), F32), jax.ShapeDtypeStruct((n * d // LANES, LANES), F32),
                   jax.ShapeDtypeStruct((n, n_exp), F32)),
        grid=(n // tm,),
        in_specs=[pl.BlockSpec((tm, d_s5), row), pl.BlockSpec((tm, d_s5), row), pl.BlockSpec((tm, d_sgu), row),
                  pl.BlockSpec((tm, d_sgu), row), pl.BlockSpec((tm, d), row), pl.BlockSpec((tm, d), row),
                  pl.BlockSpec((tm, d), row),
                  vec(d_s5), pl.BlockSpec(wglu.shape, fixed), vec(d_s5),
                  pl.BlockSpec(sguw.shape, fixed3), pl.BlockSpec(sgub.shape, fixed),
                  pl.BlockSpec(wb1.shape, fixed), pl.BlockSpec(wb2.shape, fixed), pl.BlockSpec(wout.shape, fixed),
                  vec(d), pl.BlockSpec((1, 1, d), per_batch), vec(d),
                  pl.BlockSpec((1, 1, d), per_batch), pl.BlockSpec((1, 1, d), per_batch),
                  pl.BlockSpec(wr.shape, fixed)],
        out_specs=(pl.BlockSpec((tm, d), row), pl.BlockSpec((tm * d // LANES, LANES), row),
                   pl.BlockSpec((tm, n_exp), row)),
        compiler_params=_params(("parallel",)),
        name="mix",
    )(y, u, gu, vn, g1, g2, x2, s5_d.reshape(1, d_s5), wglu, bglu.reshape(1, d_s5), sguw, sgub,
      wb1, wb2, wout, npost.reshape(1, d), gt, npre.reshape(1, d), shf, scf, wr)


def _route_kernel(lg_ref, bias_ref, idx_ref, w_ref, cnt_ref, *, n_exp, tn):
    per_group = n_exp // N_EXPERT_GROUPS
    neg = jnp.float32(-jnp.inf)

    scores = jax.nn.sigmoid(lg_ref[...])
    sel = scores + bias_ref[...]
    gs = []
    for g in range(N_EXPERT_GROUPS):
        sg = sel[g * per_group:(g + 1) * per_group, :]
        m1 = jnp.max(sg, axis=0, keepdims=True)
        is_m1 = sg == m1
        n_m1 = jnp.sum(jnp.where(is_m1, 1.0, 0.0), axis=0, keepdims=True)
        rest = jnp.max(jnp.where(is_m1, neg, sg), axis=0, keepdims=True)
        gs.append(m1 + jnp.where(n_m1 >= 2.0, m1, rest))
    gsm = jnp.concatenate(gs, axis=0)
    g_iota = lax.broadcasted_iota(I32, gsm.shape, 0)
    e_iota = lax.broadcasted_iota(I32, sel.shape, 0).astype(F32)
    masked = []
    for g in range(N_EXPERT_GROUPS):
        mine = gsm[g:g + 1, :]
        beats = jnp.where(gsm > mine, 1.0, jnp.where(gsm == mine, jnp.where(g_iota < g, 1.0, 0.0), 0.0))
        n_beats = jnp.sum(beats, axis=0, keepdims=True)
        masked.append(jnp.where(n_beats < float(TOPK_GROUPS), sel[g * per_group:(g + 1) * per_group, :], neg))
    start = jnp.concatenate(masked, axis=0)
    selm = start
    picked, vals = [], []
    for _ in range(TOP_K):
        m = jnp.max(selm, axis=0, keepdims=True)
        first = jnp.min(jnp.where(selm == m, e_iota, float(n_exp)), axis=0, keepdims=True)
        one = e_iota == first
        picked.append(first)
        vals.append(jnp.sum(jnp.where(one, scores, 0.0), axis=0, keepdims=True))
        selm = jnp.where(one, neg, selm)
    idx_ref[...] = jnp.concatenate(picked, axis=0).astype(I32)
    wv = jnp.concatenate(vals, axis=0)
    w_ref[...] = wv / jnp.sum(wv, axis=0, keepdims=True) * ROUTE_SCALE
    hot = jnp.where(selm == neg, jnp.where(start == neg, 0.0, 1.0), 0.0)

    @pl.when(pl.program_id(0) == 0)
    def _():
        cnt_ref[...] = jnp.zeros_like(cnt_ref)

    cnt_ref[...] += jnp.sum(hot, axis=1, keepdims=True)


def _route(logits_t, bias, tn):
    n_exp, n = logits_t.shape
    col = lambda i: (0, i)
    fixed = lambda i: (0, 0)
    return pl.pallas_call(
        functools.partial(_route_kernel, n_exp=n_exp, tn=tn),
        out_shape=(jax.ShapeDtypeStruct((TOP_K, n), I32), jax.ShapeDtypeStruct((TOP_K, n), F32),
                   jax.ShapeDtypeStruct((n_exp, 1), F32)),
        grid=(n // tn,),
        in_specs=[pl.BlockSpec((n_exp, tn), col), pl.BlockSpec((n_exp, 1), fixed)],
        out_specs=(pl.BlockSpec((TOP_K, tn), col), pl.BlockSpec((TOP_K, tn), col),
                   pl.BlockSpec((n_exp, 1), fixed)),
        compiler_params=_params(("arbitrary",)),
        name="route",
    )(logits_t, bias.reshape(n_exp, 1))


def _dest_kernel(idx_ref, start_ref, dest_ref, carry_ref, *, n_exp, tn):
    @pl.when(pl.program_id(0) == 0)
    def _():
        carry_ref[...] = start_ref[...]

    idx = idx_ref[...]
    e_iota = lax.broadcasted_iota(I32, (n_exp, tn), 0)
    hot = jnp.zeros((n_exp, tn), F32)
    for k in range(TOP_K):
        hot = hot + jnp.where(e_iota == idx[k:k + 1, :], 1.0, 0.0)
    before = jnp.where(lax.broadcasted_iota(I32, (tn, tn), 0) < lax.broadcasted_iota(I32, (tn, tn), 1), 1.0, 0.0)
    rank = _dot(hot.astype(BF16), before.astype(BF16)) + carry_ref[...]
    dest_ref[...] = jnp.concatenate(
        [jnp.sum(jnp.where(e_iota == idx[k:k + 1, :], rank, 0.0), axis=0, keepdims=True)
         for k in range(TOP_K)], axis=0).astype(I32)
    carry_ref[...] += jnp.sum(hot, axis=1, keepdims=True)


def _dest(idx, start, tn):
    n = idx.shape[1]
    n_exp = start.shape[0]
    col = lambda i: (0, i)
    return pl.pallas_call(
        functools.partial(_dest_kernel, n_exp=n_exp, tn=tn),
        out_shape=jax.ShapeDtypeStruct((TOP_K, n), I32),
        grid=(n // tn,),
        in_specs=[pl.BlockSpec((TOP_K, tn), col), pl.BlockSpec((n_exp, 1), lambda i: (0, 0))],
        out_specs=pl.BlockSpec((TOP_K, tn), col),
        scratch_shapes=[pltpu.VMEM((n_exp, 1), F32)],
        compiler_params=_params(("arbitrary",)),
        name="dest",
    )(idx, start)


def _dispatch_kernel(dest_ref, hp_ref, xs_ref, sem, *, tm, pieces):
    def start(r, c):
        src = hp_ref.at[pl.ds(pl.multiple_of(r * pieces, pieces), pieces), :]
        for k in range(TOP_K):
            row = pl.multiple_of(dest_ref[r * TOP_K + k], pieces)
            pltpu.make_async_copy(src, xs_ref.at[pl.ds(row, pieces), :], sem).start(priority=k % 2)
        return c

    lax.fori_loop(0, tm, start, 0)
    for k in range(TOP_K):
        pltpu.make_async_copy(hp_ref, xs_ref.at[pl.ds(0, tm * pieces), :], sem).wait()


def _dispatch(dest, hp, cap, tm, pieces):
    n = hp.shape[0] // pieces
    return pl.pallas_call(
        functools.partial(_dispatch_kernel, tm=tm, pieces=pieces),
        out_shape=jax.ShapeDtypeStruct((cap * pieces, LANES), hp.dtype),
        grid=(n // tm,),
        in_specs=[pl.BlockSpec((tm * TOP_K,), lambda i: (i,), memory_space=pltpu.SMEM),
                  pl.BlockSpec((tm * pieces, LANES), lambda i: (i, 0))],
        out_specs=pl.BlockSpec(memory_space=pl.ANY),
        scratch_shapes=[pltpu.SemaphoreType.DMA(())],
        compiler_params=_params(("arbitrary",)),
        name="dispatch",
    )(dest, hp)


def _expert_kernel(cnt_ref, first_ref, tot_ref, xs_hbm, wg_ref, wu_ref, wd_ref, ys_hbm,
                   xbuf, ybuf, wgu_s, wd_s, xsem, ysem, *, rows, pieces):
    e = pl.program_id(0)
    cnt, first, total = cnt_ref[e], first_ref[e], tot_ref[0]
    n_blk = (cnt + rows - 1) // rows
    d_e = wg_ref.shape[2]
    phys = rows * pieces

    def block_rows(g):
        return pl.ds(pl.multiple_of(g * phys, phys), phys)

    def x_copy(g, slot):
        return pltpu.make_async_copy(xs_hbm.at[block_rows(g), :], xbuf.at[slot], xsem.at[slot])

    def y_copy(g, slot):
        return pltpu.make_async_copy(ybuf.at[slot], ys_hbm.at[block_rows(g), :], ysem.at[slot])

    x_slots, y_slots = xbuf.shape[0], ybuf.shape[0]
    ahead = x_slots - 1

    @pl.when(n_blk > 0)
    def _():
        @pl.when(first == 0)
        def _():
            for a in range(ahead):
                @pl.when(a < total)
                def _():
                    x_copy(a, a).start()

        wgu_s[:, :d_e] = wg_ref[0].astype(BF16)
        wgu_s[:, d_e:] = wu_ref[0].astype(BF16)
        wd_s[...] = wd_ref[0].astype(BF16)

        def block(j, c):
            g = first + j
            slot = g % x_slots
            x_copy(g, slot).wait()

            @pl.when(g + ahead < total)
            def _():
                x_copy(g + ahead, (g + ahead) % x_slots).start()

            yslot = g % y_slots

            @pl.when(g >= y_slots)
            def _():
                y_copy(g - y_slots, yslot).wait()

            x = _load_row_tiles(xbuf, 0, rows, pieces, lead=(slot,))
            r_iota = lax.broadcasted_iota(I32, x.shape, 0) + j * rows
            xb = jnp.where(r_iota < cnt, x, 0.0).astype(BF16)
            gu = _dot(xb, wgu_s[...])
            gate, up = gu[:, :d_e], gu[:, d_e:]
            hid = (gate * jax.nn.sigmoid(gate) * up).astype(BF16)
            y = _dot(hid, wd_s[...])
            for s in range(pieces):
                ybuf[yslot, pl.ds(s, rows, stride=pieces), :] = y[:, s * LANES:(s + 1) * LANES]
            y_copy(g, yslot).start()
            return c

        lax.fori_loop(0, n_blk, block, 0)

        @pl.when(first + n_blk == total)
        def _():
            for back in range(1, y_slots + 1):
                @pl.when(total >= back)
                def _():
                    y_copy(total - back, (total - back) % y_slots).wait()


def _experts(cnt, blk_first, n_used, xs, w_gate, w_up, w_down, rows):
    n_exp, d, d_e = w_gate.shape
    pieces = d // LANES
    hbm = pl.BlockSpec(memory_space=pl.ANY)
    w_map = lambda e, cnt, first, tot: (e, 0, 0)
    grid_spec = pltpu.PrefetchScalarGridSpec(
        num_scalar_prefetch=3,
        grid=(n_exp,),
        in_specs=[hbm, pl.BlockSpec((1, d, d_e), w_map), pl.BlockSpec((1, d, d_e), w_map),
                  pl.BlockSpec((1, d_e, d), w_map)],
        out_specs=hbm,
        scratch_shapes=[pltpu.VMEM((EXPERT_X_SLOTS, rows * pieces, LANES), F32),
                        pltpu.VMEM((EXPERT_Y_SLOTS, rows * pieces, LANES), F32),
                        pltpu.VMEM((d, 2 * d_e), BF16), pltpu.VMEM((d_e, d), BF16),
                        pltpu.SemaphoreType.DMA((EXPERT_X_SLOTS,)), pltpu.SemaphoreType.DMA((EXPERT_Y_SLOTS,))],
    )
    return pl.pallas_call(
        functools.partial(_expert_kernel, rows=rows, pieces=pieces),
        out_shape=jax.ShapeDtypeStruct(xs.shape, F32),
        grid_spec=grid_spec,
        compiler_params=_params(("arbitrary",)),
        name="experts",
    )(cnt, blk_first, n_used, xs, w_gate, w_up, w_down)


def _combine_kernel(dest_ref, next_ref, ys_ref, wt_ref, hp_ref, x1_ref, wsgu_ref, wsd_ref, npost_ref, gt_ref,
                    o_ref, buf, acc, sem, *, tm, pieces):
    i = pl.program_id(0)
    last = pl.num_programs(0) - 1
    cur = i % 2

    def start_token(rows_ref, half, r):
        slot = pl.ds(pl.multiple_of(r * pieces, pieces), pieces)
        for k in range(TOP_K):
            row = pl.multiple_of(rows_ref[r * TOP_K + k], pieces)
            pltpu.make_async_copy(ys_ref.at[pl.ds(row, pieces), :], buf.at[half, k, slot, :],
                                  sem.at[half]).start(priority=k % 2)

    def wait_half(half):
        for k in range(TOP_K):
            pltpu.make_async_copy(ys_ref.at[pl.ds(0, tm * pieces), :], buf.at[half, k], sem.at[half]).wait()

    @pl.when(i == 0)
    def _():
        def first(r, c):
            start_token(dest_ref, 0, r)
            return c

        lax.fori_loop(0, tm, first, 0)

    wait_half(cur)

    def token(r, c):
        start_token(next_ref, 1 - cur, r)
        slot = pl.ds(pl.multiple_of(r * pieces, pieces), pieces)
        terms = [wt_ref[r * TOP_K + k] * buf[cur, k, slot, :] for k in range(TOP_K)]
        while len(terms) > 1:
            terms = [terms[j] + terms[j + 1] for j in range(0, len(terms), 2)]
        acc[slot, :] = terms[0]
        return c

    lax.fori_loop(0, tm, token, 0, unroll=2)

    @pl.when(i == last)
    def _():
        wait_half(1 - cur)

    hb = _load_row_tiles(hp_ref, 0, tm, pieces).astype(BF16)
    d_sh = wsd_ref.shape[0]
    gu = _dot(hb, wsgu_ref[...])
    gate, up = gu[:, :d_sh], gu[:, d_sh:]
    shared = _dot((gate * jax.nn.sigmoid(gate) * up).astype(BF16), wsd_ref[...])
    fx = shared + _load_row_tiles(acc, 0, tm, pieces)
    o_ref[...] = x1_ref[...] + gt_ref[0] * _rms(fx, npost_ref[...])


def _combine(dest, ys, wt, hp, x1, wsgu, wsd, npost, gt, tokens_per_batch, tm):
    n, d = x1.shape
    pieces = d // LANES
    w = LANES
    tiles_per_batch = tokens_per_batch // tm
    row = lambda i: (i, 0)
    fixed = lambda i: (0, 0)
    return pl.pallas_call(
        functools.partial(_combine_kernel, tm=tm, pieces=pieces),
        out_shape=jax.ShapeDtypeStruct((n, d), F32),
        grid=(n // tm,),
        in_specs=[pl.BlockSpec((tm * TOP_K,), lambda i: (i,), memory_space=pltpu.SMEM),
                  pl.BlockSpec((tm * TOP_K,), lambda i: (jnp.minimum(i + 1, n // tm - 1),), memory_space=pltpu.SMEM),
                  pl.BlockSpec(memory_space=pl.ANY),
                  pl.BlockSpec((tm * TOP_K,), lambda i: (i,), memory_space=pltpu.SMEM),
                  pl.BlockSpec((tm * pieces, w), row), pl.BlockSpec((tm, d), row),
                  pl.BlockSpec(wsgu.shape, fixed), pl.BlockSpec(wsd.shape, fixed),
                  pl.BlockSpec((1, d), fixed), pl.BlockSpec((1, 1, d), lambda i: (i // tiles_per_batch, 0, 0))],
        out_specs=pl.BlockSpec((tm, d), row),
        scratch_shapes=[pltpu.VMEM((2, TOP_K, tm * pieces, w), F32), pltpu.VMEM((tm * pieces, w), F32),
                        pltpu.SemaphoreType.DMA((2,))],
        compiler_params=_params(("arbitrary",)),
        name="combine",
    )(dest, dest, ys, wt, hp, x1, wsgu, wsd, npost.reshape(1, d), gt)


def _layer(x, ctx, mod_x, mod_c, norm_pre_mix, norm_post_mix, norm_pre_ffn, norm_post_ffn,
           w_in, s5_a_re, s5_a_im, s5_log_dt, s5_b_re, s5_b_im, s5_c_re, s5_c_im, s5_d, w_glu, b_glu,
           sgu_norm, sgu_w, sgu_b, w_branch_s5, w_branch_sgu, w_out,
           w_router, router_bias, w_exp_gate, w_exp_up, w_exp_down, w_sh_gate, w_sh_up, w_sh_down):
    bsz, t_len, d = x.shape
    c_len = ctx.shape[1]
    n = bsz * t_len
    d_s5, d_sgu = w_glu.shape[0], sgu_norm.shape[0]
    g_n = d_s5 // S5_GROUP_CH
    n_exp = w_router.shape[1]
    chunk = S5_CHUNK
    tm = min(512, t_len)
    assert t_len % tm == 0 and tm % SGU_CHUNK == 0 and t_len % chunk == 0 and c_len % chunk == 0

    sh_m, sc_m, gt_m, sh_f, sc_f, gt_f = [v.reshape(bsz, 1, d) for v in jnp.split(mod_x, 6, axis=-1)]
    csh_m, csc_m = mod_c[:d], mod_c[d:2 * d]

    w_in_bf = w_in.astype(BF16)

    nc_lat, nc_ctx = t_len // chunk, c_len // chunk
    n_steps = max(1, (max(nc_lat, nc_ctx) - 1).bit_length())
    m_t, ke_t, kc_t, alp = _s5_tables(s5_a_re, s5_a_im, s5_log_dt, s5_b_re, s5_b_im, s5_c_re, s5_c_im,
                                      chunk, n_steps)

    ctx2 = ctx.reshape(bsz * c_len, d)
    u_ctx = _inproj_u(ctx2, norm_pre_mix, csh_m, csc_m, w_in_bf[:, :d_s5], min(512, bsz * c_len))
    zero_init = jnp.zeros((g_n, bsz, 4 * S5_STATE), F32)
    (ctx_final,) = _s5(_to_groups(u_ctx, bsz, nc_ctx, chunk, g_n), ke_t, alp, zero_init, nb=bsz, nc=nc_ctx)

    x2 = x.reshape(n, d)
    u, gu, vn, g1, g2 = _inproj(x2, norm_pre_mix, sh_m, sc_m, w_in_bf, sgu_norm, t_len, tm, d_s5, d_sgu)
    _, yg = _s5(_to_groups(u, bsz, nc_lat, chunk, g_n), ke_t, alp, ctx_final, m_t, kc_t, nb=bsz, nc=nc_lat)
    y = _from_groups(yg, bsz, nc_lat, chunk, g_n)

    ch = d_sgu // SGU_GROUPS
    per_tile = LANES // ch
    sguw = sgu_w.reshape(SGU_GROUPS // per_tile, per_tile, SGU_CHUNK, SGU_CHUNK)
    sguw = sguw.transpose(0, 2, 1, 3).reshape(SGU_GROUPS // per_tile, SGU_CHUNK, per_tile * SGU_CHUNK).astype(BF16)
    sgub = jnp.repeat(sgu_b.T, ch, axis=1)

    x1, hp, logits = _mix(y, u, gu, vn, g1, g2, x2, s5_d, w_glu.astype(BF16), b_glu, sguw, sgub,
                          w_branch_s5.astype(BF16), w_branch_sgu.astype(BF16), w_out.astype(BF16),
                          norm_post_mix, gt_m, norm_pre_ffn, sh_f, sc_f, w_router.astype(BF16), t_len, tm)

    tn = min(512, n)
    idx, wts, counts = _route(logits.T, router_bias, tn)
    rows = EXPERT_ROWS
    cnt = counts.reshape(n_exp).astype(I32)
    nblk = (cnt + rows - 1) // rows
    blk_end = jnp.cumsum(nblk)
    blk_start = blk_end - nblk
    cap = ((n * TOP_K) // rows + n_exp) * rows
    dest = _dest(idx, (blk_start * rows).astype(F32).reshape(n_exp, 1), tn)
    pieces = d // LANES
    dest = dest.T.reshape(n * TOP_K) * pieces

    xs = _dispatch(dest, hp, cap, min(512, t_len), pieces)
    ys = _experts(cnt, blk_start.astype(I32), blk_end[-1:].astype(I32), xs, w_exp_gate, w_exp_up, w_exp_down, rows)
    wsgu = jnp.concatenate([w_sh_gate, w_sh_up], axis=1).astype(BF16)
    out = _combine(dest, ys, wts.T.reshape(n * TOP_K), hp, x1, wsgu, w_sh_down.astype(BF16), norm_post_ffn,
                   gt_f, t_len, min(512, t_len))
    return out.reshape(bsz, t_len, d)


def kernel(x, c, ctx, c_ctx, w_mod, b_mod, norm_pre_mix, norm_post_mix, norm_pre_ffn, norm_post_ffn, w_in, s5_a_re, s5_a_im, s5_log_dt, s5_b_re, s5_b_im, s5_c_re, s5_c_im, s5_d, w_glu, b_glu, sgu_norm, sgu_w, sgu_b, w_branch_s5, w_branch_sgu, w_out, w_router, router_bias, w_exp_gate, w_exp_up, w_exp_down, w_sh_gate, w_sh_up, w_sh_down):
    depth = w_mod.shape[0]
    assert depth == 1, "the context stream is only carried through the last layer's S5 states"
    bsz = x.shape[0]
    pad = (-(bsz + 1)) % 8
    cpad = jnp.concatenate([c, c_ctx[None, :], jnp.zeros((pad, c.shape[1]), c.dtype)], axis=0)
    mod = _modulation(cpad, w_mod[0], b_mod[0])
    return _layer(x, ctx, mod[:bsz], mod[bsz], norm_pre_mix[0], norm_post_mix[0], norm_pre_ffn[0],
                  norm_post_ffn[0], w_in[0], s5_a_re[0], s5_a_im[0], s5_log_dt[0], s5_b_re[0], s5_b_im[0],
                  s5_c_re[0], s5_c_im[0], s5_d[0], w_glu[0], b_glu[0], sgu_norm[0], sgu_w[0], sgu_b[0],
                  w_branch_s5[0], w_branch_sgu[0], w_out[0], w_router[0], router_bias[0],
                  w_exp_gate[0], w_exp_up[0], w_exp_down[0], w_sh_gate[0], w_sh_up[0], w_sh_down[0])
```

```python
import functools
import math

import jax
import jax.numpy as jnp
from jax import lax
from jax.experimental import pallas as pl
from jax.experimental.pallas import tpu as pltpu

F32 = jnp.float32
BF16 = jnp.bfloat16
I32 = jnp.int32

EPS = 1e-6
S5_GROUP_CH = 16
S5_STATE = 64
S5_CHUNK = 64
SGU_GROUPS = 8
SGU_CHUNK = 128
N_EXPERT_GROUPS = 8
TOPK_GROUPS = 4
TOP_K = 8
ROUTE_SCALE = 2.5
LANES = 128
EXPERT_ROWS = 256
EXPERT_X_SLOTS = 4
EXPERT_Y_SLOTS = 3
VMEM_LIMIT = 56 * 1024 * 1024


def _params(sem):
    return pltpu.CompilerParams(dimension_semantics=sem, vmem_limit_bytes=VMEM_LIMIT)


def _rms(v, g):
    return v * lax.rsqrt(jnp.mean(v * v, axis=-1, keepdims=True) + EPS) * g


def _dot(a, b):
    return jnp.dot(a, b, preferred_element_type=F32)


def _store_row_tiles(ref, first_row, val):
    pieces = val.shape[1] // LANES
    for s in range(pieces):
        ref[pl.ds(first_row * pieces + s, val.shape[0], stride=pieces), :] = val[:, s * LANES:(s + 1) * LANES]


def _load_row_tiles(ref, first_row, n_rows, pieces, lead=()):
    return jnp.concatenate([ref[lead + (pl.ds(first_row * pieces + s, n_rows, stride=pieces), slice(None))]
                            for s in range(pieces)], axis=1)


def _mod_kernel(c_ref, w_ref, b_ref, o_ref):
    cv = c_ref[...]
    s = cv * jax.nn.sigmoid(cv)
    o_ref[...] = _dot(s.astype(BF16), w_ref[...].astype(BF16)) + b_ref[...]


def _modulation(cpad, w_mod, b_mod):
    d, n6 = w_mod.shape
    tn = 1024
    return pl.pallas_call(
        _mod_kernel,
        out_shape=jax.ShapeDtypeStruct((cpad.shape[0], n6), F32),
        grid=(n6 // tn,),
        in_specs=[pl.BlockSpec(cpad.shape, lambda j: (0, 0)),
                  pl.BlockSpec((d, tn), lambda j: (0, j)),
                  pl.BlockSpec((1, tn), lambda j: (0, j))],
        out_specs=pl.BlockSpec((cpad.shape[0], tn), lambda j: (0, j)),
        compiler_params=_params(("parallel",)),
        name="mod",
    )(cpad, w_mod, b_mod.reshape(1, n6))


def _inproj_kernel(x_ref, g_ref, sh_ref, sc_ref, w_ref, sgun_ref,
                   u_ref, ut_ref, gu_ref, vn_ref, g1_ref, g2_ref, *, d_s5, d_sgu, d_model):
    h = _rms(x_ref[...], g_ref[...])
    hb = (h * (1.0 + sc_ref[0]) + sh_ref[0]).astype(BF16)
    o1, o2, o3, o4 = d_s5, d_s5 + d_sgu, d_s5 + 2 * d_sgu, d_s5 + 2 * d_sgu + d_model
    u = _dot(hb, w_ref[:, 0:o1])
    u_ref[...] = u.astype(BF16)
    ut_ref[...] = u.T.astype(BF16)
    gu_ref[...] = jax.nn.gelu(_dot(hb, w_ref[:, o1:o2])).astype(BF16)
    v = jax.nn.gelu(_dot(hb, w_ref[:, o2:o3]))
    vn_ref[...] = _rms(v, sgun_ref[...]).astype(BF16)
    g1_ref[...] = jax.nn.sigmoid(_dot(hb, w_ref[:, o3:o4])).astype(BF16)
    g2_ref[...] = jax.nn.sigmoid(_dot(hb, w_ref[:, o4:o4 + d_model])).astype(BF16)


def _inproj(x2, norm_g, shift, scale, w_in_bf, sgu_norm, tokens_per_batch, tm, d_s5, d_sgu):
    n, d = x2.shape
    tiles_per_batch = tokens_per_batch // tm
    row = lambda i: (i, 0)
    fixed = lambda i: (0, 0)
    per_batch = lambda i: (i // tiles_per_batch, 0, 0)
    return pl.pallas_call(
        functools.partial(_inproj_kernel, d_s5=d_s5, d_sgu=d_sgu, d_model=d),
        out_shape=(jax.ShapeDtypeStruct((n, d_s5), BF16), jax.ShapeDtypeStruct((d_s5, n), BF16),
                   jax.ShapeDtypeStruct((n, d_sgu), BF16),
                   jax.ShapeDtypeStruct((n, d_sgu), BF16), jax.ShapeDtypeStruct((n, d), BF16),
                   jax.ShapeDtypeStruct((n, d), BF16)),
        grid=(n // tm,),
        in_specs=[pl.BlockSpec((tm, d), row), pl.BlockSpec((1, d), fixed),
                  pl.BlockSpec((1, 1, d), per_batch), pl.BlockSpec((1, 1, d), per_batch),
                  pl.BlockSpec(w_in_bf.shape, fixed), pl.BlockSpec((1, d_sgu), fixed)],
        out_specs=(pl.BlockSpec((tm, d_s5), row), pl.BlockSpec((d_s5, tm), lambda i: (0, i)),
                   pl.BlockSpec((tm, d_sgu), row),
                   pl.BlockSpec((tm, d_sgu), row), pl.BlockSpec((tm, d), row), pl.BlockSpec((tm, d), row)),
        compiler_params=_params(("parallel",)),
        name="inproj",
    )(x2, norm_g.reshape(1, d), shift, scale, w_in_bf, sgu_norm.reshape(1, d_sgu))


def _inproj_u_kernel(x_ref, g_ref, sh_ref, sc_ref, w_ref, u_ref):
    h = _rms(x_ref[...], g_ref[...])
    hb = (h * (1.0 + sc_ref[...]) + sh_ref[...]).astype(BF16)
    u_ref[...] = _dot(hb, w_ref[...]).T.astype(BF16)


def _inproj_u(x2, norm_g, shift, scale, w_u_bf, tm):
    n, d = x2.shape
    d_s5 = w_u_bf.shape[1]
    row = lambda i: (i, 0)
    fixed = lambda i: (0, 0)
    return pl.pallas_call(
        _inproj_u_kernel,
        out_shape=jax.ShapeDtypeStruct((d_s5, n), BF16),
        grid=(n // tm,),
        in_specs=[pl.BlockSpec((tm, d), row), pl.BlockSpec((1, d), fixed), pl.BlockSpec((1, d), fixed),
                  pl.BlockSpec((1, d), fixed), pl.BlockSpec(w_u_bf.shape, fixed)],
        out_specs=pl.BlockSpec((d_s5, tm), lambda i: (0, i)),
        compiler_params=_params(("parallel",)),
        name="inproj_ctx",
    )(x2, norm_g.reshape(1, d), shift.reshape(1, d), scale.reshape(1, d), w_u_bf)


def _s5_tables(a_re, a_im, log_dt, b_re, b_im, c_re, c_im, chunk, n_steps):
    hi = lax.Precision.HIGHEST
    g_n, p_n, j_n = a_re.shape[1], a_re.shape[2], b_re.shape[3]
    dt = jnp.exp(log_dt)[..., None]
    lam_re, lam_im = dt * a_re, dt * a_im
    ab_re, ab_im = jnp.exp(lam_re) * jnp.cos(lam_im), jnp.exp(lam_re) * jnp.sin(lam_im)
    den = a_re * a_re + a_im * a_im
    q_re = ((ab_re - 1.0) * a_re + ab_im * a_im) / den
    q_im = (ab_im * a_re - (ab_re - 1.0) * a_im) / den
    bb_re = q_re[..., None] * b_re - q_im[..., None] * b_im
    bb_im = q_re[..., None] * b_im + q_im[..., None] * b_re
    k = jnp.arange(chunk + 1, dtype=F32)[:, None, None, None]
    mag = jnp.exp(k * lam_re[None])
    pw_re, pw_im = mag * jnp.cos(k * lam_im[None]), mag * jnp.sin(k * lam_im[None])

    ct_re, ct_im = c_re.transpose(0, 1, 3, 2), c_im.transpose(0, 1, 3, 2)

    def lag_kernels(d):
        w_re = bb_re[d][:, :, :, None] * ct_re[d][:, :, None, :] - bb_im[d][:, :, :, None] * ct_im[d][:, :, None, :]
        w_im = bb_re[d][:, :, :, None] * ct_im[d][:, :, None, :] + bb_im[d][:, :, :, None] * ct_re[d][:, :, None, :]
        w = jnp.concatenate([w_re, -w_im], axis=1).reshape(g_n, 2 * p_n, j_n * j_n)
        a = jnp.concatenate([pw_re[:chunk, d], pw_im[:chunk, d]], axis=-1).transpose(1, 0, 2)
        return jnp.einsum('gkp,gpn->gkn', a, w, precision=hi)

    kf, kb = lag_kernels(0), lag_kernels(1)
    lags = jnp.concatenate([kb[:, 1:][:, ::-1], kf[:, 0:1] + kb[:, 0:1], kf[:, 1:],
                            jnp.zeros_like(kf[:, 0:1])], axis=1)
    lags = lags.transpose(0, 2, 1).reshape(g_n, j_n, j_n, 2 * chunk)

    def both(f, b, axis):
        return jnp.concatenate([f, b], axis=axis)

    pt_re = both(pw_re[:chunk, 0][::-1], pw_re[:chunk, 1], -1).transpose(1, 0, 2)
    pt_im = both(pw_im[:chunk, 0][::-1], pw_im[:chunk, 1], -1).transpose(1, 0, 2)
    bt_re = both(bb_re[0].transpose(0, 2, 1), bb_re[1].transpose(0, 2, 1), -1)
    bt_im = both(bb_im[0].transpose(0, 2, 1), bb_im[1].transpose(0, 2, 1), -1)
    e_re = pt_re[:, None] * bt_re[:, :, None] - pt_im[:, None] * bt_im[:, :, None]
    e_im = pt_re[:, None] * bt_im[:, :, None] + pt_im[:, None] * bt_re[:, :, None]
    ke = jnp.concatenate([e_re, e_im], axis=-1).reshape(g_n, j_n * chunk, 4 * p_n)

    q_re = both(pw_re[1:chunk + 1, 0], pw_re[1:chunk + 1, 1][::-1], -1).transpose(1, 2, 0)
    q_im = both(pw_im[1:chunk + 1, 0], pw_im[1:chunk + 1, 1][::-1], -1).transpose(1, 2, 0)
    q_re, q_im = jnp.tile(q_re, (1, 1, j_n)), jnp.tile(q_im, (1, 1, j_n))
    cc_re = jnp.repeat(both(ct_re[0], ct_re[1], 1), chunk, axis=2)
    cc_im = jnp.repeat(both(ct_im[0], ct_im[1], 1), chunk, axis=2)
    kc = jnp.concatenate([q_re * cc_re - q_im * cc_im, -(q_re * cc_im + q_im * cc_re)], axis=1)

    e = (chunk * (2.0 ** jnp.arange(n_steps, dtype=F32)))[:, None, None, None]
    mg = jnp.exp(e * lam_re[None])
    al_re, al_im = mg * jnp.cos(e * lam_im[None]), mg * jnp.sin(e * lam_im[None])
    alp = jnp.concatenate([al_re[:, 0], al_re[:, 1], al_im[:, 0], al_im[:, 1]], axis=-1)
    return lags, ke.astype(BF16), kc.astype(BF16), alp.transpose(1, 0, 2)


def _build_toeplitz(lag_ref, m_s, chunk):
    j_n = lag_ref.shape[1]
    per_tile = LANES // chunk
    low = lax.broadcasted_iota(I32, (chunk, LANES), 1) < chunk
    for i in range(j_n):
        for q in range(j_n // per_tile):
            tile = None
            for p in range(per_tile):
                j = q * per_tile + p
                row = jnp.broadcast_to(lag_ref[0, i, j:j + 1, :], (chunk, LANES))
                rot = pltpu.roll(row, (p * chunk + LANES - (chunk - 1)) % LANES, 1, stride=1, stride_axis=0)
                tile = rot if tile is None else jnp.where(low, tile, rot)
            m_s[i * chunk:(i + 1) * chunk, q * LANES:(q + 1) * LANES] = tile.astype(BF16)


def _s5_kernel(*refs, nb, nc, readout):
    if readout:
        u_ref, ke_ref, alp_ref, init_ref, lag_ref, kc_ref, fin_ref, y_ref, m_s = refs
    else:
        u_ref, ke_ref, alp_ref, init_ref, fin_ref = refs
    rows = nb * nc
    half = 2 * S5_STATE
    u = u_ref[0]
    e = _dot(u, ke_ref[0])
    er, ei = e[:, :half], e[:, half:]
    r_idx = lax.broadcasted_iota(I32, (rows, half), 0)
    if nc & (nc - 1) == 0:
        n_idx, b_idx = r_idx & (nc - 1), r_idx >> (nc.bit_length() - 1)
    else:
        n_idx, b_idx = lax.rem(r_idx, nc), lax.div(r_idx, nc)
    is_f = lax.broadcasted_iota(I32, (rows, half), 1) < S5_STATE
    seen = jnp.where(is_f, n_idx, nc - 1 - n_idx)
    init = init_ref[0]
    init_r = jnp.zeros((rows, half), F32)
    init_i = jnp.zeros((rows, half), F32)
    for b in range(nb):
        init_r = jnp.where(b_idx == b, init[b:b + 1, :half], init_r)
        init_i = jnp.where(b_idx == b, init[b:b + 1, half:], init_i)
    alp = alp_ref[0]
    ar, ai = alp[0:1, :half], alp[0:1, half:]
    er = er + jnp.where(seen == 0, ar * init_r - ai * init_i, 0.0)
    ei = ei + jnp.where(seen == 0, ar * init_i + ai * init_r, 0.0)

    def from_prev(v, dist):
        return jnp.where(is_f, pltpu.roll(v, dist, 0), pltpu.roll(v, rows - dist, 0))

    d, s = 1, 0
    while d < nc:
        ar, ai = alp[s:s + 1, :half], alp[s:s + 1, half:]
        sr = jnp.where(seen >= d, from_prev(er, d), 0.0)
        si = jnp.where(seen >= d, from_prev(ei, d), 0.0)
        er, ei = er + (ar * sr - ai * si), ei + (ar * si + ai * sr)
        d, s = d * 2, s + 1
    is_f_row = lax.broadcasted_iota(I32, (1, half), 1) < S5_STATE
    for b in range(nb):
        lo, hi = b * nc, b * nc + nc - 1
        fin_ref[0, b:b + 1, :half] = jnp.where(is_f_row, er[hi:hi + 1], er[lo:lo + 1])
        fin_ref[0, b:b + 1, half:] = jnp.where(is_f_row, ei[hi:hi + 1], ei[lo:lo + 1])
    if readout:
        if nc > 1:
            sin_r = jnp.where(seen >= 1, from_prev(er, 1), init_r)
            sin_i = jnp.where(seen >= 1, from_prev(ei, 1), init_i)
        else:
            sin_r, sin_i = init_r, init_i
        s_in = jnp.concatenate([sin_r, sin_i], axis=1).astype(BF16)
        _build_toeplitz(lag_ref, m_s, S5_CHUNK)
        y_ref[0] = (_dot(u, m_s[...]) + _dot(s_in, kc_ref[0])).astype(BF16)


def _s5(ug, ke, alp, init, m=None, kc=None, *, nb, nc):
    g_n, rows, width = ug.shape
    readout = m is not None
    grp = lambda g: (g, 0, 0)
    in_specs = [pl.BlockSpec((1, rows, width), grp), pl.BlockSpec((1,) + ke.shape[1:], grp),
                pl.BlockSpec((1,) + alp.shape[1:], grp), pl.BlockSpec((1,) + init.shape[1:], grp)]
    out_shape = [jax.ShapeDtypeStruct(init.shape, F32)]
    out_specs = [pl.BlockSpec((1,) + init.shape[1:], grp)]
    args = [ug, ke, alp, init]
    scratch = []
    if readout:
        assert m.shape[3] == LANES and 2 * S5_CHUNK == LANES
        in_specs += [pl.BlockSpec((1,) + m.shape[1:], lambda g: (g, 0, 0, 0)),
                     pl.BlockSpec((1,) + kc.shape[1:], grp)]
        out_shape.append(jax.ShapeDtypeStruct(ug.shape, BF16))
        out_specs.append(pl.BlockSpec((1, rows, width), grp))
        args += [m, kc]
        scratch = [pltpu.VMEM((width, width), BF16)]
    return pl.pallas_call(
        functools.partial(_s5_kernel, nb=nb, nc=nc, readout=readout),
        out_shape=tuple(out_shape), grid=(g_n,), in_specs=in_specs, out_specs=tuple(out_specs),
        scratch_shapes=scratch,
        compiler_params=_params(("parallel",)),
        name="s5_readout" if readout else "s5_state",
    )(*args)


def _to_groups(ut, nb, nc, chunk, g_n):
    j_n = ut.shape[0] // g_n
    return ut.reshape(g_n, j_n, nb * nc, chunk).transpose(0, 2, 1, 3).reshape(g_n, nb * nc, j_n * chunk)


def _from_groups(y, nb, nc, chunk, g_n):
    j_n = y.shape[2] // chunk
    return y.reshape(g_n, nb * nc, j_n, chunk).transpose(0, 2, 1, 3).reshape(g_n * j_n, nb * nc * chunk)


def _mix_kernel(y_ref, u_ref, gu_ref, vn_ref, g1_ref, g2_ref, x_ref,
                d_ref, wglu_ref, bglu_ref, sguw_ref, sgub_ref, wb1_ref, wb2_ref, wout_ref,
                npost_ref, gt_ref, npre_ref, shf_ref, scf_ref, wr_ref,
                x1_ref, hp_ref, lg_ref, *, tm):
    y = jax.nn.gelu(y_ref[...].astype(F32).T + d_ref[...] * u_ref[...].astype(F32))
    y_s5 = y * jax.nn.sigmoid(_dot(y.astype(BF16), wglu_ref[...]) + bglu_ref[...])

    lane = lax.broadcasted_iota(I32, (1, LANES), 1)
    m_lo = (lane < LANES // 2).astype(F32).astype(BF16)
    m_hi = (lane >= LANES // 2).astype(F32).astype(BF16)
    vn = vn_ref[...]
    chunks = []
    for c in range(tm // SGU_CHUNK):
        vc = vn[c * SGU_CHUNK:(c + 1) * SGU_CHUNK, :]
        tiles = []
        for q in range(vc.shape[1] // LANES):
            vt = vc[:, q * LANES:(q + 1) * LANES]
            rhs = jnp.concatenate([vt * m_lo, vt * m_hi], axis=0)
            tiles.append(_dot(sguw_ref[q], rhs))
        chunks.append(jnp.concatenate(tiles, axis=1) + sgub_ref[...])
    mixed = jnp.concatenate(chunks, axis=0)
    y_sgu = gu_ref[...].astype(F32) * mixed

    merged = (g1_ref[...].astype(F32) * _dot(y_s5.astype(BF16), wb1_ref[...])
              + g2_ref[...].astype(F32) * _dot(y_sgu.astype(BF16), wb2_ref[...]))
    mx = _dot(merged.astype(BF16), wout_ref[...])
    x1 = x_ref[...] + gt_ref[0] * _rms(mx, npost_ref[...])
    x1_ref[...] = x1
    hp = _rms(x1, npre_ref[...]) * (1.0 + scf_ref[0]) + shf_ref[0]
    _store_row_tiles(hp_ref, 0, hp)
    lg_ref[...] = _dot(hp.astype(BF16), wr_ref[...])


def _mix(y, u, gu, vn, g1, g2, x2, s5_d, wglu, bglu, sguw, sgub, wb1, wb2, wout,
         npost, gt, npre, shf, scf, wr, tokens_per_batch, tm):
    n, d = x2.shape
    d_s5, d_sgu, n_exp = y.shape[0], gu.shape[1], wr.shape[1]
    tiles_per_batch = tokens_per_batch // tm
    row = lambda i: (i, 0)
    fixed = lambda i: (0, 0)
    fixed3 = lambda i: (0, 0, 0)
    per_batch = lambda i: (i // tiles_per_batch, 0, 0)
    vec = lambda w: pl.BlockSpec((1, w), fixed)
    return pl.pallas_call(
        functools.partial(_mix_kernel, tm=tm),
        out_shape=(jax.ShapeDtypeStruct((n, d), F32), jax.ShapeDtypeStruct((n * d // LANES, LANES), F32),
                   jax.ShapeDtypeStruct((n, n_exp), F32)),
        grid=(n // tm,),
        in_specs=[pl.BlockSpec((d_s5, tm), lambda i: (0, i)), pl.BlockSpec((tm, d_s5), row),
                  pl.BlockSpec((tm, d_sgu), row),
                  pl.BlockSpec((tm, d_sgu), row), pl.BlockSpec((tm, d), row), pl.BlockSpec((tm, d), row),
                  pl.BlockSpec((tm, d), row),
                  vec(d_s5), pl.BlockSpec(wglu.shape, fixed), vec(d_s5),
                  pl.BlockSpec(sguw.shape, fixed3), pl.BlockSpec(sgub.shape, fixed),
                  pl.BlockSpec(wb1.shape, fixed), pl.BlockSpec(wb2.shape, fixed), pl.BlockSpec(wout.shape, fixed),
                  vec(d), pl.BlockSpec((1, 1, d), per_batch), vec(d),
                  pl.BlockSpec((1, 1, d), per_batch), pl.BlockSpec((1, 1, d), per_batch),
                  pl.BlockSpec(wr.shape, fixed)],
        out_specs=(pl.BlockSpec((tm, d), row), pl.BlockSpec((tm * d // LANES, LANES), row),
                   pl.BlockSpec((tm, n_exp), row)),
        compiler_params=_params(("parallel",)),
        name="mix",
    )(y, u, gu, vn, g1, g2, x2, s5_d.reshape(1, d_s5), wglu, bglu.reshape(1, d_s5), sguw, sgub,
      wb1, wb2, wout, npost.reshape(1, d), gt, npre.reshape(1, d), shf, scf, wr)


def _route_kernel(lg_ref, bias_ref, idx_ref, w_ref, cnt_ref, *, n_exp, tn):
    per_group = n_exp // N_EXPERT_GROUPS
    neg = jnp.float32(-jnp.inf)

    scores = jax.nn.sigmoid(lg_ref[...])
    sel = scores + bias_ref[...]
    gs = []
    for g in range(N_EXPERT_GROUPS):
        sg = sel[g * per_group:(g + 1) * per_group, :]
        m1 = jnp.max(sg, axis=0, keepdims=True)
        is_m1 = sg == m1
        n_m1 = jnp.sum(jnp.where(is_m1, 1.0, 0.0), axis=0, keepdims=True)
        rest = jnp.max(jnp.where(is_m1, neg, sg), axis=0, keepdims=True)
        gs.append(m1 + jnp.where(n_m1 >= 2.0, m1, rest))
    gsm = jnp.concatenate(gs, axis=0)
    g_iota = lax.broadcasted_iota(I32, gsm.shape, 0)
    e_iota = lax.broadcasted_iota(I32, sel.shape, 0).astype(F32)
    masked = []
    for g in range(N_EXPERT_GROUPS):
        mine = gsm[g:g + 1, :]
        beats = jnp.where(gsm > mine, 1.0, jnp.where(gsm == mine, jnp.where(g_iota < g, 1.0, 0.0), 0.0))
        n_beats = jnp.sum(beats, axis=0, keepdims=True)
        masked.append(jnp.where(n_beats < float(TOPK_GROUPS), sel[g * per_group:(g + 1) * per_group, :], neg))
    start = jnp.concatenate(masked, axis=0)
    selm = start
    picked, vals = [], []
    for _ in range(TOP_K):
        m = jnp.max(selm, axis=0, keepdims=True)
        first = jnp.min(jnp.where(selm == m, e_iota, float(n_exp)), axis=0, keepdims=True)
        one = e_iota == first
        picked.append(first)
        vals.append(jnp.sum(jnp.where(one, scores, 0.0), axis=0, keepdims=True))
        selm = jnp.where(one, neg, selm)
    idx_ref[...] = jnp.concatenate(picked, axis=0).astype(I32)
    wv = jnp.concatenate(vals, axis=0)
    w_ref[...] = wv / jnp.sum(wv, axis=0, keepdims=True) * ROUTE_SCALE
    hot = jnp.where(selm == neg, jnp.where(start == neg, 0.0, 1.0), 0.0)

    @pl.when(pl.program_id(0) == 0)
    def _():
        cnt_ref[...] = jnp.zeros_like(cnt_ref)

    cnt_ref[...] += jnp.sum(hot, axis=1, keepdims=True)


def _route(logits_t, bias, tn):
    n_exp, n = logits_t.shape
    col = lambda i: (0, i)
    fixed = lambda i: (0, 0)
    return pl.pallas_call(
        functools.partial(_route_kernel, n_exp=n_exp, tn=tn),
        out_shape=(jax.ShapeDtypeStruct((TOP_K, n), I32), jax.ShapeDtypeStruct((TOP_K, n), F32),
                   jax.ShapeDtypeStruct((n_exp, 1), F32)),
        grid=(n // tn,),
        in_specs=[pl.BlockSpec((n_exp, tn), col), pl.BlockSpec((n_exp, 1), fixed)],
        out_specs=(pl.BlockSpec((TOP_K, tn), col), pl.BlockSpec((TOP_K, tn), col),
                   pl.BlockSpec((n_exp, 1), fixed)),
        compiler_params=_params(("arbitrary",)),
        name="route",
    )(logits_t, bias.reshape(n_exp, 1))


def _dest_kernel(idx_ref, start_ref, dest_ref, carry_ref, *, n_exp, tn):
    @pl.when(pl.program_id(0) == 0)
    def _():
        carry_ref[...] = start_ref[...]

    idx = idx_ref[...]
    e_iota = lax.broadcasted_iota(I32, (n_exp, tn), 0)
    hot = jnp.zeros((n_exp, tn), F32)
    for k in range(TOP_K):
        hot = hot + jnp.where(e_iota == idx[k:k + 1, :], 1.0, 0.0)
    before = jnp.where(lax.broadcasted_iota(I32, (tn, tn), 0) < lax.broadcasted_iota(I32, (tn, tn), 1), 1.0, 0.0)
    rank = _dot(hot.astype(BF16), before.astype(BF16)) + carry_ref[...]
    dest_ref[...] = jnp.concatenate(
        [jnp.sum(jnp.where(e_iota == idx[k:k + 1, :], rank, 0.0), axis=0, keepdims=True)
         for k in range(TOP_K)], axis=0).astype(I32)
    carry_ref[...] += jnp.sum(hot, axis=1, keepdims=True)


def _dest(idx, start, tn):
    n = idx.shape[1]
    n_exp = start.shape[0]
    col = lambda i: (0, i)
    return pl.pallas_call(
        functools.partial(_dest_kernel, n_exp=n_exp, tn=tn),
        out_shape=jax.ShapeDtypeStruct((TOP_K, n), I32),
        grid=(n // tn,),
        in_specs=[pl.BlockSpec((TOP_K, tn), col), pl.BlockSpec((n_exp, 1), lambda i: (0, 0))],
        out_specs=pl.BlockSpec((TOP_K, tn), col),
        scratch_shapes=[pltpu.VMEM((n_exp, 1), F32)],
        compiler_params=_params(("arbitrary",)),
        name="dest",
    )(idx, start)


def _dispatch_kernel(dest_ref, hp_ref, xs_ref, sem, *, tm, pieces):
    def start(r, c):
        src = hp_ref.at[pl.ds(pl.multiple_of(r * pieces, pieces), pieces), :]
        for k in range(TOP_K):
            row = pl.multiple_of(dest_ref[r * TOP_K + k], pieces)
            pltpu.make_async_copy(src, xs_ref.at[pl.ds(row, pieces), :], sem).start(priority=k % 2)
        return c

    lax.fori_loop(0, tm, start, 0)
    for k in range(TOP_K):
        pltpu.make_async_copy(hp_ref, xs_ref.at[pl.ds(0, tm * pieces), :], sem).wait()


def _dispatch(dest, hp, cap, tm, pieces):
    n = hp.shape[0] // pieces
    return pl.pallas_call(
        functools.partial(_dispatch_kernel, tm=tm, pieces=pieces),
        out_shape=jax.ShapeDtypeStruct((cap * pieces, LANES), hp.dtype),
        grid=(n // tm,),
        in_specs=[pl.BlockSpec((tm * TOP_K,), lambda i: (i,), memory_space=pltpu.SMEM),
                  pl.BlockSpec((tm * pieces, LANES), lambda i: (i, 0))],
        out_specs=pl.BlockSpec(memory_space=pl.ANY),
        scratch_shapes=[pltpu.SemaphoreType.DMA(())],
        compiler_params=_params(("arbitrary",)),
        name="dispatch",
    )(dest, hp)


def _expert_kernel(cnt_ref, first_ref, tot_ref, xs_hbm, wg_ref, wu_ref, wd_ref, ys_hbm,
                   xbuf, ybuf, wgu_s, wd_s, xsem, ysem, *, rows, pieces):
    e = pl.program_id(0)
    cnt, first, total = cnt_ref[e], first_ref[e], tot_ref[0]
    n_blk = (cnt + rows - 1) // rows
    d_e = wg_ref.shape[2]
    phys = rows * pieces

    def block_rows(g):
        return pl.ds(pl.multiple_of(g * phys, phys), phys)

    def x_copy(g, slot):
        return pltpu.make_async_copy(xs_hbm.at[block_rows(g), :], xbuf.at[slot], xsem.at[slot])

    def y_copy(g, slot):
        return pltpu.make_async_copy(ybuf.at[slot], ys_hbm.at[block_rows(g), :], ysem.at[slot])

    x_slots, y_slots = xbuf.shape[0], ybuf.shape[0]
    ahead = x_slots - 1

    @pl.when(n_blk > 0)
    def _():
        @pl.when(first == 0)
        def _():
            for a in range(ahead):
                @pl.when(a < total)
                def _():
                    x_copy(a, a).start()

        wgu_s[:, :d_e] = wg_ref[0].astype(BF16)
        wgu_s[:, d_e:] = wu_ref[0].astype(BF16)
        wd_s[...] = wd_ref[0].astype(BF16)

        def block(j, c):
            g = first + j
            slot = g % x_slots
            x_copy(g, slot).wait()

            @pl.when(g + ahead < total)
            def _():
                x_copy(g + ahead, (g + ahead) % x_slots).start()

            yslot = g % y_slots

            @pl.when(g >= y_slots)
            def _():
                y_copy(g - y_slots, yslot).wait()

            x = _load_row_tiles(xbuf, 0, rows, pieces, lead=(slot,))
            r_iota = lax.broadcasted_iota(I32, x.shape, 0) + j * rows
            xb = jnp.where(r_iota < cnt, x, 0.0).astype(BF16)
            gu = _dot(xb, wgu_s[...])
            gate, up = gu[:, :d_e], gu[:, d_e:]
            hid = (gate * jax.nn.sigmoid(gate) * up).astype(BF16)
            y = _dot(hid, wd_s[...])
            for s in range(pieces):
                ybuf[yslot, pl.ds(s, rows, stride=pieces), :] = y[:, s * LANES:(s + 1) * LANES]
            y_copy(g, yslot).start()
            return c

        lax.fori_loop(0, n_blk, block, 0)

        @pl.when(first + n_blk == total)
        def _():
            for back in range(1, y_slots + 1):
                @pl.when(total >= back)
                def _():
                    y_copy(total - back, (total - back) % y_slots).wait()


def _experts(cnt, blk_first, n_used, xs, w_gate, w_up, w_down, rows):
    n_exp, d, d_e = w_gate.shape
    pieces = d // LANES
    hbm = pl.BlockSpec(memory_space=pl.ANY)
    w_map = lambda e, cnt, first, tot: (e, 0, 0)
    grid_spec = pltpu.PrefetchScalarGridSpec(
        num_scalar_prefetch=3,
        grid=(n_exp,),
        in_specs=[hbm, pl.BlockSpec((1, d, d_e), w_map), pl.BlockSpec((1, d, d_e), w_map),
                  pl.BlockSpec((1, d_e, d), w_map)],
        out_specs=hbm,
        scratch_shapes=[pltpu.VMEM((EXPERT_X_SLOTS, rows * pieces, LANES), F32),
                        pltpu.VMEM((EXPERT_Y_SLOTS, rows * pieces, LANES), F32),
                        pltpu.VMEM((d, 2 * d_e), BF16), pltpu.VMEM((d_e, d), BF16),
                        pltpu.SemaphoreType.DMA((EXPERT_X_SLOTS,)), pltpu.SemaphoreType.DMA((EXPERT_Y_SLOTS,))],
    )
    return pl.pallas_call(
        functools.partial(_expert_kernel, rows=rows, pieces=pieces),
        out_shape=jax.ShapeDtypeStruct(xs.shape, F32),
        grid_spec=grid_spec,
        compiler_params=_params(("arbitrary",)),
        name="experts",
    )(cnt, blk_first, n_used, xs, w_gate, w_up, w_down)


def _combine_kernel(dest_ref, next_ref, ys_ref, wt_ref, hp_ref, x1_ref, wsgu_ref, wsd_ref, npost_ref, gt_ref,
                    o_ref, buf, acc, sem, *, tm, pieces):
    i = pl.program_id(0)
    last = pl.num_programs(0) - 1
    cur = i % 2

    def start_token(rows_ref, half, r):
        slot = pl.ds(pl.multiple_of(r * pieces, pieces), pieces)
        for k in range(TOP_K):
            row = pl.multiple_of(rows_ref[r * TOP_K + k], pieces)
            pltpu.make_async_copy(ys_ref.at[pl.ds(row, pieces), :], buf.at[half, k, slot, :],
                                  sem.at[half]).start(priority=k % 2)

    def wait_half(half):
        for k in range(TOP_K):
            pltpu.make_async_copy(ys_ref.at[pl.ds(0, tm * pieces), :], buf.at[half, k], sem.at[half]).wait()

    @pl.when(i == 0)
    def _():
        def first(r, c):
            start_token(dest_ref, 0, r)
            return c

        lax.fori_loop(0, tm, first, 0)

    wait_half(cur)

    def token(r, c):
        start_token(next_ref, 1 - cur, r)
        slot = pl.ds(pl.multiple_of(r * pieces, pieces), pieces)
        terms = [wt_ref[r * TOP_K + k] * buf[cur, k, slot, :] for k in range(TOP_K)]
        while len(terms) > 1:
            terms = [terms[j] + terms[j + 1] for j in range(0, len(terms), 2)]
        acc[slot, :] = terms[0]
        return c

    lax.fori_loop(0, tm, token, 0, unroll=2)

    @pl.when(i == last)
    def _():
        wait_half(1 - cur)

    hb = _load_row_tiles(hp_ref, 0, tm, pieces).astype(BF16)
    d_sh = wsd_ref.shape[0]
    gu = _dot(hb, wsgu_ref[...])
    gate, up = gu[:, :d_sh], gu[:, d_sh:]
    shared = _dot((gate * jax.nn.sigmoid(gate) * up).astype(BF16), wsd_ref[...])
    fx = shared + _load_row_tiles(acc, 0, tm, pieces)
    o_ref[...] = x1_ref[...] + gt_ref[0] * _rms(fx, npost_ref[...])


def _combine(dest, ys, wt, hp, x1, wsgu, wsd, npost, gt, tokens_per_batch, tm):
    n, d = x1.shape
    pieces = d // LANES
    w = LANES
    tiles_per_batch = tokens_per_batch // tm
    row = lambda i: (i, 0)
    fixed = lambda i: (0, 0)
    return pl.pallas_call(
        functools.partial(_combine_kernel, tm=tm, pieces=pieces),
        out_shape=jax.ShapeDtypeStruct((n, d), F32),
        grid=(n // tm,),
        in_specs=[pl.BlockSpec((tm * TOP_K,), lambda i: (i,), memory_space=pltpu.SMEM),
                  pl.BlockSpec((tm * TOP_K,), lambda i: (jnp.minimum(i + 1, n // tm - 1),), memory_space=pltpu.SMEM),
                  pl.BlockSpec(memory_space=pl.ANY),
                  pl.BlockSpec((tm * TOP_K,), lambda i: (i,), memory_space=pltpu.SMEM),
                  pl.BlockSpec((tm * pieces, w), row), pl.BlockSpec((tm, d), row),
                  pl.BlockSpec(wsgu.shape, fixed), pl.BlockSpec(wsd.shape, fixed),
                  pl.BlockSpec((1, d), fixed), pl.BlockSpec((1, 1, d), lambda i: (i // tiles_per_batch, 0, 0))],
        out_specs=pl.BlockSpec((tm, d), row),
        scratch_shapes=[pltpu.VMEM((2, TOP_K, tm * pieces, w), F32), pltpu.VMEM((tm * pieces, w), F32),
                        pltpu.SemaphoreType.DMA((2,))],
        compiler_params=_params(("arbitrary",)),
        name="combine",
    )(dest, dest, ys, wt, hp, x1, wsgu, wsd, npost.reshape(1, d), gt)


def _layer(x, ctx, mod_x, mod_c, norm_pre_mix, norm_post_mix, norm_pre_ffn, norm_post_ffn,
           w_in, s5_a_re, s5_a_im, s5_log_dt, s5_b_re, s5_b_im, s5_c_re, s5_c_im, s5_d, w_glu, b_glu,
           sgu_norm, sgu_w, sgu_b, w_branch_s5, w_branch_sgu, w_out,
           w_router, router_bias, w_exp_gate, w_exp_up, w_exp_down, w_sh_gate, w_sh_up, w_sh_down):
    bsz, t_len, d = x.shape
    c_len = ctx.shape[1]
    n = bsz * t_len
    d_s5, d_sgu = w_glu.shape[0], sgu_norm.shape[0]
    g_n = d_s5 // S5_GROUP_CH
    n_exp = w_router.shape[1]
    chunk = S5_CHUNK
    tm = min(512, t_len)
    assert t_len % tm == 0 and tm % SGU_CHUNK == 0 and t_len % chunk == 0 and c_len % chunk == 0

    sh_m, sc_m, gt_m, sh_f, sc_f, gt_f = [v.reshape(bsz, 1, d) for v in jnp.split(mod_x, 6, axis=-1)]
    csh_m, csc_m = mod_c[:d], mod_c[d:2 * d]

    w_in_bf = w_in.astype(BF16)

    nc_lat, nc_ctx = t_len // chunk, c_len // chunk
    n_steps = max(1, (max(nc_lat, nc_ctx) - 1).bit_length())
    m_t, ke_t, kc_t, alp = _s5_tables(s5_a_re, s5_a_im, s5_log_dt, s5_b_re, s5_b_im, s5_c_re, s5_c_im,
                                      chunk, n_steps)

    ctx2 = ctx.reshape(bsz * c_len, d)
    u_ctx = _inproj_u(ctx2, norm_pre_mix, csh_m, csc_m, w_in_bf[:, :d_s5], min(512, bsz * c_len))
    zero_init = jnp.zeros((g_n, bsz, 4 * S5_STATE), F32)
    (ctx_final,) = _s5(_to_groups(u_ctx, bsz, nc_ctx, chunk, g_n), ke_t, alp, zero_init, nb=bsz, nc=nc_ctx)

    x2 = x.reshape(n, d)
    u, ut, gu, vn, g1, g2 = _inproj(x2, norm_pre_mix, sh_m, sc_m, w_in_bf, sgu_norm, t_len, tm, d_s5, d_sgu)
    _, yg = _s5(_to_groups(ut, bsz, nc_lat, chunk, g_n), ke_t, alp, ctx_final, m_t, kc_t, nb=bsz, nc=nc_lat)
    y = _from_groups(yg, bsz, nc_lat, chunk, g_n)

    ch = d_sgu // SGU_GROUPS
    per_tile = LANES // ch
    sguw = sgu_w.reshape(SGU_GROUPS // per_tile, per_tile, SGU_CHUNK, SGU_CHUNK)
    sguw = sguw.transpose(0, 2, 1, 3).reshape(SGU_GROUPS // per_tile, SGU_CHUNK, per_tile * SGU_CHUNK).astype(BF16)
    sgub = jnp.repeat(sgu_b.T, ch, axis=1)

    x1, hp, logits = _mix(y, u, gu, vn, g1, g2, x2, s5_d, w_glu.astype(BF16), b_glu, sguw, sgub,
                          w_branch_s5.astype(BF16), w_branch_sgu.astype(BF16), w_out.astype(BF16),
                          norm_post_mix, gt_m, norm_pre_ffn, sh_f, sc_f, w_router.astype(BF16), t_len, tm)

    tn = min(512, n)
    idx, wts, counts = _route(logits.T, router_bias, tn)
    rows = EXPERT_ROWS
    cnt = counts.reshape(n_exp).astype(I32)
    nblk = (cnt + rows - 1) // rows
    blk_end = jnp.cumsum(nblk)
    blk_start = blk_end - nblk
    cap = ((n * TOP_K) // rows + n_exp) * rows
    dest = _dest(idx, (blk_start * rows).astype(F32).reshape(n_exp, 1), tn)
    pieces = d // LANES
    dest = dest.T.reshape(n * TOP_K) * pieces

    xs = _dispatch(dest, hp, cap, min(512, t_len), pieces)
    ys = _experts(cnt, blk_start.astype(I32), blk_end[-1:].astype(I32), xs, w_exp_gate, w_exp_up, w_exp_down, rows)
    wsgu = jnp.concatenate([w_sh_gate, w_sh_up], axis=1).astype(BF16)
    out = _combine(dest, ys, wts.T.reshape(n * TOP_K), hp, x1, wsgu, w_sh_down.astype(BF16), norm_post_ffn,
                   gt_f, t_len, min(512, t_len))
    return out.reshape(bsz, t_len, d)


def kernel(x, c, ctx, c_ctx, w_mod, b_mod, norm_pre_mix, norm_post_mix, norm_pre_ffn, norm_post_ffn, w_in, s5_a_re, s5_a_im, s5_log_dt, s5_b_re, s5_b_im, s5_c_re, s5_c_im, s5_d, w_glu, b_glu, sgu_norm, sgu_w, sgu_b, w_branch_s5, w_branch_sgu, w_out, w_router, router_bias, w_exp_gate, w_exp_up, w_exp_down, w_sh_gate, w_sh_up, w_sh_down):
    depth = w_mod.shape[0]
    assert depth == 1, "the context stream is only carried through the last layer's S5 states"
    bsz = x.shape[0]
    pad = (-(bsz + 1)) % 8
    cpad = jnp.concatenate([c, c_ctx[None, :], jnp.zeros((pad, c.shape[1]), c.dtype)], axis=0)
    mod = _modulation(cpad, w_mod[0], b_mod[0])
    return _layer(x, ctx, mod[:bsz], mod[bsz], norm_pre_mix[0], norm_post_mix[0], norm_pre_ffn[0],
                  norm_post_ffn[0], w_in[0], s5_a_re[0], s5_a_im[0], s5_log_dt[0], s5_b_re[0], s5_b_im[0],
                  s5_c_re[0], s5_c_im[0], s5_d[0], w_glu[0], b_glu[0], sgu_norm[0], sgu_w[0], sgu_b[0],
                  w_branch_s5[0], w_branch_sgu[0], w_out[0], w_router[0], router_bias[0],
                  w_exp_gate[0], w_exp_up[0], w_exp_down[0], w_sh_gate[0], w_sh_up[0], w_sh_down[0])
```

```python
import functools
import math

import jax
import jax.numpy as jnp
from jax import lax
from jax.experimental import pallas as pl
from jax.experimental.pallas import tpu as pltpu

F32 = jnp.float32
BF16 = jnp.bfloat16
I32 = jnp.int32

EPS = 1e-6
S5_GROUP_CH = 16
S5_STATE = 64
S5_CHUNK = 64
SGU_GROUPS = 8
SGU_CHUNK = 128
N_EXPERT_GROUPS = 8
TOPK_GROUPS = 4
TOP_K = 8
ROUTE_SCALE = 2.5
LANES = 128
EXPERT_ROWS = 256
EXPERT_X_SLOTS = 4
EXPERT_Y_SLOTS = 3
VMEM_LIMIT = 56 * 1024 * 1024


def _params(sem):
    return pltpu.CompilerParams(dimension_semantics=sem, vmem_limit_bytes=VMEM_LIMIT)


def _rms(v, g):
    return v * lax.rsqrt(jnp.mean(v * v, axis=-1, keepdims=True) + EPS) * g


def _dot(a, b):
    return jnp.dot(a, b, preferred_element_type=F32)


def _store_row_tiles(ref, first_row, val):
    pieces = val.shape[1] // LANES
    for s in range(pieces):
        ref[pl.ds(first_row * pieces + s, val.shape[0], stride=pieces), :] = val[:, s * LANES:(s + 1) * LANES]


def _load_row_tiles(ref, first_row, n_rows, pieces, lead=()):
    return jnp.concatenate([ref[lead + (pl.ds(first_row * pieces + s, n_rows, stride=pieces), slice(None))]
                            for s in range(pieces)], axis=1)


def _mod_kernel(c_ref, w_ref, b_ref, o_ref):
    cv = c_ref[...]
    s = cv * jax.nn.sigmoid(cv)
    o_ref[...] = _dot(s.astype(BF16), w_ref[...].astype(BF16)) + b_ref[...]


def _modulation(cpad, w_mod, b_mod):
    d, n6 = w_mod.shape
    tn = 1024
    return pl.pallas_call(
        _mod_kernel,
        out_shape=jax.ShapeDtypeStruct((cpad.shape[0], n6), F32),
        grid=(n6 // tn,),
        in_specs=[pl.BlockSpec(cpad.shape, lambda j: (0, 0)),
                  pl.BlockSpec((d, tn), lambda j: (0, j)),
                  pl.BlockSpec((1, tn), lambda j: (0, j))],
        out_specs=pl.BlockSpec((cpad.shape[0], tn), lambda j: (0, j)),
        compiler_params=_params(("parallel",)),
        name="mod",
    )(cpad, w_mod, b_mod.reshape(1, n6))


def _inproj_kernel(x_ref, g_ref, sh_ref, sc_ref, w_ref, sgun_ref,
                   u_ref, ut_ref, gu_ref, vn_ref, g1_ref, g2_ref, *, d_s5, d_sgu, d_model):
    h = _rms(x_ref[...], g_ref[...])
    hb = (h * (1.0 + sc_ref[0]) + sh_ref[0]).astype(BF16)
    o1, o2, o3, o4 = d_s5, d_s5 + d_sgu, d_s5 + 2 * d_sgu, d_s5 + 2 * d_sgu + d_model
    u = _dot(hb, w_ref[:, 0:o1])
    u_ref[...] = u.astype(BF16)
    ut_ref[...] = u.T.astype(BF16)
    gu_ref[...] = jax.nn.gelu(_dot(hb, w_ref[:, o1:o2])).astype(BF16)
    v = jax.nn.gelu(_dot(hb, w_ref[:, o2:o3]))
    vn_ref[...] = _rms(v, sgun_ref[...]).astype(BF16)
    g1_ref[...] = jax.nn.sigmoid(_dot(hb, w_ref[:, o3:o4])).astype(BF16)
    g2_ref[...] = jax.nn.sigmoid(_dot(hb, w_ref[:, o4:o4 + d_model])).astype(BF16)


def _inproj(x2, norm_g, shift, scale, w_in_bf, sgu_norm, tokens_per_batch, tm, d_s5, d_sgu):
    n, d = x2.shape
    tiles_per_batch = tokens_per_batch // tm
    row = lambda i: (i, 0)
    fixed = lambda i: (0, 0)
    per_batch = lambda i: (i // tiles_per_batch, 0, 0)
    return pl.pallas_call(
        functools.partial(_inproj_kernel, d_s5=d_s5, d_sgu=d_sgu, d_model=d),
        out_shape=(jax.ShapeDtypeStruct((n, d_s5), BF16), jax.ShapeDtypeStruct((d_s5, n), BF16),
                   jax.ShapeDtypeStruct((n, d_sgu), BF16),
                   jax.ShapeDtypeStruct((n, d_sgu), BF16), jax.ShapeDtypeStruct((n, d), BF16),
                   jax.ShapeDtypeStruct((n, d), BF16)),
        grid=(n // tm,),
        in_specs=[pl.BlockSpec((tm, d), row), pl.BlockSpec((1, d), fixed),
                  pl.BlockSpec((1, 1, d), per_batch), pl.BlockSpec((1, 1, d), per_batch),
                  pl.BlockSpec(w_in_bf.shape, fixed), pl.BlockSpec((1, d_sgu), fixed)],
        out_specs=(pl.BlockSpec((tm, d_s5), row), pl.BlockSpec((d_s5, tm), lambda i: (0, i)),
                   pl.BlockSpec((tm, d_sgu), row),
                   pl.BlockSpec((tm, d_sgu), row), pl.BlockSpec((tm, d), row), pl.BlockSpec((tm, d), row)),
        compiler_params=_params(("parallel",)),
        name="inproj",
    )(x2, norm_g.reshape(1, d), shift, scale, w_in_bf, sgu_norm.reshape(1, d_sgu))


def _inproj_u_kernel(x_ref, g_ref, sh_ref, sc_ref, w_ref, u_ref):
    h = _rms(x_ref[...], g_ref[...])
    hb = (h * (1.0 + sc_ref[...]) + sh_ref[...]).astype(BF16)
    u_ref[...] = _dot(hb, w_ref[...]).T.astype(BF16)


def _inproj_u(x2, norm_g, shift, scale, w_u_bf, tm):
    n, d = x2.shape
    d_s5 = w_u_bf.shape[1]
    row = lambda i: (i, 0)
    fixed = lambda i: (0, 0)
    return pl.pallas_call(
        _inproj_u_kernel,
        out_shape=jax.ShapeDtypeStruct((d_s5, n), BF16),
        grid=(n // tm,),
        in_specs=[pl.BlockSpec((tm, d), row), pl.BlockSpec((1, d), fixed), pl.BlockSpec((1, d), fixed),
                  pl.BlockSpec((1, d), fixed), pl.BlockSpec(w_u_bf.shape, fixed)],
        out_specs=pl.BlockSpec((d_s5, tm), lambda i: (0, i)),
        compiler_params=_params(("parallel",)),
        name="inproj_ctx",
    )(x2, norm_g.reshape(1, d), shift.reshape(1, d), scale.reshape(1, d), w_u_bf)


def _s5_tables(a_re, a_im, log_dt, b_re, b_im, c_re, c_im, chunk, n_steps):
    hi = lax.Precision.HIGHEST
    g_n, p_n, j_n = a_re.shape[1], a_re.shape[2], b_re.shape[3]
    dt = jnp.exp(log_dt)[..., None]
    lam_re, lam_im = dt * a_re, dt * a_im
    ab_re, ab_im = jnp.exp(lam_re) * jnp.cos(lam_im), jnp.exp(lam_re) * jnp.sin(lam_im)
    den = a_re * a_re + a_im * a_im
    q_re = ((ab_re - 1.0) * a_re + ab_im * a_im) / den
    q_im = (ab_im * a_re - (ab_re - 1.0) * a_im) / den
    bb_re = q_re[..., None] * b_re - q_im[..., None] * b_im
    bb_im = q_re[..., None] * b_im + q_im[..., None] * b_re
    k = jnp.arange(chunk + 1, dtype=F32)[:, None, None, None]
    mag = jnp.exp(k * lam_re[None])
    pw_re, pw_im = mag * jnp.cos(k * lam_im[None]), mag * jnp.sin(k * lam_im[None])

    ct_re, ct_im = c_re.transpose(0, 1, 3, 2), c_im.transpose(0, 1, 3, 2)

    def lag_kernels(d):
        w_re = bb_re[d][:, :, :, None] * ct_re[d][:, :, None, :] - bb_im[d][:, :, :, None] * ct_im[d][:, :, None, :]
        w_im = bb_re[d][:, :, :, None] * ct_im[d][:, :, None, :] + bb_im[d][:, :, :, None] * ct_re[d][:, :, None, :]
        w = jnp.concatenate([w_re, -w_im], axis=1).reshape(g_n, 2 * p_n, j_n * j_n)
        a = jnp.concatenate([pw_re[:chunk, d], pw_im[:chunk, d]], axis=-1).transpose(1, 0, 2)
        return jnp.einsum('gkp,gpn->gkn', a, w, precision=hi)

    kf, kb = lag_kernels(0), lag_kernels(1)
    lags = jnp.concatenate([kb[:, 1:][:, ::-1], kf[:, 0:1] + kb[:, 0:1], kf[:, 1:],
                            jnp.zeros_like(kf[:, 0:1])], axis=1)
    lags = lags.transpose(0, 2, 1).reshape(g_n, j_n, j_n, 2 * chunk)

    def both(f, b, axis):
        return jnp.concatenate([f, b], axis=axis)

    pt_re = both(pw_re[:chunk, 0][::-1], pw_re[:chunk, 1], -1).transpose(1, 0, 2)
    pt_im = both(pw_im[:chunk, 0][::-1], pw_im[:chunk, 1], -1).transpose(1, 0, 2)
    bt_re = both(bb_re[0].transpose(0, 2, 1), bb_re[1].transpose(0, 2, 1), -1)
    bt_im = both(bb_im[0].transpose(0, 2, 1), bb_im[1].transpose(0, 2, 1), -1)
    e_re = pt_re[:, None] * bt_re[:, :, None] - pt_im[:, None] * bt_im[:, :, None]
    e_im = pt_re[:, None] * bt_im[:, :, None] + pt_im[:, None] * bt_re[:, :, None]
    ke = jnp.concatenate([e_re, e_im], axis=-1).reshape(g_n, j_n * chunk, 4 * p_n)

    q_re = both(pw_re[1:chunk + 1, 0], pw_re[1:chunk + 1, 1][::-1], -1).transpose(1, 2, 0)
    q_im = both(pw_im[1:chunk + 1, 0], pw_im[1:chunk + 1, 1][::-1], -1).transpose(1, 2, 0)
    q_re, q_im = jnp.tile(q_re, (1, 1, j_n)), jnp.tile(q_im, (1, 1, j_n))
    cc_re = jnp.repeat(both(ct_re[0], ct_re[1], 1), chunk, axis=2)
    cc_im = jnp.repeat(both(ct_im[0], ct_im[1], 1), chunk, axis=2)
    kc = jnp.concatenate([q_re * cc_re - q_im * cc_im, -(q_re * cc_im + q_im * cc_re)], axis=1)

    e = (chunk * (2.0 ** jnp.arange(n_steps, dtype=F32)))[:, None, None, None]
    mg = jnp.exp(e * lam_re[None])
    al_re, al_im = mg * jnp.cos(e * lam_im[None]), mg * jnp.sin(e * lam_im[None])
    alp = jnp.concatenate([al_re[:, 0], al_re[:, 1], al_im[:, 0], al_im[:, 1]], axis=-1)
    return lags, ke.astype(BF16), kc.astype(BF16), alp.transpose(1, 0, 2)


def _build_toeplitz(lag_ref, m_s, chunk):
    j_n = lag_ref.shape[1]
    per_tile = LANES // chunk
    low = lax.broadcasted_iota(I32, (chunk, LANES), 1) < chunk
    for i in range(j_n):
        for q in range(j_n // per_tile):
            tile = None
            for p in range(per_tile):
                j = q * per_tile + p
                row = jnp.broadcast_to(lag_ref[0, i, j:j + 1, :], (chunk, LANES))
                rot = pltpu.roll(row, (p * chunk + LANES - (chunk - 1)) % LANES, 1, stride=1, stride_axis=0)
                tile = rot if tile is None else jnp.where(low, tile, rot)
            m_s[i * chunk:(i + 1) * chunk, q * LANES:(q + 1) * LANES] = tile.astype(BF16)


def _s5_kernel(*refs, nb, nc, readout):
    if readout:
        u_ref, ke_ref, alp_ref, init_ref, lag_ref, kc_ref, fin_ref, y_ref, m_s = refs
    else:
        u_ref, ke_ref, alp_ref, init_ref, fin_ref = refs
    rows = nb * nc
    half = 2 * S5_STATE
    u = u_ref[0]
    e = _dot(u, ke_ref[0])
    er, ei = e[:, :half], e[:, half:]
    r_idx = lax.broadcasted_iota(I32, (rows, half), 0)
    if nc & (nc - 1) == 0:
        n_idx, b_idx = r_idx & (nc - 1), r_idx >> (nc.bit_length() - 1)
    else:
        n_idx, b_idx = lax.rem(r_idx, nc), lax.div(r_idx, nc)
    is_f = lax.broadcasted_iota(I32, (rows, half), 1) < S5_STATE
    seen = jnp.where(is_f, n_idx, nc - 1 - n_idx)
    init = init_ref[0]
    init_r = jnp.zeros((rows, half), F32)
    init_i = jnp.zeros((rows, half), F32)
    for b in range(nb):
        init_r = jnp.where(b_idx == b, init[b:b + 1, :half], init_r)
        init_i = jnp.where(b_idx == b, init[b:b + 1, half:], init_i)
    alp = alp_ref[0]
    ar, ai = alp[0:1, :half], alp[0:1, half:]
    er = er + jnp.where(seen == 0, ar * init_r - ai * init_i, 0.0)
    ei = ei + jnp.where(seen == 0, ar * init_i + ai * init_r, 0.0)

    def from_prev(v, dist):
        return jnp.where(is_f, pltpu.roll(v, dist, 0), pltpu.roll(v, rows - dist, 0))

    d, s = 1, 0
    while d < nc:
        ar, ai = alp[s:s + 1, :half], alp[s:s + 1, half:]
        sr = jnp.where(seen >= d, from_prev(er, d), 0.0)
        si = jnp.where(seen >= d, from_prev(ei, d), 0.0)
        er, ei = er + (ar * sr - ai * si), ei + (ar * si + ai * sr)
        d, s = d * 2, s + 1
    is_f_row = lax.broadcasted_iota(I32, (1, half), 1) < S5_STATE
    for b in range(nb):
        lo, hi = b * nc, b * nc + nc - 1
        fin_ref[0, b:b + 1, :half] = jnp.where(is_f_row, er[hi:hi + 1], er[lo:lo + 1])
        fin_ref[0, b:b + 1, half:] = jnp.where(is_f_row, ei[hi:hi + 1], ei[lo:lo + 1])
    if readout:
        if nc > 1:
            sin_r = jnp.where(seen >= 1, from_prev(er, 1), init_r)
            sin_i = jnp.where(seen >= 1, from_prev(ei, 1), init_i)
        else:
            sin_r, sin_i = init_r, init_i
        s_in = jnp.concatenate([sin_r, sin_i], axis=1).astype(BF16)
        _build_toeplitz(lag_ref, m_s, S5_CHUNK)
        y_ref[0] = (_dot(u, m_s[...]) + _dot(s_in, kc_ref[0])).astype(BF16)


def _s5(ug, ke, alp, init, m=None, kc=None, *, nb, nc):
    g_n, rows, width = ug.shape
    readout = m is not None
    grp = lambda g: (g, 0, 0)
    in_specs = [pl.BlockSpec((1, rows, width), grp), pl.BlockSpec((1,) + ke.shape[1:], grp),
                pl.BlockSpec((1,) + alp.shape[1:], grp), pl.BlockSpec((1,) + init.shape[1:], grp)]
    out_shape = [jax.ShapeDtypeStruct(init.shape, F32)]
    out_specs = [pl.BlockSpec((1,) + init.shape[1:], grp)]
    args = [ug, ke, alp, init]
    scratch = []
    if readout:
        assert m.shape[3] == LANES and 2 * S5_CHUNK == LANES
        in_specs += [pl.BlockSpec((1,) + m.shape[1:], lambda g: (g, 0, 0, 0)),
                     pl.BlockSpec((1,) + kc.shape[1:], grp)]
        out_shape.append(jax.ShapeDtypeStruct(ug.shape, BF16))
        out_specs.append(pl.BlockSpec((1, rows, width), grp))
        args += [m, kc]
        scratch = [pltpu.VMEM((width, width), BF16)]
    return pl.pallas_call(
        functools.partial(_s5_kernel, nb=nb, nc=nc, readout=readout),
        out_shape=tuple(out_shape), grid=(g_n,), in_specs=in_specs, out_specs=tuple(out_specs),
        scratch_shapes=scratch,
        compiler_params=_params(("parallel",)),
        name="s5_readout" if readout else "s5_state",
    )(*args)


def _to_groups(ut, nb, nc, chunk, g_n):
    j_n = ut.shape[0] // g_n
    return ut.reshape(g_n, j_n, nb * nc, chunk).transpose(0, 2, 1, 3).reshape(g_n, nb * nc, j_n * chunk)


def _from_groups(y, nb, nc, chunk, g_n):
    j_n = y.shape[2] // chunk
    return y.reshape(g_n, nb * nc, j_n, chunk).transpose(0, 2, 1, 3).reshape(g_n * j_n, nb * nc * chunk)


def _mix_kernel(y_ref, u_ref, gu_ref, vn_ref, g1_ref, g2_ref, x_ref,
                d_ref, wglu_ref, bglu_ref, sguw_ref, sgub_ref, wb1_ref, wb2_ref, wout_ref,
                npost_ref, gt_ref, npre_ref, shf_ref, scf_ref, wr_ref,
                x1_ref, hp_ref, lg_ref, *, tm):
    y = jax.nn.gelu(y_ref[...].astype(F32).T + d_ref[...] * u_ref[...].astype(F32))
    y_s5 = y * jax.nn.sigmoid(_dot(y.astype(BF16), wglu_ref[...]) + bglu_ref[...])

    lane = lax.broadcasted_iota(I32, (1, LANES), 1)
    m_lo = (lane < LANES // 2).astype(F32).astype(BF16)
    m_hi = (lane >= LANES // 2).astype(F32).astype(BF16)
    vn = vn_ref[...]
    chunks = []
    for c in range(tm // SGU_CHUNK):
        vc = vn[c * SGU_CHUNK:(c + 1) * SGU_CHUNK, :]
        tiles = []
        for q in range(vc.shape[1] // LANES):
            vt = vc[:, q * LANES:(q + 1) * LANES]
            rhs = jnp.concatenate([vt * m_lo, vt * m_hi], axis=0)
            tiles.append(_dot(sguw_ref[q], rhs))
        chunks.append(jnp.concatenate(tiles, axis=1) + sgub_ref[...])
    mixed = jnp.concatenate(chunks, axis=0)
    y_sgu = gu_ref[...].astype(F32) * mixed

    merged = (g1_ref[...].astype(F32) * _dot(y_s5.astype(BF16), wb1_ref[...])
              + g2_ref[...].astype(F32) * _dot(y_sgu.astype(BF16), wb2_ref[...]))
    mx = _dot(merged.astype(BF16), wout_ref[...])
    x1 = x_ref[...] + gt_ref[0] * _rms(mx, npost_ref[...])
    x1_ref[...] = x1
    hp = _rms(x1, npre_ref[...]) * (1.0 + scf_ref[0]) + shf_ref[0]
    _store_row_tiles(hp_ref, 0, hp)
    lg_ref[...] = _dot(hp.astype(BF16), wr_ref[...])


def _mix(y, u, gu, vn, g1, g2, x2, s5_d, wglu, bglu, sguw, sgub, wb1, wb2, wout,
         npost, gt, npre, shf, scf, wr, tokens_per_batch, tm):
    n, d = x2.shape
    d_s5, d_sgu, n_exp = y.shape[0], gu.shape[1], wr.shape[1]
    tiles_per_batch = tokens_per_batch // tm
    row = lambda i: (i, 0)
    fixed = lambda i: (0, 0)
    fixed3 = lambda i: (0, 0, 0)
    per_batch = lambda i: (i // tiles_per_batch, 0, 0)
    vec = lambda w: pl.BlockSpec((1, w), fixed)
    return pl.pallas_call(
        functools.partial(_mix_kernel, tm=tm),
        out_shape=(jax.ShapeDtypeStruct((n, d), F32), jax.ShapeDtypeStruct((n * d // LANES, LANES), F32),
                   jax.ShapeDtypeStruct((n, n_exp), F32)),
        grid=(n // tm,),
        in_specs=[pl.BlockSpec((d_s5, tm), lambda i: (0, i)), pl.BlockSpec((tm, d_s5), row),
                  pl.BlockSpec((tm, d_sgu), row),
                  pl.BlockSpec((tm, d_sgu), row), pl.BlockSpec((tm, d), row), pl.BlockSpec((tm, d), row),
                  pl.BlockSpec((tm, d), row),
                  vec(d_s5), pl.BlockSpec(wglu.shape, fixed), vec(d_s5),
                  pl.BlockSpec(sguw.shape, fixed3), pl.BlockSpec(sgub.shape, fixed),
                  pl.BlockSpec(wb1.shape, fixed), pl.BlockSpec(wb2.shape, fixed), pl.BlockSpec(wout.shape, fixed),
                  vec(d), pl.BlockSpec((1, 1, d), per_batch), vec(d),
                  pl.BlockSpec((1, 1, d), per_batch), pl.BlockSpec((1, 1, d), per_batch),
                  pl.BlockSpec(wr.shape, fixed)],
        out_specs=(pl.BlockSpec((tm, d), row), pl.BlockSpec((tm * d // LANES, LANES), row),
                   pl.BlockSpec((tm, n_exp), row)),
        compiler_params=_params(("parallel",)),
        name="mix",
    )(y, u, gu, vn, g1, g2, x2, s5_d.reshape(1, d_s5), wglu, bglu.reshape(1, d_s5), sguw, sgub,
      wb1, wb2, wout, npost.reshape(1, d), gt, npre.reshape(1, d), shf, scf, wr)


def _route_kernel(lg_ref, bias_ref, idx_ref, w_ref, cnt_ref, *, n_exp, tn):
    per_group = n_exp // N_EXPERT_GROUPS
    neg = jnp.float32(-jnp.inf)

    scores = jax.nn.sigmoid(lg_ref[...])
    sel = scores + bias_ref[...]
    gs = []
    for g in range(N_EXPERT_GROUPS):
        sg = sel[g * per_group:(g + 1) * per_group, :]
        m1 = jnp.max(sg, axis=0, keepdims=True)
        is_m1 = sg == m1
        n_m1 = jnp.sum(jnp.where(is_m1, 1.0, 0.0), axis=0, keepdims=True)
        rest = jnp.max(jnp.where(is_m1, neg, sg), axis=0, keepdims=True)
        gs.append(m1 + jnp.where(n_m1 >= 2.0, m1, rest))
    gsm = jnp.concatenate(gs, axis=0)
    g_iota = lax.broadcasted_iota(I32, gsm.shape, 0)
    e_iota = lax.broadcasted_iota(I32, sel.shape, 0).astype(F32)
    masked = []
    for g in range(N_EXPERT_GROUPS):
        mine = gsm[g:g + 1, :]
        beats = jnp.where(gsm > mine, 1.0, jnp.where(gsm == mine, jnp.where(g_iota < g, 1.0, 0.0), 0.0))
        n_beats = jnp.sum(beats, axis=0, keepdims=True)
        masked.append(jnp.where(n_beats < float(TOPK_GROUPS), sel[g * per_group:(g + 1) * per_group, :], neg))
    start = jnp.concatenate(masked, axis=0)
    selm = start
    picked, vals = [], []
    for _ in range(TOP_K):
        m = jnp.max(selm, axis=0, keepdims=True)
        first = jnp.min(jnp.where(selm == m, e_iota, float(n_exp)), axis=0, keepdims=True)
        one = e_iota == first
        picked.append(first)
        vals.append(jnp.sum(jnp.where(one, scores, 0.0), axis=0, keepdims=True))
        selm = jnp.where(one, neg, selm)
    idx_ref[...] = jnp.concatenate(picked, axis=0).astype(I32)
    wv = jnp.concatenate(vals, axis=0)
    w_ref[...] = wv / jnp.sum(wv, axis=0, keepdims=True) * ROUTE_SCALE
    hot = jnp.where(selm == neg, jnp.where(start == neg, 0.0, 1.0), 0.0)

    @pl.when(pl.program_id(0) == 0)
    def _():
        cnt_ref[...] = jnp.zeros_like(cnt_ref)

    cnt_ref[...] += jnp.sum(hot, axis=1, keepdims=True)


def _route(logits_t, bias, tn):
    n_exp, n = logits_t.shape
    col = lambda i: (0, i)
    fixed = lambda i: (0, 0)
    return pl.pallas_call(
        functools.partial(_route_kernel, n_exp=n_exp, tn=tn),
        out_shape=(jax.ShapeDtypeStruct((TOP_K, n), I32), jax.ShapeDtypeStruct((TOP_K, n), F32),
                   jax.ShapeDtypeStruct((n_exp, 1), F32)),
        grid=(n // tn,),
        in_specs=[pl.BlockSpec((n_exp, tn), col), pl.BlockSpec((n_exp, 1), fixed)],
        out_specs=(pl.BlockSpec((TOP_K, tn), col), pl.BlockSpec((TOP_K, tn), col),
                   pl.BlockSpec((n_exp, 1), fixed)),
        compiler_params=_params(("arbitrary",)),
        name="route",
    )(logits_t, bias.reshape(n_exp, 1))


def _dest_kernel(idx_ref, start_ref, dest_ref, carry_ref, *, n_exp, tn):
    @pl.when(pl.program_id(0) == 0)
    def _():
        carry_ref[...] = start_ref[...]

    idx = idx_ref[...]
    e_iota = lax.broadcasted_iota(I32, (n_exp, tn), 0)
    hot = jnp.zeros((n_exp, tn), F32)
    for k in range(TOP_K):
        hot = hot + jnp.where(e_iota == idx[k:k + 1, :], 1.0, 0.0)
    before = jnp.where(lax.broadcasted_iota(I32, (tn, tn), 0) < lax.broadcasted_iota(I32, (tn, tn), 1), 1.0, 0.0)
    rank = _dot(hot.astype(BF16), before.astype(BF16)) + carry_ref[...]
    dest_ref[...] = jnp.concatenate(
        [jnp.sum(jnp.where(e_iota == idx[k:k + 1, :], rank, 0.0), axis=0, keepdims=True)
         for k in range(TOP_K)], axis=0).astype(I32)
    carry_ref[...] += jnp.sum(hot, axis=1, keepdims=True)


def _dest(idx, start, tn):
    n = idx.shape[1]
    n_exp = start.shape[0]
    col = lambda i: (0, i)
    return pl.pallas_call(
        functools.partial(_dest_kernel, n_exp=n_exp, tn=tn),
        out_shape=jax.ShapeDtypeStruct((TOP_K, n), I32),
        grid=(n // tn,),
        in_specs=[pl.BlockSpec((TOP_K, tn), col), pl.BlockSpec((n_exp, 1), lambda i: (0, 0))],
        out_specs=pl.BlockSpec((TOP_K, tn), col),
        scratch_shapes=[pltpu.VMEM((n_exp, 1), F32)],
        compiler_params=_params(("arbitrary",)),
        name="dest",
    )(idx, start)


def _dispatch_kernel(dest_ref, hp_ref, xs_ref, sem, *, tm, pieces):
    def start(r, c):
        src = hp_ref.at[pl.ds(pl.multiple_of(r * pieces, pieces), pieces), :]
        for k in range(TOP_K):
            row = pl.multiple_of(dest_ref[r * TOP_K + k], pieces)
            pltpu.make_async_copy(src, xs_ref.at[pl.ds(row, pieces), :], sem).start(priority=k % 2)
        return c

    lax.fori_loop(0, tm, start, 0)
    for k in range(TOP_K):
        pltpu.make_async_copy(hp_ref, xs_ref.at[pl.ds(0, tm * pieces), :], sem).wait()


def _dispatch(dest, hp, cap, tm, pieces):
    n = hp.shape[0] // pieces
    return pl.pallas_call(
        functools.partial(_dispatch_kernel, tm=tm, pieces=pieces),
        out_shape=jax.ShapeDtypeStruct((cap * pieces, LANES), hp.dtype),
        grid=(n // tm,),
        in_specs=[pl.BlockSpec((tm * TOP_K,), lambda i: (i,), memory_space=pltpu.SMEM),
                  pl.BlockSpec((tm * pieces, LANES), lambda i: (i, 0))],
        out_specs=pl.BlockSpec(memory_space=pl.ANY),
        scratch_shapes=[pltpu.SemaphoreType.DMA(())],
        compiler_params=_params(("arbitrary",)),
        name="dispatch",
    )(dest, hp)


def _expert_kernel(cnt_ref, first_ref, tot_ref, xs_hbm, wg_ref, wu_ref, wd_ref, ys_hbm,
                   xbuf, ybuf, wgu_s, wd_s, xsem, ysem, *, rows, pieces):
    e = pl.program_id(0)
    cnt, first, total = cnt_ref[e], first_ref[e], tot_ref[0]
    n_blk = (cnt + rows - 1) // rows
    d_e = wg_ref.shape[2]
    phys = rows * pieces

    def block_rows(g):
        return pl.ds(pl.multiple_of(g * phys, phys), phys)

    def x_copy(g, slot):
        return pltpu.make_async_copy(xs_hbm.at[block_rows(g), :], xbuf.at[slot], xsem.at[slot])

    def y_copy(g, slot):
        return pltpu.make_async_copy(ybuf.at[slot], ys_hbm.at[block_rows(g), :], ysem.at[slot])

    x_slots, y_slots = xbuf.shape[0], ybuf.shape[0]
    ahead = x_slots - 1

    @pl.when(n_blk > 0)
    def _():
        @pl.when(first == 0)
        def _():
            for a in range(ahead):
                @pl.when(a < total)
                def _():
                    x_copy(a, a).start()

        wgu_s[:, :d_e] = wg_ref[0].astype(BF16)
        wgu_s[:, d_e:] = wu_ref[0].astype(BF16)
        wd_s[...] = wd_ref[0].astype(BF16)

        def block(j, c):
            g = first + j
            slot = g % x_slots
            x_copy(g, slot).wait()

            @pl.when(g + ahead < total)
            def _():
                x_copy(g + ahead, (g + ahead) % x_slots).start()

            yslot = g % y_slots

            @pl.when(g >= y_slots)
            def _():
                y_copy(g - y_slots, yslot).wait()

            x = _load_row_tiles(xbuf, 0, rows, pieces, lead=(slot,))
            r_iota = lax.broadcasted_iota(I32, x.shape, 0) + j * rows
            xb = jnp.where(r_iota < cnt, x, 0.0).astype(BF16)
            gu = _dot(xb, wgu_s[...])
            gate, up = gu[:, :d_e], gu[:, d_e:]
            hid = (gate * jax.nn.sigmoid(gate) * up).astype(BF16)
            y = _dot(hid, wd_s[...])
            for s in range(pieces):
                ybuf[yslot, pl.ds(s, rows, stride=pieces), :] = y[:, s * LANES:(s + 1) * LANES]
            y_copy(g, yslot).start()
            return c

        lax.fori_loop(0, n_blk, block, 0)

        @pl.when(first + n_blk == total)
        def _():
            for back in range(1, y_slots + 1):
                @pl.when(total >= back)
                def _():
                    y_copy(total - back, (total - back) % y_slots).wait()


def _experts(cnt, blk_first, n_used, xs, w_gate, w_up, w_down, rows):
    n_exp, d, d_e = w_gate.shape
    pieces = d // LANES
    hbm = pl.BlockSpec(memory_space=pl.ANY)
    w_map = lambda e, cnt, first, tot: (e, 0, 0)
    grid_spec = pltpu.PrefetchScalarGridSpec(
        num_scalar_prefetch=3,
        grid=(n_exp,),
        in_specs=[hbm, pl.BlockSpec((1, d, d_e), w_map), pl.BlockSpec((1, d, d_e), w_map),
                  pl.BlockSpec((1, d_e, d), w_map)],
        out_specs=hbm,
        scratch_shapes=[pltpu.VMEM((EXPERT_X_SLOTS, rows * pieces, LANES), F32),
                        pltpu.VMEM((EXPERT_Y_SLOTS, rows * pieces, LANES), F32),
                        pltpu.VMEM((d, 2 * d_e), BF16), pltpu.VMEM((d_e, d), BF16),
                        pltpu.SemaphoreType.DMA((EXPERT_X_SLOTS,)), pltpu.SemaphoreType.DMA((EXPERT_Y_SLOTS,))],
    )
    return pl.pallas_call(
        functools.partial(_expert_kernel, rows=rows, pieces=pieces),
        out_shape=jax.ShapeDtypeStruct(xs.shape, F32),
        grid_spec=grid_spec,
        compiler_params=_params(("arbitrary",)),
        name="experts",
    )(cnt, blk_first, n_used, xs, w_gate, w_up, w_down)


def _combine_kernel(dest_ref, next_ref, ys_ref, wt_ref, hp_ref, x1_ref, wsgu_ref, wsd_ref, npost_ref, gt_ref,
                    o_ref, buf, acc, sem, *, tm, pieces):
    i = pl.program_id(0)
    last = pl.num_programs(0) - 1
    cur = i % 2

    def start_token(rows_ref, half, r):
        slot = pl.ds(pl.multiple_of(r * pieces, pieces), pieces)
        for k in range(TOP_K):
            row = pl.multiple_of(rows_ref[r * TOP_K + k], pieces)
            pltpu.make_async_copy(ys_ref.at[pl.ds(row, pieces), :], buf.at[half, k, slot, :],
                                  sem.at[half]).start(priority=k % 2)

    def wait_half(half):
        for k in range(TOP_K):
            pltpu.make_async_copy(ys_ref.at[pl.ds(0, tm * pieces), :], buf.at[half, k], sem.at[half]).wait()

    @pl.when(i == 0)
    def _():
        def first(r, c):
            start_token(dest_ref, 0, r)
            return c

        lax.fori_loop(0, tm, first, 0)

    wait_half(cur)

    def token(r, c):
        start_token(next_ref, 1 - cur, r)
        slot = pl.ds(pl.multiple_of(r * pieces, pieces), pieces)
        terms = [wt_ref[r * TOP_K + k] * buf[cur, k, slot, :] for k in range(TOP_K)]
        while len(terms) > 1:
            terms = [terms[j] + terms[j + 1] for j in range(0, len(terms), 2)]
        acc[slot, :] = terms[0]
        return c

    lax.fori_loop(0, tm, token, 0, unroll=4)

    @pl.when(i == last)
    def _():
        wait_half(1 - cur)

    hb = _load_row_tiles(hp_ref, 0, tm, pieces).astype(BF16)
    d_sh = wsd_ref.shape[0]
    gu = _dot(hb, wsgu_ref[...])
    gate, up = gu[:, :d_sh], gu[:, d_sh:]
    shared = _dot((gate * jax.nn.sigmoid(gate) * up).astype(BF16), wsd_ref[...])
    fx = shared + _load_row_tiles(acc, 0, tm, pieces)
    o_ref[...] = x1_ref[...] + gt_ref[0] * _rms(fx, npost_ref[...])


def _combine(dest, ys, wt, hp, x1, wsgu, wsd, npost, gt, tokens_per_batch, tm):
    n, d = x1.shape
    pieces = d // LANES
    w = LANES
    tiles_per_batch = tokens_per_batch // tm
    row = lambda i: (i, 0)
    fixed = lambda i: (0, 0)
    return pl.pallas_call(
        functools.partial(_combine_kernel, tm=tm, pieces=pieces),
        out_shape=jax.ShapeDtypeStruct((n, d), F32),
        grid=(n // tm,),
        in_specs=[pl.BlockSpec((tm * TOP_K,), lambda i: (i,), memory_space=pltpu.SMEM),
                  pl.BlockSpec((tm * TOP_K,), lambda i: (jnp.minimum(i + 1, n // tm - 1),), memory_space=pltpu.SMEM),
                  pl.BlockSpec(memory_space=pl.ANY),
                  pl.BlockSpec((tm * TOP_K,), lambda i: (i,), memory_space=pltpu.SMEM),
                  pl.BlockSpec((tm * pieces, w), row), pl.BlockSpec((tm, d), row),
                  pl.BlockSpec(wsgu.shape, fixed), pl.BlockSpec(wsd.shape, fixed),
                  pl.BlockSpec((1, d), fixed), pl.BlockSpec((1, 1, d), lambda i: (i // tiles_per_batch, 0, 0))],
        out_specs=pl.BlockSpec((tm, d), row),
        scratch_shapes=[pltpu.VMEM((2, TOP_K, tm * pieces, w), F32), pltpu.VMEM((tm * pieces, w), F32),
                        pltpu.SemaphoreType.DMA((2,))],
        compiler_params=_params(("arbitrary",)),
        name="combine",
    )(dest, dest, ys, wt, hp, x1, wsgu, wsd, npost.reshape(1, d), gt)


def _layer(x, ctx, mod_x, mod_c, norm_pre_mix, norm_post_mix, norm_pre_ffn, norm_post_ffn,
           w_in, s5_a_re, s5_a_im, s5_log_dt, s5_b_re, s5_b_im, s5_c_re, s5_c_im, s5_d, w_glu, b_glu,
           sgu_norm, sgu_w, sgu_b, w_branch_s5, w_branch_sgu, w_out,
           w_router, router_bias, w_exp_gate, w_exp_up, w_exp_down, w_sh_gate, w_sh_up, w_sh_down):
    bsz, t_len, d = x.shape
    c_len = ctx.shape[1]
    n = bsz * t_len
    d_s5, d_sgu = w_glu.shape[0], sgu_norm.shape[0]
    g_n = d_s5 // S5_GROUP_CH
    n_exp = w_router.shape[1]
    chunk = S5_CHUNK
    tm = min(512, t_len)
    assert t_len % tm == 0 and tm % SGU_CHUNK == 0 and t_len % chunk == 0 and c_len % chunk == 0

    sh_m, sc_m, gt_m, sh_f, sc_f, gt_f = [v.reshape(bsz, 1, d) for v in jnp.split(mod_x, 6, axis=-1)]
    csh_m, csc_m = mod_c[:d], mod_c[d:2 * d]

    w_in_bf = w_in.astype(BF16)

    nc_lat, nc_ctx = t_len // chunk, c_len // chunk
    n_steps = max(1, (max(nc_lat, nc_ctx) - 1).bit_length())
    m_t, ke_t, kc_t, alp = _s5_tables(s5_a_re, s5_a_im, s5_log_dt, s5_b_re, s5_b_im, s5_c_re, s5_c_im,
                                      chunk, n_steps)

    ctx2 = ctx.reshape(bsz * c_len, d)
    u_ctx = _inproj_u(ctx2, norm_pre_mix, csh_m, csc_m, w_in_bf[:, :d_s5], min(512, bsz * c_len))
    zero_init = jnp.zeros((g_n, bsz, 4 * S5_STATE), F32)
    (ctx_final,) = _s5(_to_groups(u_ctx, bsz, nc_ctx, chunk, g_n), ke_t, alp, zero_init, nb=bsz, nc=nc_ctx)

    x2 = x.reshape(n, d)
    u, ut, gu, vn, g1, g2 = _inproj(x2, norm_pre_mix, sh_m, sc_m, w_in_bf, sgu_norm, t_len, tm, d_s5, d_sgu)
    _, yg = _s5(_to_groups(ut, bsz, nc_lat, chunk, g_n), ke_t, alp, ctx_final, m_t, kc_t, nb=bsz, nc=nc_lat)
    y = _from_groups(yg, bsz, nc_lat, chunk, g_n)

    ch = d_sgu // SGU_GROUPS
    per_tile = LANES // ch
    sguw = sgu_w.reshape(SGU_GROUPS // per_tile, per_tile, SGU_CHUNK, SGU_CHUNK)
    sguw = sguw.transpose(0, 2, 1, 3).reshape(SGU_GROUPS // per_tile, SGU_CHUNK, per_tile * SGU_CHUNK).astype(BF16)
    sgub = jnp.repeat(sgu_b.T, ch, axis=1)

    x1, hp, logits = _mix(y, u, gu, vn, g1, g2, x2, s5_d, w_glu.astype(BF16), b_glu, sguw, sgub,
                          w_branch_s5.astype(BF16), w_branch_sgu.astype(BF16), w_out.astype(BF16),
                          norm_post_mix, gt_m, norm_pre_ffn, sh_f, sc_f, w_router.astype(BF16), t_len, tm)

    tn = min(512, n)
    idx, wts, counts = _route(logits.T, router_bias, tn)
    rows = EXPERT_ROWS
    cnt = counts.reshape(n_exp).astype(I32)
    nblk = (cnt + rows - 1) // rows
    blk_end = jnp.cumsum(nblk)
    blk_start = blk_end - nblk
    cap = ((n * TOP_K) // rows + n_exp) * rows
    dest = _dest(idx, (blk_start * rows).astype(F32).reshape(n_exp, 1), tn)
    pieces = d // LANES
    dest = dest.T.reshape(n * TOP_K) * pieces

    xs = _dispatch(dest, hp, cap, min(512, t_len), pieces)
    ys = _experts(cnt, blk_start.astype(I32), blk_end[-1:].astype(I32), xs, w_exp_gate, w_exp_up, w_exp_down, rows)
    wsgu = jnp.concatenate([w_sh_gate, w_sh_up], axis=1).astype(BF16)
    out = _combine(dest, ys, wts.T.reshape(n * TOP_K), hp, x1, wsgu, w_sh_down.astype(BF16), norm_post_ffn,
                   gt_f, t_len, min(512, t_len))
    return out.reshape(bsz, t_len, d)


def kernel(x, c, ctx, c_ctx, w_mod, b_mod, norm_pre_mix, norm_post_mix, norm_pre_ffn, norm_post_ffn, w_in, s5_a_re, s5_a_im, s5_log_dt, s5_b_re, s5_b_im, s5_c_re, s5_c_im, s5_d, w_glu, b_glu, sgu_norm, sgu_w, sgu_b, w_branch_s5, w_branch_sgu, w_out, w_router, router_bias, w_exp_gate, w_exp_up, w_exp_down, w_sh_gate, w_sh_up, w_sh_down):
    depth = w_mod.shape[0]
    assert depth == 1, "the context stream is only carried through the last layer's S5 states"
    bsz = x.shape[0]
    pad = (-(bsz + 1)) % 8
    cpad = jnp.concatenate([c, c_ctx[None, :], jnp.zeros((pad, c.shape[1]), c.dtype)], axis=0)
    mod = _modulation(cpad, w_mod[0], b_mod[0])
    return _layer(x, ctx, mod[:bsz], mod[bsz], norm_pre_mix[0], norm_post_mix[0], norm_pre_ffn[0],
                  norm_post_ffn[0], w_in[0], s5_a_re[0], s5_a_im[0], s5_log_dt[0], s5_b_re[0], s5_b_im[0],
                  s5_c_re[0], s5_c_im[0], s5_d[0], w_glu[0], b_glu[0], sgu_norm[0], sgu_w[0], sgu_b[0],
                  w_branch_s5[0], w_branch_sgu[0], w_out[0], w_router[0], router_bias[0],
                  w_exp_gate[0], w_exp_up[0], w_exp_down[0], w_sh_gate[0], w_sh_up[0], w_sh_down[0])
```

```python
import functools
import math

import jax
import jax.numpy as jnp
from jax import lax
from jax.experimental import pallas as pl
from jax.experimental.pallas import tpu as pltpu

F32 = jnp.float32
BF16 = jnp.bfloat16
I32 = jnp.int32

EPS = 1e-6
S5_GROUP_CH = 16
S5_STATE = 64
S5_CHUNK = 64
SGU_GROUPS = 8
SGU_CHUNK = 128
N_EXPERT_GROUPS = 8
TOPK_GROUPS = 4
TOP_K = 8
ROUTE_SCALE = 2.5
LANES = 128
EXPERT_ROWS = 256
EXPERT_X_SLOTS = 4
EXPERT_Y_SLOTS = 3
VMEM_LIMIT = 56 * 1024 * 1024


def _params(sem):
    return pltpu.CompilerParams(dimension_semantics=sem, vmem_limit_bytes=VMEM_LIMIT)


def _rms(v, g):
    return v * lax.rsqrt(jnp.mean(v * v, axis=-1, keepdims=True) + EPS) * g


def _dot(a, b):
    return jnp.dot(a, b, preferred_element_type=F32)


def _store_row_tiles(ref, first_row, val):
    pieces = val.shape[1] // LANES
    for s in range(pieces):
        ref[pl.ds(first_row * pieces + s, val.shape[0], stride=pieces), :] = val[:, s * LANES:(s + 1) * LANES]


def _load_row_tiles(ref, first_row, n_rows, pieces, lead=()):
    return jnp.concatenate([ref[lead + (pl.ds(first_row * pieces + s, n_rows, stride=pieces), slice(None))]
                            for s in range(pieces)], axis=1)


def _mod_kernel(c_ref, w_ref, b_ref, o_ref):
    cv = c_ref[...]
    s = cv * jax.nn.sigmoid(cv)
    o_ref[...] = _dot(s.astype(BF16), w_ref[...].astype(BF16)) + b_ref[...]


def _modulation(cpad, w_mod, b_mod):
    d, n6 = w_mod.shape
    tn = 1024
    return pl.pallas_call(
        _mod_kernel,
        out_shape=jax.ShapeDtypeStruct((cpad.shape[0], n6), F32),
        grid=(n6 // tn,),
        in_specs=[pl.BlockSpec(cpad.shape, lambda j: (0, 0)),
                  pl.BlockSpec((d, tn), lambda j: (0, j)),
                  pl.BlockSpec((1, tn), lambda j: (0, j))],
        out_specs=pl.BlockSpec((cpad.shape[0], tn), lambda j: (0, j)),
        compiler_params=_params(("parallel",)),
        name="mod",
    )(cpad, w_mod, b_mod.reshape(1, n6))


def _inproj_kernel(x_ref, g_ref, sh_ref, sc_ref, w_ref, sgun_ref,
                   u_ref, ut_ref, gu_ref, vn_ref, g1_ref, g2_ref, *, d_s5, d_sgu, d_model):
    h = _rms(x_ref[...], g_ref[...])
    hb = (h * (1.0 + sc_ref[0]) + sh_ref[0]).astype(BF16)
    o1, o2, o3, o4 = d_s5, d_s5 + d_sgu, d_s5 + 2 * d_sgu, d_s5 + 2 * d_sgu + d_model
    u = _dot(hb, w_ref[:, 0:o1])
    u_ref[...] = u.astype(BF16)
    ut_ref[...] = u.T.astype(BF16)
    gu_ref[...] = jax.nn.gelu(_dot(hb, w_ref[:, o1:o2])).astype(BF16)
    v = jax.nn.gelu(_dot(hb, w_ref[:, o2:o3]))
    vn_ref[...] = _rms(v, sgun_ref[...]).astype(BF16)
    g1_ref[...] = jax.nn.sigmoid(_dot(hb, w_ref[:, o3:o4])).astype(BF16)
    g2_ref[...] = jax.nn.sigmoid(_dot(hb, w_ref[:, o4:o4 + d_model])).astype(BF16)


def _inproj(x2, norm_g, shift, scale, w_in_bf, sgu_norm, tokens_per_batch, tm, d_s5, d_sgu):
    n, d = x2.shape
    tiles_per_batch = tokens_per_batch // tm
    row = lambda i: (i, 0)
    fixed = lambda i: (0, 0)
    per_batch = lambda i: (i // tiles_per_batch, 0, 0)
    return pl.pallas_call(
        functools.partial(_inproj_kernel, d_s5=d_s5, d_sgu=d_sgu, d_model=d),
        out_shape=(jax.ShapeDtypeStruct((n, d_s5), BF16), jax.ShapeDtypeStruct((d_s5, n), BF16),
                   jax.ShapeDtypeStruct((n, d_sgu), BF16),
                   jax.ShapeDtypeStruct((n, d_sgu), BF16), jax.ShapeDtypeStruct((n, d), BF16),
                   jax.ShapeDtypeStruct((n, d), BF16)),
        grid=(n // tm,),
        in_specs=[pl.BlockSpec((tm, d), row), pl.BlockSpec((1, d), fixed),
                  pl.BlockSpec((1, 1, d), per_batch), pl.BlockSpec((1, 1, d), per_batch),
                  pl.BlockSpec(w_in_bf.shape, fixed), pl.BlockSpec((1, d_sgu), fixed)],
        out_specs=(pl.BlockSpec((tm, d_s5), row), pl.BlockSpec((d_s5, tm), lambda i: (0, i)),
                   pl.BlockSpec((tm, d_sgu), row),
                   pl.BlockSpec((tm, d_sgu), row), pl.BlockSpec((tm, d), row), pl.BlockSpec((tm, d), row)),
        compiler_params=_params(("parallel",)),
        name="inproj",
    )(x2, norm_g.reshape(1, d), shift, scale, w_in_bf, sgu_norm.reshape(1, d_sgu))


def _inproj_u_kernel(x_ref, g_ref, sh_ref, sc_ref, w_ref, u_ref):
    h = _rms(x_ref[...], g_ref[...])
    hb = (h * (1.0 + sc_ref[...]) + sh_ref[...]).astype(BF16)
    u_ref[...] = _dot(hb, w_ref[...]).T.astype(BF16)


def _inproj_u(x2, norm_g, shift, scale, w_u_bf, tm):
    n, d = x2.shape
    d_s5 = w_u_bf.shape[1]
    row = lambda i: (i, 0)
    fixed = lambda i: (0, 0)
    return pl.pallas_call(
        _inproj_u_kernel,
        out_shape=jax.ShapeDtypeStruct((d_s5, n), BF16),
        grid=(n // tm,),
        in_specs=[pl.BlockSpec((tm, d), row), pl.BlockSpec((1, d), fixed), pl.BlockSpec((1, d), fixed),
                  pl.BlockSpec((1, d), fixed), pl.BlockSpec(w_u_bf.shape, fixed)],
        out_specs=pl.BlockSpec((d_s5, tm), lambda i: (0, i)),
        compiler_params=_params(("parallel",)),
        name="inproj_ctx",
    )(x2, norm_g.reshape(1, d), shift.reshape(1, d), scale.reshape(1, d), w_u_bf)


def _s5_tables(a_re, a_im, log_dt, b_re, b_im, c_re, c_im, chunk, n_steps):
    hi = lax.Precision.HIGHEST
    g_n, p_n, j_n = a_re.shape[1], a_re.shape[2], b_re.shape[3]
    dt = jnp.exp(log_dt)[..., None]
    lam_re, lam_im = dt * a_re, dt * a_im
    ab_re, ab_im = jnp.exp(lam_re) * jnp.cos(lam_im), jnp.exp(lam_re) * jnp.sin(lam_im)
    den = a_re * a_re + a_im * a_im
    q_re = ((ab_re - 1.0) * a_re + ab_im * a_im) / den
    q_im = (ab_im * a_re - (ab_re - 1.0) * a_im) / den
    bb_re = q_re[..., None] * b_re - q_im[..., None] * b_im
    bb_im = q_re[..., None] * b_im + q_im[..., None] * b_re
    k = jnp.arange(chunk + 1, dtype=F32)[:, None, None, None]
    mag = jnp.exp(k * lam_re[None])
    pw_re, pw_im = mag * jnp.cos(k * lam_im[None]), mag * jnp.sin(k * lam_im[None])

    ct_re, ct_im = c_re.transpose(0, 1, 3, 2), c_im.transpose(0, 1, 3, 2)

    def lag_kernels(d):
        w_re = bb_re[d][:, :, :, None] * ct_re[d][:, :, None, :] - bb_im[d][:, :, :, None] * ct_im[d][:, :, None, :]
        w_im = bb_re[d][:, :, :, None] * ct_im[d][:, :, None, :] + bb_im[d][:, :, :, None] * ct_re[d][:, :, None, :]
        w = jnp.concatenate([w_re, -w_im], axis=1).reshape(g_n, 2 * p_n, j_n * j_n)
        a = jnp.concatenate([pw_re[:chunk, d], pw_im[:chunk, d]], axis=-1).transpose(1, 0, 2)
        return jnp.einsum('gkp,gpn->gkn', a, w, precision=hi)

    kf, kb = lag_kernels(0), lag_kernels(1)
    lags = jnp.concatenate([kb[:, 1:][:, ::-1], kf[:, 0:1] + kb[:, 0:1], kf[:, 1:],
                            jnp.zeros_like(kf[:, 0:1])], axis=1)
    lags = lags.transpose(0, 2, 1).reshape(g_n, j_n, j_n, 2 * chunk)

    def both(f, b, axis):
        return jnp.concatenate([f, b], axis=axis)

    pt_re = both(pw_re[:chunk, 0][::-1], pw_re[:chunk, 1], -1).transpose(1, 0, 2)
    pt_im = both(pw_im[:chunk, 0][::-1], pw_im[:chunk, 1], -1).transpose(1, 0, 2)
    bt_re = both(bb_re[0].transpose(0, 2, 1), bb_re[1].transpose(0, 2, 1), -1)
    bt_im = both(bb_im[0].transpose(0, 2, 1), bb_im[1].transpose(0, 2, 1), -1)
    e_re = pt_re[:, None] * bt_re[:, :, None] - pt_im[:, None] * bt_im[:, :, None]
    e_im = pt_re[:, None] * bt_im[:, :, None] + pt_im[:, None] * bt_re[:, :, None]
    ke = jnp.concatenate([e_re, e_im], axis=-1).reshape(g_n, j_n * chunk, 4 * p_n)

    q_re = both(pw_re[1:chunk + 1, 0], pw_re[1:chunk + 1, 1][::-1], -1).transpose(1, 2, 0)
    q_im = both(pw_im[1:chunk + 1, 0], pw_im[1:chunk + 1, 1][::-1], -1).transpose(1, 2, 0)
    q_re, q_im = jnp.tile(q_re, (1, 1, j_n)), jnp.tile(q_im, (1, 1, j_n))
    cc_re = jnp.repeat(both(ct_re[0], ct_re[1], 1), chunk, axis=2)
    cc_im = jnp.repeat(both(ct_im[0], ct_im[1], 1), chunk, axis=2)
    kc = jnp.concatenate([q_re * cc_re - q_im * cc_im, -(q_re * cc_im + q_im * cc_re)], axis=1)

    e = (chunk * (2.0 ** jnp.arange(n_steps, dtype=F32)))[:, None, None, None]
    mg = jnp.exp(e * lam_re[None])
    al_re, al_im = mg * jnp.cos(e * lam_im[None]), mg * jnp.sin(e * lam_im[None])
    alp = jnp.concatenate([al_re[:, 0], al_re[:, 1], al_im[:, 0], al_im[:, 1]], axis=-1)
    return lags, ke.astype(BF16), kc.astype(BF16), alp.transpose(1, 0, 2)


def _build_toeplitz(lag_ref, m_s, chunk):
    j_n = lag_ref.shape[1]
    per_tile = LANES // chunk
    low = lax.broadcasted_iota(I32, (chunk, LANES), 1) < chunk
    for i in range(j_n):
        for q in range(j_n // per_tile):
            tile = None
            for p in range(per_tile):
                j = q * per_tile + p
                row = jnp.broadcast_to(lag_ref[0, i, j:j + 1, :], (chunk, LANES))
                rot = pltpu.roll(row, (p * chunk + LANES - (chunk - 1)) % LANES, 1, stride=1, stride_axis=0)
                tile = rot if tile is None else jnp.where(low, tile, rot)
            m_s[i * chunk:(i + 1) * chunk, q * LANES:(q + 1) * LANES] = tile.astype(BF16)


def _s5_kernel(*refs, nb, nc, readout):
    if readout:
        u_ref, ke_ref, alp_ref, init_ref, lag_ref, kc_ref, fin_ref, y_ref, m_s = refs
    else:
        u_ref, ke_ref, alp_ref, init_ref, fin_ref = refs
    rows = nb * nc
    half = 2 * S5_STATE
    u = u_ref[0]
    e = _dot(u, ke_ref[0])
    er, ei = e[:, :half], e[:, half:]
    r_idx = lax.broadcasted_iota(I32, (rows, half), 0)
    if nc & (nc - 1) == 0:
        n_idx, b_idx = r_idx & (nc - 1), r_idx >> (nc.bit_length() - 1)
    else:
        n_idx, b_idx = lax.rem(r_idx, nc), lax.div(r_idx, nc)
    is_f = lax.broadcasted_iota(I32, (rows, half), 1) < S5_STATE
    seen = jnp.where(is_f, n_idx, nc - 1 - n_idx)
    init = init_ref[0]
    init_r = jnp.zeros((rows, half), F32)
    init_i = jnp.zeros((rows, half), F32)
    for b in range(nb):
        init_r = jnp.where(b_idx == b, init[b:b + 1, :half], init_r)
        init_i = jnp.where(b_idx == b, init[b:b + 1, half:], init_i)
    alp = alp_ref[0]
    ar, ai = alp[0:1, :half], alp[0:1, half:]
    er = er + jnp.where(seen == 0, ar * init_r - ai * init_i, 0.0)
    ei = ei + jnp.where(seen == 0, ar * init_i + ai * init_r, 0.0)

    def from_prev(v, dist):
        return jnp.where(is_f, pltpu.roll(v, dist, 0), pltpu.roll(v, rows - dist, 0))

    d, s = 1, 0
    while d < nc:
        ar, ai = alp[s:s + 1, :half], alp[s:s + 1, half:]
        sr = jnp.where(seen >= d, from_prev(er, d), 0.0)
        si = jnp.where(seen >= d, from_prev(ei, d), 0.0)
        er, ei = er + (ar * sr - ai * si), ei + (ar * si + ai * sr)
        d, s = d * 2, s + 1
    is_f_row = lax.broadcasted_iota(I32, (1, half), 1) < S5_STATE
    for b in range(nb):
        lo, hi = b * nc, b * nc + nc - 1
        fin_ref[0, b:b + 1, :half] = jnp.where(is_f_row, er[hi:hi + 1], er[lo:lo + 1])
        fin_ref[0, b:b + 1, half:] = jnp.where(is_f_row, ei[hi:hi + 1], ei[lo:lo + 1])
    if readout:
        if nc > 1:
            sin_r = jnp.where(seen >= 1, from_prev(er, 1), init_r)
            sin_i = jnp.where(seen >= 1, from_prev(ei, 1), init_i)
        else:
            sin_r, sin_i = init_r, init_i
        s_in = jnp.concatenate([sin_r, sin_i], axis=1).astype(BF16)
        _build_toeplitz(lag_ref, m_s, S5_CHUNK)
        y_ref[0] = (_dot(u, m_s[...]) + _dot(s_in, kc_ref[0])).astype(BF16)


def _s5(ug, ke, alp, init, m=None, kc=None, *, nb, nc):
    g_n, rows, width = ug.shape
    readout = m is not None
    grp = lambda g: (g, 0, 0)
    in_specs = [pl.BlockSpec((1, rows, width), grp), pl.BlockSpec((1,) + ke.shape[1:], grp),
                pl.BlockSpec((1,) + alp.shape[1:], grp), pl.BlockSpec((1,) + init.shape[1:], grp)]
    out_shape = [jax.ShapeDtypeStruct(init.shape, F32)]
    out_specs = [pl.BlockSpec((1,) + init.shape[1:], grp)]
    args = [ug, ke, alp, init]
    scratch = []
    if readout:
        assert m.shape[3] == LANES and 2 * S5_CHUNK == LANES
        in_specs += [pl.BlockSpec((1,) + m.shape[1:], lambda g: (g, 0, 0, 0)),
                     pl.BlockSpec((1,) + kc.shape[1:], grp)]
        out_shape.append(jax.ShapeDtypeStruct(ug.shape, BF16))
        out_specs.append(pl.BlockSpec((1, rows, width), grp))
        args += [m, kc]
        scratch = [pltpu.VMEM((width, width), BF16)]
    return pl.pallas_call(
        functools.partial(_s5_kernel, nb=nb, nc=nc, readout=readout),
        out_shape=tuple(out_shape), grid=(g_n,), in_specs=in_specs, out_specs=tuple(out_specs),
        scratch_shapes=scratch,
        compiler_params=_params(("parallel",)),
        name="s5_readout" if readout else "s5_state",
    )(*args)


def _to_groups(ut, nb, nc, chunk, g_n):
    j_n = ut.shape[0] // g_n
    return ut.reshape(g_n, j_n, nb * nc, chunk).transpose(0, 2, 1, 3).reshape(g_n, nb * nc, j_n * chunk)


def _from_groups(y, nb, nc, chunk, g_n):
    j_n = y.shape[2] // chunk
    return y.reshape(g_n, nb * nc, j_n, chunk).transpose(0, 2, 1, 3).reshape(g_n * j_n, nb * nc * chunk)


def _mix_kernel(y_ref, u_ref, gu_ref, vn_ref, g1_ref, g2_ref, x_ref,
                d_ref, wglu_ref, bglu_ref, sguw_ref, sgub_ref, wb1_ref, wb2_ref, wout_ref,
                npost_ref, gt_ref, npre_ref, shf_ref, scf_ref, wr_ref,
                x1_ref, hp_ref, lg_ref, *, tm):
    y = jax.nn.gelu(y_ref[...].astype(F32).T + d_ref[...] * u_ref[...].astype(F32))
    y_s5 = y * jax.nn.sigmoid(_dot(y.astype(BF16), wglu_ref[...]) + bglu_ref[...])

    lane = lax.broadcasted_iota(I32, (1, LANES), 1)
    m_lo = (lane < LANES // 2).astype(F32).astype(BF16)
    m_hi = (lane >= LANES // 2).astype(F32).astype(BF16)
    vn = vn_ref[...]
    chunks = []
    for c in range(tm // SGU_CHUNK):
        vc = vn[c * SGU_CHUNK:(c + 1) * SGU_CHUNK, :]
        tiles = []
        for q in range(vc.shape[1] // LANES):
            vt = vc[:, q * LANES:(q + 1) * LANES]
            rhs = jnp.concatenate([vt * m_lo, vt * m_hi], axis=0)
            tiles.append(_dot(sguw_ref[q], rhs))
        chunks.append(jnp.concatenate(tiles, axis=1) + sgub_ref[...])
    mixed = jnp.concatenate(chunks, axis=0)
    y_sgu = gu_ref[...].astype(F32) * mixed

    merged = (g1_ref[...].astype(F32) * _dot(y_s5.astype(BF16), wb1_ref[...])
              + g2_ref[...].astype(F32) * _dot(y_sgu.astype(BF16), wb2_ref[...]))
    mx = _dot(merged.astype(BF16), wout_ref[...])
    x1 = x_ref[...] + gt_ref[0] * _rms(mx, npost_ref[...])
    x1_ref[...] = x1
    hp = _rms(x1, npre_ref[...]) * (1.0 + scf_ref[0]) + shf_ref[0]
    _store_row_tiles(hp_ref, 0, hp)
    lg_ref[...] = _dot(hp.astype(BF16), wr_ref[...])


def _mix(y, u, gu, vn, g1, g2, x2, s5_d, wglu, bglu, sguw, sgub, wb1, wb2, wout,
         npost, gt, npre, shf, scf, wr, tokens_per_batch, tm):
    n, d = x2.shape
    d_s5, d_sgu, n_exp = y.shape[0], gu.shape[1], wr.shape[1]
    tiles_per_batch = tokens_per_batch // tm
    row = lambda i: (i, 0)
    fixed = lambda i: (0, 0)
    fixed3 = lambda i: (0, 0, 0)
    per_batch = lambda i: (i // tiles_per_batch, 0, 0)
    vec = lambda w: pl.BlockSpec((1, w), fixed)
    return pl.pallas_call(
        functools.partial(_mix_kernel, tm=tm),
        out_shape=(jax.ShapeDtypeStruct((n, d), F32), jax.ShapeDtypeStruct((n * d // LANES, LANES), F32),
                   jax.ShapeDtypeStruct((n, n_exp), F32)),
        grid=(n // tm,),
        in_specs=[pl.BlockSpec((d_s5, tm), lambda i: (0, i)), pl.BlockSpec((tm, d_s5), row),
                  pl.BlockSpec((tm, d_sgu), row),
                  pl.BlockSpec((tm, d_sgu), row), pl.BlockSpec((tm, d), row), pl.BlockSpec((tm, d), row),
                  pl.BlockSpec((tm, d), row),
                  vec(d_s5), pl.BlockSpec(wglu.shape, fixed), vec(d_s5),
                  pl.BlockSpec(sguw.shape, fixed3), pl.BlockSpec(sgub.shape, fixed),
                  pl.BlockSpec(wb1.shape, fixed), pl.BlockSpec(wb2.shape, fixed), pl.BlockSpec(wout.shape, fixed),
                  vec(d), pl.BlockSpec((1, 1, d), per_batch), vec(d),
                  pl.BlockSpec((1, 1, d), per_batch), pl.BlockSpec((1, 1, d), per_batch),
                  pl.BlockSpec(wr.shape, fixed)],
        out_specs=(pl.BlockSpec((tm, d), row), pl.BlockSpec((tm * d // LANES, LANES), row),
                   pl.BlockSpec((tm, n_exp), row)),
        compiler_params=_params(("parallel",)),
        name="mix",
    )(y, u, gu, vn, g1, g2, x2, s5_d.reshape(1, d_s5), wglu, bglu.reshape(1, d_s5), sguw, sgub,
      wb1, wb2, wout, npost.reshape(1, d), gt, npre.reshape(1, d), shf, scf, wr)


def _route_kernel(lg_ref, bias_ref, idx_ref, w_ref, cnt_ref, *, n_exp, tn):
    per_group = n_exp // N_EXPERT_GROUPS
    neg = jnp.float32(-jnp.inf)

    scores = jax.nn.sigmoid(lg_ref[...])
    sel = scores + bias_ref[...]
    gs = []
    for g in range(N_EXPERT_GROUPS):
        sg = sel[g * per_group:(g + 1) * per_group, :]
        m1 = jnp.max(sg, axis=0, keepdims=True)
        is_m1 = sg == m1
        n_m1 = jnp.sum(jnp.where(is_m1, 1.0, 0.0), axis=0, keepdims=True)
        rest = jnp.max(jnp.where(is_m1, neg, sg), axis=0, keepdims=True)
        gs.append(m1 + jnp.where(n_m1 >= 2.0, m1, rest))
    gsm = jnp.concatenate(gs, axis=0)
    g_iota = lax.broadcasted_iota(I32, gsm.shape, 0)
    e_iota = lax.broadcasted_iota(I32, sel.shape, 0).astype(F32)
    masked = []
    for g in range(N_EXPERT_GROUPS):
        mine = gsm[g:g + 1, :]
        beats = jnp.where(gsm > mine, 1.0, jnp.where(gsm == mine, jnp.where(g_iota < g, 1.0, 0.0), 0.0))
        n_beats = jnp.sum(beats, axis=0, keepdims=True)
        masked.append(jnp.where(n_beats < float(TOPK_GROUPS), sel[g * per_group:(g + 1) * per_group, :], neg))
    start = jnp.concatenate(masked, axis=0)
    selm = start
    picked, vals = [], []
    for _ in range(TOP_K):
        m = jnp.max(selm, axis=0, keepdims=True)
        first = jnp.min(jnp.where(selm == m, e_iota, float(n_exp)), axis=0, keepdims=True)
        one = e_iota == first
        picked.append(first)
        vals.append(jnp.sum(jnp.where(one, scores, 0.0), axis=0, keepdims=True))
        selm = jnp.where(one, neg, selm)
    idx_ref[...] = jnp.concatenate(picked, axis=0).astype(I32)
    wv = jnp.concatenate(vals, axis=0)
    w_ref[...] = wv / jnp.sum(wv, axis=0, keepdims=True) * ROUTE_SCALE
    hot = jnp.where(selm == neg, jnp.where(start == neg, 0.0, 1.0), 0.0)

    @pl.when(pl.program_id(0) == 0)
    def _():
        cnt_ref[...] = jnp.zeros_like(cnt_ref)

    cnt_ref[...] += jnp.sum(hot, axis=1, keepdims=True)


def _route(logits_t, bias, tn):
    n_exp, n = logits_t.shape
    col = lambda i: (0, i)
    fixed = lambda i: (0, 0)
    return pl.pallas_call(
        functools.partial(_route_kernel, n_exp=n_exp, tn=tn),
        out_shape=(jax.ShapeDtypeStruct((TOP_K, n), I32), jax.ShapeDtypeStruct((TOP_K, n), F32),
                   jax.ShapeDtypeStruct((n_exp, 1), F32)),
        grid=(n // tn,),
        in_specs=[pl.BlockSpec((n_exp, tn), col), pl.BlockSpec((n_exp, 1), fixed)],
        out_specs=(pl.BlockSpec((TOP_K, tn), col), pl.BlockSpec((TOP_K, tn), col),
                   pl.BlockSpec((n_exp, 1), fixed)),
        compiler_params=_params(("arbitrary",)),
        name="route",
    )(logits_t, bias.reshape(n_exp, 1))


def _dest_kernel(idx_ref, start_ref, dest_ref, carry_ref, *, n_exp, tn):
    @pl.when(pl.program_id(0) == 0)
    def _():
        carry_ref[...] = start_ref[...]

    idx = idx_ref[...]
    e_iota = lax.broadcasted_iota(I32, (n_exp, tn), 0)
    hot = jnp.zeros((n_exp, tn), F32)
    for k in range(TOP_K):
        hot = hot + jnp.where(e_iota == idx[k:k + 1, :], 1.0, 0.0)
    before = jnp.where(lax.broadcasted_iota(I32, (tn, tn), 0) < lax.broadcasted_iota(I32, (tn, tn), 1), 1.0, 0.0)
    rank = _dot(hot.astype(BF16), before.astype(BF16)) + carry_ref[...]
    dest_ref[...] = jnp.concatenate(
        [jnp.sum(jnp.where(e_iota == idx[k:k + 1, :], rank, 0.0), axis=0, keepdims=True)
         for k in range(TOP_K)], axis=0).astype(I32)
    carry_ref[...] += jnp.sum(hot, axis=1, keepdims=True)


def _dest(idx, start, tn):
    n = idx.shape[1]
    n_exp = start.shape[0]
    col = lambda i: (0, i)
    return pl.pallas_call(
        functools.partial(_dest_kernel, n_exp=n_exp, tn=tn),
        out_shape=jax.ShapeDtypeStruct((TOP_K, n), I32),
        grid=(n // tn,),
        in_specs=[pl.BlockSpec((TOP_K, tn), col), pl.BlockSpec((n_exp, 1), lambda i: (0, 0))],
        out_specs=pl.BlockSpec((TOP_K, tn), col),
        scratch_shapes=[pltpu.VMEM((n_exp, 1), F32)],
        compiler_params=_params(("arbitrary",)),
        name="dest",
    )(idx, start)


def _dispatch_kernel(dest_ref, hp_ref, xs_ref, sem, *, tm, pieces):
    def start(r, c):
        src = hp_ref.at[pl.ds(pl.multiple_of(r * pieces, pieces), pieces), :]
        for k in range(TOP_K):
            row = pl.multiple_of(dest_ref[r * TOP_K + k], pieces)
            pltpu.make_async_copy(src, xs_ref.at[pl.ds(row, pieces), :], sem).start(priority=k % 2)
        return c

    lax.fori_loop(0, tm, start, 0, unroll=2)
    for k in range(TOP_K):
        pltpu.make_async_copy(hp_ref, xs_ref.at[pl.ds(0, tm * pieces), :], sem).wait()


def _dispatch(dest, hp, cap, tm, pieces):
    n = hp.shape[0] // pieces
    return pl.pallas_call(
        functools.partial(_dispatch_kernel, tm=tm, pieces=pieces),
        out_shape=jax.ShapeDtypeStruct((cap * pieces, LANES), hp.dtype),
        grid=(n // tm,),
        in_specs=[pl.BlockSpec((tm * TOP_K,), lambda i: (i,), memory_space=pltpu.SMEM),
                  pl.BlockSpec((tm * pieces, LANES), lambda i: (i, 0))],
        out_specs=pl.BlockSpec(memory_space=pl.ANY),
        scratch_shapes=[pltpu.SemaphoreType.DMA(())],
        compiler_params=_params(("arbitrary",)),
        name="dispatch",
    )(dest, hp)


def _expert_kernel(cnt_ref, first_ref, tot_ref, xs_hbm, wg_ref, wu_ref, wd_ref, ys_hbm,
                   xbuf, ybuf, wgu_s, wd_s, xsem, ysem, *, rows, pieces):
    e = pl.program_id(0)
    cnt, first, total = cnt_ref[e], first_ref[e], tot_ref[0]
    n_blk = (cnt + rows - 1) // rows
    d_e = wg_ref.shape[2]
    phys = rows * pieces

    def block_rows(g):
        return pl.ds(pl.multiple_of(g * phys, phys), phys)

    def x_copy(g, slot):
        return pltpu.make_async_copy(xs_hbm.at[block_rows(g), :], xbuf.at[slot], xsem.at[slot])

    def y_copy(g, slot):
        return pltpu.make_async_copy(ybuf.at[slot], ys_hbm.at[block_rows(g), :], ysem.at[slot])

    x_slots, y_slots = xbuf.shape[0], ybuf.shape[0]
    ahead = x_slots - 1

    @pl.when(n_blk > 0)
    def _():
        @pl.when(first == 0)
        def _():
            for a in range(ahead):
                @pl.when(a < total)
                def _():
                    x_copy(a, a).start()

        wgu_s[:, :d_e] = wg_ref[0].astype(BF16)
        wgu_s[:, d_e:] = wu_ref[0].astype(BF16)
        wd_s[...] = wd_ref[0].astype(BF16)

        def block(j, c):
            g = first + j
            slot = g % x_slots
            x_copy(g, slot).wait()

            @pl.when(g + ahead < total)
            def _():
                x_copy(g + ahead, (g + ahead) % x_slots).start()

            yslot = g % y_slots

            @pl.when(g >= y_slots)
            def _():
                y_copy(g - y_slots, yslot).wait()

            x = _load_row_tiles(xbuf, 0, rows, pieces, lead=(slot,))
            r_iota = lax.broadcasted_iota(I32, x.shape, 0) + j * rows
            xb = jnp.where(r_iota < cnt, x, 0.0).astype(BF16)
            gu = _dot(xb, wgu_s[...])
            gate, up = gu[:, :d_e], gu[:, d_e:]
            hid = (gate * jax.nn.sigmoid(gate) * up).astype(BF16)
            y = _dot(hid, wd_s[...])
            for s in range(pieces):
                ybuf[yslot, pl.ds(s, rows, stride=pieces), :] = y[:, s * LANES:(s + 1) * LANES]
            y_copy(g, yslot).start()
            return c

        lax.fori_loop(0, n_blk, block, 0)

        @pl.when(first + n_blk == total)
        def _():
            for back in range(1, y_slots + 1):
                @pl.when(total >= back)
                def _():
                    y_copy(total - back, (total - back) % y_slots).wait()


def _experts(cnt, blk_first, n_used, xs, w_gate, w_up, w_down, rows):
    n_exp, d, d_e = w_gate.shape
    pieces = d // LANES
    hbm = pl.BlockSpec(memory_space=pl.ANY)
    w_map = lambda e, cnt, first, tot: (e, 0, 0)
    grid_spec = pltpu.PrefetchScalarGridSpec(
        num_scalar_prefetch=3,
        grid=(n_exp,),
        in_specs=[hbm, pl.BlockSpec((1, d, d_e), w_map), pl.BlockSpec((1, d, d_e), w_map),
                  pl.BlockSpec((1, d_e, d), w_map)],
        out_specs=hbm,
        scratch_shapes=[pltpu.VMEM((EXPERT_X_SLOTS, rows * pieces, LANES), F32),
                        pltpu.VMEM((EXPERT_Y_SLOTS, rows * pieces, LANES), F32),
                        pltpu.VMEM((d, 2 * d_e), BF16), pltpu.VMEM((d_e, d), BF16),
                        pltpu.SemaphoreType.DMA((EXPERT_X_SLOTS,)), pltpu.SemaphoreType.DMA((EXPERT_Y_SLOTS,))],
    )
    return pl.pallas_call(
        functools.partial(_expert_kernel, rows=rows, pieces=pieces),
        out_shape=jax.ShapeDtypeStruct(xs.shape, F32),
        grid_spec=grid_spec,
        compiler_params=_params(("arbitrary",)),
        name="experts",
    )(cnt, blk_first, n_used, xs, w_gate, w_up, w_down)


def _combine_kernel(dest_ref, next_ref, ys_ref, wt_ref, hp_ref, x1_ref, wsgu_ref, wsd_ref, npost_ref, gt_ref,
                    o_ref, buf, acc, sem, *, tm, pieces):
    i = pl.program_id(0)
    last = pl.num_programs(0) - 1
    cur = i % 2

    def start_token(rows_ref, half, r):
        slot = pl.ds(pl.multiple_of(r * pieces, pieces), pieces)
        for k in range(TOP_K):
            row = pl.multiple_of(rows_ref[r * TOP_K + k], pieces)
            pltpu.make_async_copy(ys_ref.at[pl.ds(row, pieces), :], buf.at[half, k, slot, :],
                                  sem.at[half]).start(priority=k % 2)

    def wait_half(half):
        for k in range(TOP_K):
            pltpu.make_async_copy(ys_ref.at[pl.ds(0, tm * pieces), :], buf.at[half, k], sem.at[half]).wait()

    @pl.when(i == 0)
    def _():
        def first(r, c):
            start_token(dest_ref, 0, r)
            return c

        lax.fori_loop(0, tm, first, 0)

    def tile_pass(half):
        wait_half(half)

        def token(r, c):
            start_token(next_ref, 1 - half, r)
            slot = pl.ds(pl.multiple_of(r * pieces, pieces), pieces)
            terms = [wt_ref[r * TOP_K + k] * buf[half, k, slot, :] for k in range(TOP_K)]
            while len(terms) > 1:
                terms = [terms[j] + terms[j + 1] for j in range(0, len(terms), 2)]
            acc[slot, :] = terms[0]
            return c

        lax.fori_loop(0, tm, token, 0, unroll=4)

        @pl.when(i == last)
        def _():
            wait_half(1 - half)

    for half in range(2):
        @pl.when(cur == half)
        def _():
            tile_pass(half)

    hb = _load_row_tiles(hp_ref, 0, tm, pieces).astype(BF16)
    d_sh = wsd_ref.shape[0]
    gu = _dot(hb, wsgu_ref[...])
    gate, up = gu[:, :d_sh], gu[:, d_sh:]
    shared = _dot((gate * jax.nn.sigmoid(gate) * up).astype(BF16), wsd_ref[...])
    fx = shared + _load_row_tiles(acc, 0, tm, pieces)
    o_ref[...] = x1_ref[...] + gt_ref[0] * _rms(fx, npost_ref[...])


def _combine(dest, ys, wt, hp, x1, wsgu, wsd, npost, gt, tokens_per_batch, tm):
    n, d = x1.shape
    pieces = d // LANES
    w = LANES
    tiles_per_batch = tokens_per_batch // tm
    row = lambda i: (i, 0)
    fixed = lambda i: (0, 0)
    return pl.pallas_call(
        functools.partial(_combine_kernel, tm=tm, pieces=pieces),
        out_shape=jax.ShapeDtypeStruct((n, d), F32),
        grid=(n // tm,),
        in_specs=[pl.BlockSpec((tm * TOP_K,), lambda i: (i,), memory_space=pltpu.SMEM),
                  pl.BlockSpec((tm * TOP_K,), lambda i: (jnp.minimum(i + 1, n // tm - 1),), memory_space=pltpu.SMEM),
                  pl.BlockSpec(memory_space=pl.ANY),
                  pl.BlockSpec((tm * TOP_K,), lambda i: (i,), memory_space=pltpu.SMEM),
                  pl.BlockSpec((tm * pieces, w), row), pl.BlockSpec((tm, d), row),
                  pl.BlockSpec(wsgu.shape, fixed), pl.BlockSpec(wsd.shape, fixed),
                  pl.BlockSpec((1, d), fixed), pl.BlockSpec((1, 1, d), lambda i: (i // tiles_per_batch, 0, 0))],
        out_specs=pl.BlockSpec((tm, d), row),
        scratch_shapes=[pltpu.VMEM((2, TOP_K, tm * pieces, w), F32), pltpu.VMEM((tm * pieces, w), F32),
                        pltpu.SemaphoreType.DMA((2,))],
        compiler_params=_params(("arbitrary",)),
        name="combine",
    )(dest, dest, ys, wt, hp, x1, wsgu, wsd, npost.reshape(1, d), gt)


def _layer(x, ctx, mod_x, mod_c, norm_pre_mix, norm_post_mix, norm_pre_ffn, norm_post_ffn,
           w_in, s5_a_re, s5_a_im, s5_log_dt, s5_b_re, s5_b_im, s5_c_re, s5_c_im, s5_d, w_glu, b_glu,
           sgu_norm, sgu_w, sgu_b, w_branch_s5, w_branch_sgu, w_out,
           w_router, router_bias, w_exp_gate, w_exp_up, w_exp_down, w_sh_gate, w_sh_up, w_sh_down):
    bsz, t_len, d = x.shape
    c_len = ctx.shape[1]
    n = bsz * t_len
    d_s5, d_sgu = w_glu.shape[0], sgu_norm.shape[0]
    g_n = d_s5 // S5_GROUP_CH
    n_exp = w_router.shape[1]
    chunk = S5_CHUNK
    tm = min(512, t_len)
    assert t_len % tm == 0 and tm % SGU_CHUNK == 0 and t_len % chunk == 0 and c_len % chunk == 0

    sh_m, sc_m, gt_m, sh_f, sc_f, gt_f = [v.reshape(bsz, 1, d) for v in jnp.split(mod_x, 6, axis=-1)]
    csh_m, csc_m = mod_c[:d], mod_c[d:2 * d]

    w_in_bf = w_in.astype(BF16)

    nc_lat, nc_ctx = t_len // chunk, c_len // chunk
    n_steps = max(1, (max(nc_lat, nc_ctx) - 1).bit_length())
    m_t, ke_t, kc_t, alp = _s5_tables(s5_a_re, s5_a_im, s5_log_dt, s5_b_re, s5_b_im, s5_c_re, s5_c_im,
                                      chunk, n_steps)

    ctx2 = ctx.reshape(bsz * c_len, d)
    u_ctx = _inproj_u(ctx2, norm_pre_mix, csh_m, csc_m, w_in_bf[:, :d_s5], min(512, bsz * c_len))
    zero_init = jnp.zeros((g_n, bsz, 4 * S5_STATE), F32)
    (ctx_final,) = _s5(_to_groups(u_ctx, bsz, nc_ctx, chunk, g_n), ke_t, alp, zero_init, nb=bsz, nc=nc_ctx)

    x2 = x.reshape(n, d)
    u, ut, gu, vn, g1, g2 = _inproj(x2, norm_pre_mix, sh_m, sc_m, w_in_bf, sgu_norm, t_len, tm, d_s5, d_sgu)
    _, yg = _s5(_to_groups(ut, bsz, nc_lat, chunk, g_n), ke_t, alp, ctx_final, m_t, kc_t, nb=bsz, nc=nc_lat)
    y = _from_groups(yg, bsz, nc_lat, chunk, g_n)

    ch = d_sgu // SGU_GROUPS
    per_tile = LANES // ch
    sguw = sgu_w.reshape(SGU_GROUPS // per_tile, per_tile, SGU_CHUNK, SGU_CHUNK)
    sguw = sguw.transpose(0, 2, 1, 3).reshape(SGU_GROUPS // per_tile, SGU_CHUNK, per_tile * SGU_CHUNK).astype(BF16)
    sgub = jnp.repeat(sgu_b.T, ch, axis=1)

    x1, hp, logits = _mix(y, u, gu, vn, g1, g2, x2, s5_d, w_glu.astype(BF16), b_glu, sguw, sgub,
                          w_branch_s5.astype(BF16), w_branch_sgu.astype(BF16), w_out.astype(BF16),
                          norm_post_mix, gt_m, norm_pre_ffn, sh_f, sc_f, w_router.astype(BF16), t_len, tm)

    tn = min(512, n)
    idx, wts, counts = _route(logits.T, router_bias, tn)
    rows = EXPERT_ROWS
    cnt = counts.reshape(n_exp).astype(I32)
    nblk = (cnt + rows - 1) // rows
    blk_end = jnp.cumsum(nblk)
    blk_start = blk_end - nblk
    cap = ((n * TOP_K) // rows + n_exp) * rows
    dest = _dest(idx, (blk_start * rows).astype(F32).reshape(n_exp, 1), tn)
    pieces = d // LANES
    dest = dest.T.reshape(n * TOP_K) * pieces

    xs = _dispatch(dest, hp, cap, min(512, t_len), pieces)
    ys = _experts(cnt, blk_start.astype(I32), blk_end[-1:].astype(I32), xs, w_exp_gate, w_exp_up, w_exp_down, rows)
    wsgu = jnp.concatenate([w_sh_gate, w_sh_up], axis=1).astype(BF16)
    out = _combine(dest, ys, wts.T.reshape(n * TOP_K), hp, x1, wsgu, w_sh_down.astype(BF16), norm_post_ffn,
                   gt_f, t_len, min(512, t_len))
    return out.reshape(bsz, t_len, d)


def kernel(x, c, ctx, c_ctx, w_mod, b_mod, norm_pre_mix, norm_post_mix, norm_pre_ffn, norm_post_ffn, w_in, s5_a_re, s5_a_im, s5_log_dt, s5_b_re, s5_b_im, s5_c_re, s5_c_im, s5_d, w_glu, b_glu, sgu_norm, sgu_w, sgu_b, w_branch_s5, w_branch_sgu, w_out, w_router, router_bias, w_exp_gate, w_exp_up, w_exp_down, w_sh_gate, w_sh_up, w_sh_down):
    depth = w_mod.shape[0]
    assert depth == 1, "the context stream is only carried through the last layer's S5 states"
    bsz = x.shape[0]
    pad = (-(bsz + 1)) % 8
    cpad = jnp.concatenate([c, c_ctx[None, :], jnp.zeros((pad, c.shape[1]), c.dtype)], axis=0)
    mod = _modulation(cpad, w_mod[0], b_mod[0])
    return _layer(x, ctx, mod[:bsz], mod[bsz], norm_pre_mix[0], norm_post_mix[0], norm_pre_ffn[0],
                  norm_post_ffn[0], w_in[0], s5_a_re[0], s5_a_im[0], s5_log_dt[0], s5_b_re[0], s5_b_im[0],
                  s5_c_re[0], s5_c_im[0], s5_d[0], w_glu[0], b_glu[0], sgu_norm[0], sgu_w[0], sgu_b[0],
                  w_branch_s5[0], w_branch_sgu[0], w_out[0], w_router[0], router_bias[0],
                  w_exp_gate[0], w_exp_up[0], w_exp_down[0], w_sh_gate[0], w_sh_up[0], w_sh_down[0])
```

```python
import functools
import math

import jax
import jax.numpy as jnp
from jax import lax
from jax.experimental import pallas as pl
from jax.experimental.pallas import tpu as pltpu

F32 = jnp.float32
BF16 = jnp.bfloat16
I32 = jnp.int32

EPS = 1e-6
S5_GROUP_CH = 16
S5_STATE = 64
S5_CHUNK = 64
SGU_GROUPS = 8
SGU_CHUNK = 128
N_EXPERT_GROUPS = 8
TOPK_GROUPS = 4
TOP_K = 8
ROUTE_SCALE = 2.5
LANES = 128
EXPERT_ROWS = 256
EXPERT_X_SLOTS = 6
EXPERT_Y_SLOTS = 4
VMEM_LIMIT = 56 * 1024 * 1024


def _params(sem):
    return pltpu.CompilerParams(dimension_semantics=sem, vmem_limit_bytes=VMEM_LIMIT)


def _rms(v, g):
    return v * lax.rsqrt(jnp.mean(v * v, axis=-1, keepdims=True) + EPS) * g


def _dot(a, b):
    return jnp.dot(a, b, preferred_element_type=F32)


def _store_row_tiles(ref, first_row, val):
    pieces = val.shape[1] // LANES
    for s in range(pieces):
        ref[pl.ds(first_row * pieces + s, val.shape[0], stride=pieces), :] = val[:, s * LANES:(s + 1) * LANES]


def _load_row_tiles(ref, first_row, n_rows, pieces, lead=()):
    return jnp.concatenate([ref[lead + (pl.ds(first_row * pieces + s, n_rows, stride=pieces), slice(None))]
                            for s in range(pieces)], axis=1)


def _mod_kernel(c_ref, w_ref, b_ref, o_ref):
    cv = c_ref[...]
    s = cv * jax.nn.sigmoid(cv)
    o_ref[...] = _dot(s.astype(BF16), w_ref[...].astype(BF16)) + b_ref[...]


def _modulation(cpad, w_mod, b_mod):
    d, n6 = w_mod.shape
    tn = 1024
    return pl.pallas_call(
        _mod_kernel,
        out_shape=jax.ShapeDtypeStruct((cpad.shape[0], n6), F32),
        grid=(n6 // tn,),
        in_specs=[pl.BlockSpec(cpad.shape, lambda j: (0, 0)),
                  pl.BlockSpec((d, tn), lambda j: (0, j)),
                  pl.BlockSpec((1, tn), lambda j: (0, j))],
        out_specs=pl.BlockSpec((cpad.shape[0], tn), lambda j: (0, j)),
        compiler_params=_params(("parallel",)),
        name="mod",
    )(cpad, w_mod, b_mod.reshape(1, n6))


def _inproj_kernel(x_ref, g_ref, sh_ref, sc_ref, w_ref, sgun_ref,
                   u_ref, ut_ref, gu_ref, vn_ref, g1_ref, g2_ref, *, d_s5, d_sgu, d_model):
    h = _rms(x_ref[...], g_ref[...])
    hb = (h * (1.0 + sc_ref[0]) + sh_ref[0]).astype(BF16)
    o1, o2, o3, o4 = d_s5, d_s5 + d_sgu, d_s5 + 2 * d_sgu, d_s5 + 2 * d_sgu + d_model
    u = _dot(hb, w_ref[:, 0:o1])
    u_ref[...] = u.astype(BF16)
    ut_ref[...] = u.T.astype(BF16)
    gu_ref[...] = jax.nn.gelu(_dot(hb, w_ref[:, o1:o2])).astype(BF16)
    v = jax.nn.gelu(_dot(hb, w_ref[:, o2:o3]))
    vn_ref[...] = _rms(v, sgun_ref[...]).astype(BF16)
    g1_ref[...] = jax.nn.sigmoid(_dot(hb, w_ref[:, o3:o4])).astype(BF16)
    g2_ref[...] = jax.nn.sigmoid(_dot(hb, w_ref[:, o4:o4 + d_model])).astype(BF16)


def _inproj(x2, norm_g, shift, scale, w_in_bf, sgu_norm, tokens_per_batch, tm, d_s5, d_sgu):
    n, d = x2.shape
    tiles_per_batch = tokens_per_batch // tm
    row = lambda i: (i, 0)
    fixed = lambda i: (0, 0)
    per_batch = lambda i: (i // tiles_per_batch, 0, 0)
    return pl.pallas_call(
        functools.partial(_inproj_kernel, d_s5=d_s5, d_sgu=d_sgu, d_model=d),
        out_shape=(jax.ShapeDtypeStruct((n, d_s5), BF16), jax.ShapeDtypeStruct((d_s5, n), BF16),
                   jax.ShapeDtypeStruct((n, d_sgu), BF16),
                   jax.ShapeDtypeStruct((n, d_sgu), BF16), jax.ShapeDtypeStruct((n, d), BF16),
                   jax.ShapeDtypeStruct((n, d), BF16)),
        grid=(n // tm,),
        in_specs=[pl.BlockSpec((tm, d), row), pl.BlockSpec((1, d), fixed),
                  pl.BlockSpec((1, 1, d), per_batch), pl.BlockSpec((1, 1, d), per_batch),
                  pl.BlockSpec(w_in_bf.shape, fixed), pl.BlockSpec((1, d_sgu), fixed)],
        out_specs=(pl.BlockSpec((tm, d_s5), row), pl.BlockSpec((d_s5, tm), lambda i: (0, i)),
                   pl.BlockSpec((tm, d_sgu), row),
                   pl.BlockSpec((tm, d_sgu), row), pl.BlockSpec((tm, d), row), pl.BlockSpec((tm, d), row)),
        compiler_params=_params(("parallel",)),
        name="inproj",
    )(x2, norm_g.reshape(1, d), shift, scale, w_in_bf, sgu_norm.reshape(1, d_sgu))


def _inproj_u_kernel(x_ref, g_ref, sh_ref, sc_ref, w_ref, u_ref):
    h = _rms(x_ref[...], g_ref[...])
    hb = (h * (1.0 + sc_ref[...]) + sh_ref[...]).astype(BF16)
    u_ref[...] = _dot(hb, w_ref[...]).T.astype(BF16)


def _inproj_u(x2, norm_g, shift, scale, w_u_bf, tm):
    n, d = x2.shape
    d_s5 = w_u_bf.shape[1]
    row = lambda i: (i, 0)
    fixed = lambda i: (0, 0)
    return pl.pallas_call(
        _inproj_u_kernel,
        out_shape=jax.ShapeDtypeStruct((d_s5, n), BF16),
        grid=(n // tm,),
        in_specs=[pl.BlockSpec((tm, d), row), pl.BlockSpec((1, d), fixed), pl.BlockSpec((1, d), fixed),
                  pl.BlockSpec((1, d), fixed), pl.BlockSpec(w_u_bf.shape, fixed)],
        out_specs=pl.BlockSpec((d_s5, tm), lambda i: (0, i)),
        compiler_params=_params(("parallel",)),
        name="inproj_ctx",
    )(x2, norm_g.reshape(1, d), shift.reshape(1, d), scale.reshape(1, d), w_u_bf)


def _s5_tables(a_re, a_im, log_dt, b_re, b_im, c_re, c_im, chunk, n_steps):
    hi = lax.Precision.HIGHEST
    g_n, p_n, j_n = a_re.shape[1], a_re.shape[2], b_re.shape[3]
    dt = jnp.exp(log_dt)[..., None]
    lam_re, lam_im = dt * a_re, dt * a_im
    ab_re, ab_im = jnp.exp(lam_re) * jnp.cos(lam_im), jnp.exp(lam_re) * jnp.sin(lam_im)
    den = a_re * a_re + a_im * a_im
    q_re = ((ab_re - 1.0) * a_re + ab_im * a_im) / den
    q_im = (ab_im * a_re - (ab_re - 1.0) * a_im) / den
    bb_re = q_re[..., None] * b_re - q_im[..., None] * b_im
    bb_im = q_re[..., None] * b_im + q_im[..., None] * b_re
    k = jnp.arange(chunk + 1, dtype=F32)[:, None, None, None]
    mag = jnp.exp(k * lam_re[None])
    pw_re, pw_im = mag * jnp.cos(k * lam_im[None]), mag * jnp.sin(k * lam_im[None])

    ct_re, ct_im = c_re.transpose(0, 1, 3, 2), c_im.transpose(0, 1, 3, 2)

    def lag_kernels(d):
        w_re = bb_re[d][:, :, :, None] * ct_re[d][:, :, None, :] - bb_im[d][:, :, :, None] * ct_im[d][:, :, None, :]
        w_im = bb_re[d][:, :, :, None] * ct_im[d][:, :, None, :] + bb_im[d][:, :, :, None] * ct_re[d][:, :, None, :]
        w = jnp.concatenate([w_re, -w_im], axis=1).reshape(g_n, 2 * p_n, j_n * j_n)
        a = jnp.concatenate([pw_re[:chunk, d], pw_im[:chunk, d]], axis=-1).transpose(1, 0, 2)
        return jnp.einsum('gkp,gpn->gkn', a, w, precision=hi)

    kf, kb = lag_kernels(0), lag_kernels(1)
    lags = jnp.concatenate([kb[:, 1:][:, ::-1], kf[:, 0:1] + kb[:, 0:1], kf[:, 1:],
                            jnp.zeros_like(kf[:, 0:1])], axis=1)
    lags = lags.transpose(0, 2, 1).reshape(g_n, j_n, j_n, 2 * chunk)

    def both(f, b, axis):
        return jnp.concatenate([f, b], axis=axis)

    pt_re = both(pw_re[:chunk, 0][::-1], pw_re[:chunk, 1], -1).transpose(1, 0, 2)
    pt_im = both(pw_im[:chunk, 0][::-1], pw_im[:chunk, 1], -1).transpose(1, 0, 2)
    bt_re = both(bb_re[0].transpose(0, 2, 1), bb_re[1].transpose(0, 2, 1), -1)
    bt_im = both(bb_im[0].transpose(0, 2, 1), bb_im[1].transpose(0, 2, 1), -1)
    e_re = pt_re[:, None] * bt_re[:, :, None] - pt_im[:, None] * bt_im[:, :, None]
    e_im = pt_re[:, None] * bt_im[:, :, None] + pt_im[:, None] * bt_re[:, :, None]
    ke = jnp.concatenate([e_re, e_im], axis=-1).reshape(g_n, j_n * chunk, 4 * p_n)

    q_re = both(pw_re[1:chunk + 1, 0], pw_re[1:chunk + 1, 1][::-1], -1).transpose(1, 2, 0)
    q_im = both(pw_im[1:chunk + 1, 0], pw_im[1:chunk + 1, 1][::-1], -1).transpose(1, 2, 0)
    q_re, q_im = jnp.tile(q_re, (1, 1, j_n)), jnp.tile(q_im, (1, 1, j_n))
    cc_re = jnp.repeat(both(ct_re[0], ct_re[1], 1), chunk, axis=2)
    cc_im = jnp.repeat(both(ct_im[0], ct_im[1], 1), chunk, axis=2)
    kc = jnp.concatenate([q_re * cc_re - q_im * cc_im, -(q_re * cc_im + q_im * cc_re)], axis=1)

    e = (chunk * (2.0 ** jnp.arange(n_steps, dtype=F32)))[:, None, None, None]
    mg = jnp.exp(e * lam_re[None])
    al_re, al_im = mg * jnp.cos(e * lam_im[None]), mg * jnp.sin(e * lam_im[None])
    alp = jnp.concatenate([al_re[:, 0], al_re[:, 1], al_im[:, 0], al_im[:, 1]], axis=-1)
    return lags, ke.astype(BF16), kc.astype(BF16), alp.transpose(1, 0, 2)


def _build_toeplitz(lag_ref, m_s, chunk):
    j_n = lag_ref.shape[1]
    per_tile = LANES // chunk
    low = lax.broadcasted_iota(I32, (chunk, LANES), 1) < chunk
    for i in range(j_n):
        for q in range(j_n // per_tile):
            tile = None
            for p in range(per_tile):
                j = q * per_tile + p
                row = jnp.broadcast_to(lag_ref[0, i, j:j + 1, :], (chunk, LANES))
                rot = pltpu.roll(row, (p * chunk + LANES - (chunk - 1)) % LANES, 1, stride=1, stride_axis=0)
                tile = rot if tile is None else jnp.where(low, tile, rot)
            m_s[i * chunk:(i + 1) * chunk, q * LANES:(q + 1) * LANES] = tile.astype(BF16)


def _s5_kernel(*refs, nb, nc, readout):
    if readout:
        u_ref, ke_ref, alp_ref, init_ref, lag_ref, kc_ref, fin_ref, y_ref, m_s = refs
    else:
        u_ref, ke_ref, alp_ref, init_ref, fin_ref = refs
    rows = nb * nc
    half = 2 * S5_STATE
    u = u_ref[0]
    e = _dot(u, ke_ref[0])
    er, ei = e[:, :half], e[:, half:]
    r_idx = lax.broadcasted_iota(I32, (rows, half), 0)
    if nc & (nc - 1) == 0:
        n_idx, b_idx = r_idx & (nc - 1), r_idx >> (nc.bit_length() - 1)
    else:
        n_idx, b_idx = lax.rem(r_idx, nc), lax.div(r_idx, nc)
    is_f = lax.broadcasted_iota(I32, (rows, half), 1) < S5_STATE
    seen = jnp.where(is_f, n_idx, nc - 1 - n_idx)
    init = init_ref[0]
    init_r = jnp.zeros((rows, half), F32)
    init_i = jnp.zeros((rows, half), F32)
    for b in range(nb):
        init_r = jnp.where(b_idx == b, init[b:b + 1, :half], init_r)
        init_i = jnp.where(b_idx == b, init[b:b + 1, half:], init_i)
    alp = alp_ref[0]
    ar, ai = alp[0:1, :half], alp[0:1, half:]
    er = er + jnp.where(seen == 0, ar * init_r - ai * init_i, 0.0)
    ei = ei + jnp.where(seen == 0, ar * init_i + ai * init_r, 0.0)

    def from_prev(v, dist):
        return jnp.where(is_f, pltpu.roll(v, dist, 0), pltpu.roll(v, rows - dist, 0))

    d, s = 1, 0
    while d < nc:
        ar, ai = alp[s:s + 1, :half], alp[s:s + 1, half:]
        sr = jnp.where(seen >= d, from_prev(er, d), 0.0)
        si = jnp.where(seen >= d, from_prev(ei, d), 0.0)
        er, ei = er + (ar * sr - ai * si), ei + (ar * si + ai * sr)
        d, s = d * 2, s + 1
    is_f_row = lax.broadcasted_iota(I32, (1, half), 1) < S5_STATE
    for b in range(nb):
        lo, hi = b * nc, b * nc + nc - 1
        fin_ref[0, b:b + 1, :half] = jnp.where(is_f_row, er[hi:hi + 1], er[lo:lo + 1])
        fin_ref[0, b:b + 1, half:] = jnp.where(is_f_row, ei[hi:hi + 1], ei[lo:lo + 1])
    if readout:
        if nc > 1:
            sin_r = jnp.where(seen >= 1, from_prev(er, 1), init_r)
            sin_i = jnp.where(seen >= 1, from_prev(ei, 1), init_i)
        else:
            sin_r, sin_i = init_r, init_i
        s_in = jnp.concatenate([sin_r, sin_i], axis=1).astype(BF16)
        _build_toeplitz(lag_ref, m_s, S5_CHUNK)
        y_ref[0] = (_dot(u, m_s[...]) + _dot(s_in, kc_ref[0])).astype(BF16)


def _s5(ug, ke, alp, init, m=None, kc=None, *, nb, nc):
    g_n, rows, width = ug.shape
    readout = m is not None
    grp = lambda g: (g, 0, 0)
    in_specs = [pl.BlockSpec((1, rows, width), grp), pl.BlockSpec((1,) + ke.shape[1:], grp),
                pl.BlockSpec((1,) + alp.shape[1:], grp), pl.BlockSpec((1,) + init.shape[1:], grp)]
    out_shape = [jax.ShapeDtypeStruct(init.shape, F32)]
    out_specs = [pl.BlockSpec((1,) + init.shape[1:], grp)]
    args = [ug, ke, alp, init]
    scratch = []
    if readout:
        assert m.shape[3] == LANES and 2 * S5_CHUNK == LANES
        in_specs += [pl.BlockSpec((1,) + m.shape[1:], lambda g: (g, 0, 0, 0)),
                     pl.BlockSpec((1,) + kc.shape[1:], grp)]
        out_shape.append(jax.ShapeDtypeStruct(ug.shape, BF16))
        out_specs.append(pl.BlockSpec((1, rows, width), grp))
        args += [m, kc]
        scratch = [pltpu.VMEM((width, width), BF16)]
    return pl.pallas_call(
        functools.partial(_s5_kernel, nb=nb, nc=nc, readout=readout),
        out_shape=tuple(out_shape), grid=(g_n,), in_specs=in_specs, out_specs=tuple(out_specs),
        scratch_shapes=scratch,
        compiler_params=_params(("parallel",)),
        name="s5_readout" if readout else "s5_state",
    )(*args)


def _to_groups(ut, nb, nc, chunk, g_n):
    j_n = ut.shape[0] // g_n
    return ut.reshape(g_n, j_n, nb * nc, chunk).transpose(0, 2, 1, 3).reshape(g_n, nb * nc, j_n * chunk)


def _from_groups(y, nb, nc, chunk, g_n):
    j_n = y.shape[2] // chunk
    return y.reshape(g_n, nb * nc, j_n, chunk).transpose(0, 2, 1, 3).reshape(g_n * j_n, nb * nc * chunk)


def _mix_kernel(y_ref, u_ref, gu_ref, vn_ref, g1_ref, g2_ref, x_ref,
                d_ref, wglu_ref, bglu_ref, sguw_ref, sgub_ref, wb1_ref, wb2_ref, wout_ref,
                npost_ref, gt_ref, npre_ref, shf_ref, scf_ref, wr_ref,
                x1_ref, hp_ref, lg_ref, *, tm):
    y = jax.nn.gelu(y_ref[...].astype(F32).T + d_ref[...] * u_ref[...].astype(F32))
    y_s5 = y * jax.nn.sigmoid(_dot(y.astype(BF16), wglu_ref[...]) + bglu_ref[...])

    lane = lax.broadcasted_iota(I32, (1, LANES), 1)
    m_lo = (lane < LANES // 2).astype(F32).astype(BF16)
    m_hi = (lane >= LANES // 2).astype(F32).astype(BF16)
    vn = vn_ref[...]
    chunks = []
    for c in range(tm // SGU_CHUNK):
        vc = vn[c * SGU_CHUNK:(c + 1) * SGU_CHUNK, :]
        tiles = []
        for q in range(vc.shape[1] // LANES):
            vt = vc[:, q * LANES:(q + 1) * LANES]
            rhs = jnp.concatenate([vt * m_lo, vt * m_hi], axis=0)
            tiles.append(_dot(sguw_ref[q], rhs))
        chunks.append(jnp.concatenate(tiles, axis=1) + sgub_ref[...])
    mixed = jnp.concatenate(chunks, axis=0)
    y_sgu = gu_ref[...].astype(F32) * mixed

    merged = (g1_ref[...].astype(F32) * _dot(y_s5.astype(BF16), wb1_ref[...])
              + g2_ref[...].astype(F32) * _dot(y_sgu.astype(BF16), wb2_ref[...]))
    mx = _dot(merged.astype(BF16), wout_ref[...])
    x1 = x_ref[...] + gt_ref[0] * _rms(mx, npost_ref[...])
    x1_ref[...] = x1
    hp = _rms(x1, npre_ref[...]) * (1.0 + scf_ref[0]) + shf_ref[0]
    _store_row_tiles(hp_ref, 0, hp)
    lg_ref[...] = _dot(hp.astype(BF16), wr_ref[...])


def _mix(y, u, gu, vn, g1, g2, x2, s5_d, wglu, bglu, sguw, sgub, wb1, wb2, wout,
         npost, gt, npre, shf, scf, wr, tokens_per_batch, tm):
    n, d = x2.shape
    d_s5, d_sgu, n_exp = y.shape[0], gu.shape[1], wr.shape[1]
    tiles_per_batch = tokens_per_batch // tm
    row = lambda i: (i, 0)
    fixed = lambda i: (0, 0)
    fixed3 = lambda i: (0, 0, 0)
    per_batch = lambda i: (i // tiles_per_batch, 0, 0)
    vec = lambda w: pl.BlockSpec((1, w), fixed)
    return pl.pallas_call(
        functools.partial(_mix_kernel, tm=tm),
        out_shape=(jax.ShapeDtypeStruct((n, d), F32), jax.ShapeDtypeStruct((n * d // LANES, LANES), F32),
                   jax.ShapeDtypeStruct((n, n_exp), F32)),
        grid=(n // tm,),
        in_specs=[pl.BlockSpec((d_s5, tm), lambda i: (0, i)), pl.BlockSpec((tm, d_s5), row),
                  pl.BlockSpec((tm, d_sgu), row),
                  pl.BlockSpec((tm, d_sgu), row), pl.BlockSpec((tm, d), row), pl.BlockSpec((tm, d), row),
                  pl.BlockSpec((tm, d), row),
                  vec(d_s5), pl.BlockSpec(wglu.shape, fixed), vec(d_s5),
                  pl.BlockSpec(sguw.shape, fixed3), pl.BlockSpec(sgub.shape, fixed),
                  pl.BlockSpec(wb1.shape, fixed), pl.BlockSpec(wb2.shape, fixed), pl.BlockSpec(wout.shape, fixed),
                  vec(d), pl.BlockSpec((1, 1, d), per_batch), vec(d),
                  pl.BlockSpec((1, 1, d), per_batch), pl.BlockSpec((1, 1, d), per_batch),
                  pl.BlockSpec(wr.shape, fixed)],
        out_specs=(pl.BlockSpec((tm, d), row), pl.BlockSpec((tm * d // LANES, LANES), row),
                   pl.BlockSpec((tm, n_exp), row)),
        compiler_params=_params(("parallel",)),
        name="mix",
    )(y, u, gu, vn, g1, g2, x2, s5_d.reshape(1, d_s5), wglu, bglu.reshape(1, d_s5), sguw, sgub,
      wb1, wb2, wout, npost.reshape(1, d), gt, npre.reshape(1, d), shf, scf, wr)


def _route_kernel(lg_ref, bias_ref, idx_ref, w_ref, cnt_ref, *, n_exp, tn):
    per_group = n_exp // N_EXPERT_GROUPS
    neg = jnp.float32(-jnp.inf)

    scores = jax.nn.sigmoid(lg_ref[...])
    sel = scores + bias_ref[...]
    gs = []
    for g in range(N_EXPERT_GROUPS):
        sg = sel[g * per_group:(g + 1) * per_group, :]
        m1 = jnp.max(sg, axis=0, keepdims=True)
        is_m1 = sg == m1
        n_m1 = jnp.sum(jnp.where(is_m1, 1.0, 0.0), axis=0, keepdims=True)
        rest = jnp.max(jnp.where(is_m1, neg, sg), axis=0, keepdims=True)
        gs.append(m1 + jnp.where(n_m1 >= 2.0, m1, rest))
    gsm = jnp.concatenate(gs, axis=0)
    g_iota = lax.broadcasted_iota(I32, gsm.shape, 0)
    e_iota = lax.broadcasted_iota(I32, sel.shape, 0).astype(F32)
    masked = []
    for g in range(N_EXPERT_GROUPS):
        mine = gsm[g:g + 1, :]
        beats = jnp.where(gsm > mine, 1.0, jnp.where(gsm == mine, jnp.where(g_iota < g, 1.0, 0.0), 0.0))
        n_beats = jnp.sum(beats, axis=0, keepdims=True)
        masked.append(jnp.where(n_beats < float(TOPK_GROUPS), sel[g * per_group:(g + 1) * per_group, :], neg))
    start = jnp.concatenate(masked, axis=0)
    selm = start
    picked, vals = [], []
    for _ in range(TOP_K):
        m = jnp.max(selm, axis=0, keepdims=True)
        first = jnp.min(jnp.where(selm == m, e_iota, float(n_exp)), axis=0, keepdims=True)
        one = e_iota == first
        picked.append(first)
        vals.append(jnp.sum(jnp.where(one, scores, 0.0), axis=0, keepdims=True))
        selm = jnp.where(one, neg, selm)
    idx_ref[...] = jnp.concatenate(picked, axis=0).astype(I32)
    wv = jnp.concatenate(vals, axis=0)
    w_ref[...] = wv / jnp.sum(wv, axis=0, keepdims=True) * ROUTE_SCALE
    hot = jnp.where(selm == neg, jnp.where(start == neg, 0.0, 1.0), 0.0)

    @pl.when(pl.program_id(0) == 0)
    def _():
        cnt_ref[...] = jnp.zeros_like(cnt_ref)

    cnt_ref[...] += jnp.sum(hot, axis=1, keepdims=True)


def _route(logits_t, bias, tn):
    n_exp, n = logits_t.shape
    col = lambda i: (0, i)
    fixed = lambda i: (0, 0)
    return pl.pallas_call(
        functools.partial(_route_kernel, n_exp=n_exp, tn=tn),
        out_shape=(jax.ShapeDtypeStruct((TOP_K, n), I32), jax.ShapeDtypeStruct((TOP_K, n), F32),
                   jax.ShapeDtypeStruct((n_exp, 1), F32)),
        grid=(n // tn,),
        in_specs=[pl.BlockSpec((n_exp, tn), col), pl.BlockSpec((n_exp, 1), fixed)],
        out_specs=(pl.BlockSpec((TOP_K, tn), col), pl.BlockSpec((TOP_K, tn), col),
                   pl.BlockSpec((n_exp, 1), fixed)),
        compiler_params=_params(("arbitrary",)),
        name="route",
    )(logits_t, bias.reshape(n_exp, 1))


def _dest_kernel(idx_ref, start_ref, dest_ref, carry_ref, *, n_exp, tn):
    @pl.when(pl.program_id(0) == 0)
    def _():
        carry_ref[...] = start_ref[...]

    idx = idx_ref[...]
    e_iota = lax.broadcasted_iota(I32, (n_exp, tn), 0)
    hot = jnp.zeros((n_exp, tn), F32)
    for k in range(TOP_K):
        hot = hot + jnp.where(e_iota == idx[k:k + 1, :], 1.0, 0.0)
    before = jnp.where(lax.broadcasted_iota(I32, (tn, tn), 0) < lax.broadcasted_iota(I32, (tn, tn), 1), 1.0, 0.0)
    rank = _dot(hot.astype(BF16), before.astype(BF16)) + carry_ref[...]
    dest_ref[...] = jnp.concatenate(
        [jnp.sum(jnp.where(e_iota == idx[k:k + 1, :], rank, 0.0), axis=0, keepdims=True)
         for k in range(TOP_K)], axis=0).astype(I32)
    carry_ref[...] += jnp.sum(hot, axis=1, keepdims=True)


def _dest(idx, start, tn):
    n = idx.shape[1]
    n_exp = start.shape[0]
    col = lambda i: (0, i)
    return pl.pallas_call(
        functools.partial(_dest_kernel, n_exp=n_exp, tn=tn),
        out_shape=jax.ShapeDtypeStruct((TOP_K, n), I32),
        grid=(n // tn,),
        in_specs=[pl.BlockSpec((TOP_K, tn), col), pl.BlockSpec((n_exp, 1), lambda i: (0, 0))],
        out_specs=pl.BlockSpec((TOP_K, tn), col),
        scratch_shapes=[pltpu.VMEM((n_exp, 1), F32)],
        compiler_params=_params(("arbitrary",)),
        name="dest",
    )(idx, start)


def _dispatch_kernel(dest_ref, hp_ref, xs_ref, sem, *, tm, pieces):
    def start(r, c):
        src = hp_ref.at[pl.ds(pl.multiple_of(r * pieces, pieces), pieces), :]
        for k in range(TOP_K):
            row = pl.multiple_of(dest_ref[r * TOP_K + k], pieces)
            pltpu.make_async_copy(src, xs_ref.at[pl.ds(row, pieces), :], sem).start(priority=k % 2)
        return c

    lax.fori_loop(0, tm, start, 0, unroll=2)
    for k in range(TOP_K):
        pltpu.make_async_copy(hp_ref, xs_ref.at[pl.ds(0, tm * pieces), :], sem).wait()


def _dispatch(dest, hp, cap, tm, pieces):
    n = hp.shape[0] // pieces
    return pl.pallas_call(
        functools.partial(_dispatch_kernel, tm=tm, pieces=pieces),
        out_shape=jax.ShapeDtypeStruct((cap * pieces, LANES), hp.dtype),
        grid=(n // tm,),
        in_specs=[pl.BlockSpec((tm * TOP_K,), lambda i: (i,), memory_space=pltpu.SMEM),
                  pl.BlockSpec((tm * pieces, LANES), lambda i: (i, 0))],
        out_specs=pl.BlockSpec(memory_space=pl.ANY),
        scratch_shapes=[pltpu.SemaphoreType.DMA(())],
        compiler_params=_params(("arbitrary",)),
        name="dispatch",
    )(dest, hp)


def _expert_kernel(cnt_ref, first_ref, tot_ref, xs_hbm, wg_ref, wu_ref, wd_ref, ys_hbm,
                   xbuf, ybuf, wgu_s, wd_s, xsem, ysem, *, rows, pieces):
    e = pl.program_id(0)
    cnt, first, total = cnt_ref[e], first_ref[e], tot_ref[0]
    n_blk = (cnt + rows - 1) // rows
    d_e = wg_ref.shape[2]
    phys = rows * pieces

    def block_rows(g):
        return pl.ds(pl.multiple_of(g * phys, phys), phys)

    def x_copy(g, slot):
        return pltpu.make_async_copy(xs_hbm.at[block_rows(g), :], xbuf.at[slot], xsem.at[slot])

    def y_copy(g, slot):
        return pltpu.make_async_copy(ybuf.at[slot], ys_hbm.at[block_rows(g), :], ysem.at[slot])

    x_slots, y_slots = xbuf.shape[0], ybuf.shape[0]
    ahead = x_slots - 1

    @pl.when(n_blk > 0)
    def _():
        @pl.when(first == 0)
        def _():
            for a in range(ahead):
                @pl.when(a < total)
                def _():
                    x_copy(a, a).start()

        wgu_s[:, :d_e] = wg_ref[0].astype(BF16)
        wgu_s[:, d_e:] = wu_ref[0].astype(BF16)
        wd_s[...] = wd_ref[0].astype(BF16)

        def block(j, c):
            g = first + j
            slot = g % x_slots
            x_copy(g, slot).wait()

            @pl.when(g + ahead < total)
            def _():
                x_copy(g + ahead, (g + ahead) % x_slots).start()

            yslot = g % y_slots

            @pl.when(g >= y_slots)
            def _():
                y_copy(g - y_slots, yslot).wait()

            x = _load_row_tiles(xbuf, 0, rows, pieces, lead=(slot,))
            r_iota = lax.broadcasted_iota(I32, x.shape, 0) + j * rows
            xb = jnp.where(r_iota < cnt, x, 0.0).astype(BF16)
            gu = _dot(xb, wgu_s[...])
            gate, up = gu[:, :d_e], gu[:, d_e:]
            hid = (gate * jax.nn.sigmoid(gate) * up).astype(BF16)
            y = _dot(hid, wd_s[...])
            for s in range(pieces):
                ybuf[yslot, pl.ds(s, rows, stride=pieces), :] = y[:, s * LANES:(s + 1) * LANES]
            y_copy(g, yslot).start()
            return c

        lax.fori_loop(0, n_blk, block, 0)

        @pl.when(first + n_blk == total)
        def _():
            for back in range(1, y_slots + 1):
                @pl.when(total >= back)
                def _():
                    y_copy(total - back, (total - back) % y_slots).wait()


def _experts(cnt, blk_first, n_used, xs, w_gate, w_up, w_down, rows):
    n_exp, d, d_e = w_gate.shape
    pieces = d // LANES
    hbm = pl.BlockSpec(memory_space=pl.ANY)
    w_map = lambda e, cnt, first, tot: (e, 0, 0)
    grid_spec = pltpu.PrefetchScalarGridSpec(
        num_scalar_prefetch=3,
        grid=(n_exp,),
        in_specs=[hbm, pl.BlockSpec((1, d, d_e), w_map), pl.BlockSpec((1, d, d_e), w_map),
                  pl.BlockSpec((1, d_e, d), w_map)],
        out_specs=hbm,
        scratch_shapes=[pltpu.VMEM((EXPERT_X_SLOTS, rows * pieces, LANES), F32),
                        pltpu.VMEM((EXPERT_Y_SLOTS, rows * pieces, LANES), F32),
                        pltpu.VMEM((d, 2 * d_e), BF16), pltpu.VMEM((d_e, d), BF16),
                        pltpu.SemaphoreType.DMA((EXPERT_X_SLOTS,)), pltpu.SemaphoreType.DMA((EXPERT_Y_SLOTS,))],
    )
    return pl.pallas_call(
        functools.partial(_expert_kernel, rows=rows, pieces=pieces),
        out_shape=jax.ShapeDtypeStruct(xs.shape, F32),
        grid_spec=grid_spec,
        compiler_params=_params(("arbitrary",)),
        name="experts",
    )(cnt, blk_first, n_used, xs, w_gate, w_up, w_down)


def _combine_kernel(dest_ref, next_ref, ys_ref, wt_ref, hp_ref, x1_ref, wsgu_ref, wsd_ref, npost_ref, gt_ref,
                    o_ref, buf, acc, sem, *, tm, pieces):
    i = pl.program_id(0)
    last = pl.num_programs(0) - 1
    cur = i % 2

    def start_token(rows_ref, half, r):
        slot = pl.ds(pl.multiple_of(r * pieces, pieces), pieces)
        for k in range(TOP_K):
            row = pl.multiple_of(rows_ref[r * TOP_K + k], pieces)
            pltpu.make_async_copy(ys_ref.at[pl.ds(row, pieces), :], buf.at[half, k, slot, :],
                                  sem.at[half]).start(priority=k % 2)

    def wait_half(half):
        for k in range(TOP_K):
            pltpu.make_async_copy(ys_ref.at[pl.ds(0, tm * pieces), :], buf.at[half, k], sem.at[half]).wait()

    @pl.when(i == 0)
    def _():
        def first(r, c):
            start_token(dest_ref, 0, r)
            return c

        lax.fori_loop(0, tm, first, 0)

    def tile_pass(half):
        wait_half(half)

        def token(r, c):
            start_token(next_ref, 1 - half, r)
            slot = pl.ds(pl.multiple_of(r * pieces, pieces), pieces)
            terms = [wt_ref[r * TOP_K + k] * buf[half, k, slot, :] for k in range(TOP_K)]
            while len(terms) > 1:
                terms = [terms[j] + terms[j + 1] for j in range(0, len(terms), 2)]
            acc[slot, :] = terms[0]
            return c

        lax.fori_loop(0, tm, token, 0, unroll=4)

        @pl.when(i == last)
        def _():
            wait_half(1 - half)

    for half in range(2):
        @pl.when(cur == half)
        def _():
            tile_pass(half)

    hb = _load_row_tiles(hp_ref, 0, tm, pieces).astype(BF16)
    d_sh = wsd_ref.shape[0]
    gu = _dot(hb, wsgu_ref[...])
    gate, up = gu[:, :d_sh], gu[:, d_sh:]
    shared = _dot((gate * jax.nn.sigmoid(gate) * up).astype(BF16), wsd_ref[...])
    fx = shared + _load_row_tiles(acc, 0, tm, pieces)
    o_ref[...] = x1_ref[...] + gt_ref[0] * _rms(fx, npost_ref[...])


def _combine(dest, ys, wt, hp, x1, wsgu, wsd, npost, gt, tokens_per_batch, tm):
    n, d = x1.shape
    pieces = d // LANES
    w = LANES
    tiles_per_batch = tokens_per_batch // tm
    row = lambda i: (i, 0)
    fixed = lambda i: (0, 0)
    return pl.pallas_call(
        functools.partial(_combine_kernel, tm=tm, pieces=pieces),
        out_shape=jax.ShapeDtypeStruct((n, d), F32),
        grid=(n // tm,),
        in_specs=[pl.BlockSpec((tm * TOP_K,), lambda i: (i,), memory_space=pltpu.SMEM),
                  pl.BlockSpec((tm * TOP_K,), lambda i: (jnp.minimum(i + 1, n // tm - 1),), memory_space=pltpu.SMEM),
                  pl.BlockSpec(memory_space=pl.ANY),
                  pl.BlockSpec((tm * TOP_K,), lambda i: (i,), memory_space=pltpu.SMEM),
                  pl.BlockSpec((tm * pieces, w), row), pl.BlockSpec((tm, d), row),
                  pl.BlockSpec(wsgu.shape, fixed), pl.BlockSpec(wsd.shape, fixed),
                  pl.BlockSpec((1, d), fixed), pl.BlockSpec((1, 1, d), lambda i: (i // tiles_per_batch, 0, 0))],
        out_specs=pl.BlockSpec((tm, d), row),
        scratch_shapes=[pltpu.VMEM((2, TOP_K, tm * pieces, w), F32), pltpu.VMEM((tm * pieces, w), F32),
                        pltpu.SemaphoreType.DMA((2,))],
        compiler_params=_params(("arbitrary",)),
        name="combine",
    )(dest, dest, ys, wt, hp, x1, wsgu, wsd, npost.reshape(1, d), gt)


def _layer(x, ctx, mod_x, mod_c, norm_pre_mix, norm_post_mix, norm_pre_ffn, norm_post_ffn,
           w_in, s5_a_re, s5_a_im, s5_log_dt, s5_b_re, s5_b_im, s5_c_re, s5_c_im, s5_d, w_glu, b_glu,
           sgu_norm, sgu_w, sgu_b, w_branch_s5, w_branch_sgu, w_out,
           w_router, router_bias, w_exp_gate, w_exp_up, w_exp_down, w_sh_gate, w_sh_up, w_sh_down):
    bsz, t_len, d = x.shape
    c_len = ctx.shape[1]
    n = bsz * t_len
    d_s5, d_sgu = w_glu.shape[0], sgu_norm.shape[0]
    g_n = d_s5 // S5_GROUP_CH
    n_exp = w_router.shape[1]
    chunk = S5_CHUNK
    tm = min(512, t_len)
    assert t_len % tm == 0 and tm % SGU_CHUNK == 0 and t_len % chunk == 0 and c_len % chunk == 0

    sh_m, sc_m, gt_m, sh_f, sc_f, gt_f = [v.reshape(bsz, 1, d) for v in jnp.split(mod_x, 6, axis=-1)]
    csh_m, csc_m = mod_c[:d], mod_c[d:2 * d]

    w_in_bf = w_in.astype(BF16)

    nc_lat, nc_ctx = t_len // chunk, c_len // chunk
    n_steps = max(1, (max(nc_lat, nc_ctx) - 1).bit_length())
    m_t, ke_t, kc_t, alp = _s5_tables(s5_a_re, s5_a_im, s5_log_dt, s5_b_re, s5_b_im, s5_c_re, s5_c_im,
                                      chunk, n_steps)

    ctx2 = ctx.reshape(bsz * c_len, d)
    u_ctx = _inproj_u(ctx2, norm_pre_mix, csh_m, csc_m, w_in_bf[:, :d_s5], min(512, bsz * c_len))
    zero_init = jnp.zeros((g_n, bsz, 4 * S5_STATE), F32)
    (ctx_final,) = _s5(_to_groups(u_ctx, bsz, nc_ctx, chunk, g_n), ke_t, alp, zero_init, nb=bsz, nc=nc_ctx)

    x2 = x.reshape(n, d)
    u, ut, gu, vn, g1, g2 = _inproj(x2, norm_pre_mix, sh_m, sc_m, w_in_bf, sgu_norm, t_len, tm, d_s5, d_sgu)
    _, yg = _s5(_to_groups(ut, bsz, nc_lat, chunk, g_n), ke_t, alp, ctx_final, m_t, kc_t, nb=bsz, nc=nc_lat)
    y = _from_groups(yg, bsz, nc_lat, chunk, g_n)

    ch = d_sgu // SGU_GROUPS
    per_tile = LANES // ch
    sguw = sgu_w.reshape(SGU_GROUPS // per_tile, per_tile, SGU_CHUNK, SGU_CHUNK)
    sguw = sguw.transpose(0, 2, 1, 3).reshape(SGU_GROUPS // per_tile, SGU_CHUNK, per_tile * SGU_CHUNK).astype(BF16)
    sgub = jnp.repeat(sgu_b.T, ch, axis=1)

    x1, hp, logits = _mix(y, u, gu, vn, g1, g2, x2, s5_d, w_glu.astype(BF16), b_glu, sguw, sgub,
                          w_branch_s5.astype(BF16), w_branch_sgu.astype(BF16), w_out.astype(BF16),
                          norm_post_mix, gt_m, norm_pre_ffn, sh_f, sc_f, w_router.astype(BF16), t_len, tm)

    tn = min(512, n)
    idx, wts, counts = _route(logits.T, router_bias, tn)
    rows = EXPERT_ROWS
    cnt = counts.reshape(n_exp).astype(I32)
    nblk = (cnt + rows - 1) // rows
    blk_end = jnp.cumsum(nblk)
    blk_start = blk_end - nblk
    cap = ((n * TOP_K) // rows + n_exp) * rows
    dest = _dest(idx, (blk_start * rows).astype(F32).reshape(n_exp, 1), tn)
    pieces = d // LANES
    dest = dest.T.reshape(n * TOP_K) * pieces

    xs = _dispatch(dest, hp, cap, min(512, t_len), pieces)
    ys = _experts(cnt, blk_start.astype(I32), blk_end[-1:].astype(I32), xs, w_exp_gate, w_exp_up, w_exp_down, rows)
    wsgu = jnp.concatenate([w_sh_gate, w_sh_up], axis=1).astype(BF16)
    out = _combine(dest, ys, wts.T.reshape(n * TOP_K), hp, x1, wsgu, w_sh_down.astype(BF16), norm_post_ffn,
                   gt_f, t_len, min(512, t_len))
    return out.reshape(bsz, t_len, d)


def kernel(x, c, ctx, c_ctx, w_mod, b_mod, norm_pre_mix, norm_post_mix, norm_pre_ffn, norm_post_ffn, w_in, s5_a_re, s5_a_im, s5_log_dt, s5_b_re, s5_b_im, s5_c_re, s5_c_im, s5_d, w_glu, b_glu, sgu_norm, sgu_w, sgu_b, w_branch_s5, w_branch_sgu, w_out, w_router, router_bias, w_exp_gate, w_exp_up, w_exp_down, w_sh_gate, w_sh_up, w_sh_down):
    depth = w_mod.shape[0]
    assert depth == 1, "the context stream is only carried through the last layer's S5 states"
    bsz = x.shape[0]
    pad = (-(bsz + 1)) % 8
    cpad = jnp.concatenate([c, c_ctx[None, :], jnp.zeros((pad, c.shape[1]), c.dtype)], axis=0)
    mod = _modulation(cpad, w_mod[0], b_mod[0])
    return _layer(x, ctx, mod[:bsz], mod[bsz], norm_pre_mix[0], norm_post_mix[0], norm_pre_ffn[0],
                  norm_post_ffn[0], w_in[0], s5_a_re[0], s5_a_im[0], s5_log_dt[0], s5_b_re[0], s5_b_im[0],
                  s5_c_re[0], s5_c_im[0], s5_d[0], w_glu[0], b_glu[0], sgu_norm[0], sgu_w[0], sgu_b[0],
                  w_branch_s5[0], w_branch_sgu[0], w_out[0], w_router[0], router_bias[0],
                  w_exp_gate[0], w_exp_up[0], w_exp_down[0], w_sh_gate[0], w_sh_up[0], w_sh_down[0])
```

```python
import functools
import math

import jax
import jax.numpy as jnp
from jax import lax
from jax.experimental import pallas as pl
from jax.experimental.pallas import tpu as pltpu

F32 = jnp.float32
BF16 = jnp.bfloat16
I32 = jnp.int32

EPS = 1e-6
S5_GROUP_CH = 16
S5_STATE = 64
S5_CHUNK = 64
SGU_GROUPS = 8
SGU_CHUNK = 128
N_EXPERT_GROUPS = 8
TOPK_GROUPS = 4
TOP_K = 8
ROUTE_SCALE = 2.5
LANES = 128
EXPERT_ROWS = 256
EXPERT_X_SLOTS = 8
EXPERT_Y_SLOTS = 6
VMEM_LIMIT = 56 * 1024 * 1024


def _params(sem):
    return pltpu.CompilerParams(dimension_semantics=sem, vmem_limit_bytes=VMEM_LIMIT)


def _rms(v, g):
    return v * lax.rsqrt(jnp.mean(v * v, axis=-1, keepdims=True) + EPS) * g


def _dot(a, b):
    return jnp.dot(a, b, preferred_element_type=F32)


def _store_row_tiles(ref, first_row, val):
    pieces = val.shape[1] // LANES
    for s in range(pieces):
        ref[pl.ds(first_row * pieces + s, val.shape[0], stride=pieces), :] = val[:, s * LANES:(s + 1) * LANES]


def _load_row_tiles(ref, first_row, n_rows, pieces, lead=()):
    return jnp.concatenate([ref[lead + (pl.ds(first_row * pieces + s, n_rows, stride=pieces), slice(None))]
                            for s in range(pieces)], axis=1)


def _mod_kernel(c_ref, w_ref, b_ref, o_ref):
    cv = c_ref[...]
    s = cv * jax.nn.sigmoid(cv)
    o_ref[...] = _dot(s.astype(BF16), w_ref[...].astype(BF16)) + b_ref[...]


def _modulation(cpad, w_mod, b_mod):
    d, n6 = w_mod.shape
    tn = 1024
    return pl.pallas_call(
        _mod_kernel,
        out_shape=jax.ShapeDtypeStruct((cpad.shape[0], n6), F32),
        grid=(n6 // tn,),
        in_specs=[pl.BlockSpec(cpad.shape, lambda j: (0, 0)),
                  pl.BlockSpec((d, tn), lambda j: (0, j)),
                  pl.BlockSpec((1, tn), lambda j: (0, j))],
        out_specs=pl.BlockSpec((cpad.shape[0], tn), lambda j: (0, j)),
        compiler_params=_params(("parallel",)),
        name="mod",
    )(cpad, w_mod, b_mod.reshape(1, n6))


def _inproj_kernel(x_ref, g_ref, sh_ref, sc_ref, w_ref, sgun_ref,
                   u_ref, ut_ref, gu_ref, vn_ref, g1_ref, g2_ref, *, d_s5, d_sgu, d_model):
    h = _rms(x_ref[...], g_ref[...])
    hb = (h * (1.0 + sc_ref[0]) + sh_ref[0]).astype(BF16)
    o1, o2, o3, o4 = d_s5, d_s5 + d_sgu, d_s5 + 2 * d_sgu, d_s5 + 2 * d_sgu + d_model
    u = _dot(hb, w_ref[:, 0:o1])
    u_ref[...] = u.astype(BF16)
    ut_ref[...] = u.T.astype(BF16)
    gu_ref[...] = jax.nn.gelu(_dot(hb, w_ref[:, o1:o2])).astype(BF16)
    v = jax.nn.gelu(_dot(hb, w_ref[:, o2:o3]))
    vn_ref[...] = _rms(v, sgun_ref[...]).astype(BF16)
    g1_ref[...] = jax.nn.sigmoid(_dot(hb, w_ref[:, o3:o4])).astype(BF16)
    g2_ref[...] = jax.nn.sigmoid(_dot(hb, w_ref[:, o4:o4 + d_model])).astype(BF16)


def _inproj(x2, norm_g, shift, scale, w_in_bf, sgu_norm, tokens_per_batch, tm, d_s5, d_sgu):
    n, d = x2.shape
    tiles_per_batch = tokens_per_batch // tm
    row = lambda i: (i, 0)
    fixed = lambda i: (0, 0)
    per_batch = lambda i: (i // tiles_per_batch, 0, 0)
    return pl.pallas_call(
        functools.partial(_inproj_kernel, d_s5=d_s5, d_sgu=d_sgu, d_model=d),
        out_shape=(jax.ShapeDtypeStruct((n, d_s5), BF16), jax.ShapeDtypeStruct((d_s5, n), BF16),
                   jax.ShapeDtypeStruct((n, d_sgu), BF16),
                   jax.ShapeDtypeStruct((n, d_sgu), BF16), jax.ShapeDtypeStruct((n, d), BF16),
                   jax.ShapeDtypeStruct((n, d), BF16)),
        grid=(n // tm,),
        in_specs=[pl.BlockSpec((tm, d), row), pl.BlockSpec((1, d), fixed),
                  pl.BlockSpec((1, 1, d), per_batch), pl.BlockSpec((1, 1, d), per_batch),
                  pl.BlockSpec(w_in_bf.shape, fixed), pl.BlockSpec((1, d_sgu), fixed)],
        out_specs=(pl.BlockSpec((tm, d_s5), row), pl.BlockSpec((d_s5, tm), lambda i: (0, i)),
                   pl.BlockSpec((tm, d_sgu), row),
                   pl.BlockSpec((tm, d_sgu), row), pl.BlockSpec((tm, d), row), pl.BlockSpec((tm, d), row)),
        compiler_params=_params(("parallel",)),
        name="inproj",
    )(x2, norm_g.reshape(1, d), shift, scale, w_in_bf, sgu_norm.reshape(1, d_sgu))


def _inproj_u_kernel(x_ref, g_ref, sh_ref, sc_ref, w_ref, u_ref):
    h = _rms(x_ref[...], g_ref[...])
    hb = (h * (1.0 + sc_ref[...]) + sh_ref[...]).astype(BF16)
    u_ref[...] = _dot(hb, w_ref[...]).T.astype(BF16)


def _inproj_u(x2, norm_g, shift, scale, w_u_bf, tm):
    n, d = x2.shape
    d_s5 = w_u_bf.shape[1]
    row = lambda i: (i, 0)
    fixed = lambda i: (0, 0)
    return pl.pallas_call(
        _inproj_u_kernel,
        out_shape=jax.ShapeDtypeStruct((d_s5, n), BF16),
        grid=(n // tm,),
        in_specs=[pl.BlockSpec((tm, d), row), pl.BlockSpec((1, d), fixed), pl.BlockSpec((1, d), fixed),
                  pl.BlockSpec((1, d), fixed), pl.BlockSpec(w_u_bf.shape, fixed)],
        out_specs=pl.BlockSpec((d_s5, tm), lambda i: (0, i)),
        compiler_params=_params(("parallel",)),
        name="inproj_ctx",
    )(x2, norm_g.reshape(1, d), shift.reshape(1, d), scale.reshape(1, d), w_u_bf)


def _s5_tables(a_re, a_im, log_dt, b_re, b_im, c_re, c_im, chunk, n_steps):
    hi = lax.Precision.HIGHEST
    g_n, p_n, j_n = a_re.shape[1], a_re.shape[2], b_re.shape[3]
    dt = jnp.exp(log_dt)[..., None]
    lam_re, lam_im = dt * a_re, dt * a_im
    ab_re, ab_im = jnp.exp(lam_re) * jnp.cos(lam_im), jnp.exp(lam_re) * jnp.sin(lam_im)
    den = a_re * a_re + a_im * a_im
    q_re = ((ab_re - 1.0) * a_re + ab_im * a_im) / den
    q_im = (ab_im * a_re - (ab_re - 1.0) * a_im) / den
    bb_re = q_re[..., None] * b_re - q_im[..., None] * b_im
    bb_im = q_re[..., None] * b_im + q_im[..., None] * b_re
    k = jnp.arange(chunk + 1, dtype=F32)[:, None, None, None]
    mag = jnp.exp(k * lam_re[None])
    pw_re, pw_im = mag * jnp.cos(k * lam_im[None]), mag * jnp.sin(k * lam_im[None])

    ct_re, ct_im = c_re.transpose(0, 1, 3, 2), c_im.transpose(0, 1, 3, 2)

    def lag_kernels(d):
        w_re = bb_re[d][:, :, :, None] * ct_re[d][:, :, None, :] - bb_im[d][:, :, :, None] * ct_im[d][:, :, None, :]
        w_im = bb_re[d][:, :, :, None] * ct_im[d][:, :, None, :] + bb_im[d][:, :, :, None] * ct_re[d][:, :, None, :]
        w = jnp.concatenate([w_re, -w_im], axis=1).reshape(g_n, 2 * p_n, j_n * j_n)
        a = jnp.concatenate([pw_re[:chunk, d], pw_im[:chunk, d]], axis=-1).transpose(1, 0, 2)
        return jnp.einsum('gkp,gpn->gkn', a, w, precision=hi)

    kf, kb = lag_kernels(0), lag_kernels(1)
    lags = jnp.concatenate([kb[:, 1:][:, ::-1], kf[:, 0:1] + kb[:, 0:1], kf[:, 1:],
                            jnp.zeros_like(kf[:, 0:1])], axis=1)
    lags = lags.transpose(0, 2, 1).reshape(g_n, j_n, j_n, 2 * chunk)

    def both(f, b, axis):
        return jnp.concatenate([f, b], axis=axis)

    pt_re = both(pw_re[:chunk, 0][::-1], pw_re[:chunk, 1], -1).transpose(1, 0, 2)
    pt_im = both(pw_im[:chunk, 0][::-1], pw_im[:chunk, 1], -1).transpose(1, 0, 2)
    bt_re = both(bb_re[0].transpose(0, 2, 1), bb_re[1].transpose(0, 2, 1), -1)
    bt_im = both(bb_im[0].transpose(0, 2, 1), bb_im[1].transpose(0, 2, 1), -1)
    e_re = pt_re[:, None] * bt_re[:, :, None] - pt_im[:, None] * bt_im[:, :, None]
    e_im = pt_re[:, None] * bt_im[:, :, None] + pt_im[:, None] * bt_re[:, :, None]
    ke = jnp.concatenate([e_re, e_im], axis=-1).reshape(g_n, j_n * chunk, 4 * p_n)

    q_re = both(pw_re[1:chunk + 1, 0], pw_re[1:chunk + 1, 1][::-1], -1).transpose(1, 2, 0)
    q_im = both(pw_im[1:chunk + 1, 0], pw_im[1:chunk + 1, 1][::-1], -1).transpose(1, 2, 0)
    q_re, q_im = jnp.tile(q_re, (1, 1, j_n)), jnp.tile(q_im, (1, 1, j_n))
    cc_re = jnp.repeat(both(ct_re[0], ct_re[1], 1), chunk, axis=2)
    cc_im = jnp.repeat(both(ct_im[0], ct_im[1], 1), chunk, axis=2)
    kc = jnp.concatenate([q_re * cc_re - q_im * cc_im, -(q_re * cc_im + q_im * cc_re)], axis=1)

    e = (chunk * (2.0 ** jnp.arange(n_steps, dtype=F32)))[:, None, None, None]
    mg = jnp.exp(e * lam_re[None])
    al_re, al_im = mg * jnp.cos(e * lam_im[None]), mg * jnp.sin(e * lam_im[None])
    alp = jnp.concatenate([al_re[:, 0], al_re[:, 1], al_im[:, 0], al_im[:, 1]], axis=-1)
    return lags, ke.astype(BF16), kc.astype(BF16), alp.transpose(1, 0, 2)


def _build_toeplitz(lag_ref, m_s, chunk):
    j_n = lag_ref.shape[1]
    per_tile = LANES // chunk
    low = lax.broadcasted_iota(I32, (chunk, LANES), 1) < chunk
    for i in range(j_n):
        for q in range(j_n // per_tile):
            tile = None
            for p in range(per_tile):
                j = q * per_tile + p
                row = jnp.broadcast_to(lag_ref[0, i, j:j + 1, :], (chunk, LANES))
                rot = pltpu.roll(row, (p * chunk + LANES - (chunk - 1)) % LANES, 1, stride=1, stride_axis=0)
                tile = rot if tile is None else jnp.where(low, tile, rot)
            m_s[i * chunk:(i + 1) * chunk, q * LANES:(q + 1) * LANES] = tile.astype(BF16)


def _s5_kernel(*refs, nb, nc, readout):
    if readout:
        u_ref, ke_ref, alp_ref, init_ref, lag_ref, kc_ref, fin_ref, y_ref, m_s = refs
    else:
        u_ref, ke_ref, alp_ref, init_ref, fin_ref = refs
    rows = nb * nc
    half = 2 * S5_STATE
    u = u_ref[0]
    e = _dot(u, ke_ref[0])
    er, ei = e[:, :half], e[:, half:]
    r_idx = lax.broadcasted_iota(I32, (rows, half), 0)
    if nc & (nc - 1) == 0:
        n_idx, b_idx = r_idx & (nc - 1), r_idx >> (nc.bit_length() - 1)
    else:
        n_idx, b_idx = lax.rem(r_idx, nc), lax.div(r_idx, nc)
    is_f = lax.broadcasted_iota(I32, (rows, half), 1) < S5_STATE
    seen = jnp.where(is_f, n_idx, nc - 1 - n_idx)
    init = init_ref[0]
    init_r = jnp.zeros((rows, half), F32)
    init_i = jnp.zeros((rows, half), F32)
    for b in range(nb):
        init_r = jnp.where(b_idx == b, init[b:b + 1, :half], init_r)
        init_i = jnp.where(b_idx == b, init[b:b + 1, half:], init_i)
    alp = alp_ref[0]
    ar, ai = alp[0:1, :half], alp[0:1, half:]
    er = er + jnp.where(seen == 0, ar * init_r - ai * init_i, 0.0)
    ei = ei + jnp.where(seen == 0, ar * init_i + ai * init_r, 0.0)

    def from_prev(v, dist):
        return jnp.where(is_f, pltpu.roll(v, dist, 0), pltpu.roll(v, rows - dist, 0))

    d, s = 1, 0
    while d < nc:
        ar, ai = alp[s:s + 1, :half], alp[s:s + 1, half:]
        sr = jnp.where(seen >= d, from_prev(er, d), 0.0)
        si = jnp.where(seen >= d, from_prev(ei, d), 0.0)
        er, ei = er + (ar * sr - ai * si), ei + (ar * si + ai * sr)
        d, s = d * 2, s + 1
    is_f_row = lax.broadcasted_iota(I32, (1, half), 1) < S5_STATE
    for b in range(nb):
        lo, hi = b * nc, b * nc + nc - 1
        fin_ref[0, b:b + 1, :half] = jnp.where(is_f_row, er[hi:hi + 1], er[lo:lo + 1])
        fin_ref[0, b:b + 1, half:] = jnp.where(is_f_row, ei[hi:hi + 1], ei[lo:lo + 1])
    if readout:
        if nc > 1:
            sin_r = jnp.where(seen >= 1, from_prev(er, 1), init_r)
            sin_i = jnp.where(seen >= 1, from_prev(ei, 1), init_i)
        else:
            sin_r, sin_i = init_r, init_i
        s_in = jnp.concatenate([sin_r, sin_i], axis=1).astype(BF16)
        _build_toeplitz(lag_ref, m_s, S5_CHUNK)
        y_ref[0] = (_dot(u, m_s[...]) + _dot(s_in, kc_ref[0])).astype(BF16)


def _s5(ug, ke, alp, init, m=None, kc=None, *, nb, nc):
    g_n, rows, width = ug.shape
    readout = m is not None
    grp = lambda g: (g, 0, 0)
    in_specs = [pl.BlockSpec((1, rows, width), grp), pl.BlockSpec((1,) + ke.shape[1:], grp),
                pl.BlockSpec((1,) + alp.shape[1:], grp), pl.BlockSpec((1,) + init.shape[1:], grp)]
    out_shape = [jax.ShapeDtypeStruct(init.shape, F32)]
    out_specs = [pl.BlockSpec((1,) + init.shape[1:], grp)]
    args = [ug, ke, alp, init]
    scratch = []
    if readout:
        assert m.shape[3] == LANES and 2 * S5_CHUNK == LANES
        in_specs += [pl.BlockSpec((1,) + m.shape[1:], lambda g: (g, 0, 0, 0)),
                     pl.BlockSpec((1,) + kc.shape[1:], grp)]
        out_shape.append(jax.ShapeDtypeStruct(ug.shape, BF16))
        out_specs.append(pl.BlockSpec((1, rows, width), grp))
        args += [m, kc]
        scratch = [pltpu.VMEM((width, width), BF16)]
    return pl.pallas_call(
        functools.partial(_s5_kernel, nb=nb, nc=nc, readout=readout),
        out_shape=tuple(out_shape), grid=(g_n,), in_specs=in_specs, out_specs=tuple(out_specs),
        scratch_shapes=scratch,
        compiler_params=_params(("parallel",)),
        name="s5_readout" if readout else "s5_state",
    )(*args)


def _to_groups(ut, nb, nc, chunk, g_n):
    j_n = ut.shape[0] // g_n
    return ut.reshape(g_n, j_n, nb * nc, chunk).transpose(0, 2, 1, 3).reshape(g_n, nb * nc, j_n * chunk)


def _from_groups(y, nb, nc, chunk, g_n):
    j_n = y.shape[2] // chunk
    return y.reshape(g_n, nb * nc, j_n, chunk).transpose(0, 2, 1, 3).reshape(g_n * j_n, nb * nc * chunk)


def _mix_kernel(y_ref, u_ref, gu_ref, vn_ref, g1_ref, g2_ref, x_ref,
                d_ref, wglu_ref, bglu_ref, sguw_ref, sgub_ref, wb1_ref, wb2_ref, wout_ref,
                npost_ref, gt_ref, npre_ref, shf_ref, scf_ref, wr_ref,
                x1_ref, hp_ref, lg_ref, *, tm):
    y = jax.nn.gelu(y_ref[...].astype(F32).T + d_ref[...] * u_ref[...].astype(F32))
    y_s5 = y * jax.nn.sigmoid(_dot(y.astype(BF16), wglu_ref[...]) + bglu_ref[...])

    lane = lax.broadcasted_iota(I32, (1, LANES), 1)
    m_lo = (lane < LANES // 2).astype(F32).astype(BF16)
    m_hi = (lane >= LANES // 2).astype(F32).astype(BF16)
    vn = vn_ref[...]
    chunks = []
    for c in range(tm // SGU_CHUNK):
        vc = vn[c * SGU_CHUNK:(c + 1) * SGU_CHUNK, :]
        tiles = []
        for q in range(vc.shape[1] // LANES):
            vt = vc[:, q * LANES:(q + 1) * LANES]
            rhs = jnp.concatenate([vt * m_lo, vt * m_hi], axis=0)
            tiles.append(_dot(sguw_ref[q], rhs))
        chunks.append(jnp.concatenate(tiles, axis=1) + sgub_ref[...])
    mixed = jnp.concatenate(chunks, axis=0)
    y_sgu = gu_ref[...].astype(F32) * mixed

    merged = (g1_ref[...].astype(F32) * _dot(y_s5.astype(BF16), wb1_ref[...])
              + g2_ref[...].astype(F32) * _dot(y_sgu.astype(BF16), wb2_ref[...]))
    mx = _dot(merged.astype(BF16), wout_ref[...])
    x1 = x_ref[...] + gt_ref[0] * _rms(mx, npost_ref[...])
    x1_ref[...] = x1
    hp = _rms(x1, npre_ref[...]) * (1.0 + scf_ref[0]) + shf_ref[0]
    _store_row_tiles(hp_ref, 0, hp)
    lg_ref[...] = _dot(hp.astype(BF16), wr_ref[...])


def _mix(y, u, gu, vn, g1, g2, x2, s5_d, wglu, bglu, sguw, sgub, wb1, wb2, wout,
         npost, gt, npre, shf, scf, wr, tokens_per_batch, tm):
    n, d = x2.shape
    d_s5, d_sgu, n_exp = y.shape[0], gu.shape[1], wr.shape[1]
    tiles_per_batch = tokens_per_batch // tm
    row = lambda i: (i, 0)
    fixed = lambda i: (0, 0)
    fixed3 = lambda i: (0, 0, 0)
    per_batch = lambda i: (i // tiles_per_batch, 0, 0)
    vec = lambda w: pl.BlockSpec((1, w), fixed)
    return pl.pallas_call(
        functools.partial(_mix_kernel, tm=tm),
        out_shape=(jax.ShapeDtypeStruct((n, d), F32), jax.ShapeDtypeStruct((n * d // LANES, LANES), F32),
                   jax.ShapeDtypeStruct((n, n_exp), F32)),
        grid=(n // tm,),
        in_specs=[pl.BlockSpec((d_s5, tm), lambda i: (0, i)), pl.BlockSpec((tm, d_s5), row),
                  pl.BlockSpec((tm, d_sgu), row),
                  pl.BlockSpec((tm, d_sgu), row), pl.BlockSpec((tm, d), row), pl.BlockSpec((tm, d), row),
                  pl.BlockSpec((tm, d), row),
                  vec(d_s5), pl.BlockSpec(wglu.shape, fixed), vec(d_s5),
                  pl.BlockSpec(sguw.shape, fixed3), pl.BlockSpec(sgub.shape, fixed),
                  pl.BlockSpec(wb1.shape, fixed), pl.BlockSpec(wb2.shape, fixed), pl.BlockSpec(wout.shape, fixed),
                  vec(d), pl.BlockSpec((1, 1, d), per_batch), vec(d),
                  pl.BlockSpec((1, 1, d), per_batch), pl.BlockSpec((1, 1, d), per_batch),
                  pl.BlockSpec(wr.shape, fixed)],
        out_specs=(pl.BlockSpec((tm, d), row), pl.BlockSpec((tm * d // LANES, LANES), row),
                   pl.BlockSpec((tm, n_exp), row)),
        compiler_params=_params(("parallel",)),
        name="mix",
    )(y, u, gu, vn, g1, g2, x2, s5_d.reshape(1, d_s5), wglu, bglu.reshape(1, d_s5), sguw, sgub,
      wb1, wb2, wout, npost.reshape(1, d), gt, npre.reshape(1, d), shf, scf, wr)


def _route_kernel(lg_ref, bias_ref, idx_ref, w_ref, cnt_ref, *, n_exp, tn):
    per_group = n_exp // N_EXPERT_GROUPS
    neg = jnp.float32(-jnp.inf)

    scores = jax.nn.sigmoid(lg_ref[...])
    sel = scores + bias_ref[...]
    gs = []
    for g in range(N_EXPERT_GROUPS):
        sg = sel[g * per_group:(g + 1) * per_group, :]
        m1 = jnp.max(sg, axis=0, keepdims=True)
        is_m1 = sg == m1
        n_m1 = jnp.sum(jnp.where(is_m1, 1.0, 0.0), axis=0, keepdims=True)
        rest = jnp.max(jnp.where(is_m1, neg, sg), axis=0, keepdims=True)
        gs.append(m1 + jnp.where(n_m1 >= 2.0, m1, rest))
    gsm = jnp.concatenate(gs, axis=0)
    g_iota = lax.broadcasted_iota(I32, gsm.shape, 0)
    e_iota = lax.broadcasted_iota(I32, sel.shape, 0).astype(F32)
    masked = []
    for g in range(N_EXPERT_GROUPS):
        mine = gsm[g:g + 1, :]
        beats = jnp.where(gsm > mine, 1.0, jnp.where(gsm == mine, jnp.where(g_iota < g, 1.0, 0.0), 0.0))
        n_beats = jnp.sum(beats, axis=0, keepdims=True)
        masked.append(jnp.where(n_beats < float(TOPK_GROUPS), sel[g * per_group:(g + 1) * per_group, :], neg))
    start = jnp.concatenate(masked, axis=0)
    selm = start
    picked, vals = [], []
    for _ in range(TOP_K):
        m = jnp.max(selm, axis=0, keepdims=True)
        first = jnp.min(jnp.where(selm == m, e_iota, float(n_exp)), axis=0, keepdims=True)
        one = e_iota == first
        picked.append(first)
        vals.append(jnp.sum(jnp.where(one, scores, 0.0), axis=0, keepdims=True))
        selm = jnp.where(one, neg, selm)
    idx_ref[...] = jnp.concatenate(picked, axis=0).astype(I32)
    wv = jnp.concatenate(vals, axis=0)
    w_ref[...] = wv / jnp.sum(wv, axis=0, keepdims=True) * ROUTE_SCALE
    hot = jnp.where(selm == neg, jnp.where(start == neg, 0.0, 1.0), 0.0)

    @pl.when(pl.program_id(0) == 0)
    def _():
        cnt_ref[...] = jnp.zeros_like(cnt_ref)

    cnt_ref[...] += jnp.sum(hot, axis=1, keepdims=True)


def _route(logits_t, bias, tn):
    n_exp, n = logits_t.shape
    col = lambda i: (0, i)
    fixed = lambda i: (0, 0)
    return pl.pallas_call(
        functools.partial(_route_kernel, n_exp=n_exp, tn=tn),
        out_shape=(jax.ShapeDtypeStruct((TOP_K, n), I32), jax.ShapeDtypeStruct((TOP_K, n), F32),
                   jax.ShapeDtypeStruct((n_exp, 1), F32)),
        grid=(n // tn,),
        in_specs=[pl.BlockSpec((n_exp, tn), col), pl.BlockSpec((n_exp, 1), fixed)],
        out_specs=(pl.BlockSpec((TOP_K, tn), col), pl.BlockSpec((TOP_K, tn), col),
                   pl.BlockSpec((n_exp, 1), fixed)),
        compiler_params=_params(("arbitrary",)),
        name="route",
    )(logits_t, bias.reshape(n_exp, 1))


def _dest_kernel(idx_ref, start_ref, dest_ref, carry_ref, *, n_exp, tn):
    @pl.when(pl.program_id(0) == 0)
    def _():
        carry_ref[...] = start_ref[...]

    idx = idx_ref[...]
    e_iota = lax.broadcasted_iota(I32, (n_exp, tn), 0)
    hot = jnp.zeros((n_exp, tn), F32)
    for k in range(TOP_K):
        hot = hot + jnp.where(e_iota == idx[k:k + 1, :], 1.0, 0.0)
    before = jnp.where(lax.broadcasted_iota(I32, (tn, tn), 0) < lax.broadcasted_iota(I32, (tn, tn), 1), 1.0, 0.0)
    rank = _dot(hot.astype(BF16), before.astype(BF16)) + carry_ref[...]
    dest_ref[...] = jnp.concatenate(
        [jnp.sum(jnp.where(e_iota == idx[k:k + 1, :], rank, 0.0), axis=0, keepdims=True)
         for k in range(TOP_K)], axis=0).astype(I32)
    carry_ref[...] += jnp.sum(hot, axis=1, keepdims=True)


def _dest(idx, start, tn):
    n = idx.shape[1]
    n_exp = start.shape[0]
    col = lambda i: (0, i)
    return pl.pallas_call(
        functools.partial(_dest_kernel, n_exp=n_exp, tn=tn),
        out_shape=jax.ShapeDtypeStruct((TOP_K, n), I32),
        grid=(n // tn,),
        in_specs=[pl.BlockSpec((TOP_K, tn), col), pl.BlockSpec((n_exp, 1), lambda i: (0, 0))],
        out_specs=pl.BlockSpec((TOP_K, tn), col),
        scratch_shapes=[pltpu.VMEM((n_exp, 1), F32)],
        compiler_params=_params(("arbitrary",)),
        name="dest",
    )(idx, start)


def _dispatch_kernel(dest_ref, hp_ref, xs_ref, sem, *, tm, pieces):
    def start(r, c):
        src = hp_ref.at[pl.ds(pl.multiple_of(r * pieces, pieces), pieces), :]
        for k in range(TOP_K):
            row = pl.multiple_of(dest_ref[r * TOP_K + k], pieces)
            pltpu.make_async_copy(src, xs_ref.at[pl.ds(row, pieces), :], sem).start(priority=k % 2)
        return c

    lax.fori_loop(0, tm, start, 0, unroll=2)
    for k in range(TOP_K):
        pltpu.make_async_copy(hp_ref, xs_ref.at[pl.ds(0, tm * pieces), :], sem).wait()


def _dispatch(dest, hp, cap, tm, pieces):
    n = hp.shape[0] // pieces
    return pl.pallas_call(
        functools.partial(_dispatch_kernel, tm=tm, pieces=pieces),
        out_shape=jax.ShapeDtypeStruct((cap * pieces, LANES), hp.dtype),
        grid=(n // tm,),
        in_specs=[pl.BlockSpec((tm * TOP_K,), lambda i: (i,), memory_space=pltpu.SMEM),
                  pl.BlockSpec((tm * pieces, LANES), lambda i: (i, 0))],
        out_specs=pl.BlockSpec(memory_space=pl.ANY),
        scratch_shapes=[pltpu.SemaphoreType.DMA(())],
        compiler_params=_params(("arbitrary",)),
        name="dispatch",
    )(dest, hp)


def _expert_kernel(cnt_ref, first_ref, tot_ref, xs_hbm, wg_ref, wu_ref, wd_ref, ys_hbm,
                   xbuf, ybuf, wgu_s, wd_s, xsem, ysem, *, rows, pieces):
    e = pl.program_id(0)
    cnt, first, total = cnt_ref[e], first_ref[e], tot_ref[0]
    n_blk = (cnt + rows - 1) // rows
    d_e = wg_ref.shape[2]
    phys = rows * pieces

    def block_rows(g):
        return pl.ds(pl.multiple_of(g * phys, phys), phys)

    def x_copy(g, slot):
        return pltpu.make_async_copy(xs_hbm.at[block_rows(g), :], xbuf.at[slot], xsem.at[slot])

    def y_copy(g, slot):
        return pltpu.make_async_copy(ybuf.at[slot], ys_hbm.at[block_rows(g), :], ysem.at[slot])

    x_slots, y_slots = xbuf.shape[0], ybuf.shape[0]
    ahead = x_slots - 1

    @pl.when(n_blk > 0)
    def _():
        @pl.when(first == 0)
        def _():
            for a in range(ahead):
                @pl.when(a < total)
                def _():
                    x_copy(a, a).start()

        wgu_s[:, :d_e] = wg_ref[0].astype(BF16)
        wgu_s[:, d_e:] = wu_ref[0].astype(BF16)
        wd_s[...] = wd_ref[0].astype(BF16)

        def block(j, c):
            g = first + j
            slot = g % x_slots
            x_copy(g, slot).wait()

            @pl.when(g + ahead < total)
            def _():
                x_copy(g + ahead, (g + ahead) % x_slots).start()

            yslot = g % y_slots

            @pl.when(g >= y_slots)
            def _():
                y_copy(g - y_slots, yslot).wait()

            x = _load_row_tiles(xbuf, 0, rows, pieces, lead=(slot,))
            r_iota = lax.broadcasted_iota(I32, x.shape, 0) + j * rows
            xb = jnp.where(r_iota < cnt, x, 0.0).astype(BF16)
            gu = _dot(xb, wgu_s[...])
            gate, up = gu[:, :d_e], gu[:, d_e:]
            hid = (gate * jax.nn.sigmoid(gate) * up).astype(BF16)
            y = _dot(hid, wd_s[...])
            for s in range(pieces):
                ybuf[yslot, pl.ds(s, rows, stride=pieces), :] = y[:, s * LANES:(s + 1) * LANES]
            y_copy(g, yslot).start()
            return c

        lax.fori_loop(0, n_blk, block, 0)

        @pl.when(first + n_blk == total)
        def _():
            for back in range(1, y_slots + 1):
                @pl.when(total >= back)
                def _():
                    y_copy(total - back, (total - back) % y_slots).wait()


def _experts(cnt, blk_first, n_used, xs, w_gate, w_up, w_down, rows):
    n_exp, d, d_e = w_gate.shape
    pieces = d // LANES
    hbm = pl.BlockSpec(memory_space=pl.ANY)
    w_map = lambda e, cnt, first, tot: (e, 0, 0)
    grid_spec = pltpu.PrefetchScalarGridSpec(
        num_scalar_prefetch=3,
        grid=(n_exp,),
        in_specs=[hbm, pl.BlockSpec((1, d, d_e), w_map), pl.BlockSpec((1, d, d_e), w_map),
                  pl.BlockSpec((1, d_e, d), w_map)],
        out_specs=hbm,
        scratch_shapes=[pltpu.VMEM((EXPERT_X_SLOTS, rows * pieces, LANES), F32),
                        pltpu.VMEM((EXPERT_Y_SLOTS, rows * pieces, LANES), F32),
                        pltpu.VMEM((d, 2 * d_e), BF16), pltpu.VMEM((d_e, d), BF16),
                        pltpu.SemaphoreType.DMA((EXPERT_X_SLOTS,)), pltpu.SemaphoreType.DMA((EXPERT_Y_SLOTS,))],
    )
    return pl.pallas_call(
        functools.partial(_expert_kernel, rows=rows, pieces=pieces),
        out_shape=jax.ShapeDtypeStruct(xs.shape, F32),
        grid_spec=grid_spec,
        compiler_params=_params(("arbitrary",)),
        name="experts",
    )(cnt, blk_first, n_used, xs, w_gate, w_up, w_down)


def _combine_kernel(dest_ref, next_ref, ys_ref, wt_ref, hp_ref, x1_ref, wsgu_ref, wsd_ref, npost_ref, gt_ref,
                    o_ref, buf, acc, sem, *, tm, pieces):
    i = pl.program_id(0)
    last = pl.num_programs(0) - 1
    cur = i % 2

    def start_token(rows_ref, half, r):
        slot = pl.ds(pl.multiple_of(r * pieces, pieces), pieces)
        for k in range(TOP_K):
            row = pl.multiple_of(rows_ref[r * TOP_K + k], pieces)
            pltpu.make_async_copy(ys_ref.at[pl.ds(row, pieces), :], buf.at[half, k, slot, :],
                                  sem.at[half]).start(priority=k % 2)

    def wait_half(half):
        for k in range(TOP_K):
            pltpu.make_async_copy(ys_ref.at[pl.ds(0, tm * pieces), :], buf.at[half, k], sem.at[half]).wait()

    @pl.when(i == 0)
    def _():
        def first(r, c):
            start_token(dest_ref, 0, r)
            return c

        lax.fori_loop(0, tm, first, 0)

    def tile_pass(half):
        wait_half(half)

        def token(r, c):
            start_token(next_ref, 1 - half, r)
            slot = pl.ds(pl.multiple_of(r * pieces, pieces), pieces)
            terms = [wt_ref[r * TOP_K + k] * buf[half, k, slot, :] for k in range(TOP_K)]
            while len(terms) > 1:
                terms = [terms[j] + terms[j + 1] for j in range(0, len(terms), 2)]
            acc[slot, :] = terms[0]
            return c

        lax.fori_loop(0, tm, token, 0, unroll=4)

        @pl.when(i == last)
        def _():
            wait_half(1 - half)

    for half in range(2):
        @pl.when(cur == half)
        def _():
            tile_pass(half)

    hb = _load_row_tiles(hp_ref, 0, tm, pieces).astype(BF16)
    d_sh = wsd_ref.shape[0]
    gu = _dot(hb, wsgu_ref[...])
    gate, up = gu[:, :d_sh], gu[:, d_sh:]
    shared = _dot((gate * jax.nn.sigmoid(gate) * up).astype(BF16), wsd_ref[...])
    fx = shared + _load_row_tiles(acc, 0, tm, pieces)
    o_ref[...] = x1_ref[...] + gt_ref[0] * _rms(fx, npost_ref[...])


def _combine(dest, ys, wt, hp, x1, wsgu, wsd, npost, gt, tokens_per_batch, tm):
    n, d = x1.shape
    pieces = d // LANES
    w = LANES
    tiles_per_batch = tokens_per_batch // tm
    row = lambda i: (i, 0)
    fixed = lambda i: (0, 0)
    return pl.pallas_call(
        functools.partial(_combine_kernel, tm=tm, pieces=pieces),
        out_shape=jax.ShapeDtypeStruct((n, d), F32),
        grid=(n // tm,),
        in_specs=[pl.BlockSpec((tm * TOP_K,), lambda i: (i,), memory_space=pltpu.SMEM),
                  pl.BlockSpec((tm * TOP_K,), lambda i: (jnp.minimum(i + 1, n // tm - 1),), memory_space=pltpu.SMEM),
                  pl.BlockSpec(memory_space=pl.ANY),
                  pl.BlockSpec((tm * TOP_K,), lambda i: (i,), memory_space=pltpu.SMEM),
                  pl.BlockSpec((tm * pieces, w), row), pl.BlockSpec((tm, d), row),
                  pl.BlockSpec(wsgu.shape, fixed), pl.BlockSpec(wsd.shape, fixed),
                  pl.BlockSpec((1, d), fixed), pl.BlockSpec((1, 1, d), lambda i: (i // tiles_per_batch, 0, 0))],
        out_specs=pl.BlockSpec((tm, d), row),
        scratch_shapes=[pltpu.VMEM((2, TOP_K, tm * pieces, w), F32), pltpu.VMEM((tm * pieces, w), F32),
                        pltpu.SemaphoreType.DMA((2,))],
        compiler_params=_params(("arbitrary",)),
        name="combine",
    )(dest, dest, ys, wt, hp, x1, wsgu, wsd, npost.reshape(1, d), gt)


def _layer(x, ctx, mod_x, mod_c, norm_pre_mix, norm_post_mix, norm_pre_ffn, norm_post_ffn,
           w_in, s5_a_re, s5_a_im, s5_log_dt, s5_b_re, s5_b_im, s5_c_re, s5_c_im, s5_d, w_glu, b_glu,
           sgu_norm, sgu_w, sgu_b, w_branch_s5, w_branch_sgu, w_out,
           w_router, router_bias, w_exp_gate, w_exp_up, w_exp_down, w_sh_gate, w_sh_up, w_sh_down):
    bsz, t_len, d = x.shape
    c_len = ctx.shape[1]
    n = bsz * t_len
    d_s5, d_sgu = w_glu.shape[0], sgu_norm.shape[0]
    g_n = d_s5 // S5_GROUP_CH
    n_exp = w_router.shape[1]
    chunk = S5_CHUNK
    tm = min(512, t_len)
    assert t_len % tm == 0 and tm % SGU_CHUNK == 0 and t_len % chunk == 0 and c_len % chunk == 0

    sh_m, sc_m, gt_m, sh_f, sc_f, gt_f = [v.reshape(bsz, 1, d) for v in jnp.split(mod_x, 6, axis=-1)]
    csh_m, csc_m = mod_c[:d], mod_c[d:2 * d]

    w_in_bf = w_in.astype(BF16)

    nc_lat, nc_ctx = t_len // chunk, c_len // chunk
    n_steps = max(1, (max(nc_lat, nc_ctx) - 1).bit_length())
    m_t, ke_t, kc_t, alp = _s5_tables(s5_a_re, s5_a_im, s5_log_dt, s5_b_re, s5_b_im, s5_c_re, s5_c_im,
                                      chunk, n_steps)

    ctx2 = ctx.reshape(bsz * c_len, d)
    u_ctx = _inproj_u(ctx2, norm_pre_mix, csh_m, csc_m, w_in_bf[:, :d_s5], min(512, bsz * c_len))
    zero_init = jnp.zeros((g_n, bsz, 4 * S5_STATE), F32)
    (ctx_final,) = _s5(_to_groups(u_ctx, bsz, nc_ctx, chunk, g_n), ke_t, alp, zero_init, nb=bsz, nc=nc_ctx)

    x2 = x.reshape(n, d)
    u, ut, gu, vn, g1, g2 = _inproj(x2, norm_pre_mix, sh_m, sc_m, w_in_bf, sgu_norm, t_len, tm, d_s5, d_sgu)
    _, yg = _s5(_to_groups(ut, bsz, nc_lat, chunk, g_n), ke_t, alp, ctx_final, m_t, kc_t, nb=bsz, nc=nc_lat)
    y = _from_groups(yg, bsz, nc_lat, chunk, g_n)

    ch = d_sgu // SGU_GROUPS
    per_tile = LANES // ch
    sguw = sgu_w.reshape(SGU_GROUPS // per_tile, per_tile, SGU_CHUNK, SGU_CHUNK)
    sguw = sguw.transpose(0, 2, 1, 3).reshape(SGU_GROUPS // per_tile, SGU_CHUNK, per_tile * SGU_CHUNK).astype(BF16)
    sgub = jnp.repeat(sgu_b.T, ch, axis=1)

    x1, hp, logits = _mix(y, u, gu, vn, g1, g2, x2, s5_d, w_glu.astype(BF16), b_glu, sguw, sgub,
                          w_branch_s5.astype(BF16), w_branch_sgu.astype(BF16), w_out.astype(BF16),
                          norm_post_mix, gt_m, norm_pre_ffn, sh_f, sc_f, w_router.astype(BF16), t_len, tm)

    tn = min(512, n)
    idx, wts, counts = _route(logits.T, router_bias, tn)
    rows = EXPERT_ROWS
    cnt = counts.reshape(n_exp).astype(I32)
    nblk = (cnt + rows - 1) // rows
    blk_end = jnp.cumsum(nblk)
    blk_start = blk_end - nblk
    cap = ((n * TOP_K) // rows + n_exp) * rows
    dest = _dest(idx, (blk_start * rows).astype(F32).reshape(n_exp, 1), tn)
    pieces = d // LANES
    dest = dest.T.reshape(n * TOP_K) * pieces

    xs = _dispatch(dest, hp, cap, min(512, t_len), pieces)
    ys = _experts(cnt, blk_start.astype(I32), blk_end[-1:].astype(I32), xs, w_exp_gate, w_exp_up, w_exp_down, rows)
    wsgu = jnp.concatenate([w_sh_gate, w_sh_up], axis=1).astype(BF16)
    out = _combine(dest, ys, wts.T.reshape(n * TOP_K), hp, x1, wsgu, w_sh_down.astype(BF16), norm_post_ffn,
                   gt_f, t_len, min(512, t_len))
    return out.reshape(bsz, t_len, d)


def kernel(x, c, ctx, c_ctx, w_mod, b_mod, norm_pre_mix, norm_post_mix, norm_pre_ffn, norm_post_ffn, w_in, s5_a_re, s5_a_im, s5_log_dt, s5_b_re, s5_b_im, s5_c_re, s5_c_im, s5_d, w_glu, b_glu, sgu_norm, sgu_w, sgu_b, w_branch_s5, w_branch_sgu, w_out, w_router, router_bias, w_exp_gate, w_exp_up, w_exp_down, w_sh_gate, w_sh_up, w_sh_down):
    depth = w_mod.shape[0]
    assert depth == 1, "the context stream is only carried through the last layer's S5 states"
    bsz = x.shape[0]
    pad = (-(bsz + 1)) % 8
    cpad = jnp.concatenate([c, c_ctx[None, :], jnp.zeros((pad, c.shape[1]), c.dtype)], axis=0)
    mod = _modulation(cpad, w_mod[0], b_mod[0])
    return _layer(x, ctx, mod[:bsz], mod[bsz], norm_pre_mix[0], norm_post_mix[0], norm_pre_ffn[0],
                  norm_post_ffn[0], w_in[0], s5_a_re[0], s5_a_im[0], s5_log_dt[0], s5_b_re[0], s5_b_im[0],
                  s5_c_re[0], s5_c_im[0], s5_d[0], w_glu[0], b_glu[0], sgu_norm[0], sgu_w[0], sgu_b[0],
                  w_branch_s5[0], w_branch_sgu[0], w_out[0], w_router[0], router_bias[0],
                  w_exp_gate[0], w_exp_up[0], w_exp_down[0], w_sh_gate[0], w_sh_up[0], w_sh_down[0])
```

```python
import functools
import math

import jax
import jax.numpy as jnp
from jax import lax
from jax.experimental import pallas as pl
from jax.experimental.pallas import tpu as pltpu

F32 = jnp.float32
BF16 = jnp.bfloat16
I32 = jnp.int32

EPS = 1e-6
S5_GROUP_CH = 16
S5_STATE = 64
S5_CHUNK = 64
SGU_GROUPS = 8
SGU_CHUNK = 128
N_EXPERT_GROUPS = 8
TOPK_GROUPS = 4
TOP_K = 8
ROUTE_SCALE = 2.5
LANES = 128
EXPERT_ROWS = 256
EXPERT_X_SLOTS = 6
EXPERT_Y_SLOTS = 4
VMEM_LIMIT = 56 * 1024 * 1024


def _params(sem):
    return pltpu.CompilerParams(dimension_semantics=sem, vmem_limit_bytes=VMEM_LIMIT)


def _rms(v, g):
    return v * lax.rsqrt(jnp.mean(v * v, axis=-1, keepdims=True) + EPS) * g


def _dot(a, b):
    return jnp.dot(a, b, preferred_element_type=F32)


def _store_row_tiles(ref, first_row, val):
    pieces = val.shape[1] // LANES
    for s in range(pieces):
        ref[pl.ds(first_row * pieces + s, val.shape[0], stride=pieces), :] = val[:, s * LANES:(s + 1) * LANES]


def _load_row_tiles(ref, first_row, n_rows, pieces, lead=()):
    return jnp.concatenate([ref[lead + (pl.ds(first_row * pieces + s, n_rows, stride=pieces), slice(None))]
                            for s in range(pieces)], axis=1)


def _mod_kernel(c_ref, w_ref, b_ref, o_ref):
    cv = c_ref[...]
    s = cv * jax.nn.sigmoid(cv)
    o_ref[...] = _dot(s.astype(BF16), w_ref[...].astype(BF16)) + b_ref[...]


def _modulation(cpad, w_mod, b_mod):
    d, n6 = w_mod.shape
    tn = 1024
    return pl.pallas_call(
        _mod_kernel,
        out_shape=jax.ShapeDtypeStruct((cpad.shape[0], n6), F32),
        grid=(n6 // tn,),
        in_specs=[pl.BlockSpec(cpad.shape, lambda j: (0, 0)),
                  pl.BlockSpec((d, tn), lambda j: (0, j)),
                  pl.BlockSpec((1, tn), lambda j: (0, j))],
        out_specs=pl.BlockSpec((cpad.shape[0], tn), lambda j: (0, j)),
        compiler_params=_params(("parallel",)),
        name="mod",
    )(cpad, w_mod, b_mod.reshape(1, n6))


def _inproj_kernel(x_ref, g_ref, sh_ref, sc_ref, w_ref, sgun_ref,
                   u_ref, ut_ref, gu_ref, vn_ref, g1_ref, g2_ref, *, d_s5, d_sgu, d_model):
    h = _rms(x_ref[...], g_ref[...])
    hb = (h * (1.0 + sc_ref[0]) + sh_ref[0]).astype(BF16)
    o1, o2, o3, o4 = d_s5, d_s5 + d_sgu, d_s5 + 2 * d_sgu, d_s5 + 2 * d_sgu + d_model
    u = _dot(hb, w_ref[:, 0:o1])
    u_ref[...] = u.astype(BF16)
    ut_ref[...] = u.T.astype(BF16)
    gu_ref[...] = jax.nn.gelu(_dot(hb, w_ref[:, o1:o2])).astype(BF16)
    v = jax.nn.gelu(_dot(hb, w_ref[:, o2:o3]))
    vn_ref[...] = _rms(v, sgun_ref[...]).astype(BF16)
    g1_ref[...] = jax.nn.sigmoid(_dot(hb, w_ref[:, o3:o4])).astype(BF16)
    g2_ref[...] = jax.nn.sigmoid(_dot(hb, w_ref[:, o4:o4 + d_model])).astype(BF16)


def _inproj(x2, norm_g, shift, scale, w_in_bf, sgu_norm, tokens_per_batch, tm, d_s5, d_sgu):
    n, d = x2.shape
    tiles_per_batch = tokens_per_batch // tm
    row = lambda i: (i, 0)
    fixed = lambda i: (0, 0)
    per_batch = lambda i: (i // tiles_per_batch, 0, 0)
    return pl.pallas_call(
        functools.partial(_inproj_kernel, d_s5=d_s5, d_sgu=d_sgu, d_model=d),
        out_shape=(jax.ShapeDtypeStruct((n, d_s5), BF16), jax.ShapeDtypeStruct((d_s5, n), BF16),
                   jax.ShapeDtypeStruct((n, d_sgu), BF16),
                   jax.ShapeDtypeStruct((n, d_sgu), BF16), jax.ShapeDtypeStruct((n, d), BF16),
                   jax.ShapeDtypeStruct((n, d), BF16)),
        grid=(n // tm,),
        in_specs=[pl.BlockSpec((tm, d), row), pl.BlockSpec((1, d), fixed),
                  pl.BlockSpec((1, 1, d), per_batch), pl.BlockSpec((1, 1, d), per_batch),
                  pl.BlockSpec(w_in_bf.shape, fixed), pl.BlockSpec((1, d_sgu), fixed)],
        out_specs=(pl.BlockSpec((tm, d_s5), row), pl.BlockSpec((d_s5, tm), lambda i: (0, i)),
                   pl.BlockSpec((tm, d_sgu), row),
                   pl.BlockSpec((tm, d_sgu), row), pl.BlockSpec((tm, d), row), pl.BlockSpec((tm, d), row)),
        compiler_params=_params(("parallel",)),
        name="inproj",
    )(x2, norm_g.reshape(1, d), shift, scale, w_in_bf, sgu_norm.reshape(1, d_sgu))


def _inproj_u_kernel(x_ref, g_ref, sh_ref, sc_ref, w_ref, u_ref):
    h = _rms(x_ref[...], g_ref[...])
    hb = (h * (1.0 + sc_ref[...]) + sh_ref[...]).astype(BF16)
    u_ref[...] = _dot(hb, w_ref[...]).T.astype(BF16)


def _inproj_u(x2, norm_g, shift, scale, w_u_bf, tm):
    n, d = x2.shape
    d_s5 = w_u_bf.shape[1]
    row = lambda i: (i, 0)
    fixed = lambda i: (0, 0)
    return pl.pallas_call(
        _inproj_u_kernel,
        out_shape=jax.ShapeDtypeStruct((d_s5, n), BF16),
        grid=(n // tm,),
        in_specs=[pl.BlockSpec((tm, d), row), pl.BlockSpec((1, d), fixed), pl.BlockSpec((1, d), fixed),
                  pl.BlockSpec((1, d), fixed), pl.BlockSpec(w_u_bf.shape, fixed)],
        out_specs=pl.BlockSpec((d_s5, tm), lambda i: (0, i)),
        compiler_params=_params(("parallel",)),
        name="inproj_ctx",
    )(x2, norm_g.reshape(1, d), shift.reshape(1, d), scale.reshape(1, d), w_u_bf)


def _s5_tables(a_re, a_im, log_dt, b_re, b_im, c_re, c_im, chunk, n_steps):
    hi = lax.Precision.HIGHEST
    g_n, p_n, j_n = a_re.shape[1], a_re.shape[2], b_re.shape[3]
    dt = jnp.exp(log_dt)[..., None]
    lam_re, lam_im = dt * a_re, dt * a_im
    ab_re, ab_im = jnp.exp(lam_re) * jnp.cos(lam_im), jnp.exp(lam_re) * jnp.sin(lam_im)
    den = a_re * a_re + a_im * a_im
    q_re = ((ab_re - 1.0) * a_re + ab_im * a_im) / den
    q_im = (ab_im * a_re - (ab_re - 1.0) * a_im) / den
    bb_re = q_re[..., None] * b_re - q_im[..., None] * b_im
    bb_im = q_re[..., None] * b_im + q_im[..., None] * b_re
    k = jnp.arange(chunk + 1, dtype=F32)[:, None, None, None]
    mag = jnp.exp(k * lam_re[None])
    pw_re, pw_im = mag * jnp.cos(k * lam_im[None]), mag * jnp.sin(k * lam_im[None])

    ct_re, ct_im = c_re.transpose(0, 1, 3, 2), c_im.transpose(0, 1, 3, 2)

    def lag_kernels(d):
        w_re = bb_re[d][:, :, :, None] * ct_re[d][:, :, None, :] - bb_im[d][:, :, :, None] * ct_im[d][:, :, None, :]
        w_im = bb_re[d][:, :, :, None] * ct_im[d][:, :, None, :] + bb_im[d][:, :, :, None] * ct_re[d][:, :, None, :]
        w = jnp.concatenate([w_re, -w_im], axis=1).reshape(g_n, 2 * p_n, j_n * j_n)
        a = jnp.concatenate([pw_re[:chunk, d], pw_im[:chunk, d]], axis=-1).transpose(1, 0, 2)
        return jnp.einsum('gkp,gpn->gkn', a, w, precision=hi)

    kf, kb = lag_kernels(0), lag_kernels(1)
    lags = jnp.concatenate([kb[:, 1:][:, ::-1], kf[:, 0:1] + kb[:, 0:1], kf[:, 1:],
                            jnp.zeros_like(kf[:, 0:1])], axis=1)
    lags = lags.transpose(0, 2, 1).reshape(g_n, j_n, j_n, 2 * chunk)

    def both(f, b, axis):
        return jnp.concatenate([f, b], axis=axis)

    pt_re = both(pw_re[:chunk, 0][::-1], pw_re[:chunk, 1], -1).transpose(1, 0, 2)
    pt_im = both(pw_im[:chunk, 0][::-1], pw_im[:chunk, 1], -1).transpose(1, 0, 2)
    bt_re = both(bb_re[0].transpose(0, 2, 1), bb_re[1].transpose(0, 2, 1), -1)
    bt_im = both(bb_im[0].transpose(0, 2, 1), bb_im[1].transpose(0, 2, 1), -1)
    e_re = pt_re[:, None] * bt_re[:, :, None] - pt_im[:, None] * bt_im[:, :, None]
    e_im = pt_re[:, None] * bt_im[:, :, None] + pt_im[:, None] * bt_re[:, :, None]
    ke = jnp.concatenate([e_re, e_im], axis=-1).reshape(g_n, j_n * chunk, 4 * p_n)

    q_re = both(pw_re[1:chunk + 1, 0], pw_re[1:chunk + 1, 1][::-1], -1).transpose(1, 2, 0)
    q_im = both(pw_im[1:chunk + 1, 0], pw_im[1:chunk + 1, 1][::-1], -1).transpose(1, 2, 0)
    q_re, q_im = jnp.tile(q_re, (1, 1, j_n)), jnp.tile(q_im, (1, 1, j_n))
    cc_re = jnp.repeat(both(ct_re[0], ct_re[1], 1), chunk, axis=2)
    cc_im = jnp.repeat(both(ct_im[0], ct_im[1], 1), chunk, axis=2)
    kc = jnp.concatenate([q_re * cc_re - q_im * cc_im, -(q_re * cc_im + q_im * cc_re)], axis=1)

    e = (chunk * (2.0 ** jnp.arange(n_steps, dtype=F32)))[:, None, None, None]
    mg = jnp.exp(e * lam_re[None])
    al_re, al_im = mg * jnp.cos(e * lam_im[None]), mg * jnp.sin(e * lam_im[None])
    alp = jnp.concatenate([al_re[:, 0], al_re[:, 1], al_im[:, 0], al_im[:, 1]], axis=-1)
    return lags, ke.astype(BF16), kc.astype(BF16), alp.transpose(1, 0, 2)


def _build_toeplitz(lag_ref, m_s, chunk):
    j_n = lag_ref.shape[1]
    per_tile = LANES // chunk
    low = lax.broadcasted_iota(I32, (chunk, LANES), 1) < chunk
    for i in range(j_n):
        for q in range(j_n // per_tile):
            tile = None
            for p in range(per_tile):
                j = q * per_tile + p
                row = jnp.broadcast_to(lag_ref[0, i, j:j + 1, :], (chunk, LANES))
                rot = pltpu.roll(row, (p * chunk + LANES - (chunk - 1)) % LANES, 1, stride=1, stride_axis=0)
                tile = rot if tile is None else jnp.where(low, tile, rot)
            m_s[i * chunk:(i + 1) * chunk, q * LANES:(q + 1) * LANES] = tile.astype(BF16)


def _s5_kernel(*refs, nb, nc, readout):
    if readout:
        u_ref, ke_ref, alp_ref, init_ref, lag_ref, kc_ref, fin_ref, y_ref, m_s = refs
    else:
        u_ref, ke_ref, alp_ref, init_ref, fin_ref = refs
    rows = nb * nc
    half = 2 * S5_STATE
    u = u_ref[0]
    e = _dot(u, ke_ref[0])
    er, ei = e[:, :half], e[:, half:]
    r_idx = lax.broadcasted_iota(I32, (rows, half), 0)
    if nc & (nc - 1) == 0:
        n_idx, b_idx = r_idx & (nc - 1), r_idx >> (nc.bit_length() - 1)
    else:
        n_idx, b_idx = lax.rem(r_idx, nc), lax.div(r_idx, nc)
    is_f = lax.broadcasted_iota(I32, (rows, half), 1) < S5_STATE
    seen = jnp.where(is_f, n_idx, nc - 1 - n_idx)
    init = init_ref[0]
    init_r = jnp.zeros((rows, half), F32)
    init_i = jnp.zeros((rows, half), F32)
    for b in range(nb):
        init_r = jnp.where(b_idx == b, init[b:b + 1, :half], init_r)
        init_i = jnp.where(b_idx == b, init[b:b + 1, half:], init_i)
    alp = alp_ref[0]
    ar, ai = alp[0:1, :half], alp[0:1, half:]
    er = er + jnp.where(seen == 0, ar * init_r - ai * init_i, 0.0)
    ei = ei + jnp.where(seen == 0, ar * init_i + ai * init_r, 0.0)

    def from_prev(v, dist):
        return jnp.where(is_f, pltpu.roll(v, dist, 0), pltpu.roll(v, rows - dist, 0))

    d, s = 1, 0
    while d < nc:
        ar, ai = alp[s:s + 1, :half], alp[s:s + 1, half:]
        sr = jnp.where(seen >= d, from_prev(er, d), 0.0)
        si = jnp.where(seen >= d, from_prev(ei, d), 0.0)
        er, ei = er + (ar * sr - ai * si), ei + (ar * si + ai * sr)
        d, s = d * 2, s + 1
    is_f_row = lax.broadcasted_iota(I32, (1, half), 1) < S5_STATE
    for b in range(nb):
        lo, hi = b * nc, b * nc + nc - 1
        fin_ref[0, b:b + 1, :half] = jnp.where(is_f_row, er[hi:hi + 1], er[lo:lo + 1])
        fin_ref[0, b:b + 1, half:] = jnp.where(is_f_row, ei[hi:hi + 1], ei[lo:lo + 1])
    if readout:
        if nc > 1:
            sin_r = jnp.where(seen >= 1, from_prev(er, 1), init_r)
            sin_i = jnp.where(seen >= 1, from_prev(ei, 1), init_i)
        else:
            sin_r, sin_i = init_r, init_i
        s_in = jnp.concatenate([sin_r, sin_i], axis=1).astype(BF16)
        _build_toeplitz(lag_ref, m_s, S5_CHUNK)
        y_ref[0] = (_dot(u, m_s[...]) + _dot(s_in, kc_ref[0])).astype(BF16)


def _s5(ug, ke, alp, init, m=None, kc=None, *, nb, nc):
    g_n, rows, width = ug.shape
    readout = m is not None
    grp = lambda g: (g, 0, 0)
    in_specs = [pl.BlockSpec((1, rows, width), grp), pl.BlockSpec((1,) + ke.shape[1:], grp),
                pl.BlockSpec((1,) + alp.shape[1:], grp), pl.BlockSpec((1,) + init.shape[1:], grp)]
    out_shape = [jax.ShapeDtypeStruct(init.shape, F32)]
    out_specs = [pl.BlockSpec((1,) + init.shape[1:], grp)]
    args = [ug, ke, alp, init]
    scratch = []
    if readout:
        assert m.shape[3] == LANES and 2 * S5_CHUNK == LANES
        in_specs += [pl.BlockSpec((1,) + m.shape[1:], lambda g: (g, 0, 0, 0)),
                     pl.BlockSpec((1,) + kc.shape[1:], grp)]
        out_shape.append(jax.ShapeDtypeStruct(ug.shape, BF16))
        out_specs.append(pl.BlockSpec((1, rows, width), grp))
        args += [m, kc]
        scratch = [pltpu.VMEM((width, width), BF16)]
    return pl.pallas_call(
        functools.partial(_s5_kernel, nb=nb, nc=nc, readout=readout),
        out_shape=tuple(out_shape), grid=(g_n,), in_specs=in_specs, out_specs=tuple(out_specs),
        scratch_shapes=scratch,
        compiler_params=_params(("parallel",)),
        name="s5_readout" if readout else "s5_state",
    )(*args)


def _to_groups(ut, nb, nc, chunk, g_n):
    j_n = ut.shape[0] // g_n
    return ut.reshape(g_n, j_n, nb * nc, chunk).transpose(0, 2, 1, 3).reshape(g_n, nb * nc, j_n * chunk)


def _from_groups(y, nb, nc, chunk, g_n):
    j_n = y.shape[2] // chunk
    return y.reshape(g_n, nb * nc, j_n, chunk).transpose(0, 2, 1, 3).reshape(g_n * j_n, nb * nc * chunk)


def _mix_kernel(y_ref, u_ref, gu_ref, vn_ref, g1_ref, g2_ref, x_ref,
                d_ref, wglu_ref, bglu_ref, sguw_ref, sgub_ref, wb1_ref, wb2_ref, wout_ref,
                npost_ref, gt_ref, npre_ref, shf_ref, scf_ref, wr_ref,
                x1_ref, hp_ref, lg_ref, *, tm):
    y = jax.nn.gelu(y_ref[...].astype(F32).T + d_ref[...] * u_ref[...].astype(F32))
    y_s5 = y * jax.nn.sigmoid(_dot(y.astype(BF16), wglu_ref[...]) + bglu_ref[...])

    lane = lax.broadcasted_iota(I32, (1, LANES), 1)
    m_lo = (lane < LANES // 2).astype(F32).astype(BF16)
    m_hi = (lane >= LANES // 2).astype(F32).astype(BF16)
    vn = vn_ref[...]
    chunks = []
    for c in range(tm // SGU_CHUNK):
        vc = vn[c * SGU_CHUNK:(c + 1) * SGU_CHUNK, :]
        tiles = []
        for q in range(vc.shape[1] // LANES):
            vt = vc[:, q * LANES:(q + 1) * LANES]
            rhs = jnp.concatenate([vt * m_lo, vt * m_hi], axis=0)
            tiles.append(_dot(sguw_ref[q], rhs))
        chunks.append(jnp.concatenate(tiles, axis=1) + sgub_ref[...])
    mixed = jnp.concatenate(chunks, axis=0)
    y_sgu = gu_ref[...].astype(F32) * mixed

    merged = (g1_ref[...].astype(F32) * _dot(y_s5.astype(BF16), wb1_ref[...])
              + g2_ref[...].astype(F32) * _dot(y_sgu.astype(BF16), wb2_ref[...]))
    mx = _dot(merged.astype(BF16), wout_ref[...])
    x1 = x_ref[...] + gt_ref[0] * _rms(mx, npost_ref[...])
    x1_ref[...] = x1
    hp = _rms(x1, npre_ref[...]) * (1.0 + scf_ref[0]) + shf_ref[0]
    _store_row_tiles(hp_ref, 0, hp)
    lg_ref[...] = _dot(hp.astype(BF16), wr_ref[...])


def _mix(y, u, gu, vn, g1, g2, x2, s5_d, wglu, bglu, sguw, sgub, wb1, wb2, wout,
         npost, gt, npre, shf, scf, wr, tokens_per_batch, tm):
    n, d = x2.shape
    d_s5, d_sgu, n_exp = y.shape[0], gu.shape[1], wr.shape[1]
    tiles_per_batch = tokens_per_batch // tm
    row = lambda i: (i, 0)
    fixed = lambda i: (0, 0)
    fixed3 = lambda i: (0, 0, 0)
    per_batch = lambda i: (i // tiles_per_batch, 0, 0)
    vec = lambda w: pl.BlockSpec((1, w), fixed)
    return pl.pallas_call(
        functools.partial(_mix_kernel, tm=tm),
        out_shape=(jax.ShapeDtypeStruct((n, d), F32), jax.ShapeDtypeStruct((n * d // LANES, LANES), F32),
                   jax.ShapeDtypeStruct((n, n_exp), F32)),
        grid=(n // tm,),
        in_specs=[pl.BlockSpec((d_s5, tm), lambda i: (0, i)), pl.BlockSpec((tm, d_s5), row),
                  pl.BlockSpec((tm, d_sgu), row),
                  pl.BlockSpec((tm, d_sgu), row), pl.BlockSpec((tm, d), row), pl.BlockSpec((tm, d), row),
                  pl.BlockSpec((tm, d), row),
                  vec(d_s5), pl.BlockSpec(wglu.shape, fixed), vec(d_s5),
                  pl.BlockSpec(sguw.shape, fixed3), pl.BlockSpec(sgub.shape, fixed),
                  pl.BlockSpec(wb1.shape, fixed), pl.BlockSpec(wb2.shape, fixed), pl.BlockSpec(wout.shape, fixed),
                  vec(d), pl.BlockSpec((1, 1, d), per_batch), vec(d),
                  pl.BlockSpec((1, 1, d), per_batch), pl.BlockSpec((1, 1, d), per_batch),
                  pl.BlockSpec(wr.shape, fixed)],
        out_specs=(pl.BlockSpec((tm, d), row), pl.BlockSpec((tm * d // LANES, LANES), row),
                   pl.BlockSpec((tm, n_exp), row)),
        compiler_params=_params(("parallel",)),
        name="mix",
    )(y, u, gu, vn, g1, g2, x2, s5_d.reshape(1, d_s5), wglu, bglu.reshape(1, d_s5), sguw, sgub,
      wb1, wb2, wout, npost.reshape(1, d), gt, npre.reshape(1, d), shf, scf, wr)


def _route_kernel(lg_ref, bias_ref, idx_ref, w_ref, cnt_ref, *, n_exp, tn):
    per_group = n_exp // N_EXPERT_GROUPS
    neg = jnp.float32(-jnp.inf)

    scores = jax.nn.sigmoid(lg_ref[...])
    sel = scores + bias_ref[...]
    gs = []
    for g in range(N_EXPERT_GROUPS):
        sg = sel[g * per_group:(g + 1) * per_group, :]
        m1 = jnp.max(sg, axis=0, keepdims=True)
        is_m1 = sg == m1
        n_m1 = jnp.sum(jnp.where(is_m1, 1.0, 0.0), axis=0, keepdims=True)
        rest = jnp.max(jnp.where(is_m1, neg, sg), axis=0, keepdims=True)
        gs.append(m1 + jnp.where(n_m1 >= 2.0, m1, rest))
    gsm = jnp.concatenate(gs, axis=0)
    g_iota = lax.broadcasted_iota(I32, gsm.shape, 0)
    e_iota = lax.broadcasted_iota(I32, sel.shape, 0).astype(F32)
    masked = []
    for g in range(N_EXPERT_GROUPS):
        mine = gsm[g:g + 1, :]
        beats = jnp.where(gsm > mine, 1.0, jnp.where(gsm == mine, jnp.where(g_iota < g, 1.0, 0.0), 0.0))
        n_beats = jnp.sum(beats, axis=0, keepdims=True)
        masked.append(jnp.where(n_beats < float(TOPK_GROUPS), sel[g * per_group:(g + 1) * per_group, :], neg))
    start = jnp.concatenate(masked, axis=0)
    selm = start
    picked, vals = [], []
    for _ in range(TOP_K):
        m = jnp.max(selm, axis=0, keepdims=True)
        first = jnp.min(jnp.where(selm == m, e_iota, float(n_exp)), axis=0, keepdims=True)
        one = e_iota == first
        picked.append(first)
        vals.append(jnp.sum(jnp.where(one, scores, 0.0), axis=0, keepdims=True))
        selm = jnp.where(one, neg, selm)
    idx_ref[...] = jnp.concatenate(picked, axis=0).astype(I32)
    wv = jnp.concatenate(vals, axis=0)
    w_ref[...] = wv / jnp.sum(wv, axis=0, keepdims=True) * ROUTE_SCALE
    hot = jnp.where(selm == neg, jnp.where(start == neg, 0.0, 1.0), 0.0)

    @pl.when(pl.program_id(0) == 0)
    def _():
        cnt_ref[...] = jnp.zeros_like(cnt_ref)

    cnt_ref[...] += jnp.sum(hot, axis=1, keepdims=True)


def _route(logits_t, bias, tn):
    n_exp, n = logits_t.shape
    col = lambda i: (0, i)
    fixed = lambda i: (0, 0)
    return pl.pallas_call(
        functools.partial(_route_kernel, n_exp=n_exp, tn=tn),
        out_shape=(jax.ShapeDtypeStruct((TOP_K, n), I32), jax.ShapeDtypeStruct((TOP_K, n), F32),
                   jax.ShapeDtypeStruct((n_exp, 1), F32)),
        grid=(n // tn,),
        in_specs=[pl.BlockSpec((n_exp, tn), col), pl.BlockSpec((n_exp, 1), fixed)],
        out_specs=(pl.BlockSpec((TOP_K, tn), col), pl.BlockSpec((TOP_K, tn), col),
                   pl.BlockSpec((n_exp, 1), fixed)),
        compiler_params=_params(("arbitrary",)),
        name="route",
    )(logits_t, bias.reshape(n_exp, 1))


def _dest_kernel(idx_ref, start_ref, dest_ref, carry_ref, *, n_exp, tn):
    @pl.when(pl.program_id(0) == 0)
    def _():
        carry_ref[...] = start_ref[...]

    idx = idx_ref[...]
    e_iota = lax.broadcasted_iota(I32, (n_exp, tn), 0)
    hot = jnp.zeros((n_exp, tn), F32)
    for k in range(TOP_K):
        hot = hot + jnp.where(e_iota == idx[k:k + 1, :], 1.0, 0.0)
    before = jnp.where(lax.broadcasted_iota(I32, (tn, tn), 0) < lax.broadcasted_iota(I32, (tn, tn), 1), 1.0, 0.0)
    rank = _dot(hot.astype(BF16), before.astype(BF16)) + carry_ref[...]
    dest_ref[...] = jnp.concatenate(
        [jnp.sum(jnp.where(e_iota == idx[k:k + 1, :], rank, 0.0), axis=0, keepdims=True)
         for k in range(TOP_K)], axis=0).astype(I32)
    carry_ref[...] += jnp.sum(hot, axis=1, keepdims=True)


def _dest(idx, start, tn):
    n = idx.shape[1]
    n_exp = start.shape[0]
    col = lambda i: (0, i)
    return pl.pallas_call(
        functools.partial(_dest_kernel, n_exp=n_exp, tn=tn),
        out_shape=jax.ShapeDtypeStruct((TOP_K, n), I32),
        grid=(n // tn,),
        in_specs=[pl.BlockSpec((TOP_K, tn), col), pl.BlockSpec((n_exp, 1), lambda i: (0, 0))],
        out_specs=pl.BlockSpec((TOP_K, tn), col),
        scratch_shapes=[pltpu.VMEM((n_exp, 1), F32)],
        compiler_params=_params(("arbitrary",)),
        name="dest",
    )(idx, start)


def _dispatch_kernel(dest_ref, hp_ref, xs_ref, sem, *, tm, pieces):
    def start(r, c):
        src = hp_ref.at[pl.ds(pl.multiple_of(r * pieces, pieces), pieces), :]
        for k in range(TOP_K):
            row = pl.multiple_of(dest_ref[r * TOP_K + k], pieces)
            pltpu.make_async_copy(src, xs_ref.at[pl.ds(row, pieces), :], sem).start(priority=k % 2)
        return c

    lax.fori_loop(0, tm, start, 0, unroll=2)
    for k in range(TOP_K):
        pltpu.make_async_copy(hp_ref, xs_ref.at[pl.ds(0, tm * pieces), :], sem).wait()


def _dispatch(dest, hp, cap, tm, pieces):
    n = hp.shape[0] // pieces
    return pl.pallas_call(
        functools.partial(_dispatch_kernel, tm=tm, pieces=pieces),
        out_shape=jax.ShapeDtypeStruct((cap * pieces, LANES), hp.dtype),
        grid=(n // tm,),
        in_specs=[pl.BlockSpec((tm * TOP_K,), lambda i: (i,), memory_space=pltpu.SMEM),
                  pl.BlockSpec((tm * pieces, LANES), lambda i: (i, 0))],
        out_specs=pl.BlockSpec(memory_space=pl.ANY),
        scratch_shapes=[pltpu.SemaphoreType.DMA(())],
        compiler_params=_params(("arbitrary",)),
        name="dispatch",
    )(dest, hp)


def _expert_kernel(cnt_ref, first_ref, tot_ref, xs_hbm, wg_ref, wu_ref, wd_ref, ys_hbm,
                   xbuf, ybuf, wgu_s, wd_s, xsem, ysem, *, rows, pieces):
    e = pl.program_id(0)
    cnt, first, total = cnt_ref[e], first_ref[e], tot_ref[0]
    n_blk = (cnt + rows - 1) // rows
    d_e = wg_ref.shape[2]
    phys = rows * pieces

    def block_rows(g):
        return pl.ds(pl.multiple_of(g * phys, phys), phys)

    def x_copy(g, slot):
        return pltpu.make_async_copy(xs_hbm.at[block_rows(g), :], xbuf.at[slot], xsem.at[slot])

    def y_copy(g, slot):
        return pltpu.make_async_copy(ybuf.at[slot], ys_hbm.at[block_rows(g), :], ysem.at[slot])

    x_slots, y_slots = xbuf.shape[0], ybuf.shape[0]
    ahead = x_slots - 1

    @pl.when(n_blk > 0)
    def _():
        @pl.when(first == 0)
        def _():
            for a in range(ahead):
                @pl.when(a < total)
                def _():
                    x_copy(a, a).start()

        wgu_s[:, :d_e] = wg_ref[0].astype(BF16)
        wgu_s[:, d_e:] = wu_ref[0].astype(BF16)
        wd_s[...] = wd_ref[0].astype(BF16)

        def block(j, c):
            g = first + j
            slot = g % x_slots
            x_copy(g, slot).wait()

            @pl.when(g + ahead < total)
            def _():
                x_copy(g + ahead, (g + ahead) % x_slots).start()

            yslot = g % y_slots

            @pl.when(g >= y_slots)
            def _():
                y_copy(g - y_slots, yslot).wait()

            x = _load_row_tiles(xbuf, 0, rows, pieces, lead=(slot,))
            r_iota = lax.broadcasted_iota(I32, x.shape, 0) + j * rows
            xb = jnp.where(r_iota < cnt, x, 0.0).astype(BF16)
            gu = _dot(xb, wgu_s[...])
            gate, up = gu[:, :d_e], gu[:, d_e:]
            hid = (gate * jax.nn.sigmoid(gate) * up).astype(BF16)
            y = _dot(hid, wd_s[...])
            for s in range(pieces):
                ybuf[yslot, pl.ds(s, rows, stride=pieces), :] = y[:, s * LANES:(s + 1) * LANES]
            y_copy(g, yslot).start()
            return c

        lax.fori_loop(0, n_blk, block, 0)

        @pl.when(first + n_blk == total)
        def _():
            for back in range(1, y_slots + 1):
                @pl.when(total >= back)
                def _():
                    y_copy(total - back, (total - back) % y_slots).wait()


def _experts(cnt, blk_first, n_used, xs, w_gate, w_up, w_down, rows):
    n_exp, d, d_e = w_gate.shape
    pieces = d // LANES
    hbm = pl.BlockSpec(memory_space=pl.ANY)
    w_map = lambda e, cnt, first, tot: (e, 0, 0)
    grid_spec = pltpu.PrefetchScalarGridSpec(
        num_scalar_prefetch=3,
        grid=(n_exp,),
        in_specs=[hbm, pl.BlockSpec((1, d, d_e), w_map), pl.BlockSpec((1, d, d_e), w_map),
                  pl.BlockSpec((1, d_e, d), w_map)],
        out_specs=hbm,
        scratch_shapes=[pltpu.VMEM((EXPERT_X_SLOTS, rows * pieces, LANES), F32),
                        pltpu.VMEM((EXPERT_Y_SLOTS, rows * pieces, LANES), F32),
                        pltpu.VMEM((d, 2 * d_e), BF16), pltpu.VMEM((d_e, d), BF16),
                        pltpu.SemaphoreType.DMA((EXPERT_X_SLOTS,)), pltpu.SemaphoreType.DMA((EXPERT_Y_SLOTS,))],
    )
    return pl.pallas_call(
        functools.partial(_expert_kernel, rows=rows, pieces=pieces),
        out_shape=jax.ShapeDtypeStruct(xs.shape, F32),
        grid_spec=grid_spec,
        compiler_params=_params(("arbitrary",)),
        name="experts",
    )(cnt, blk_first, n_used, xs, w_gate, w_up, w_down)


def _combine_kernel(dest_ref, next_ref, ys_ref, wt_ref, hp_ref, x1_ref, wsgu_ref, wsd_ref, npost_ref, gt_ref,
                    o_ref, buf, acc, sem, *, tm, pieces):
    i = pl.program_id(0)
    last = pl.num_programs(0) - 1
    cur = i % 2

    def start_token(rows_ref, half, r):
        slot = pl.ds(pl.multiple_of(r * pieces, pieces), pieces)
        for k in range(TOP_K):
            row = pl.multiple_of(rows_ref[r * TOP_K + k], pieces)
            pltpu.make_async_copy(ys_ref.at[pl.ds(row, pieces), :], buf.at[half, k, slot, :],
                                  sem.at[half]).start(priority=k % 2)

    def wait_half(half):
        for k in range(TOP_K):
            pltpu.make_async_copy(ys_ref.at[pl.ds(0, tm * pieces), :], buf.at[half, k], sem.at[half]).wait()

    @pl.when(i == 0)
    def _():
        def first(r, c):
            start_token(dest_ref, 0, r)
            return c

        lax.fori_loop(0, tm, first, 0)

    def tile_pass(half):
        wait_half(half)

        def token(r, c):
            start_token(next_ref, 1 - half, r)
            slot = pl.ds(pl.multiple_of(r * pieces, pieces), pieces)
            terms = [wt_ref[r * TOP_K + k] * buf[half, k, slot, :] for k in range(TOP_K)]
            while len(terms) > 1:
                terms = [terms[j] + terms[j + 1] for j in range(0, len(terms), 2)]
            acc[slot, :] = terms[0]
            return c

        lax.fori_loop(0, tm, token, 0, unroll=4)

        @pl.when(i == last)
        def _():
            wait_half(1 - half)

    for half in range(2):
        @pl.when(cur == half)
        def _():
            tile_pass(half)

    hb = _load_row_tiles(hp_ref, 0, tm, pieces).astype(BF16)
    d_sh = wsd_ref.shape[0]
    gu = _dot(hb, wsgu_ref[...])
    gate, up = gu[:, :d_sh], gu[:, d_sh:]
    shared = _dot((gate * jax.nn.sigmoid(gate) * up).astype(BF16), wsd_ref[...])
    fx = shared + _load_row_tiles(acc, 0, tm, pieces)
    o_ref[...] = x1_ref[...] + gt_ref[0] * _rms(fx, npost_ref[...])


def _combine(dest, ys, wt, hp, x1, wsgu, wsd, npost, gt, tokens_per_batch, tm):
    n, d = x1.shape
    pieces = d // LANES
    w = LANES
    tiles_per_batch = tokens_per_batch // tm
    row = lambda i: (i, 0)
    fixed = lambda i: (0, 0)
    return pl.pallas_call(
        functools.partial(_combine_kernel, tm=tm, pieces=pieces),
        out_shape=jax.ShapeDtypeStruct((n, d), F32),
        grid=(n // tm,),
        in_specs=[pl.BlockSpec((tm * TOP_K,), lambda i: (i,), memory_space=pltpu.SMEM),
                  pl.BlockSpec((tm * TOP_K,), lambda i: (jnp.minimum(i + 1, n // tm - 1),), memory_space=pltpu.SMEM),
                  pl.BlockSpec(memory_space=pl.ANY),
                  pl.BlockSpec((tm * TOP_K,), lambda i: (i,), memory_space=pltpu.SMEM),
                  pl.BlockSpec((tm * pieces, w), row), pl.BlockSpec((tm, d), row),
                  pl.BlockSpec(wsgu.shape, fixed), pl.BlockSpec(wsd.shape, fixed),
                  pl.BlockSpec((1, d), fixed), pl.BlockSpec((1, 1, d), lambda i: (i // tiles_per_batch, 0, 0))],
        out_specs=pl.BlockSpec((tm, d), row),
        scratch_shapes=[pltpu.VMEM((2, TOP_K, tm * pieces, w), F32), pltpu.VMEM((tm * pieces, w), F32),
                        pltpu.SemaphoreType.DMA((2,))],
        compiler_params=_params(("arbitrary",)),
        name="combine",
    )(dest, dest, ys, wt, hp, x1, wsgu, wsd, npost.reshape(1, d), gt)


def _layer(x, ctx, mod_x, mod_c, norm_pre_mix, norm_post_mix, norm_pre_ffn, norm_post_ffn,
           w_in, s5_a_re, s5_a_im, s5_log_dt, s5_b_re, s5_b_im, s5_c_re, s5_c_im, s5_d, w_glu, b_glu,
           sgu_norm, sgu_w, sgu_b, w_branch_s5, w_branch_sgu, w_out,
           w_router, router_bias, w_exp_gate, w_exp_up, w_exp_down, w_sh_gate, w_sh_up, w_sh_down):
    bsz, t_len, d = x.shape
    c_len = ctx.shape[1]
    n = bsz * t_len
    d_s5, d_sgu = w_glu.shape[0], sgu_norm.shape[0]
    g_n = d_s5 // S5_GROUP_CH
    n_exp = w_router.shape[1]
    chunk = S5_CHUNK
    tm = min(512, t_len)
    assert t_len % tm == 0 and tm % SGU_CHUNK == 0 and t_len % chunk == 0 and c_len % chunk == 0

    sh_m, sc_m, gt_m, sh_f, sc_f, gt_f = [v.reshape(bsz, 1, d) for v in jnp.split(mod_x, 6, axis=-1)]
    csh_m, csc_m = mod_c[:d], mod_c[d:2 * d]

    w_in_bf = w_in.astype(BF16)

    nc_lat, nc_ctx = t_len // chunk, c_len // chunk
    n_steps = max(1, (max(nc_lat, nc_ctx) - 1).bit_length())
    m_t, ke_t, kc_t, alp = _s5_tables(s5_a_re, s5_a_im, s5_log_dt, s5_b_re, s5_b_im, s5_c_re, s5_c_im,
                                      chunk, n_steps)

    ctx2 = ctx.reshape(bsz * c_len, d)
    u_ctx = _inproj_u(ctx2, norm_pre_mix, csh_m, csc_m, w_in_bf[:, :d_s5], min(512, bsz * c_len))
    zero_init = jnp.zeros((g_n, bsz, 4 * S5_STATE), F32)
    (ctx_final,) = _s5(_to_groups(u_ctx, bsz, nc_ctx, chunk, g_n), ke_t, alp, zero_init, nb=bsz, nc=nc_ctx)

    x2 = x.reshape(n, d)
    u, ut, gu, vn, g1, g2 = _inproj(x2, norm_pre_mix, sh_m, sc_m, w_in_bf, sgu_norm, t_len, tm, d_s5, d_sgu)
    _, yg = _s5(_to_groups(ut, bsz, nc_lat, chunk, g_n), ke_t, alp, ctx_final, m_t, kc_t, nb=bsz, nc=nc_lat)
    y = _from_groups(yg, bsz, nc_lat, chunk, g_n)

    ch = d_sgu // SGU_GROUPS
    per_tile = LANES // ch
    sguw = sgu_w.reshape(SGU_GROUPS // per_tile, per_tile, SGU_CHUNK, SGU_CHUNK)
    sguw = sguw.transpose(0, 2, 1, 3).reshape(SGU_GROUPS // per_tile, SGU_CHUNK, per_tile * SGU_CHUNK).astype(BF16)
    sgub = jnp.repeat(sgu_b.T, ch, axis=1)

    x1, hp, logits = _mix(y, u, gu, vn, g1, g2, x2, s5_d, w_glu.astype(BF16), b_glu, sguw, sgub,
                          w_branch_s5.astype(BF16), w_branch_sgu.astype(BF16), w_out.astype(BF16),
                          norm_post_mix, gt_m, norm_pre_ffn, sh_f, sc_f, w_router.astype(BF16), t_len, tm)

    tn = min(512, n)
    idx, wts, counts = _route(logits.T, router_bias, tn)
    rows = EXPERT_ROWS
    cnt = counts.reshape(n_exp).astype(I32)
    nblk = (cnt + rows - 1) // rows
    blk_end = jnp.cumsum(nblk)
    blk_start = blk_end - nblk
    cap = ((n * TOP_K) // rows + n_exp) * rows
    dest = _dest(idx, (blk_start * rows).astype(F32).reshape(n_exp, 1), tn)
    pieces = d // LANES
    dest = dest.T.reshape(n * TOP_K) * pieces

    xs = _dispatch(dest, hp, cap, min(1024, t_len), pieces)
    ys = _experts(cnt, blk_start.astype(I32), blk_end[-1:].astype(I32), xs, w_exp_gate, w_exp_up, w_exp_down, rows)
    wsgu = jnp.concatenate([w_sh_gate, w_sh_up], axis=1).astype(BF16)
    out = _combine(dest, ys, wts.T.reshape(n * TOP_K), hp, x1, wsgu, w_sh_down.astype(BF16), norm_post_ffn,
                   gt_f, t_len, min(512, t_len))
    return out.reshape(bsz, t_len, d)


def kernel(x, c, ctx, c_ctx, w_mod, b_mod, norm_pre_mix, norm_post_mix, norm_pre_ffn, norm_post_ffn, w_in, s5_a_re, s5_a_im, s5_log_dt, s5_b_re, s5_b_im, s5_c_re, s5_c_im, s5_d, w_glu, b_glu, sgu_norm, sgu_w, sgu_b, w_branch_s5, w_branch_sgu, w_out, w_router, router_bias, w_exp_gate, w_exp_up, w_exp_down, w_sh_gate, w_sh_up, w_sh_down):
    depth = w_mod.shape[0]
    assert depth == 1, "the context stream is only carried through the last layer's S5 states"
    bsz = x.shape[0]
    pad = (-(bsz + 1)) % 8
    cpad = jnp.concatenate([c, c_ctx[None, :], jnp.zeros((pad, c.shape[1]), c.dtype)], axis=0)
    mod = _modulation(cpad, w_mod[0], b_mod[0])
    return _layer(x, ctx, mod[:bsz], mod[bsz], norm_pre_mix[0], norm_post_mix[0], norm_pre_ffn[0],
                  norm_post_ffn[0], w_in[0], s5_a_re[0], s5_a_im[0], s5_log_dt[0], s5_b_re[0], s5_b_im[0],
                  s5_c_re[0], s5_c_im[0], s5_d[0], w_glu[0], b_glu[0], sgu_norm[0], sgu_w[0], sgu_b[0],
                  w_branch_s5[0], w_branch_sgu[0], w_out[0], w_router[0], router_bias[0],
                  w_exp_gate[0], w_exp_up[0], w_exp_down[0], w_sh_gate[0], w_sh_up[0], w_sh_down[0])
```
